```python
import math
import jax
import jax.numpy as jnp
from jax import lax
import numpy as np

D_MODEL = 2048
BATCH = 8
SEQ = 2048
DEPTH = 2

CHUNK = 64
Q_BLOCK = 128
EPS = 1e-6

GLA_DV = 128
GLA_DK = GLA_DV // 2
GLA_WIDTH = 3 * D_MODEL // 8
GLA_HEADS = GLA_WIDTH // GLA_DV
GLA_KEY_WIDTH = GLA_HEADS * GLA_DK
GLA_LOWRANK = 16
GLA_TAU = 16.0

LRU_WIDTH = D_MODEL // 4
LRU_BLOCKS = 8
LRU_BLOCK_DIM = LRU_WIDTH // LRU_BLOCKS
CONV_WIDTH = 4
LRU_C = 8.0

DIFF_DH = 64
DIFF_DV = 2 * DIFF_DH
DIFF_WIDTH = D_MODEL - GLA_WIDTH - LRU_WIDTH
DIFF_HEADS = DIFF_WIDTH // DIFF_DV
DIFF_QK_WIDTH = DIFF_HEADS * 2 * DIFF_DH

MIX_WIDTH = GLA_WIDTH + LRU_WIDTH + DIFF_WIDTH

REL_BUCKETS = 32
REL_MAX_DIST = 128

N_GROUPS = 8
EXPERTS_PER_GROUP = 8
N_EXPERTS = N_GROUPS * EXPERTS_PER_GROUP
TOP_K = 2
D_EXPERT = D_MODEL // 4
MOE_BLOCK = 128

IN_SIZES = (GLA_KEY_WIDTH, GLA_KEY_WIDTH, GLA_WIDTH, GLA_WIDTH, GLA_LOWRANK,
            LRU_WIDTH, LRU_WIDTH,
            DIFF_QK_WIDTH, DIFF_QK_WIDTH, DIFF_WIDTH)
IN_WIDTH = sum(IN_SIZES)
IN_SPLITS = tuple(int(s) for s in np.cumsum(IN_SIZES)[:-1])

kernel_name = "hybrid_gla_rglru_diffattn_hmoe"


def rmsnorm(x, g):
    xf = x.astype(jnp.float32)
    y = xf * lax.rsqrt(jnp.mean(xf * xf, axis=-1, keepdims=True) + EPS)
    return (y * g.astype(jnp.float32)).astype(x.dtype)


def head_rms(o):
    return o * lax.rsqrt(jnp.mean(o * o, axis=-1, keepdims=True) + EPS)


def t5_bucket(rel):
    nb = REL_BUCKETS // 2
    ret = (rel > 0).astype(jnp.int32) * nb
    n = jnp.abs(rel)
    max_exact = nb // 2
    nf = jnp.maximum(n, 1).astype(jnp.float32)
    large = max_exact + (jnp.log(nf / max_exact) / math.log(REL_MAX_DIST / max_exact)
                         * (nb - max_exact)).astype(jnp.int32)
    large = jnp.minimum(large, nb - 1)
    return ret + jnp.where(n < max_exact, n, large)


def gla_mixer(q, k, v, og, a_lr, w_a2, b_a, norm_g):
    f32 = jnp.float32
    B, S, _ = q.shape
    N = S // CHUNK
    log_alpha = jax.nn.log_sigmoid(a_lr.astype(f32) @ w_a2.astype(f32) + b_a.astype(f32)) / GLA_TAU

    def heads(t, d):
        return t.astype(f32).reshape(B, N, CHUNK, GLA_HEADS, d).transpose(0, 3, 1, 2, 4)

    qh = heads(q, GLA_DK) * (GLA_DK ** -0.5)
    kh = heads(k, GLA_DK)
    vh = heads(v, GLA_DV)
    G = jnp.cumsum(heads(log_alpha, GLA_DK), axis=3)
    G_last = G[:, :, :, -1:, :]
    eG = jnp.exp(G)
    enG = jnp.exp(-G)
    q_fwd = qh * eG
    a_fwd = jnp.einsum('bhnid,bhnjd->bhnij', q_fwd, kh * enG)
    a_bwd = jnp.einsum('bhnid,bhnjd->bhnij', qh * enG, kh * eG)
    causal = jnp.tril(jnp.ones((CHUNK, CHUNK), dtype=bool))
    attn = jnp.where(causal, a_fwd, a_bwd)
    o_intra = jnp.einsum('bhnij,bhnjv->bhniv', attn, vh)
    kv = jnp.einsum('bhncd,bhncv->bhndv', kh * jnp.exp(G_last - G), vh)
    decay = jnp.exp(G_last[:, :, :, 0, :])

    def step(state, inp):
        kv_n, dec_n = inp
        return dec_n[..., None] * state + kv_n, state

    init = jnp.zeros((B, GLA_HEADS, GLA_DK, GLA_DV), f32)
    _, s_prev = lax.scan(step, init, (jnp.moveaxis(kv, 2, 0), jnp.moveaxis(decay, 2, 0)))
    s_prev = jnp.moveaxis(s_prev, 0, 2)
    o_inter = jnp.einsum('bhncd,bhndv->bhncv', q_fwd, s_prev)
    o = (o_intra + o_inter).transpose(0, 2, 3, 1, 4).reshape(B, S, GLA_HEADS, GLA_DV)
    o = head_rms(o).reshape(B, S, GLA_WIDTH) * norm_g.astype(f32)
    return o * jax.nn.silu(og.astype(f32))


def rglru_mixer(y_br, x_br, conv_w, conv_b, wa, ba, wx, bx, lam):
    f32 = jnp.float32
    B, S, W = x_br.shape
    xp = jnp.pad(x_br.astype(f32), ((0, 0), (CONV_WIDTH - 1, 0), (0, 0)))
    xc = conv_b.astype(f32)
    for i in range(CONV_WIDTH):
        xc = xc + xp[:, i:i + S] * conv_w[i].astype(f32)
    xb = xc.reshape(B, S, LRU_BLOCKS, LRU_BLOCK_DIM)
    r = jax.nn.sigmoid(jnp.einsum('bsnd,nde->bsne', xb, wa.astype(f32)).reshape(B, S, W) + ba.astype(f32))
    ig = jax.nn.sigmoid(jnp.einsum('bsnd,nde->bsne', xb, wx.astype(f32)).reshape(B, S, W) + bx.astype(f32))
    log_a = -LRU_C * r * jax.nn.softplus(-lam.astype(f32))
    a = jnp.exp(log_a)
    b = jnp.sqrt(-jnp.expm1(2.0 * log_a)) * (ig * xc)

    def combine(lhs, rhs):
        a1, b1 = lhs
        a2, b2 = rhs
        return a1 * a2, a2 * b1 + b2

    _, h = lax.associative_scan(combine, (a, b), axis=1)
    return h * jax.nn.gelu(y_br.astype(f32))


def diff_attention(q, k, v, lq1, lk1, lq2, lk2, subln_g, rel_table, layer_idx):
    f32 = jnp.float32
    B, S, _ = q.shape
    qh = q.astype(f32).reshape(B, S, DIFF_HEADS, 2, DIFF_DH).transpose(0, 2, 3, 1, 4) * (DIFF_DH ** -0.5)
    kh = k.astype(f32).reshape(B, S, DIFF_HEADS, 2, DIFF_DH).transpose(0, 2, 3, 1, 4)
    vh = v.astype(f32).reshape(B, S, DIFF_HEADS, DIFF_DV).transpose(0, 2, 1, 3)
    lam_init = 0.8 - 0.6 * math.exp(-0.3 * layer_idx)
    lam = (jnp.exp(jnp.sum(lq1.astype(f32) * lk1.astype(f32)))
           - jnp.exp(jnp.sum(lq2.astype(f32) * lk2.astype(f32))) + lam_init)
    table = rel_table.astype(f32)
    pos = jnp.arange(S, dtype=jnp.int32)
    outs = []
    for blk in range(S // Q_BLOCK):
        q0 = blk * Q_BLOCK
        kv_len = q0 + Q_BLOCK
        qp = pos[q0:kv_len]
        kp = pos[:kv_len]
        bias = table[t5_bucket(kp[None, :] - qp[:, None])].transpose(2, 0, 1)[:, None]
        mask = (kp[None, :] // CHUNK) <= (qp[:, None] // CHUNK)
        logits = jnp.einsum('bhmqd,bhmkd->bhmqk', qh[:, :, :, q0:kv_len], kh[:, :, :, :kv_len]) + bias
        p = jax.nn.softmax(jnp.where(mask, logits, -1e30), axis=-1)
        attn = p[:, :, 0] - lam * p[:, :, 1]
        outs.append(jnp.einsum('bhqk,bhkv->bhqv', attn, vh[:, :, :kv_len]))
    o = jnp.concatenate(outs, axis=2)
    o = head_rms(o) * subln_g.astype(f32) * (1.0 - lam_init)
    return o.transpose(0, 2, 1, 3).reshape(B, S, DIFF_WIDTH)


def hier_moe(u, wg, bg, we, be, w1, w3, w2):
    f32 = jnp.float32
    B, S, D = u.shape
    T = B * S
    xt = u.reshape(T, D)
    tok_ids = jnp.arange(T, dtype=jnp.int32)
    g_logits = (xt @ wg + bg).astype(f32)
    g_prob = jax.nn.softmax(g_logits, axis=-1)
    g_idx = jnp.argmax(g_logits, axis=-1).astype(jnp.int32)
    g_w = g_prob[tok_ids, g_idx]
    e_all = (xt @ we + be).astype(f32).reshape(T, N_GROUPS, EXPERTS_PER_GROUP)
    e_logits = e_all[tok_ids, g_idx]
    top_v, top_i = lax.top_k(e_logits, TOP_K)
    e_w = jax.nn.softmax(top_v, axis=-1)
    expert = (g_idx[:, None] * EXPERTS_PER_GROUP + top_i).reshape(-1)
    weight = (g_w[:, None] * e_w).reshape(-1)
    tok = jnp.repeat(tok_ids, TOP_K)
    A = T * TOP_K
    order = jnp.argsort(expert)
    e_sorted = expert[order]
    counts = jnp.bincount(expert, length=N_EXPERTS)
    starts = jnp.cumsum(counts) - counts
    padded = (counts + MOE_BLOCK - 1) // MOE_BLOCK * MOE_BLOCK
    pends = jnp.cumsum(padded)
    pstarts = pends - padded
    dest = pstarts[e_sorted] + (jnp.arange(A, dtype=jnp.int32) - starts[e_sorted])
    P = A + N_EXPERTS * MOE_BLOCK
    n_blocks = P // MOE_BLOCK
    slot_tok = jnp.zeros((P,), jnp.int32).at[dest].set(tok[order])
    slot_w = jnp.zeros((P,), f32).at[dest].set(weight[order])
    block_start = jnp.arange(n_blocks, dtype=pends.dtype) * MOE_BLOCK
    block_expert = jnp.minimum(jnp.searchsorted(pends, block_start, side='right'), N_EXPERTS - 1)
    xs = xt[slot_tok].reshape(n_blocks, MOE_BLOCK, D)

    def run_block(args):
        xb, e = args
        hid = jax.nn.silu(xb @ w1[e]) * (xb @ w3[e])
        return hid @ w2[e]

    y = lax.map(run_block, (xs, block_expert)).reshape(P, D)
    out = jnp.zeros((T, D), f32).at[slot_tok].add(y.astype(f32) * slot_w[:, None])
    return out.reshape(B, S, D).astype(u.dtype)


def setup_inputs(seed: int = 0) -> dict:
    key = jax.random.key(seed)
    ks = jax.random.split(key, 32)
    L, D = DEPTH, D_MODEL

    def nrm(k, shape, scale):
        return jax.random.normal(k, shape, jnp.float32) * scale

    u = jax.random.uniform(ks[15], (L, LRU_WIDTH), jnp.float32, minval=0.9, maxval=0.999)
    a0 = u ** (1.0 / LRU_C)
    return {
        "x": nrm(ks[0], (BATCH, SEQ, D), 1.0),
        "c": nrm(ks[1], (BATCH, D), 1.0),
        "ada_w": nrm(ks[2], (L, D, 6 * D), 0.5 * D ** -0.5),
        "ada_b": nrm(ks[3], (L, 6 * D), 0.02),
        "norm1_g": 1.0 + nrm(ks[4], (L, D), 0.02),
        "w_in": nrm(ks[5], (L, D, IN_WIDTH), D ** -0.5),
        "gla_w_a2": nrm(ks[6], (L, GLA_LOWRANK, GLA_KEY_WIDTH), GLA_LOWRANK ** -0.5),
        "gla_b_a": nrm(ks[7], (L, GLA_KEY_WIDTH), 0.02),
        "gla_norm_g": 1.0 + nrm(ks[8], (L, GLA_WIDTH), 0.02),
        "lru_conv_w": nrm(ks[9], (L, CONV_WIDTH, LRU_WIDTH), CONV_WIDTH ** -0.5),
        "lru_conv_b": nrm(ks[10], (L, LRU_WIDTH), 0.02),
        "lru_wa": nrm(ks[11], (L, LRU_BLOCKS, LRU_BLOCK_DIM, LRU_BLOCK_DIM), LRU_BLOCK_DIM ** -0.5),
        "lru_ba": nrm(ks[12], (L, LRU_WIDTH), 0.02),
        "lru_wx": nrm(ks[13], (L, LRU_BLOCKS, LRU_BLOCK_DIM, LRU_BLOCK_DIM), LRU_BLOCK_DIM ** -0.5),
        "lru_bx": nrm(ks[14], (L, LRU_WIDTH), 0.02),
        "lru_lambda": jnp.log(a0) - jnp.log1p(-a0),
        "diff_lq1": nrm(ks[16], (L, DIFF_DH), 0.1),
        "diff_lk1": nrm(ks[17], (L, DIFF_DH), 0.1),
        "diff_lq2": nrm(ks[18], (L, DIFF_DH), 0.1),
        "diff_lk2": nrm(ks[19], (L, DIFF_DH), 0.1),
        "diff_subln_g": 1.0 + nrm(ks[20], (L, DIFF_DV), 0.02),
        "rel_bias": nrm(ks[21], (REL_BUCKETS, DIFF_HEADS), 0.5),
        "w_out": nrm(ks[22], (L, MIX_WIDTH, D), MIX_WIDTH ** -0.5),
        "norm2_g": 1.0 + nrm(ks[23], (L, D), 0.02),
        "router_g_w": nrm(ks[24], (L, D, N_GROUPS), D ** -0.5),
        "router_g_b": nrm(ks[25], (L, N_GROUPS), 0.01),
        "router_e_w": nrm(ks[26], (L, D, N_EXPERTS), D ** -0.5),
        "router_e_b": nrm(ks[27], (L, N_EXPERTS), 0.01),
        "moe_w1": nrm(ks[28], (L, N_EXPERTS, D, D_EXPERT), D ** -0.5),
        "moe_w3": nrm(ks[29], (L, N_EXPERTS, D, D_EXPERT), D ** -0.5),
        "moe_w2": nrm(ks[30], (L, N_EXPERTS, D_EXPERT, D), D_EXPERT ** -0.5),
        "final_g": 1.0 + nrm(ks[31], (D,), 0.02),
    }


def reference(x, c, ada_w, ada_b, norm1_g, w_in, gla_w_a2, gla_b_a, gla_norm_g,
              lru_conv_w, lru_conv_b, lru_wa, lru_ba, lru_wx, lru_bx, lru_lambda,
              diff_lq1, diff_lk1, diff_lq2, diff_lk2, diff_subln_g, rel_bias, w_out,
              norm2_g, router_g_w, router_g_b, router_e_w, router_e_b,
              moe_w1, moe_w3, moe_w2, final_g):
    h = x
    for l in range(DEPTH):
        mod = jax.nn.silu(c) @ ada_w[l] + ada_b[l]
        sh1, sc1, g1, sh2, sc2, g2 = jnp.split(mod[:, None, :], 6, axis=-1)
        u = rmsnorm(h, norm1_g[l]) * (1.0 + sc1) + sh1
        proj = u @ w_in[l]
        gq, gk, gv, gog, ga, ly, lx, dq, dk, dv = jnp.split(proj, IN_SPLITS, axis=-1)
        o_gla = gla_mixer(gq, gk, gv, gog, ga, gla_w_a2[l], gla_b_a[l], gla_norm_g[l])
        o_lru = rglru_mixer(ly, lx, lru_conv_w[l], lru_conv_b[l], lru_wa[l], lru_ba[l],
                            lru_wx[l], lru_bx[l], lru_lambda[l])
        o_diff = diff_attention(dq, dk, dv, diff_lq1[l], diff_lk1[l], diff_lq2[l], diff_lk2[l],
                                diff_subln_g[l], rel_bias, l)
        mixed = jnp.concatenate([o_gla, o_lru, o_diff], axis=-1).astype(u.dtype)
        h = h + g1 * (mixed @ w_out[l])
        u = rmsnorm(h, norm2_g[l]) * (1.0 + sc2) + sh2
        h = h + g2 * hier_moe(u, router_g_w[l], router_g_b[l], router_e_w[l], router_e_b[l],
                              moe_w1[l], moe_w3[l], moe_w2[l])
    return rmsnorm(h, final_g)
```

```python
import functools
import math

import jax
import jax.numpy as jnp
from jax import lax
from jax.experimental import pallas as pl
from jax.experimental.pallas import tpu as pltpu

F32 = jnp.float32
BF16 = jnp.bfloat16
U32 = jnp.uint32

EPS = 1e-6
CHUNK = 64

GLA_DV = 128
GLA_DK = 64
GLA_HEADS = 6
GLA_WIDTH = GLA_HEADS * GLA_DV
GLA_KEY_WIDTH = GLA_HEADS * GLA_DK
GLA_LOWRANK = 16
GLA_TAU = 16.0

LRU_WIDTH = 512
LRU_BLOCKS = 8
LRU_BLOCK_DIM = LRU_WIDTH // LRU_BLOCKS
CONV_WIDTH = 4
LRU_C = 8.0

DIFF_DH = 64
DIFF_DV = 128
DIFF_HEADS = 6
DIFF_WIDTH = DIFF_HEADS * DIFF_DV

REL_BUCKETS = 32
REL_MAX_DIST = 128

N_GROUPS = 8
EXPERTS_PER_GROUP = 8
N_EXPERTS = 64
TOP_K = 2

LANES = 128
VMEM_LIMIT = 56 * 1024 * 1024

COL_GV = 0
COL_GOG = 768
COL_DQ = 1536
COL_DK = 2304
COL_DV = 3072
COL_GQ = 3840
COL_GK = 4224
COL_LY = 4608
COL_LX = 5120
COL_GA = 5632
PROJ_WIDTH = 5760

Q_TILE = 128
GLA_TILE = 256
LRU_TILE = 256
MOE_BLK = 256
GATHER_ROWS = 256


def _params(sem, vmem=VMEM_LIMIT):
    return pltpu.CompilerParams(dimension_semantics=sem, vmem_limit_bytes=vmem)


def _sigmoid(x):
    return 1.0 / (1.0 + jnp.exp(-x))


def _softplus(x):
    return jnp.maximum(x, 0.0) + jnp.log1p(jnp.exp(-jnp.abs(x)))


def _dot(a, b):
    return jnp.dot(a, b, preferred_element_type=F32)


def _dot_nt(a, b):
    return lax.dot_general(a, b, (((1,), (1,)), ((), ())), preferred_element_type=F32)


def _dot_tn(a, b):
    return lax.dot_general(a, b, (((0,), (0,)), ((), ())), preferred_element_type=F32)


def _pack_bf16_pair(lo, hi):
    lo_bits = lax.bitcast_convert_type(lo.astype(BF16).astype(F32), U32)
    hi_bits = lax.bitcast_convert_type(hi.astype(BF16).astype(F32), U32)
    return (hi_bits & jnp.uint32(0xFFFF0000)) | (lo_bits >> 16)


def _unpack_bf16_pair(w):
    lo = lax.bitcast_convert_type(w << 16, F32)
    hi = lax.bitcast_convert_type(w & jnp.uint32(0xFFFF0000), F32)
    return lo, hi


def _ada_kernel(c_ref, w_ref, b_ref, o_ref):
    c = c_ref[...]
    s = c * _sigmoid(c)
    o_ref[0] = _dot(s.astype(BF16), w_ref[0].astype(BF16)) + b_ref[0]


def _ada_mod(c, ada_w, ada_b):
    L, D, N = ada_w.shape
    B = c.shape[0]
    tn = 1024
    return pl.pallas_call(
        _ada_kernel,
        grid=(L, N // tn),
        in_specs=[
            pl.BlockSpec((B, D), lambda l, j: (0, 0)),
            pl.BlockSpec((1, D, tn), lambda l, j: (l, 0, j)),
            pl.BlockSpec((1, 1, tn), lambda l, j: (l, 0, j)),
        ],
        out_specs=pl.BlockSpec((1, B, tn), lambda l, j: (l, 0, j)),
        out_shape=jax.ShapeDtypeStruct((L, B, N), F32),
        compiler_params=_params(("parallel", "parallel")),
    )(c, ada_w, ada_b.reshape(L, 1, N))


def _modulated_norm(x, g, sc, sh):
    ms = jnp.mean(x * x, axis=-1, keepdims=True)
    return (x * lax.rsqrt(ms + EPS) * g) * (1.0 + sc) + sh


def _inproj_kernel(h_ref, sh_ref, sc_ref, g_ref, w_ref, o_ref, u_scr):
    @pl.when(pl.program_id(1) == 0)
    def _():
        u = _modulated_norm(h_ref[...], g_ref[...], sc_ref[0], sh_ref[0])
        u_scr[...] = u.astype(BF16)

    o_ref[...] = _dot(u_scr[...], w_ref[...]).astype(o_ref.dtype)


def _inproj(h, mod3, norm_g, w_perm, S):
    T, D = h.shape
    N = w_perm.shape[1]
    tm, tn = 512, 1152
    per_b = S // tm
    return pl.pallas_call(
        _inproj_kernel,
        grid=(T // tm, N // tn),
        in_specs=[
            pl.BlockSpec((tm, D), lambda i, j: (i, 0)),
            pl.BlockSpec((1, 1, D), lambda i, j: (i // per_b, 0, 0)),
            pl.BlockSpec((1, 1, D), lambda i, j: (i // per_b, 0, 1)),
            pl.BlockSpec((1, D), lambda i, j: (0, 0)),
            pl.BlockSpec((D, tn), lambda i, j: (0, j)),
        ],
        out_specs=pl.BlockSpec((tm, tn), lambda i, j: (i, j)),
        out_shape=jax.ShapeDtypeStruct((T, N), BF16),
        scratch_shapes=[pltpu.VMEM((tm, D), BF16)],
        compiler_params=_params(("parallel", "arbitrary")),
    )(h, mod3, mod3, norm_g.reshape(1, D), w_perm)


def _permute_w_in(w):
    D = w.shape[0]
    gq, gk, gv, gog, ga, ly, lx, dq, dk, dv = jnp.split(
        w, [384, 768, 1536, 2304, 2320, 2832, 3344, 4112, 4880], axis=1)
    ga = jnp.concatenate([ga, jnp.zeros((D, LANES - GLA_LOWRANK), w.dtype)], axis=1)
    return jnp.concatenate([gv, gog, dq, dk, dv, gq, gk, ly, lx, ga], axis=1).astype(BF16)


def _gla_kernel(q_ref, k_ref, v_ref, og_ref, alr_ref, wa2_ref, ba_ref, ng_ref, o_ref, st_ref):
    tb = q_ref.shape[0]
    n_chunks = tb // CHUNK

    @pl.when(pl.program_id(1) == 0)
    def _():
        st_ref[...] = jnp.zeros_like(st_ref)

    row = lax.broadcasted_iota(jnp.int32, (tb, tb), 0)
    col = lax.broadcasted_iota(jnp.int32, (tb, tb), 1)
    same_chunk = (row // CHUNK) == (col // CHUNK)
    causal = col <= row
    tril = jnp.where(same_chunk & causal, 1.0, 0.0).astype(BF16)
    chunk_ones = jnp.where(same_chunk, 1.0, 0.0).astype(BF16)
    lane = lax.broadcasted_iota(jnp.int32, (1, LANES), 1)
    half_masks = (lane < GLA_DK, lane >= GLA_DK)

    alr = alr_ref[...]
    for p in range(GLA_HEADS // 2):
        cs = slice(p * LANES, (p + 1) * LANES)
        z = _dot(alr, wa2_ref[:, cs]) + ba_ref[:, cs]
        la = (jnp.minimum(z, 0.0) - jnp.log1p(jnp.exp(-jnp.abs(z)))) * (1.0 / GLA_TAU)
        la_hi = la.astype(BF16)
        la_lo = (la - la_hi.astype(F32)).astype(BF16)
        G = _dot(tril, la_hi) + _dot(tril, la_lo)
        Gl = _dot(chunk_ones, la_hi) + _dot(chunk_ones, la_lo)
        eG = jnp.exp(G)
        enG = jnp.exp(-G)
        q = q_ref[:, cs].astype(F32) * (GLA_DK ** -0.5)
        k = k_ref[:, cs].astype(F32)
        qf = q * eG
        qb = q * enG
        kf = (k * eG).astype(BF16)
        kb = (k * enG).astype(BF16)
        kd = k * jnp.exp(Gl - G)
        for hh in range(2):
            head = 2 * p + hh
            m = half_masks[hh]
            vs = slice(head * GLA_DV, (head + 1) * GLA_DV)
            qf_h = jnp.where(m, qf, 0.0).astype(BF16)
            qb_h = jnp.where(m, qb, 0.0).astype(BF16)
            kd_h = jnp.where(m, kd, 0.0).astype(BF16)
            v_h = v_ref[:, vs]
            a_f = _dot_nt(qf_h, kb)
            a_b = _dot_nt(qb_h, kf)
            attn = jnp.where(same_chunk, jnp.where(causal, a_f, a_b), 0.0)
            o_intra = _dot(attn.astype(BF16), v_h)
            st = st_ref[head]
            inter = []
            for c in range(n_chunks):
                rs = slice(c * CHUNK, (c + 1) * CHUNK)
                inter.append(_dot_nt(qf_h[rs], st.astype(BF16)))
                decay = jnp.exp(Gl[c * CHUNK:c * CHUNK + 1, :])
                st = st * decay + _dot_tn(v_h[rs], kd_h[rs])
            st_ref[head] = st
            o = o_intra + jnp.concatenate(inter, axis=0)
            o = o * lax.rsqrt(jnp.mean(o * o, axis=-1, keepdims=True) + EPS)
            og = og_ref[:, vs].astype(F32)
            o_ref[:, vs] = (o * ng_ref[:, vs] * (og * _sigmoid(og))).astype(o_ref.dtype)


def _gla(proj, wa2_pad, b_a, norm_g, B, S):
    T = proj.shape[0]
    tb = GLA_TILE
    nt = S // tb
    row = lambda b, i: b * nt + i
    return pl.pallas_call(
        _gla_kernel,
        grid=(B, nt),
        in_specs=[
            pl.BlockSpec((tb, GLA_KEY_WIDTH), lambda b, i: (row(b, i), COL_GQ // GLA_KEY_WIDTH)),
            pl.BlockSpec((tb, GLA_KEY_WIDTH), lambda b, i: (row(b, i), COL_GK // GLA_KEY_WIDTH)),
            pl.BlockSpec((tb, GLA_WIDTH), lambda b, i: (row(b, i), COL_GV // GLA_WIDTH)),
            pl.BlockSpec((tb, GLA_WIDTH), lambda b, i: (row(b, i), COL_GOG // GLA_WIDTH)),
            pl.BlockSpec((tb, LANES), lambda b, i: (row(b, i), COL_GA // LANES)),
            pl.BlockSpec((LANES, GLA_KEY_WIDTH), lambda b, i: (0, 0)),
            pl.BlockSpec((1, GLA_KEY_WIDTH), lambda b, i: (0, 0)),
            pl.BlockSpec((1, GLA_WIDTH), lambda b, i: (0, 0)),
        ],
        out_specs=pl.BlockSpec((tb, GLA_WIDTH), lambda b, i: (row(b, i), 0)),
        out_shape=jax.ShapeDtypeStruct((T, GLA_WIDTH), BF16),
        scratch_shapes=[pltpu.VMEM((GLA_HEADS, GLA_DV, LANES), F32)],
        compiler_params=_params(("parallel", "arbitrary")),
    )(proj, proj, proj, proj, proj, wa2_pad, b_a.reshape(1, -1), norm_g.reshape(1, -1))


def _lru_kernel(y_ref, x_ref, cw_ref, cb_ref, wg_ref, bg_ref, lam_ref, o_ref, *scratch):
    B, ts, W = x_ref.shape
    n_planes = W // LANES
    a_scr = scratch[0:n_planes]
    b_scr = scratch[n_planes:2 * n_planes]
    h_scr = scratch[2 * n_planes:3 * n_planes]
    xc_scr, tail_scr, carry_scr = scratch[3 * n_planes:]

    @pl.when(pl.program_id(0) == 0)
    def _():
        tail_scr[...] = jnp.zeros_like(tail_scr)
        carry_scr[...] = jnp.zeros_like(carry_scr)

    cw = cw_ref[...]
    cb = cb_ref[...]
    sp = _softplus(-lam_ref[...])
    row8 = lax.broadcasted_iota(jnp.int32, (8, W), 0)
    for b in range(B):
        x = x_ref[b].astype(F32)
        tail = tail_scr[b]
        xc = cb + cw[CONV_WIDTH - 1:CONV_WIDTH, :] * x
        head = cb + cw[CONV_WIDTH - 1:CONV_WIDTH, :] * x[0:8]
        for d in range(1, CONV_WIDTH):
            wd = cw[CONV_WIDTH - 1 - d:CONV_WIDTH - d, :]
            xr = pltpu.roll(x, d, 0)
            xc = xc + wd * xr
            head = head + wd * jnp.where(row8 < d, pltpu.roll(tail, d, 0), xr[0:8])
        tail_scr[b] = x[ts - 8:ts]
        xc_scr[...] = xc
        xc_scr[0:8] = head
        xc = xc_scr[...]
        gates = _sigmoid(_dot(xc.astype(BF16), wg_ref[...]) + bg_ref[...])
        r = gates[:, :W]
        ig = gates[:, W:]
        log_a = (-LRU_C) * r * sp
        a = jnp.exp(log_a)
        b_in = jnp.sqrt(-jnp.tanh(log_a) * (a * a + 1.0)) * (ig * xc)
        rows = slice(b * ts, (b + 1) * ts)
        for k in range(n_planes):
            a_scr[k][rows] = a[:, k * LANES:(k + 1) * LANES]
            b_scr[k][rows] = b_in[:, k * LANES:(k + 1) * LANES]

    def step(t, hs):
        idx = pl.ds(t, B, stride=ts)
        out = []
        for k in range(n_planes):
            hk = a_scr[k][idx, :] * hs[k] + b_scr[k][idx, :]
            h_scr[k][idx, :] = hk
            out.append(hk)
        return tuple(out)

    hs = lax.fori_loop(0, ts, step, tuple(carry_scr[k] for k in range(n_planes)), unroll=8)
    for k in range(n_planes):
        carry_scr[k] = hs[k]

    for b in range(B):
        rows = slice(b * ts, (b + 1) * ts)
        y = y_ref[b].astype(F32)
        gelu = 0.5 * y * (1.0 + jnp.tanh(math.sqrt(2.0 / math.pi) * (y + 0.044715 * (y * y * y))))
        h = jnp.concatenate([h_scr[k][rows] for k in range(n_planes)], axis=1)
        o_ref[b] = (h * gelu).astype(o_ref.dtype)


def _lru(proj3, conv_w, conv_b, w_gates, b_gates, lam):
    B, S, _ = proj3.shape
    W = LRU_WIDTH
    ts = LRU_TILE
    n_planes = W // LANES
    full = lambda shape: pl.BlockSpec(shape, lambda i: (0,) * len(shape))
    return pl.pallas_call(
        _lru_kernel,
        grid=(S // ts,),
        in_specs=[
            pl.BlockSpec((B, ts, W), lambda i: (0, i, COL_LY // W)),
            pl.BlockSpec((B, ts, W), lambda i: (0, i, COL_LX // W)),
            full((CONV_WIDTH, W)),
            full((1, W)),
            full((W, 2 * W)),
            full((1, 2 * W)),
            full((1, W)),
        ],
        out_specs=pl.BlockSpec((B, ts, W), lambda i: (0, i, 0)),
        out_shape=jax.ShapeDtypeStruct((B, S, W), BF16),
        scratch_shapes=(
            [pltpu.VMEM((B * ts, LANES), F32) for _ in range(3 * n_planes)]
            + [pltpu.VMEM((ts, W), F32), pltpu.VMEM((B, 8, W), F32), pltpu.VMEM((n_planes, B, LANES), F32)]),
        compiler_params=_params(("arbitrary",)),
    )(proj3, proj3, conv_w, conv_b.reshape(1, W), w_gates, b_gates, lam.reshape(1, W))


def _block_diag(w):
    n, d, _ = w.shape
    eye = jnp.eye(n, dtype=w.dtype)
    return (eye[:, None, :, None] * w[:, :, None, :]).reshape(n * d, n * d)


def _t5_bucket(rel):
    nb = REL_BUCKETS // 2
    ret = (rel > 0).astype(jnp.int32) * nb
    n = jnp.abs(rel)
    max_exact = nb // 2
    nf = jnp.maximum(n, 1).astype(jnp.float32)
    large = max_exact + (jnp.log(nf / max_exact) / math.log(REL_MAX_DIST / max_exact)
                         * (nb - max_exact)).astype(jnp.int32)
    large = jnp.minimum(large, nb - 1)
    return ret + jnp.where(n < max_exact, n, large)


def _bias_tiles(rel_bias):
    tq = Q_TILE
    qp = jnp.arange(tq, dtype=jnp.int32)[:, None]
    kp = jnp.arange(tq, dtype=jnp.int32)[None, :]
    table = rel_bias.astype(F32)
    diag = table[_t5_bucket(kp - qp)]
    mask = (kp // CHUNK) <= (qp // CHUNK)
    diag = jnp.where(mask[:, :, None], diag, -1e30)
    prev = table[_t5_bucket(kp - tq - qp)]
    far = jnp.broadcast_to(table[_t5_bucket(jnp.int32(-tq - 1))], (tq, tq, table.shape[1]))
    tiles = jnp.stack([diag, prev, far], axis=0).transpose(3, 0, 1, 2)
    return jnp.concatenate([tiles, tiles], axis=2)


def _diff_kernel(lam_init, q_ref, k_ref, v_ref, bias_ref, lqk_ref, g_ref, o_ref, m_scr, l_scr, acc_scr):
    i = pl.program_id(2)
    tq = q_ref.shape[0]
    lane = lax.broadcasted_iota(jnp.int32, (1, LANES), 1)
    q = q_ref[...].astype(F32) * (DIFF_DH ** -0.5)
    qs = jnp.concatenate([jnp.where(lane < DIFF_DH, q, 0.0), jnp.where(lane >= DIFF_DH, q, 0.0)],
                         axis=0).astype(BF16)
    m_scr[...] = jnp.full_like(m_scr, -1e30)
    l_scr[...] = jnp.zeros_like(l_scr)
    acc_scr[...] = jnp.zeros_like(acc_scr)

    def body(j, carry):
        ks = pl.ds(pl.multiple_of(j * tq, tq), tq)
        kind = jnp.minimum(i - j, 2)
        s = _dot_nt(qs, k_ref[ks, :]) + bias_ref[0, kind]
        m_prev = m_scr[...]
        m_new = jnp.maximum(m_prev, jnp.max(s, axis=-1, keepdims=True))
        alpha = jnp.exp(m_prev - m_new)
        p = jnp.exp(s - m_new)
        l_scr[...] = alpha * l_scr[...] + jnp.sum(p, axis=-1, keepdims=True)
        acc_scr[...] = alpha * acc_scr[...] + _dot(p.astype(BF16), v_ref[ks, :])
        m_scr[...] = m_new
        return carry

    lax.fori_loop(0, i + 1, body, 0)

    lqk = lqk_ref[...]
    lam = (jnp.exp(jnp.sum(lqk[0:1] * lqk[1:2], axis=-1, keepdims=True))
           - jnp.exp(jnp.sum(lqk[2:3] * lqk[3:4], axis=-1, keepdims=True)) + lam_init)
    o = acc_scr[...] / l_scr[...]
    o = o[:tq] - lam * o[tq:]
    o = o * lax.rsqrt(jnp.mean(o * o, axis=-1, keepdims=True) + EPS)
    o_ref[...] = (o * g_ref[...] * (1.0 - lam_init)).astype(o_ref.dtype)


def _diff_attention(proj, bias_tiles, lqk, subln_g, layer_idx, B, S):
    T = proj.shape[0]
    tq = Q_TILE
    nq = S // tq
    lam_init = 0.8 - 0.6 * math.exp(-0.3 * layer_idx)
    return pl.pallas_call(
        functools.partial(_diff_kernel, lam_init),
        grid=(B, DIFF_HEADS, nq),
        in_specs=[
            pl.BlockSpec((tq, LANES), lambda b, h, i: (b * nq + i, COL_DQ // LANES + h)),
            pl.BlockSpec((S, LANES), lambda b, h, i: (b, COL_DK // LANES + h)),
            pl.BlockSpec((S, LANES), lambda b, h, i: (b, COL_DV // LANES + h)),
            pl.BlockSpec((1, 3, 2 * tq, tq), lambda b, h, i: (h, 0, 0, 0)),
            pl.BlockSpec((4, DIFF_DH), lambda b, h, i: (0, 0)),
            pl.BlockSpec((1, DIFF_DV), lambda b, h, i: (0, 0)),
        ],
        out_specs=pl.BlockSpec((tq, LANES), lambda b, h, i: (b * nq + i, h)),
        out_shape=jax.ShapeDtypeStruct((T, DIFF_WIDTH), BF16),
        scratch_shapes=[
            pltpu.VMEM((2 * tq, 1), F32),
            pltpu.VMEM((2 * tq, 1), F32),
            pltpu.VMEM((2 * tq, DIFF_DV), F32),
        ],
        compiler_params=_params(("parallel", "parallel", "arbitrary")),
    )(proj, proj, proj, bias_tiles, lqk, subln_g.reshape(1, DIFF_DV))


def _outproj_kernel(h_ref, og_ref, ol_ref, od_ref, w_ref, g1_ref, sh2_ref, sc2_ref, n2_ref, rw_ref, rb_ref,
                    hn_ref, u2_ref, eid_ref, ew_ref):
    D = h_ref.shape[1]
    acc = _dot(og_ref[...], w_ref[0:GLA_WIDTH, :])
    acc += _dot(ol_ref[...], w_ref[GLA_WIDTH:GLA_WIDTH + LRU_WIDTH, :])
    acc += _dot(od_ref[...], w_ref[GLA_WIDTH + LRU_WIDTH:, :])
    hn = h_ref[...] + g1_ref[0] * acc
    hn_ref[...] = hn
    u2 = _modulated_norm(hn, n2_ref[...], sc2_ref[0], sh2_ref[0])
    u2_ref[...] = _pack_bf16_pair(u2[:, :D // 2], u2[:, D // 2:])

    logits = _dot(u2.astype(BF16), rw_ref[...]) + rb_ref[...]
    lane = lax.broadcasted_iota(jnp.int32, logits.shape, 1)
    lane_f = lane.astype(F32)
    neg = jnp.float32(-jnp.inf)
    gmask = lane < N_GROUPS
    gl = jnp.where(gmask, logits, neg)
    gmax = jnp.max(gl, axis=-1, keepdims=True)
    gidx = jnp.min(jnp.where(gl == gmax, lane_f, float(LANES)), axis=-1, keepdims=True)
    g_w = 1.0 / jnp.sum(jnp.where(gmask, jnp.exp(gl - gmax), 0.0), axis=-1, keepdims=True)
    egroup = ((lane - N_GROUPS) >> 3).astype(F32)
    emask = (lane >= N_GROUPS) & (lane < N_GROUPS + N_EXPERTS) & (egroup == gidx)
    el = jnp.where(emask, logits, neg)
    v1 = jnp.max(el, axis=-1, keepdims=True)
    i1 = jnp.min(jnp.where(el == v1, lane_f, float(LANES)), axis=-1, keepdims=True)
    el2 = jnp.where(lane_f == i1, neg, el)
    v2 = jnp.max(el2, axis=-1, keepdims=True)
    i2 = jnp.min(jnp.where(el2 == v2, lane_f, float(LANES)), axis=-1, keepdims=True)
    e21 = jnp.exp(v2 - v1)
    w1 = g_w / (1.0 + e21)
    w2 = g_w * e21 / (1.0 + e21)
    eid = jnp.where(lane == 0, i1, jnp.where(lane == 1, i2, float(N_GROUPS))) - float(N_GROUPS)
    eid_ref[...] = eid.astype(jnp.int32)
    ew_ref[...] = jnp.where(lane == 0, w1, jnp.where(lane == 1, w2, 0.0))


def _outproj(h, o_gla, o_lru, o_diff, w_out, mod3, norm2_g, rw, rb, S):
    T, D = h.shape
    tm = 256
    per_b = S // tm
    rowblk = lambda width: pl.BlockSpec((tm, width), lambda i: (i, 0))
    modblk = lambda k: pl.BlockSpec((1, 1, D), lambda i: (i // per_b, 0, k))
    full = lambda shape: pl.BlockSpec(shape, lambda i: (0,) * len(shape))
    return pl.pallas_call(
        _outproj_kernel,
        grid=(T // tm,),
        in_specs=[
            rowblk(D), rowblk(GLA_WIDTH), rowblk(LRU_WIDTH), rowblk(DIFF_WIDTH),
            full((D, D)),
            modblk(2), modblk(3), modblk(4),
            full((1, D)),
            full((D, LANES)),
            full((1, LANES)),
        ],
        out_specs=[rowblk(D), rowblk(D // 2), rowblk(LANES), rowblk(LANES)],
        out_shape=[
            jax.ShapeDtypeStruct((T, D), F32),
            jax.ShapeDtypeStruct((T, D // 2), U32),
            jax.ShapeDtypeStruct((T, LANES), jnp.int32),
            jax.ShapeDtypeStruct((T, LANES), F32),
        ],
        compiler_params=_params(("parallel",)),
    )(h, o_gla, o_lru, o_diff, w_out, mod3, mod3, mod3, norm2_g.reshape(1, D), rw, rb)


def _dispatch(eid):
    T = eid.shape[0]
    A = T * TOP_K
    blk = MOE_BLK
    P = A + N_EXPERTS * blk
    n_blocks = P // blk
    expert = eid.reshape(A)
    onehot = (expert[:, None] == jnp.arange(N_EXPERTS, dtype=jnp.int32)[None, :]).astype(jnp.int32)
    csum = jnp.cumsum(onehot, axis=0)
    counts = csum[-1]
    padded = (counts + blk - 1) // blk * blk
    pends = jnp.cumsum(padded)
    pstarts = pends - padded
    dest = jnp.sum(onehot * (csum - 1 + pstarts[None, :]), axis=1).astype(jnp.int32)
    tok = jnp.arange(A, dtype=jnp.int32) // TOP_K
    slot_tok = jnp.zeros((P,), jnp.int32).at[dest].set(tok)
    n_used = (pends[-1] // blk).astype(jnp.int32)
    block_start = jnp.arange(n_blocks, dtype=jnp.int32) * blk
    block_expert = jnp.minimum(jnp.searchsorted(pends, block_start, side='right'),
                               N_EXPERTS - 1).astype(jnp.int32)
    last_used = block_expert[jnp.maximum(n_used - 1, 0)]
    block_expert = jnp.where(jnp.arange(n_blocks) < n_used, block_expert, last_used)
    return slot_tok, dest.reshape(T, TOP_K), block_expert, n_used.reshape(1)


def _gather_kernel(tok_ref, src_ref, o_ref, sem):
    rows = o_ref.shape[0]
    base = pl.program_id(0) * rows

    def issue(r, carry):
        t = tok_ref[base + r]
        pltpu.make_async_copy(src_ref.at[pl.ds(t, 1)], o_ref.at[pl.ds(r, 1)], sem).start()
        return carry

    lax.fori_loop(0, rows, issue, 0, unroll=8)

    def drain(r, carry):
        pltpu.make_async_copy(src_ref.at[pl.ds(0, 1)], o_ref.at[pl.ds(r, 1)], sem).wait()
        return carry

    lax.fori_loop(0, rows, drain, 0, unroll=8)


def _gather_rows(slot_tok, src):
    P = slot_tok.shape[0]
    W = src.shape[1]
    rows = GATHER_ROWS
    return pl.pallas_call(
        _gather_kernel,
        grid_spec=pltpu.PrefetchScalarGridSpec(
            num_scalar_prefetch=1,
            grid=(P // rows,),
            in_specs=[pl.BlockSpec(memory_space=pl.ANY)],
            out_specs=pl.BlockSpec((rows, W), lambda i, tok: (i, 0)),
            scratch_shapes=[pltpu.SemaphoreType.DMA(())],
        ),
        out_shape=jax.ShapeDtypeStruct((P, W), src.dtype),
        compiler_params=_params(("arbitrary",)),
    )(slot_tok, src)


def _expert_kernel(be_ref, nu_ref, xs_ref, w1_ref, w3_ref, w2_ref, y_ref, w1b, w3b, w2b):
    i = pl.program_id(0)
    D = w1_ref.shape[0]
    changed = (i == 0) | (be_ref[i] != be_ref[jnp.maximum(i - 1, 0)])

    @pl.when(changed)
    def _():
        w1b[...] = w1_ref[...].astype(BF16)
        w3b[...] = w3_ref[...].astype(BF16)
        w2b[...] = w2_ref[...].astype(BF16)

    @pl.when(i < nu_ref[0])
    def _():
        lo, hi = _unpack_bf16_pair(xs_ref[...])
        lo = lo.astype(BF16)
        hi = hi.astype(BF16)
        a = _dot(lo, w1b[0:D // 2, :]) + _dot(hi, w1b[D // 2:, :])
        g = _dot(lo, w3b[0:D // 2, :]) + _dot(hi, w3b[D // 2:, :])
        hid = ((a * _sigmoid(a)) * g).astype(BF16)
        y_ref[...] = _pack_bf16_pair(_dot(hid, w2b[:, 0:D // 2]), _dot(hid, w2b[:, D // 2:]))

    @pl.when(i >= nu_ref[0])
    def _():
        y_ref[...] = jnp.zeros_like(y_ref)


def _experts(block_expert, n_used, xs, w1, w3, w2, layer):
    P, W = xs.shape
    _, _, D, DE = w1.shape
    blk = MOE_BLK
    n_blocks = P // blk
    rowmap = lambda i, be, nu: (jnp.minimum(i, jnp.maximum(nu[0] - 1, 0)), 0)
    return pl.pallas_call(
        _expert_kernel,
        grid_spec=pltpu.PrefetchScalarGridSpec(
            num_scalar_prefetch=2,
            grid=(n_blocks,),
            in_specs=[
                pl.BlockSpec((blk, W), rowmap),
                pl.BlockSpec((None, None, D, DE), lambda i, be, nu: (layer, be[i], 0, 0)),
                pl.BlockSpec((None, None, D, DE), lambda i, be, nu: (layer, be[i], 0, 0)),
                pl.BlockSpec((None, None, DE, D), lambda i, be, nu: (layer, be[i], 0, 0)),
            ],
            out_specs=pl.BlockSpec((blk, W), lambda i, be, nu: (i, 0)),
            scratch_shapes=[
                pltpu.VMEM((D, DE), BF16),
                pltpu.VMEM((D, DE), BF16),
                pltpu.VMEM((DE, D), BF16),
            ],
        ),
        out_shape=jax.ShapeDtypeStruct((P, W), U32),
        compiler_params=_params(("arbitrary",)),
    )(block_expert, n_used, xs, w1, w3, w2)


def _combine_kernel(final, d0_ref, d1_ref, y_ref, h_ref, g2_ref, ew_ref, fg_ref, o_ref, buf, sem):
    tc, D = h_ref.shape
    base = pl.program_id(0) * tc

    def issue(r, carry):
        pltpu.make_async_copy(y_ref.at[pl.ds(d0_ref[base + r], 1)], buf.at[0, pl.ds(r, 1)], sem).start()
        pltpu.make_async_copy(y_ref.at[pl.ds(d1_ref[base + r], 1)], buf.at[1, pl.ds(r, 1)], sem).start()
        return carry

    lax.fori_loop(0, tc, issue, 0, unroll=8)

    def drain(r, carry):
        pltpu.make_async_copy(y_ref.at[pl.ds(0, 1)], buf.at[0, pl.ds(r, 1)], sem).wait()
        pltpu.make_async_copy(y_ref.at[pl.ds(0, 1)], buf.at[1, pl.ds(r, 1)], sem).wait()
        return carry

    lax.fori_loop(0, tc, drain, 0, unroll=8)

    ew = ew_ref[...]
    w0 = ew[:, 0:1]
    w1 = ew[:, 1:2]
    lo0, hi0 = _unpack_bf16_pair(buf[0])
    lo1, hi1 = _unpack_bf16_pair(buf[1])
    moe = jnp.concatenate([w0 * lo0 + w1 * lo1, w0 * hi0 + w1 * hi1], axis=1)
    hn = h_ref[...] + g2_ref[0] * moe
    if final:
        hn = hn * lax.rsqrt(jnp.mean(hn * hn, axis=-1, keepdims=True) + EPS) * fg_ref[...]
    o_ref[...] = hn


def _combine(dest, y, h, mod3, ew, final_g, S, final):
    T, D = h.shape
    tc = 256
    per_b = S // tc
    return pl.pallas_call(
        functools.partial(_combine_kernel, final),
        grid_spec=pltpu.PrefetchScalarGridSpec(
            num_scalar_prefetch=2,
            grid=(T // tc,),
            in_specs=[
                pl.BlockSpec(memory_space=pl.ANY),
                pl.BlockSpec((tc, D), lambda i, d0, d1: (i, 0)),
                pl.BlockSpec((1, 1, D), lambda i, d0, d1: (i // per_b, 0, 5)),
                pl.BlockSpec((tc, LANES), lambda i, d0, d1: (i, 0)),
                pl.BlockSpec((1, D), lambda i, d0, d1: (0, 0)),
            ],
            out_specs=pl.BlockSpec((tc, D), lambda i, d0, d1: (i, 0)),
            scratch_shapes=[
                pltpu.VMEM((2, tc, D // 2), U32),
                pltpu.SemaphoreType.DMA(()),
            ],
        ),
        out_shape=jax.ShapeDtypeStruct((T, D), F32),
        compiler_params=_params(("arbitrary",)),
    )(dest[:, 0], dest[:, 1], y, h, mod3, ew, final_g.reshape(1, D))


def kernel(x, c, ada_w, ada_b, norm1_g, w_in, gla_w_a2, gla_b_a, gla_norm_g, lru_conv_w, lru_conv_b,
           lru_wa, lru_ba, lru_wx, lru_bx, lru_lambda, diff_lq1, diff_lk1, diff_lq2, diff_lk2,
           diff_subln_g, rel_bias, w_out, norm2_g, router_g_w, router_g_b, router_e_w, router_e_b,
           moe_w1, moe_w3, moe_w2, final_g):
    B, S, D = x.shape
    T = B * S
    L = ada_w.shape[0]
    h = x.reshape(T, D)
    mod = _ada_mod(c, ada_w, ada_b)
    bias_tiles = _bias_tiles(rel_bias)
    for l in range(L):
        mod3 = mod[l][:, None, :]
        proj = _inproj(h, mod3, norm1_g[l], _permute_w_in(w_in[l]), S)
        wa2_pad = jnp.concatenate(
            [gla_w_a2[l], jnp.zeros((LANES - GLA_LOWRANK, GLA_KEY_WIDTH), F32)], axis=0).astype(BF16)
        o_gla = _gla(proj, wa2_pad, gla_b_a[l], gla_norm_g[l], B, S)
        w_gates = jnp.concatenate([_block_diag(lru_wa[l]), _block_diag(lru_wx[l])], axis=1).astype(BF16)
        b_gates = jnp.concatenate([lru_ba[l], lru_bx[l]]).reshape(1, 2 * LRU_WIDTH)
        o_lru = _lru(proj.reshape(B, S, PROJ_WIDTH), lru_conv_w[l], lru_conv_b[l], w_gates, b_gates,
                     lru_lambda[l]).reshape(T, LRU_WIDTH)
        lqk = jnp.stack([diff_lq1[l], diff_lk1[l], diff_lq2[l], diff_lk2[l]], axis=0)
        o_diff = _diff_attention(proj, bias_tiles, lqk, diff_subln_g[l], l, B, S)
        rw = jnp.concatenate(
            [router_g_w[l], router_e_w[l], jnp.zeros((D, LANES - N_GROUPS - N_EXPERTS), F32)], axis=1).astype(BF16)
        rb = jnp.concatenate(
            [router_g_b[l], router_e_b[l], jnp.zeros((LANES - N_GROUPS - N_EXPERTS,), F32)]).reshape(1, LANES)
        h, u2, eid, ew = _outproj(h, o_gla, o_lru, o_diff, w_out[l].astype(BF16), mod3, norm2_g[l], rw, rb, S)
        slot_tok, dest, block_expert, n_used = _dispatch(eid[:, :TOP_K])
        xs = _gather_rows(slot_tok, u2)
        y = _experts(block_expert, n_used, xs, moe_w1, moe_w3, moe_w2, l)
        h = _combine(dest, y, h, mod3, ew, final_g, S, final=(l == L - 1))
    return h.reshape(B, S, D)
```

```python
import functools
import math

import jax
import jax.numpy as jnp
from jax import lax
from jax.experimental import pallas as pl
from jax.experimental.pallas import tpu as pltpu

F32 = jnp.float32
BF16 = jnp.bfloat16
U32 = jnp.uint32

EPS = 1e-6
CHUNK = 64

GLA_DV = 128
GLA_DK = 64
GLA_HEADS = 6
GLA_WIDTH = GLA_HEADS * GLA_DV
GLA_KEY_WIDTH = GLA_HEADS * GLA_DK
GLA_LOWRANK = 16
GLA_TAU = 16.0

LRU_WIDTH = 512
LRU_BLOCKS = 8
LRU_BLOCK_DIM = LRU_WIDTH // LRU_BLOCKS
CONV_WIDTH = 4
LRU_C = 8.0

DIFF_DH = 64
DIFF_DV = 128
DIFF_HEADS = 6
DIFF_WIDTH = DIFF_HEADS * DIFF_DV

REL_BUCKETS = 32
REL_MAX_DIST = 128

N_GROUPS = 8
EXPERTS_PER_GROUP = 8
N_EXPERTS = 64
TOP_K = 2

LANES = 128
SUBLANES = 8
VMEM_LIMIT = 56 * 1024 * 1024

COL_GV = 0
COL_GOG = 768
COL_DQ = 1536
COL_DK = 2304
COL_DV = 3072
COL_GQ = 3840
COL_GK = 4224
COL_LY = 4608
COL_LX = 5120
COL_GA = 5632
PROJ_WIDTH = 5760

ATT_TILE = 512
GLA_TILE = 256
LRU_TILE = 256
MOE_BLK = 256
GATHER_ROWS = 512


def _params(sem, vmem=VMEM_LIMIT):
    return pltpu.CompilerParams(dimension_semantics=sem, vmem_limit_bytes=vmem)


def _sigmoid(x):
    return 1.0 / (1.0 + jnp.exp(-x))


def _softplus(x):
    return jnp.maximum(x, 0.0) + jnp.log1p(jnp.exp(-jnp.abs(x)))


def _dot(a, b):
    return jnp.dot(a, b, preferred_element_type=F32)


def _dot_nt(a, b):
    return lax.dot_general(a, b, (((1,), (1,)), ((), ())), preferred_element_type=F32)


def _dot_tn(a, b):
    return lax.dot_general(a, b, (((0,), (0,)), ((), ())), preferred_element_type=F32)


def _pack_bf16_pair(lo, hi):
    lo_bits = lax.bitcast_convert_type(lo.astype(BF16).astype(F32), U32)
    hi_bits = lax.bitcast_convert_type(hi.astype(BF16).astype(F32), U32)
    return (hi_bits & jnp.uint32(0xFFFF0000)) | (lo_bits >> 16)


def _unpack_bf16_pair(w):
    lo = lax.bitcast_convert_type(w << 16, F32)
    hi = lax.bitcast_convert_type(w & jnp.uint32(0xFFFF0000), F32)
    return lo, hi


def _store_row_tiles(ref, words):
    rows = words.shape[0]
    for s in range(SUBLANES):
        ref[pl.ds(s, rows, stride=SUBLANES), :] = words[:, s * LANES:(s + 1) * LANES]


def _load_row_tiles(ref):
    rows = ref.shape[0] // SUBLANES
    return jnp.concatenate([ref[pl.ds(s, rows, stride=SUBLANES), :] for s in range(SUBLANES)], axis=1)


def _ada_kernel(c_ref, w_ref, b_ref, o_ref):
    c = c_ref[...]
    s = c * _sigmoid(c)
    o_ref[0] = _dot(s.astype(BF16), w_ref[0].astype(BF16)) + b_ref[0]


def _ada_mod(c, ada_w, ada_b):
    L, D, N = ada_w.shape
    B = c.shape[0]
    tn = 1024
    return pl.pallas_call(
        _ada_kernel,
        grid=(L, N // tn),
        in_specs=[
            pl.BlockSpec((B, D), lambda l, j: (0, 0)),
            pl.BlockSpec((1, D, tn), lambda l, j: (l, 0, j)),
            pl.BlockSpec((1, 1, tn), lambda l, j: (l, 0, j)),
        ],
        out_specs=pl.BlockSpec((1, B, tn), lambda l, j: (l, 0, j)),
        out_shape=jax.ShapeDtypeStruct((L, B, N), F32),
        compiler_params=_params(("parallel", "parallel")),
    )(c, ada_w, ada_b.reshape(L, 1, N))


def _modulated_norm(x, g, sc, sh):
    ms = jnp.mean(x * x, axis=-1, keepdims=True)
    return (x * lax.rsqrt(ms + EPS) * g) * (1.0 + sc) + sh


def _inproj_kernel(h_ref, sh_ref, sc_ref, g_ref, w_ref, o_ref, u_scr):
    @pl.when(pl.program_id(1) == 0)
    def _():
        u = _modulated_norm(h_ref[...], g_ref[...], sc_ref[0], sh_ref[0])
        u_scr[...] = u.astype(BF16)

    o_ref[...] = _dot(u_scr[...], w_ref[...]).astype(o_ref.dtype)


def _inproj(h, mod3, norm_g, w_perm, S):
    T, D = h.shape
    N = w_perm.shape[1]
    tm, tn = 512, 1152
    per_b = S // tm
    return pl.pallas_call(
        _inproj_kernel,
        grid=(T // tm, N // tn),
        in_specs=[
            pl.BlockSpec((tm, D), lambda i, j: (i, 0)),
            pl.BlockSpec((1, 1, D), lambda i, j: (i // per_b, 0, 0)),
            pl.BlockSpec((1, 1, D), lambda i, j: (i // per_b, 0, 1)),
            pl.BlockSpec((1, D), lambda i, j: (0, 0)),
            pl.BlockSpec((D, tn), lambda i, j: (0, j)),
        ],
        out_specs=pl.BlockSpec((tm, tn), lambda i, j: (i, j)),
        out_shape=jax.ShapeDtypeStruct((T, N), BF16),
        scratch_shapes=[pltpu.VMEM((tm, D), BF16)],
        compiler_params=_params(("parallel", "arbitrary")),
    )(h, mod3, mod3, norm_g.reshape(1, D), w_perm)


def _permute_w_in(w):
    D = w.shape[0]
    gq, gk, gv, gog, ga, ly, lx, dq, dk, dv = jnp.split(
        w, [384, 768, 1536, 2304, 2320, 2832, 3344, 4112, 4880], axis=1)
    ga = jnp.concatenate([ga, jnp.zeros((D, LANES - GLA_LOWRANK), w.dtype)], axis=1)
    return jnp.concatenate([gv, gog, dq, dk, dv, gq, gk, ly, lx, ga], axis=1).astype(BF16)


def _gla_kernel(q_ref, k_ref, v_ref, og_ref, alr_ref, wa2_ref, ba_ref, ng_ref, o_ref, st_ref):
    tb = q_ref.shape[0]
    n_chunks = tb // CHUNK

    @pl.when(pl.program_id(1) == 0)
    def _():
        st_ref[...] = jnp.zeros_like(st_ref)

    row = lax.broadcasted_iota(jnp.int32, (tb, tb), 0)
    col = lax.broadcasted_iota(jnp.int32, (tb, tb), 1)
    same_chunk = (row // CHUNK) == (col // CHUNK)
    causal = col <= row
    tril = jnp.where(same_chunk & causal, 1.0, 0.0).astype(BF16)
    chunk_ones = jnp.where(same_chunk, 1.0, 0.0).astype(BF16)
    lane = lax.broadcasted_iota(jnp.int32, (1, LANES), 1)
    half_masks = (lane < GLA_DK, lane >= GLA_DK)

    alr = alr_ref[...]
    for p in range(GLA_HEADS // 2):
        cs = slice(p * LANES, (p + 1) * LANES)
        z = _dot(alr, wa2_ref[:, cs]) + ba_ref[:, cs]
        la = (jnp.minimum(z, 0.0) - jnp.log1p(jnp.exp(-jnp.abs(z)))) * (1.0 / GLA_TAU)
        la_hi = la.astype(BF16)
        la_lo = (la - la_hi.astype(F32)).astype(BF16)
        G = _dot(tril, la_hi) + _dot(tril, la_lo)
        Gl = _dot(chunk_ones, la_hi) + _dot(chunk_ones, la_lo)
        eG = jnp.exp(G)
        enG = jnp.exp(-G)
        q = q_ref[:, cs].astype(F32) * (GLA_DK ** -0.5)
        k = k_ref[:, cs].astype(F32)
        qf = q * eG
        qb = q * enG
        kf = (k * eG).astype(BF16)
        kb = (k * enG).astype(BF16)
        kd = k * jnp.exp(Gl - G)
        for hh in range(2):
            head = 2 * p + hh
            m = half_masks[hh]
            vs = slice(head * GLA_DV, (head + 1) * GLA_DV)
            qf_h = jnp.where(m, qf, 0.0).astype(BF16)
            qb_h = jnp.where(m, qb, 0.0).astype(BF16)
            kd_h = jnp.where(m, kd, 0.0).astype(BF16)
            v_h = v_ref[:, vs]
            a_f = _dot_nt(qf_h, kb)
            a_b = _dot_nt(qb_h, kf)
            attn = jnp.where(same_chunk, jnp.where(causal, a_f, a_b), 0.0)
            o_intra = _dot(attn.astype(BF16), v_h)
            st = st_ref[head]
            inter = []
            for c in range(n_chunks):
                rs = slice(c * CHUNK, (c + 1) * CHUNK)
                inter.append(_dot_nt(qf_h[rs], st.astype(BF16)))
                decay = jnp.exp(Gl[c * CHUNK:c * CHUNK + 1, :])
                st = st * decay + _dot_tn(v_h[rs], kd_h[rs])
            st_ref[head] = st
            o = o_intra + jnp.concatenate(inter, axis=0)
            o = o * lax.rsqrt(jnp.mean(o * o, axis=-1, keepdims=True) + EPS)
            og = og_ref[:, vs].astype(F32)
            o_ref[:, vs] = (o * ng_ref[:, vs] * (og * _sigmoid(og))).astype(o_ref.dtype)


def _gla(proj, wa2_pad, b_a, norm_g, B, S):
    T = proj.shape[0]
    tb = GLA_TILE
    nt = S // tb
    row = lambda b, i: b * nt + i
    return pl.pallas_call(
        _gla_kernel,
        grid=(B, nt),
        in_specs=[
            pl.BlockSpec((tb, GLA_KEY_WIDTH), lambda b, i: (row(b, i), COL_GQ // GLA_KEY_WIDTH)),
            pl.BlockSpec((tb, GLA_KEY_WIDTH), lambda b, i: (row(b, i), COL_GK // GLA_KEY_WIDTH)),
            pl.BlockSpec((tb, GLA_WIDTH), lambda b, i: (row(b, i), COL_GV // GLA_WIDTH)),
            pl.BlockSpec((tb, GLA_WIDTH), lambda b, i: (row(b, i), COL_GOG // GLA_WIDTH)),
            pl.BlockSpec((tb, LANES), lambda b, i: (row(b, i), COL_GA // LANES)),
            pl.BlockSpec((LANES, GLA_KEY_WIDTH), lambda b, i: (0, 0)),
            pl.BlockSpec((1, GLA_KEY_WIDTH), lambda b, i: (0, 0)),
            pl.BlockSpec((1, GLA_WIDTH), lambda b, i: (0, 0)),
        ],
        out_specs=pl.BlockSpec((tb, GLA_WIDTH), lambda b, i: (row(b, i), 0)),
        out_shape=jax.ShapeDtypeStruct((T, GLA_WIDTH), BF16),
        scratch_shapes=[pltpu.VMEM((GLA_HEADS, GLA_DV, LANES), F32)],
        compiler_params=_params(("parallel", "arbitrary")),
    )(proj, proj, proj, proj, proj, wa2_pad, b_a.reshape(1, -1), norm_g.reshape(1, -1))


def _lru_kernel(y_ref, x_ref, cw_ref, cb_ref, wg_ref, bg_ref, lam_ref, o_ref, *scratch):
    B, ts, W = x_ref.shape
    n_planes = W // LANES
    a_scr = scratch[0:n_planes]
    b_scr = scratch[n_planes:2 * n_planes]
    h_scr = scratch[2 * n_planes:3 * n_planes]
    xc_scr, tail_scr, carry_scr = scratch[3 * n_planes:]

    @pl.when(pl.program_id(0) == 0)
    def _():
        tail_scr[...] = jnp.zeros_like(tail_scr)
        carry_scr[...] = jnp.zeros_like(carry_scr)

    cw = cw_ref[...]
    cb = cb_ref[...]
    sp = _softplus(-lam_ref[...])
    row8 = lax.broadcasted_iota(jnp.int32, (8, W), 0)
    for b in range(B):
        x = x_ref[b].astype(F32)
        tail = tail_scr[b]
        xc = cb + cw[CONV_WIDTH - 1:CONV_WIDTH, :] * x
        head = cb + cw[CONV_WIDTH - 1:CONV_WIDTH, :] * x[0:8]
        for d in range(1, CONV_WIDTH):
            wd = cw[CONV_WIDTH - 1 - d:CONV_WIDTH - d, :]
            xr = pltpu.roll(x, d, 0)
            xc = xc + wd * xr
            head = head + wd * jnp.where(row8 < d, pltpu.roll(tail, d, 0), xr[0:8])
        tail_scr[b] = x[ts - 8:ts]
        xc_scr[...] = xc
        xc_scr[0:8] = head
        xc = xc_scr[...]
        gates = _sigmoid(_dot(xc.astype(BF16), wg_ref[...]) + bg_ref[...])
        r = gates[:, :W]
        ig = gates[:, W:]
        log_a = (-LRU_C) * r * sp
        a = jnp.exp(log_a)
        b_in = jnp.sqrt(-jnp.tanh(log_a) * (a * a + 1.0)) * (ig * xc)
        rows = slice(b * ts, (b + 1) * ts)
        for k in range(n_planes):
            a_scr[k][rows] = a[:, k * LANES:(k + 1) * LANES]
            b_scr[k][rows] = b_in[:, k * LANES:(k + 1) * LANES]

    def step(t, hs):
        idx = pl.ds(t, B, stride=ts)
        out = []
        for k in range(n_planes):
            hk = a_scr[k][idx, :] * hs[k] + b_scr[k][idx, :]
            h_scr[k][idx, :] = hk
            out.append(hk)
        return tuple(out)

    hs = lax.fori_loop(0, ts, step, tuple(carry_scr[k] for k in range(n_planes)), unroll=8)
    for k in range(n_planes):
        carry_scr[k] = hs[k]

    for b in range(B):
        rows = slice(b * ts, (b + 1) * ts)
        y = y_ref[b].astype(F32)
        gelu = 0.5 * y * (1.0 + jnp.tanh(math.sqrt(2.0 / math.pi) * (y + 0.044715 * (y * y * y))))
        h = jnp.concatenate([h_scr[k][rows] for k in range(n_planes)], axis=1)
        o_ref[b] = (h * gelu).astype(o_ref.dtype)


def _lru(proj3, conv_w, conv_b, w_gates, b_gates, lam):
    B, S, _ = proj3.shape
    W = LRU_WIDTH
    ts = LRU_TILE
    n_planes = W // LANES
    full = lambda shape: pl.BlockSpec(shape, lambda i: (0,) * len(shape))
    return pl.pallas_call(
        _lru_kernel,
        grid=(S // ts,),
        in_specs=[
            pl.BlockSpec((B, ts, W), lambda i: (0, i, COL_LY // W)),
            pl.BlockSpec((B, ts, W), lambda i: (0, i, COL_LX // W)),
            full((CONV_WIDTH, W)),
            full((1, W)),
            full((W, 2 * W)),
            full((1, 2 * W)),
            full((1, W)),
        ],
        out_specs=pl.BlockSpec((B, ts, W), lambda i: (0, i, 0)),
        out_shape=jax.ShapeDtypeStruct((B, S, W), BF16),
        scratch_shapes=(
            [pltpu.VMEM((B * ts, LANES), F32) for _ in range(3 * n_planes)]
            + [pltpu.VMEM((ts, W), F32), pltpu.VMEM((B, 8, W), F32), pltpu.VMEM((n_planes, B, LANES), F32)]),
        compiler_params=_params(("arbitrary",)),
    )(proj3, proj3, conv_w, conv_b.reshape(1, W), w_gates, b_gates, lam.reshape(1, W))


def _block_diag(w):
    n, d, _ = w.shape
    eye = jnp.eye(n, dtype=w.dtype)
    return (eye[:, None, :, None] * w[:, :, None, :]).reshape(n * d, n * d)


def _t5_bucket(rel):
    nb = REL_BUCKETS // 2
    ret = (rel > 0).astype(jnp.int32) * nb
    n = jnp.abs(rel)
    max_exact = nb // 2
    nf = jnp.maximum(n, 1).astype(jnp.float32)
    large = max_exact + (jnp.log(nf / max_exact) / math.log(REL_MAX_DIST / max_exact)
                         * (nb - max_exact)).astype(jnp.int32)
    large = jnp.minimum(large, nb - 1)
    return ret + jnp.where(n < max_exact, n, large)


def _bias_kernel(bucket_ref, table_ref, o_ref):
    h = pl.program_id(0)
    bucket = bucket_ref[0]
    acc = jnp.full(bucket.shape, -1e30, F32)
    for b in range(REL_BUCKETS):
        acc = jnp.where(bucket == b, table_ref[b, h], acc)
    o_ref[0, 0] = acc


def _bias_tiles(rel_bias):
    t = ATT_TILE
    H = rel_bias.shape[1]
    qp = jnp.arange(t, dtype=jnp.int32)[:, None]
    kp = jnp.arange(t, dtype=jnp.int32)[None, :]
    mask = (kp // CHUNK) <= (qp // CHUNK)
    buckets = jnp.stack([jnp.where(mask, _t5_bucket(kp - qp), REL_BUCKETS), _t5_bucket(kp - t - qp)], axis=0)
    table = rel_bias.astype(F32)
    tiles = pl.pallas_call(
        _bias_kernel,
        grid=(H, 2),
        in_specs=[
            pl.BlockSpec((1, t, t), lambda h, k: (k, 0, 0)),
            pl.BlockSpec(memory_space=pltpu.SMEM),
        ],
        out_specs=pl.BlockSpec((1, 1, t, t), lambda h, k: (h, k, 0, 0)),
        out_shape=jax.ShapeDtypeStruct((H, 2, t, t), F32),
        compiler_params=_params(("parallel", "parallel")),
    )(buckets, table)
    far_bucket = _t5_bucket(jnp.full((1,), -t - 1, jnp.int32))
    far = jnp.sum(jnp.where(jnp.arange(REL_BUCKETS)[:, None] == far_bucket, table, 0.0), axis=0)
    return tiles, jnp.broadcast_to(far[:, None, None], (H, 1, t))


def _diff_kernel(lam_init, q_ref, k_ref, v_ref, bias_ref, far_ref, lqk_ref, g_ref, o_ref,
                 qs_scr, m_scr, l_scr, acc_scr):
    i = pl.program_id(2)
    t = q_ref.shape[0]
    n_groups = t // LANES
    lane = lax.broadcasted_iota(jnp.int32, (1, LANES), 1)
    q = q_ref[...].astype(F32) * (DIFF_DH ** -0.5)
    qs_scr[0:t] = jnp.where(lane < DIFF_DH, q, 0.0).astype(BF16)
    qs_scr[t:2 * t] = jnp.where(lane >= DIFF_DH, q, 0.0).astype(BF16)
    m_scr[...] = jnp.full_like(m_scr, -1e30)
    l_scr[...] = jnp.zeros_like(l_scr)
    acc_scr[...] = jnp.zeros_like(acc_scr)

    def tile(ks, bias):
        s = _dot_nt(qs_scr[...], k_ref[ks, :]) + bias
        groups = [s[:, c * LANES:(c + 1) * LANES] for c in range(n_groups)]
        mx = functools.reduce(jnp.maximum, groups)
        m_prev = m_scr[...]
        m_new = jnp.maximum(m_prev, jnp.max(mx, axis=-1, keepdims=True))
        alpha = jnp.exp(m_prev - m_new)
        ps = [jnp.exp(g - m_new) for g in groups]
        l_scr[...] = alpha * l_scr[...] + functools.reduce(jnp.add, ps)
        p = jnp.concatenate(ps, axis=1).astype(BF16)
        acc_scr[...] = alpha * acc_scr[...] + _dot(p, v_ref[ks, :])
        m_scr[...] = m_new

    def far_body(j, carry):
        tile(pl.ds(pl.multiple_of(j * t, t), t), far_ref[0])
        return carry

    lax.fori_loop(0, jnp.maximum(i - 1, 0), far_body, 0)

    @pl.when(i >= 1)
    def _():
        b = bias_ref[0, 1]
        tile(pl.ds(pl.multiple_of((i - 1) * t, t), t), jnp.concatenate([b, b], axis=0))

    b = bias_ref[0, 0]
    tile(pl.ds(pl.multiple_of(i * t, t), t), jnp.concatenate([b, b], axis=0))

    lqk = lqk_ref[...]
    lam = (jnp.exp(jnp.sum(lqk[0:1] * lqk[1:2], axis=-1, keepdims=True))
           - jnp.exp(jnp.sum(lqk[2:3] * lqk[3:4], axis=-1, keepdims=True)) + lam_init)
    o = acc_scr[...] / jnp.sum(l_scr[...], axis=-1, keepdims=True)
    o = o[:t] - lam * o[t:]
    o = o * lax.rsqrt(jnp.mean(o * o, axis=-1, keepdims=True) + EPS)
    o_ref[...] = (o * g_ref[...] * (1.0 - lam_init)).astype(o_ref.dtype)


def _diff_attention(proj, bias, lqk, subln_g, layer_idx, B, S):
    T = proj.shape[0]
    t = ATT_TILE
    nq = S // t
    tiles, far = bias
    lam_init = 0.8 - 0.6 * math.exp(-0.3 * layer_idx)
    return pl.pallas_call(
        functools.partial(_diff_kernel, lam_init),
        grid=(B, DIFF_HEADS, nq),
        in_specs=[
            pl.BlockSpec((t, LANES), lambda b, h, i: (b * nq + i, COL_DQ // LANES + h)),
            pl.BlockSpec((S, LANES), lambda b, h, i: (b, COL_DK // LANES + h)),
            pl.BlockSpec((S, LANES), lambda b, h, i: (b, COL_DV // LANES + h)),
            pl.BlockSpec((1, 2, t, t), lambda b, h, i: (h, 0, 0, 0)),
            pl.BlockSpec((1, 1, t), lambda b, h, i: (h, 0, 0)),
            pl.BlockSpec((4, DIFF_DH), lambda b, h, i: (0, 0)),
            pl.BlockSpec((1, DIFF_DV), lambda b, h, i: (0, 0)),
        ],
        out_specs=pl.BlockSpec((t, LANES), lambda b, h, i: (b * nq + i, h)),
        out_shape=jax.ShapeDtypeStruct((T, DIFF_WIDTH), BF16),
        scratch_shapes=[
            pltpu.VMEM((2 * t, LANES), BF16),
            pltpu.VMEM((2 * t, LANES), F32),
            pltpu.VMEM((2 * t, LANES), F32),
            pltpu.VMEM((2 * t, DIFF_DV), F32),
        ],
        compiler_params=_params(("parallel", "parallel", "arbitrary")),
    )(proj, proj, proj, tiles, far, lqk, subln_g.reshape(1, DIFF_DV))


def _outproj_kernel(h_ref, og_ref, ol_ref, od_ref, w_ref, g1_ref, sh2_ref, sc2_ref, n2_ref, rw_ref, rb_ref,
                    hn_ref, u2_ref, eid_ref, ew_ref):
    D = h_ref.shape[1]
    acc = _dot(og_ref[...], w_ref[0:GLA_WIDTH, :])
    acc += _dot(ol_ref[...], w_ref[GLA_WIDTH:GLA_WIDTH + LRU_WIDTH, :])
    acc += _dot(od_ref[...], w_ref[GLA_WIDTH + LRU_WIDTH:, :])
    hn = h_ref[...] + g1_ref[0] * acc
    hn_ref[...] = hn
    u2 = _modulated_norm(hn, n2_ref[...], sc2_ref[0], sh2_ref[0])
    _store_row_tiles(u2_ref, _pack_bf16_pair(u2[:, :D // 2], u2[:, D // 2:]))

    logits = _dot(u2.astype(BF16), rw_ref[...]) + rb_ref[...]
    lane = lax.broadcasted_iota(jnp.int32, logits.shape, 1)
    lane_f = lane.astype(F32)
    neg = jnp.float32(-jnp.inf)
    gmask = lane < N_GROUPS
    gl = jnp.where(gmask, logits, neg)
    gmax = jnp.max(gl, axis=-1, keepdims=True)
    gidx = jnp.min(jnp.where(gl == gmax, lane_f, float(LANES)), axis=-1, keepdims=True)
    g_w = 1.0 / jnp.sum(jnp.where(gmask, jnp.exp(gl - gmax), 0.0), axis=-1, keepdims=True)
    egroup = ((lane - N_GROUPS) >> 3).astype(F32)
    emask = (lane >= N_GROUPS) & (lane < N_GROUPS + N_EXPERTS) & (egroup == gidx)
    el = jnp.where(emask, logits, neg)
    v1 = jnp.max(el, axis=-1, keepdims=True)
    i1 = jnp.min(jnp.where(el == v1, lane_f, float(LANES)), axis=-1, keepdims=True)
    el2 = jnp.where(lane_f == i1, neg, el)
    v2 = jnp.max(el2, axis=-1, keepdims=True)
    i2 = jnp.min(jnp.where(el2 == v2, lane_f, float(LANES)), axis=-1, keepdims=True)
    e21 = jnp.exp(v2 - v1)
    w1 = g_w / (1.0 + e21)
    w2 = g_w * e21 / (1.0 + e21)
    eid = jnp.where(lane == 0, i1, jnp.where(lane == 1, i2, float(N_GROUPS))) - float(N_GROUPS)
    eid_ref[...] = eid.astype(jnp.int32)
    ew_ref[...] = jnp.where(lane == 0, w1, jnp.where(lane == 1, w2, 0.0))


def _outproj(h, o_gla, o_lru, o_diff, w_out, mod3, norm2_g, rw, rb, S):
    T, D = h.shape
    tm = 256
    per_b = S // tm
    rowblk = lambda width: pl.BlockSpec((tm, width), lambda i: (i, 0))
    modblk = lambda k: pl.BlockSpec((1, 1, D), lambda i: (i // per_b, 0, k))
    full = lambda shape: pl.BlockSpec(shape, lambda i: (0,) * len(shape))
    return pl.pallas_call(
        _outproj_kernel,
        grid=(T // tm,),
        in_specs=[
            rowblk(D), rowblk(GLA_WIDTH), rowblk(LRU_WIDTH), rowblk(DIFF_WIDTH),
            full((D, D)),
            modblk(2), modblk(3), modblk(4),
            full((1, D)),
            full((D, LANES)),
            full((1, LANES)),
        ],
        out_specs=[rowblk(D), pl.BlockSpec((tm * SUBLANES, LANES), lambda i: (i, 0)), rowblk(LANES), rowblk(LANES)],
        out_shape=[
            jax.ShapeDtypeStruct((T, D), F32),
            jax.ShapeDtypeStruct((T * SUBLANES, LANES), U32),
            jax.ShapeDtypeStruct((T, LANES), jnp.int32),
            jax.ShapeDtypeStruct((T, LANES), F32),
        ],
        compiler_params=_params(("parallel",)),
    )(h, o_gla, o_lru, o_diff, w_out, mod3, mod3, mod3, norm2_g.reshape(1, D), rw, rb)


def _dispatch(eid):
    T = eid.shape[0]
    A = T * TOP_K
    blk = MOE_BLK
    P = A + N_EXPERTS * blk
    n_blocks = P // blk
    expert = eid.reshape(A)
    onehot = (expert[:, None] == jnp.arange(N_EXPERTS, dtype=jnp.int32)[None, :]).astype(jnp.int32)
    csum = jnp.cumsum(onehot, axis=0)
    counts = csum[-1]
    padded = (counts + blk - 1) // blk * blk
    pends = jnp.cumsum(padded)
    pstarts = pends - padded
    dest = jnp.sum(onehot * (csum - 1 + pstarts[None, :]), axis=1).astype(jnp.int32)
    tok = jnp.arange(A, dtype=jnp.int32) // TOP_K
    slot_tok = jnp.zeros((P,), jnp.int32).at[dest].set(tok)
    n_used = (pends[-1] // blk).astype(jnp.int32)
    block_start = jnp.arange(n_blocks, dtype=jnp.int32) * blk
    block_expert = jnp.minimum(jnp.searchsorted(pends, block_start, side='right'),
                               N_EXPERTS - 1).astype(jnp.int32)
    last_used = block_expert[jnp.maximum(n_used - 1, 0)]
    block_expert = jnp.where(jnp.arange(n_blocks) < n_used, block_expert, last_used)
    return slot_tok, dest.reshape(T, TOP_K), block_expert, n_used.reshape(1)


def _gather_kernel(tok_ref, src_ref, o_ref, sem):
    rows = o_ref.shape[0] // SUBLANES
    base = pl.program_id(0) * rows

    def issue(r, carry):
        t = tok_ref[base + r]
        pltpu.make_async_copy(src_ref.at[pl.ds(pl.multiple_of(t * SUBLANES, SUBLANES), SUBLANES)],
                              o_ref.at[pl.ds(pl.multiple_of(r * SUBLANES, SUBLANES), SUBLANES)], sem).start()
        return carry

    lax.fori_loop(0, rows, issue, 0, unroll=16)
    pltpu.make_async_copy(src_ref.at[pl.ds(0, rows * SUBLANES)], o_ref, sem).wait()


def _gather_rows(slot_tok, src):
    P = slot_tok.shape[0]
    rows = GATHER_ROWS
    return pl.pallas_call(
        _gather_kernel,
        grid_spec=pltpu.PrefetchScalarGridSpec(
            num_scalar_prefetch=1,
            grid=(P // rows,),
            in_specs=[pl.BlockSpec(memory_space=pl.ANY)],
            out_specs=pl.BlockSpec((rows * SUBLANES, LANES), lambda i, tok: (i, 0)),
            scratch_shapes=[pltpu.SemaphoreType.DMA(())],
        ),
        out_shape=jax.ShapeDtypeStruct((P * SUBLANES, LANES), src.dtype),
        compiler_params=_params(("arbitrary",)),
    )(slot_tok, src)


def _expert_kernel(be_ref, nu_ref, xs_ref, w1_ref, w3_ref, w2_ref, y_ref, w1b, w3b, w2b):
    i = pl.program_id(0)
    D = w1_ref.shape[0]
    changed = (i == 0) | (be_ref[i] != be_ref[jnp.maximum(i - 1, 0)])

    @pl.when(changed)
    def _():
        w1b[...] = w1_ref[...].astype(BF16)
        w3b[...] = w3_ref[...].astype(BF16)
        w2b[...] = w2_ref[...].astype(BF16)

    @pl.when(i < nu_ref[0])
    def _():
        lo, hi = _unpack_bf16_pair(_load_row_tiles(xs_ref))
        lo = lo.astype(BF16)
        hi = hi.astype(BF16)
        a = _dot(lo, w1b[0:D // 2, :]) + _dot(hi, w1b[D // 2:, :])
        g = _dot(lo, w3b[0:D // 2, :]) + _dot(hi, w3b[D // 2:, :])
        hid = ((a * _sigmoid(a)) * g).astype(BF16)
        _store_row_tiles(y_ref, _pack_bf16_pair(_dot(hid, w2b[:, 0:D // 2]), _dot(hid, w2b[:, D // 2:])))

    @pl.when(i >= nu_ref[0])
    def _():
        y_ref[...] = jnp.zeros_like(y_ref)


def _experts(block_expert, n_used, xs, w1, w3, w2, layer):
    _, _, D, DE = w1.shape
    blk = MOE_BLK
    n_blocks = xs.shape[0] // (blk * SUBLANES)
    rowmap = lambda i, be, nu: (jnp.minimum(i, jnp.maximum(nu[0] - 1, 0)), 0)
    return pl.pallas_call(
        _expert_kernel,
        grid_spec=pltpu.PrefetchScalarGridSpec(
            num_scalar_prefetch=2,
            grid=(n_blocks,),
            in_specs=[
                pl.BlockSpec((blk * SUBLANES, LANES), rowmap),
                pl.BlockSpec((None, None, D, DE), lambda i, be, nu: (layer, be[i], 0, 0)),
                pl.BlockSpec((None, None, D, DE), lambda i, be, nu: (layer, be[i], 0, 0)),
                pl.BlockSpec((None, None, DE, D), lambda i, be, nu: (layer, be[i], 0, 0)),
            ],
            out_specs=pl.BlockSpec((blk * SUBLANES, LANES), lambda i, be, nu: (i, 0)),
            scratch_shapes=[
                pltpu.VMEM((D, DE), BF16),
                pltpu.VMEM((D, DE), BF16),
                pltpu.VMEM((DE, D), BF16),
            ],
        ),
        out_shape=jax.ShapeDtypeStruct(xs.shape, U32),
        compiler_params=_params(("arbitrary",)),
    )(block_expert, n_used, xs, w1, w3, w2)


def _combine_kernel(final, d0_ref, d1_ref, y_ref, h_ref, g2_ref, ew_ref, fg_ref, o_ref, buf0, buf1, sem):
    tc, D = h_ref.shape
    base = pl.program_id(0) * tc

    def tile_copy(slot, buf, r):
        return pltpu.make_async_copy(y_ref.at[pl.ds(pl.multiple_of(slot * SUBLANES, SUBLANES), SUBLANES)],
                                     buf.at[pl.ds(pl.multiple_of(r * SUBLANES, SUBLANES), SUBLANES)], sem)

    def issue(r, carry):
        tile_copy(d0_ref[base + r], buf0, r).start()
        tile_copy(d1_ref[base + r], buf1, r).start()
        return carry

    lax.fori_loop(0, tc, issue, 0, unroll=8)
    pltpu.make_async_copy(y_ref.at[pl.ds(0, tc * SUBLANES)], buf0, sem).wait()
    pltpu.make_async_copy(y_ref.at[pl.ds(0, tc * SUBLANES)], buf1, sem).wait()

    ew = ew_ref[...]
    w0 = ew[:, 0:1]
    w1 = ew[:, 1:2]
    lo0, hi0 = _unpack_bf16_pair(_load_row_tiles(buf0))
    lo1, hi1 = _unpack_bf16_pair(_load_row_tiles(buf1))
    moe = jnp.concatenate([w0 * lo0 + w1 * lo1, w0 * hi0 + w1 * hi1], axis=1)
    hn = h_ref[...] + g2_ref[0] * moe
    if final:
        hn = hn * lax.rsqrt(jnp.mean(hn * hn, axis=-1, keepdims=True) + EPS) * fg_ref[...]
    o_ref[...] = hn


def _combine(dest, y, h, mod3, ew, final_g, S, final):
    T, D = h.shape
    tc = 256
    per_b = S // tc
    return pl.pallas_call(
        functools.partial(_combine_kernel, final),
        grid_spec=pltpu.PrefetchScalarGridSpec(
            num_scalar_prefetch=2,
            grid=(T // tc,),
            in_specs=[
                pl.BlockSpec(memory_space=pl.ANY),
                pl.BlockSpec((tc, D), lambda i, d0, d1: (i, 0)),
                pl.BlockSpec((1, 1, D), lambda i, d0, d1: (i // per_b, 0, 5)),
                pl.BlockSpec((tc, LANES), lambda i, d0, d1: (i, 0)),
                pl.BlockSpec((1, D), lambda i, d0, d1: (0, 0)),
            ],
            out_specs=pl.BlockSpec((tc, D), lambda i, d0, d1: (i, 0)),
            scratch_shapes=[
                pltpu.VMEM((tc * SUBLANES, LANES), U32),
                pltpu.VMEM((tc * SUBLANES, LANES), U32),
                pltpu.SemaphoreType.DMA(()),
            ],
        ),
        out_shape=jax.ShapeDtypeStruct((T, D), F32),
        compiler_params=_params(("arbitrary",)),
    )(dest[:, 0], dest[:, 1], y, h, mod3, ew, final_g.reshape(1, D))


def kernel(x, c, ada_w, ada_b, norm1_g, w_in, gla_w_a2, gla_b_a, gla_norm_g, lru_conv_w, lru_conv_b,
           lru_wa, lru_ba, lru_wx, lru_bx, lru_lambda, diff_lq1, diff_lk1, diff_lq2, diff_lk2,
           diff_subln_g, rel_bias, w_out, norm2_g, router_g_w, router_g_b, router_e_w, router_e_b,
           moe_w1, moe_w3, moe_w2, final_g):
    B, S, D = x.shape
    T = B * S
    L = ada_w.shape[0]
    h = x.reshape(T, D)
    mod = _ada_mod(c, ada_w, ada_b)
    bias = _bias_tiles(rel_bias)
    for l in range(L):
        mod3 = mod[l][:, None, :]
        proj = _inproj(h, mod3, norm1_g[l], _permute_w_in(w_in[l]), S)
        wa2_pad = jnp.concatenate(
            [gla_w_a2[l], jnp.zeros((LANES - GLA_LOWRANK, GLA_KEY_WIDTH), F32)], axis=0).astype(BF16)
        o_gla = _gla(proj, wa2_pad, gla_b_a[l], gla_norm_g[l], B, S)
        w_gates = jnp.concatenate([_block_diag(lru_wa[l]), _block_diag(lru_wx[l])], axis=1).astype(BF16)
        b_gates = jnp.concatenate([lru_ba[l], lru_bx[l]]).reshape(1, 2 * LRU_WIDTH)
        o_lru = _lru(proj.reshape(B, S, PROJ_WIDTH), lru_conv_w[l], lru_conv_b[l], w_gates, b_gates,
                     lru_lambda[l]).reshape(T, LRU_WIDTH)
        lqk = jnp.stack([diff_lq1[l], diff_lk1[l], diff_lq2[l], diff_lk2[l]], axis=0)
        o_diff = _diff_attention(proj, bias, lqk, diff_subln_g[l], l, B, S)
        rw = jnp.concatenate(
            [router_g_w[l], router_e_w[l], jnp.zeros((D, LANES - N_GROUPS - N_EXPERTS), F32)], axis=1).astype(BF16)
        rb = jnp.concatenate(
            [router_g_b[l], router_e_b[l], jnp.zeros((LANES - N_GROUPS - N_EXPERTS,), F32)]).reshape(1, LANES)
        h, u2, eid, ew = _outproj(h, o_gla, o_lru, o_diff, w_out[l].astype(BF16), mod3, norm2_g[l], rw, rb, S)
        slot_tok, dest, block_expert, n_used = _dispatch(eid[:, :TOP_K])
        xs = _gather_rows(slot_tok, u2)
        y = _experts(block_expert, n_used, xs, moe_w1, moe_w3, moe_w2, l)
        h = _combine(dest, y, h, mod3, ew, final_g, S, final=(l == L - 1))
    return h.reshape(B, S, D)
```

```python
import functools
import math

import jax
import jax.numpy as jnp
from jax import lax
from jax.experimental import pallas as pl
from jax.experimental.pallas import tpu as pltpu

F32 = jnp.float32
BF16 = jnp.bfloat16
U32 = jnp.uint32

EPS = 1e-6
CHUNK = 64

GLA_DV = 128
GLA_DK = 64
GLA_HEADS = 6
GLA_WIDTH = GLA_HEADS * GLA_DV
GLA_KEY_WIDTH = GLA_HEADS * GLA_DK
GLA_LOWRANK = 16
GLA_TAU = 16.0

LRU_WIDTH = 512
LRU_BLOCKS = 8
LRU_BLOCK_DIM = LRU_WIDTH // LRU_BLOCKS
CONV_WIDTH = 4
LRU_C = 8.0

DIFF_DH = 64
DIFF_DV = 128
DIFF_HEADS = 6
DIFF_WIDTH = DIFF_HEADS * DIFF_DV

REL_BUCKETS = 32
REL_MAX_DIST = 128

N_GROUPS = 8
EXPERTS_PER_GROUP = 8
N_EXPERTS = 64
TOP_K = 2

LANES = 128
SUBLANES = 8
VMEM_LIMIT = 56 * 1024 * 1024

COL_GV = 0
COL_GOG = 768
COL_DQ = 1536
COL_DK = 2304
COL_DV = 3072
COL_GQ = 3840
COL_GK = 4224
COL_LY = 4608
COL_LX = 5120
COL_GA = 5632
PROJ_WIDTH = 5760

ATT_TILE = 512
GLA_TILE = 256
LRU_TILE = 256
MOE_BLK = 256
GATHER_ROWS = 512


def _params(sem, vmem=VMEM_LIMIT):
    return pltpu.CompilerParams(dimension_semantics=sem, vmem_limit_bytes=vmem)


def _sigmoid(x):
    return 1.0 / (1.0 + jnp.exp(-x))


def _softplus(x):
    return jnp.maximum(x, 0.0) + jnp.log1p(jnp.exp(-jnp.abs(x)))


def _dot(a, b):
    return jnp.dot(a, b, preferred_element_type=F32)


def _dot_nt(a, b):
    return lax.dot_general(a, b, (((1,), (1,)), ((), ())), preferred_element_type=F32)


def _dot_tn(a, b):
    return lax.dot_general(a, b, (((0,), (0,)), ((), ())), preferred_element_type=F32)


def _pack_bf16_pair(lo, hi):
    lo_bits = lax.bitcast_convert_type(lo.astype(BF16).astype(F32), U32)
    hi_bits = lax.bitcast_convert_type(hi.astype(BF16).astype(F32), U32)
    return (hi_bits & jnp.uint32(0xFFFF0000)) | (lo_bits >> 16)


def _unpack_bf16_pair(w):
    lo = lax.bitcast_convert_type(w << 16, F32)
    hi = lax.bitcast_convert_type(w & jnp.uint32(0xFFFF0000), F32)
    return lo, hi


def _store_row_tiles(ref, words):
    rows = words.shape[0]
    for s in range(SUBLANES):
        ref[pl.ds(s, rows, stride=SUBLANES), :] = words[:, s * LANES:(s + 1) * LANES]


def _load_row_tiles(ref):
    rows = ref.shape[0] // SUBLANES
    return jnp.concatenate([ref[pl.ds(s, rows, stride=SUBLANES), :] for s in range(SUBLANES)], axis=1)


def _ada_kernel(c_ref, w_ref, b_ref, o_ref):
    c = c_ref[...]
    s = c * _sigmoid(c)
    o_ref[0] = _dot(s.astype(BF16), w_ref[0].astype(BF16)) + b_ref[0]


def _ada_mod(c, ada_w, ada_b):
    L, D, N = ada_w.shape
    B = c.shape[0]
    tn = 1024
    return pl.pallas_call(
        _ada_kernel,
        grid=(L, N // tn),
        in_specs=[
            pl.BlockSpec((B, D), lambda l, j: (0, 0)),
            pl.BlockSpec((1, D, tn), lambda l, j: (l, 0, j)),
            pl.BlockSpec((1, 1, tn), lambda l, j: (l, 0, j)),
        ],
        out_specs=pl.BlockSpec((1, B, tn), lambda l, j: (l, 0, j)),
        out_shape=jax.ShapeDtypeStruct((L, B, N), F32),
        compiler_params=_params(("parallel", "parallel")),
    )(c, ada_w, ada_b.reshape(L, 1, N))


def _modulated_norm(x, g, sc, sh):
    ms = jnp.mean(x * x, axis=-1, keepdims=True)
    return (x * lax.rsqrt(ms + EPS) * g) * (1.0 + sc) + sh


def _inproj_kernel(h_ref, sh_ref, sc_ref, g_ref, w_ref, o_ref, u_scr):
    @pl.when(pl.program_id(1) == 0)
    def _():
        u = _modulated_norm(h_ref[...], g_ref[...], sc_ref[0], sh_ref[0])
        u_scr[...] = u.astype(BF16)

    o_ref[...] = _dot(u_scr[...], w_ref[...]).astype(o_ref.dtype)


def _inproj(h, mod3, norm_g, w_perm, S):
    T, D = h.shape
    N = w_perm.shape[1]
    tm, tn = 512, 1152
    per_b = S // tm
    return pl.pallas_call(
        _inproj_kernel,
        grid=(T // tm, N // tn),
        in_specs=[
            pl.BlockSpec((tm, D), lambda i, j: (i, 0)),
            pl.BlockSpec((1, 1, D), lambda i, j: (i // per_b, 0, 0)),
            pl.BlockSpec((1, 1, D), lambda i, j: (i // per_b, 0, 1)),
            pl.BlockSpec((1, D), lambda i, j: (0, 0)),
            pl.BlockSpec((D, tn), lambda i, j: (0, j)),
        ],
        out_specs=pl.BlockSpec((tm, tn), lambda i, j: (i, j)),
        out_shape=jax.ShapeDtypeStruct((T, N), BF16),
        scratch_shapes=[pltpu.VMEM((tm, D), BF16)],
        compiler_params=_params(("parallel", "arbitrary")),
    )(h, mod3, mod3, norm_g.reshape(1, D), w_perm)


def _permute_w_in(w):
    D = w.shape[0]
    gq, gk, gv, gog, ga, ly, lx, dq, dk, dv = jnp.split(
        w, [384, 768, 1536, 2304, 2320, 2832, 3344, 4112, 4880], axis=1)
    ga = jnp.concatenate([ga, jnp.zeros((D, LANES - GLA_LOWRANK), w.dtype)], axis=1)
    return jnp.concatenate([gv, gog, dq, dk, dv, gq, gk, ly, lx, ga], axis=1).astype(BF16)


def _gla_kernel(q_ref, k_ref, v_ref, og_ref, alr_ref, wa2_ref, ba_ref, ng_ref, o_ref, st_ref):
    tb = q_ref.shape[0]
    n_chunks = tb // CHUNK

    @pl.when(pl.program_id(1) == 0)
    def _():
        st_ref[...] = jnp.zeros_like(st_ref)

    row = lax.broadcasted_iota(jnp.int32, (tb, tb), 0)
    col = lax.broadcasted_iota(jnp.int32, (tb, tb), 1)
    same_chunk = (row // CHUNK) == (col // CHUNK)
    causal = col <= row
    tril = jnp.where(same_chunk & causal, 1.0, 0.0).astype(BF16)
    chunk_ones = jnp.where(same_chunk, 1.0, 0.0).astype(BF16)
    lane = lax.broadcasted_iota(jnp.int32, (1, LANES), 1)
    half_masks = (lane < GLA_DK, lane >= GLA_DK)

    alr = alr_ref[...]
    for p in range(GLA_HEADS // 2):
        cs = slice(p * LANES, (p + 1) * LANES)
        z = _dot(alr, wa2_ref[:, cs]) + ba_ref[:, cs]
        la = (jnp.minimum(z, 0.0) - jnp.log1p(jnp.exp(-jnp.abs(z)))) * (1.0 / GLA_TAU)
        la_hi = la.astype(BF16)
        la_lo = (la - la_hi.astype(F32)).astype(BF16)
        G = _dot(tril, la_hi) + _dot(tril, la_lo)
        Gl = _dot(chunk_ones, la_hi) + _dot(chunk_ones, la_lo)
        eG = jnp.exp(G)
        enG = jnp.exp(-G)
        q = q_ref[:, cs].astype(F32) * (GLA_DK ** -0.5)
        k = k_ref[:, cs].astype(F32)
        qf = q * eG
        qb = q * enG
        kf = (k * eG).astype(BF16)
        kb = (k * enG).astype(BF16)
        kd = k * jnp.exp(Gl - G)
        for hh in range(2):
            head = 2 * p + hh
            m = half_masks[hh]
            vs = slice(head * GLA_DV, (head + 1) * GLA_DV)
            qf_h = jnp.where(m, qf, 0.0).astype(BF16)
            qb_h = jnp.where(m, qb, 0.0).astype(BF16)
            kd_h = jnp.where(m, kd, 0.0).astype(BF16)
            v_h = v_ref[:, vs]
            a_f = _dot_nt(qf_h, kb)
            a_b = _dot_nt(qb_h, kf)
            attn = jnp.where(same_chunk, jnp.where(causal, a_f, a_b), 0.0)
            o_intra = _dot(attn.astype(BF16), v_h)
            st = st_ref[head]
            inter = []
            for c in range(n_chunks):
                rs = slice(c * CHUNK, (c + 1) * CHUNK)
                inter.append(_dot_nt(qf_h[rs], st.astype(BF16)))
                decay = jnp.exp(Gl[c * CHUNK:c * CHUNK + 1, :])
                st = st * decay + _dot_tn(v_h[rs], kd_h[rs])
            st_ref[head] = st
            o = o_intra + jnp.concatenate(inter, axis=0)
            o = o * lax.rsqrt(jnp.mean(o * o, axis=-1, keepdims=True) + EPS)
            og = og_ref[:, vs].astype(F32)
            o_ref[:, vs] = (o * ng_ref[:, vs] * (og * _sigmoid(og))).astype(o_ref.dtype)


def _gla(proj, wa2_pad, b_a, norm_g, B, S):
    T = proj.shape[0]
    tb = GLA_TILE
    nt = S // tb
    row = lambda b, i: b * nt + i
    return pl.pallas_call(
        _gla_kernel,
        grid=(B, nt),
        in_specs=[
            pl.BlockSpec((tb, GLA_KEY_WIDTH), lambda b, i: (row(b, i), COL_GQ // GLA_KEY_WIDTH)),
            pl.BlockSpec((tb, GLA_KEY_WIDTH), lambda b, i: (row(b, i), COL_GK // GLA_KEY_WIDTH)),
            pl.BlockSpec((tb, GLA_WIDTH), lambda b, i: (row(b, i), COL_GV // GLA_WIDTH)),
            pl.BlockSpec((tb, GLA_WIDTH), lambda b, i: (row(b, i), COL_GOG // GLA_WIDTH)),
            pl.BlockSpec((tb, LANES), lambda b, i: (row(b, i), COL_GA // LANES)),
            pl.BlockSpec((LANES, GLA_KEY_WIDTH), lambda b, i: (0, 0)),
            pl.BlockSpec((1, GLA_KEY_WIDTH), lambda b, i: (0, 0)),
            pl.BlockSpec((1, GLA_WIDTH), lambda b, i: (0, 0)),
        ],
        out_specs=pl.BlockSpec((tb, GLA_WIDTH), lambda b, i: (row(b, i), 0)),
        out_shape=jax.ShapeDtypeStruct((T, GLA_WIDTH), BF16),
        scratch_shapes=[pltpu.VMEM((GLA_HEADS, GLA_DV, LANES), F32)],
        compiler_params=_params(("parallel", "arbitrary")),
    )(proj, proj, proj, proj, proj, wa2_pad, b_a.reshape(1, -1), norm_g.reshape(1, -1))


def _lru_kernel(y_ref, x_ref, cw_ref, cb_ref, wg_ref, bg_ref, lam_ref, o_ref, *scratch):
    B, ts, W = x_ref.shape
    n_planes = W // LANES
    a_scr = scratch[0:n_planes]
    b_scr = scratch[n_planes:2 * n_planes]
    h_scr = scratch[2 * n_planes:3 * n_planes]
    xc_scr, tail_scr, carry_scr = scratch[3 * n_planes:]

    @pl.when(pl.program_id(0) == 0)
    def _():
        tail_scr[...] = jnp.zeros_like(tail_scr)
        carry_scr[...] = jnp.zeros_like(carry_scr)

    cw = cw_ref[...]
    cb = cb_ref[...]
    sp = _softplus(-lam_ref[...])
    row8 = lax.broadcasted_iota(jnp.int32, (8, W), 0)
    for b in range(B):
        x = x_ref[b].astype(F32)
        tail = tail_scr[b]
        xc = cb + cw[CONV_WIDTH - 1:CONV_WIDTH, :] * x
        head = cb + cw[CONV_WIDTH - 1:CONV_WIDTH, :] * x[0:8]
        for d in range(1, CONV_WIDTH):
            wd = cw[CONV_WIDTH - 1 - d:CONV_WIDTH - d, :]
            xr = pltpu.roll(x, d, 0)
            xc = xc + wd * xr
            head = head + wd * jnp.where(row8 < d, pltpu.roll(tail, d, 0), xr[0:8])
        tail_scr[b] = x[ts - 8:ts]
        xc_scr[...] = xc
        xc_scr[0:8] = head
        xc = xc_scr[...]
        gates = _sigmoid(_dot(xc.astype(BF16), wg_ref[...]) + bg_ref[...])
        r = gates[:, :W]
        ig = gates[:, W:]
        log_a = (-LRU_C) * r * sp
        a = jnp.exp(log_a)
        b_in = jnp.sqrt(-jnp.tanh(log_a) * (a * a + 1.0)) * (ig * xc)
        rows = slice(b * ts, (b + 1) * ts)
        for k in range(n_planes):
            a_scr[k][rows] = a[:, k * LANES:(k + 1) * LANES]
            b_scr[k][rows] = b_in[:, k * LANES:(k + 1) * LANES]

    def step(t, hs):
        idx = pl.ds(t, B, stride=ts)
        out = []
        for k in range(n_planes):
            hk = a_scr[k][idx, :] * hs[k] + b_scr[k][idx, :]
            h_scr[k][idx, :] = hk
            out.append(hk)
        return tuple(out)

    hs = lax.fori_loop(0, ts, step, tuple(carry_scr[k] for k in range(n_planes)), unroll=8)
    for k in range(n_planes):
        carry_scr[k] = hs[k]

    for b in range(B):
        rows = slice(b * ts, (b + 1) * ts)
        y = y_ref[b].astype(F32)
        gelu = 0.5 * y * (1.0 + jnp.tanh(math.sqrt(2.0 / math.pi) * (y + 0.044715 * (y * y * y))))
        h = jnp.concatenate([h_scr[k][rows] for k in range(n_planes)], axis=1)
        o_ref[b] = (h * gelu).astype(o_ref.dtype)


def _lru(proj3, conv_w, conv_b, w_gates, b_gates, lam):
    B, S, _ = proj3.shape
    W = LRU_WIDTH
    ts = LRU_TILE
    n_planes = W // LANES
    full = lambda shape: pl.BlockSpec(shape, lambda i: (0,) * len(shape))
    return pl.pallas_call(
        _lru_kernel,
        grid=(S // ts,),
        in_specs=[
            pl.BlockSpec((B, ts, W), lambda i: (0, i, COL_LY // W)),
            pl.BlockSpec((B, ts, W), lambda i: (0, i, COL_LX // W)),
            full((CONV_WIDTH, W)),
            full((1, W)),
            full((W, 2 * W)),
            full((1, 2 * W)),
            full((1, W)),
        ],
        out_specs=pl.BlockSpec((B, ts, W), lambda i: (0, i, 0)),
        out_shape=jax.ShapeDtypeStruct((B, S, W), BF16),
        scratch_shapes=(
            [pltpu.VMEM((B * ts, LANES), F32) for _ in range(3 * n_planes)]
            + [pltpu.VMEM((ts, W), F32), pltpu.VMEM((B, 8, W), F32), pltpu.VMEM((n_planes, B, LANES), F32)]),
        compiler_params=_params(("arbitrary",)),
    )(proj3, proj3, conv_w, conv_b.reshape(1, W), w_gates, b_gates, lam.reshape(1, W))


def _block_diag(w):
    n, d, _ = w.shape
    eye = jnp.eye(n, dtype=w.dtype)
    return (eye[:, None, :, None] * w[:, :, None, :]).reshape(n * d, n * d)


def _t5_bucket(rel):
    nb = REL_BUCKETS // 2
    ret = (rel > 0).astype(jnp.int32) * nb
    n = jnp.abs(rel)
    max_exact = nb // 2
    nf = jnp.maximum(n, 1).astype(jnp.float32)
    large = max_exact + (jnp.log(nf / max_exact) / math.log(REL_MAX_DIST / max_exact)
                         * (nb - max_exact)).astype(jnp.int32)
    large = jnp.minimum(large, nb - 1)
    return ret + jnp.where(n < max_exact, n, large)


def _bias_kernel(bucket_ref, table_ref, o_ref):
    h = pl.program_id(0)
    bucket = bucket_ref[0]
    acc = jnp.full(bucket.shape, -1e30, F32)
    for b in range(REL_BUCKETS):
        acc = jnp.where(bucket == b, table_ref[b, h], acc)
    o_ref[0, 0] = acc


def _bias_tiles(rel_bias):
    t = ATT_TILE
    H = rel_bias.shape[1]
    qp = jnp.arange(t, dtype=jnp.int32)[:, None]
    kp = jnp.arange(t, dtype=jnp.int32)[None, :]
    mask = (kp // CHUNK) <= (qp // CHUNK)
    buckets = jnp.stack([jnp.where(mask, _t5_bucket(kp - qp), REL_BUCKETS), _t5_bucket(kp - t - qp)], axis=0)
    table = rel_bias.astype(F32)
    tiles = pl.pallas_call(
        _bias_kernel,
        grid=(H, 2),
        in_specs=[
            pl.BlockSpec((1, t, t), lambda h, k: (k, 0, 0)),
            pl.BlockSpec(memory_space=pltpu.SMEM),
        ],
        out_specs=pl.BlockSpec((1, 1, t, t), lambda h, k: (h, k, 0, 0)),
        out_shape=jax.ShapeDtypeStruct((H, 2, t, t), F32),
        compiler_params=_params(("parallel", "parallel")),
    )(buckets, table)
    far_bucket = _t5_bucket(jnp.full((1,), -t - 1, jnp.int32))
    far = jnp.sum(jnp.where(jnp.arange(REL_BUCKETS)[:, None] == far_bucket, table, 0.0), axis=0)
    return tiles, jnp.broadcast_to(far[:, None, None], (H, 1, t))


def _diff_kernel(lam_init, q_ref, k_ref, v_ref, bias_ref, far_ref, lqk_ref, g_ref, o_ref,
                 qs_scr, m_scr, l_scr, acc_scr):
    i = pl.program_id(2)
    t = q_ref.shape[0]
    n_groups = t // LANES
    lane = lax.broadcasted_iota(jnp.int32, (1, LANES), 1)
    q = q_ref[...].astype(F32) * (DIFF_DH ** -0.5)
    qs_scr[0:t] = jnp.where(lane < DIFF_DH, q, 0.0).astype(BF16)
    qs_scr[t:2 * t] = jnp.where(lane >= DIFF_DH, q, 0.0).astype(BF16)
    m_scr[...] = jnp.full_like(m_scr, -1e30)
    l_scr[...] = jnp.zeros_like(l_scr)
    acc_scr[...] = jnp.zeros_like(acc_scr)

    def tile(ks, bias):
        s = _dot_nt(qs_scr[...], k_ref[ks, :]) + bias
        groups = [s[:, c * LANES:(c + 1) * LANES] for c in range(n_groups)]
        mx = functools.reduce(jnp.maximum, groups)
        m_prev = m_scr[...]
        m_new = jnp.maximum(m_prev, jnp.max(mx, axis=-1, keepdims=True))
        alpha = jnp.exp(m_prev - m_new)
        ps = [jnp.exp(g - m_new) for g in groups]
        l_scr[...] = alpha * l_scr[...] + functools.reduce(jnp.add, ps)
        p = jnp.concatenate(ps, axis=1).astype(BF16)
        acc_scr[...] = alpha * acc_scr[...] + _dot(p, v_ref[ks, :])
        m_scr[...] = m_new

    def far_body(j, carry):
        tile(pl.ds(pl.multiple_of(j * t, t), t), far_ref[0])
        return carry

    lax.fori_loop(0, jnp.maximum(i - 1, 0), far_body, 0)

    @pl.when(i >= 1)
    def _():
        b = bias_ref[0, 1]
        tile(pl.ds(pl.multiple_of((i - 1) * t, t), t), jnp.concatenate([b, b], axis=0))

    b = bias_ref[0, 0]
    tile(pl.ds(pl.multiple_of(i * t, t), t), jnp.concatenate([b, b], axis=0))

    lqk = lqk_ref[...]
    lam = (jnp.exp(jnp.sum(lqk[0:1] * lqk[1:2], axis=-1, keepdims=True))
           - jnp.exp(jnp.sum(lqk[2:3] * lqk[3:4], axis=-1, keepdims=True)) + lam_init)
    o = acc_scr[...] / jnp.sum(l_scr[...], axis=-1, keepdims=True)
    o = o[:t] - lam * o[t:]
    o = o * lax.rsqrt(jnp.mean(o * o, axis=-1, keepdims=True) + EPS)
    o_ref[...] = (o * g_ref[...] * (1.0 - lam_init)).astype(o_ref.dtype)


def _diff_attention(proj, bias, lqk, subln_g, layer_idx, B, S):
    T = proj.shape[0]
    t = ATT_TILE
    nq = S // t
    tiles, far = bias
    lam_init = 0.8 - 0.6 * math.exp(-0.3 * layer_idx)
    return pl.pallas_call(
        functools.partial(_diff_kernel, lam_init),
        grid=(B, DIFF_HEADS, nq),
        in_specs=[
            pl.BlockSpec((t, LANES), lambda b, h, i: (b * nq + i, COL_DQ // LANES + h)),
            pl.BlockSpec((S, LANES), lambda b, h, i: (b, COL_DK // LANES + h)),
            pl.BlockSpec((S, LANES), lambda b, h, i: (b, COL_DV // LANES + h)),
            pl.BlockSpec((1, 2, t, t), lambda b, h, i: (h, 0, 0, 0)),
            pl.BlockSpec((1, 1, t), lambda b, h, i: (h, 0, 0)),
            pl.BlockSpec((4, DIFF_DH), lambda b, h, i: (0, 0)),
            pl.BlockSpec((1, DIFF_DV), lambda b, h, i: (0, 0)),
        ],
        out_specs=pl.BlockSpec((t, LANES), lambda b, h, i: (b * nq + i, h)),
        out_shape=jax.ShapeDtypeStruct((T, DIFF_WIDTH), BF16),
        scratch_shapes=[
            pltpu.VMEM((2 * t, LANES), BF16),
            pltpu.VMEM((2 * t, LANES), F32),
            pltpu.VMEM((2 * t, LANES), F32),
            pltpu.VMEM((2 * t, DIFF_DV), F32),
        ],
        compiler_params=_params(("parallel", "parallel", "arbitrary")),
    )(proj, proj, proj, tiles, far, lqk, subln_g.reshape(1, DIFF_DV))


def _outproj_kernel(h_ref, og_ref, ol_ref, od_ref, w_ref, g1_ref, sh2_ref, sc2_ref, n2_ref, rw_ref, rb_ref,
                    hn_ref, u2_ref, eid_ref, ew_ref):
    D = h_ref.shape[1]
    acc = _dot(og_ref[...], w_ref[0:GLA_WIDTH, :])
    acc += _dot(ol_ref[...], w_ref[GLA_WIDTH:GLA_WIDTH + LRU_WIDTH, :])
    acc += _dot(od_ref[...], w_ref[GLA_WIDTH + LRU_WIDTH:, :])
    hn = h_ref[...] + g1_ref[0] * acc
    hn_ref[...] = hn
    u2 = _modulated_norm(hn, n2_ref[...], sc2_ref[0], sh2_ref[0])
    _store_row_tiles(u2_ref, _pack_bf16_pair(u2[:, :D // 2], u2[:, D // 2:]))

    logits = _dot(u2.astype(BF16), rw_ref[...]) + rb_ref[...]
    lane = lax.broadcasted_iota(jnp.int32, logits.shape, 1)
    lane_f = lane.astype(F32)
    neg = jnp.float32(-jnp.inf)
    gmask = lane < N_GROUPS
    gl = jnp.where(gmask, logits, neg)
    gmax = jnp.max(gl, axis=-1, keepdims=True)
    gidx = jnp.min(jnp.where(gl == gmax, lane_f, float(LANES)), axis=-1, keepdims=True)
    g_w = 1.0 / jnp.sum(jnp.where(gmask, jnp.exp(gl - gmax), 0.0), axis=-1, keepdims=True)
    egroup = ((lane - N_GROUPS) >> 3).astype(F32)
    emask = (lane >= N_GROUPS) & (lane < N_GROUPS + N_EXPERTS) & (egroup == gidx)
    el = jnp.where(emask, logits, neg)
    v1 = jnp.max(el, axis=-1, keepdims=True)
    i1 = jnp.min(jnp.where(el == v1, lane_f, float(LANES)), axis=-1, keepdims=True)
    el2 = jnp.where(lane_f == i1, neg, el)
    v2 = jnp.max(el2, axis=-1, keepdims=True)
    i2 = jnp.min(jnp.where(el2 == v2, lane_f, float(LANES)), axis=-1, keepdims=True)
    e21 = jnp.exp(v2 - v1)
    w1 = g_w / (1.0 + e21)
    w2 = g_w * e21 / (1.0 + e21)
    eid = jnp.where(lane == 0, i1, jnp.where(lane == 1, i2, float(N_GROUPS))) - float(N_GROUPS)
    eid_ref[...] = eid.astype(jnp.int32)
    ew_ref[...] = jnp.where(lane == 0, w1, jnp.where(lane == 1, w2, 0.0))


def _outproj(h, o_gla, o_lru, o_diff, w_out, mod3, norm2_g, rw, rb, S):
    T, D = h.shape
    tm = 256
    per_b = S // tm
    rowblk = lambda width: pl.BlockSpec((tm, width), lambda i: (i, 0))
    modblk = lambda k: pl.BlockSpec((1, 1, D), lambda i: (i // per_b, 0, k))
    full = lambda shape: pl.BlockSpec(shape, lambda i: (0,) * len(shape))
    return pl.pallas_call(
        _outproj_kernel,
        grid=(T // tm,),
        in_specs=[
            rowblk(D), rowblk(GLA_WIDTH), rowblk(LRU_WIDTH), rowblk(DIFF_WIDTH),
            full((D, D)),
            modblk(2), modblk(3), modblk(4),
            full((1, D)),
            full((D, LANES)),
            full((1, LANES)),
        ],
        out_specs=[rowblk(D), pl.BlockSpec((tm * SUBLANES, LANES), lambda i: (i, 0)), rowblk(LANES), rowblk(LANES)],
        out_shape=[
            jax.ShapeDtypeStruct((T, D), F32),
            jax.ShapeDtypeStruct((T * SUBLANES, LANES), U32),
            jax.ShapeDtypeStruct((T, LANES), jnp.int32),
            jax.ShapeDtypeStruct((T, LANES), F32),
        ],
        compiler_params=_params(("parallel",)),
    )(h, o_gla, o_lru, o_diff, w_out, mod3, mod3, mod3, norm2_g.reshape(1, D), rw, rb)


def _dispatch(eid):
    T = eid.shape[0]
    A = T * TOP_K
    blk = MOE_BLK
    P = A + N_EXPERTS * blk
    n_blocks = P // blk
    expert = eid.reshape(A)
    onehot = (expert[:, None] == jnp.arange(N_EXPERTS, dtype=jnp.int32)[None, :]).astype(jnp.int32)
    csum = jnp.cumsum(onehot, axis=0)
    counts = csum[-1]
    padded = (counts + blk - 1) // blk * blk
    pends = jnp.cumsum(padded)
    pstarts = pends - padded
    dest = jnp.sum(onehot * (csum - 1 + pstarts[None, :]), axis=1).astype(jnp.int32)
    n_used = (pends[-1] // blk).astype(jnp.int32)
    block_idx = jnp.arange(n_blocks, dtype=jnp.int32)
    block_expert = jnp.minimum(jnp.searchsorted(pends, block_idx * blk, side='right'),
                               N_EXPERTS - 1).astype(jnp.int32)
    last_used = block_expert[jnp.maximum(n_used - 1, 0)]
    block_expert = jnp.where(block_idx < n_used, block_expert, last_used)
    following = jnp.concatenate([block_expert[1:], jnp.full((1,), -1, jnp.int32)])
    zero_block = ((block_idx >= n_used - 1) | (following != block_expert)).astype(jnp.int32)
    return dest.reshape(T, TOP_K), zero_block, block_expert, n_used.reshape(1)


def _scatter_kernel(d0_ref, d1_ref, zb_ref, src_ref, o_ref, zero_buf, sem, zero_sem):
    rows = src_ref.shape[0] // SUBLANES
    base = pl.program_id(0) * rows
    fill_rows = zero_buf.shape[0]
    n_blocks = o_ref.shape[0] // fill_rows

    @pl.when(pl.program_id(0) == 0)
    def _():
        zero_buf[...] = jnp.zeros_like(zero_buf)

        def for_each_fill(fn):
            def body(j, carry):
                @pl.when(zb_ref[j] == 1)
                def _():
                    fn(pltpu.make_async_copy(
                        zero_buf, o_ref.at[pl.ds(pl.multiple_of(j * fill_rows, fill_rows), fill_rows)], zero_sem))
                return carry
            lax.fori_loop(0, n_blocks, body, 0)

        for_each_fill(lambda copy: copy.start())
        for_each_fill(lambda copy: copy.wait())

    def row_copy(r, slot):
        return pltpu.make_async_copy(src_ref.at[pl.ds(pl.multiple_of(r * SUBLANES, SUBLANES), SUBLANES)],
                                     o_ref.at[pl.ds(pl.multiple_of(slot * SUBLANES, SUBLANES), SUBLANES)], sem)

    def issue(r, carry):
        row_copy(r, d0_ref[base + r]).start()
        row_copy(r, d1_ref[base + r]).start()
        return carry

    lax.fori_loop(0, rows, issue, 0, unroll=8)
    for _ in range(TOP_K):
        pltpu.make_async_copy(src_ref, o_ref.at[pl.ds(0, rows * SUBLANES)], sem).wait()


def _scatter_rows(dest, zero_block, src):
    T = dest.shape[0]
    P = T * TOP_K + N_EXPERTS * MOE_BLK
    rows = GATHER_ROWS
    return pl.pallas_call(
        _scatter_kernel,
        grid_spec=pltpu.PrefetchScalarGridSpec(
            num_scalar_prefetch=3,
            grid=(T // rows,),
            in_specs=[pl.BlockSpec((rows * SUBLANES, LANES), lambda i, d0, d1, zb: (i, 0))],
            out_specs=pl.BlockSpec(memory_space=pl.ANY),
            scratch_shapes=[
                pltpu.VMEM((MOE_BLK * SUBLANES, LANES), src.dtype),
                pltpu.SemaphoreType.DMA(()),
                pltpu.SemaphoreType.DMA(()),
            ],
        ),
        out_shape=jax.ShapeDtypeStruct((P * SUBLANES, LANES), src.dtype),
        compiler_params=_params(("arbitrary",)),
    )(dest[:, 0], dest[:, 1], zero_block, src)


def _expert_kernel(layer, be_ref, first_ref, next_ref, slot_ref, nu_ref, xs_ref, w1_hbm, w3_hbm, w2_hbm, y_ref,
                   w1f, w3f, w2f, w1b, w3b, w2b, sems):
    i = pl.program_id(0)
    D = w1b.shape[0]

    def weight_copies(e, slot):
        return (pltpu.make_async_copy(w1_hbm.at[layer, e], w1f.at[slot], sems.at[slot, 0]),
                pltpu.make_async_copy(w3_hbm.at[layer, e], w3f.at[slot], sems.at[slot, 1]),
                pltpu.make_async_copy(w2_hbm.at[layer, e], w2f.at[slot], sems.at[slot, 2]))

    @pl.when(i == 0)
    def _():
        for c in weight_copies(be_ref[0], 0):
            c.start()

    @pl.when(first_ref[i] == 1)
    def _():
        slot = slot_ref[i]
        for c in weight_copies(be_ref[i], slot):
            c.wait()

        @pl.when(next_ref[i] >= 0)
        def _():
            for c in weight_copies(next_ref[i], 1 - slot):
                c.start()

        w1b[...] = w1f[slot].astype(BF16)
        w3b[...] = w3f[slot].astype(BF16)
        w2b[...] = w2f[slot].astype(BF16)

    @pl.when(i < nu_ref[0])
    def _():
        lo, hi = _unpack_bf16_pair(_load_row_tiles(xs_ref))
        lo = lo.astype(BF16)
        hi = hi.astype(BF16)
        a = _dot(lo, w1b[0:D // 2, :]) + _dot(hi, w1b[D // 2:, :])
        g = _dot(lo, w3b[0:D // 2, :]) + _dot(hi, w3b[D // 2:, :])
        hid = ((a * _sigmoid(a)) * g).astype(BF16)
        _store_row_tiles(y_ref, _pack_bf16_pair(_dot(hid, w2b[:, 0:D // 2]), _dot(hid, w2b[:, D // 2:])))

    @pl.when(i >= nu_ref[0])
    def _():
        y_ref[...] = jnp.zeros_like(y_ref)


def _segment_plan(block_expert, n_used):
    n = block_expert.shape[0]
    idx = jnp.arange(n, dtype=jnp.int32)
    prev = jnp.concatenate([jnp.full((1,), -1, jnp.int32), block_expert[:-1]])
    first = ((block_expert != prev) & (idx < n_used[0])).astype(jnp.int32)
    slot = (jnp.cumsum(first) - 1) % 2
    later_first = jnp.where(first == 1, idx, n)
    next_idx = lax.cummin(jnp.concatenate([later_first[1:], jnp.full((1,), n, jnp.int32)]), reverse=True)
    next_expert = jnp.where(next_idx < n, block_expert[jnp.minimum(next_idx, n - 1)], -1)
    return first, next_expert.astype(jnp.int32), slot.astype(jnp.int32)


def _experts(block_expert, n_used, xs, w1, w3, w2, layer):
    _, _, D, DE = w1.shape
    blk = MOE_BLK
    n_blocks = block_expert.shape[0]
    first, next_expert, slot = _segment_plan(block_expert, n_used)
    rowmap = lambda i, *refs: (jnp.minimum(i, jnp.maximum(refs[4][0] - 1, 0)), 0)
    hbm = pl.BlockSpec(memory_space=pl.ANY)
    return pl.pallas_call(
        functools.partial(_expert_kernel, layer),
        grid_spec=pltpu.PrefetchScalarGridSpec(
            num_scalar_prefetch=5,
            grid=(n_blocks,),
            in_specs=[pl.BlockSpec((blk * SUBLANES, LANES), rowmap), hbm, hbm, hbm],
            out_specs=pl.BlockSpec((blk * SUBLANES, LANES), lambda i, *refs: (i, 0)),
            scratch_shapes=[
                pltpu.VMEM((2, D, DE), F32),
                pltpu.VMEM((2, D, DE), F32),
                pltpu.VMEM((2, DE, D), F32),
                pltpu.VMEM((D, DE), BF16),
                pltpu.VMEM((D, DE), BF16),
                pltpu.VMEM((DE, D), BF16),
                pltpu.SemaphoreType.DMA((2, 3)),
            ],
        ),
        out_shape=jax.ShapeDtypeStruct((n_blocks * blk * SUBLANES, LANES), U32),
        compiler_params=_params(("arbitrary",)),
    )(block_expert, first, next_expert, slot, n_used, xs, w1, w3, w2)


def _combine_kernel(final, d0_ref, d1_ref, y_ref, h_ref, g2_ref, ew_ref, fg_ref, o_ref, buf0, buf1, sem):
    tc, D = h_ref.shape
    base = pl.program_id(0) * tc

    def tile_copy(slot, buf, r):
        return pltpu.make_async_copy(y_ref.at[pl.ds(pl.multiple_of(slot * SUBLANES, SUBLANES), SUBLANES)],
                                     buf.at[pl.ds(pl.multiple_of(r * SUBLANES, SUBLANES), SUBLANES)], sem)

    def issue(r, carry):
        tile_copy(d0_ref[base + r], buf0, r).start()
        tile_copy(d1_ref[base + r], buf1, r).start()
        return carry

    lax.fori_loop(0, tc, issue, 0, unroll=8)
    pltpu.make_async_copy(y_ref.at[pl.ds(0, tc * SUBLANES)], buf0, sem).wait()
    pltpu.make_async_copy(y_ref.at[pl.ds(0, tc * SUBLANES)], buf1, sem).wait()

    ew = ew_ref[...]
    w0 = ew[:, 0:1]
    w1 = ew[:, 1:2]
    lo0, hi0 = _unpack_bf16_pair(_load_row_tiles(buf0))
    lo1, hi1 = _unpack_bf16_pair(_load_row_tiles(buf1))
    moe = jnp.concatenate([w0 * lo0 + w1 * lo1, w0 * hi0 + w1 * hi1], axis=1)
    hn = h_ref[...] + g2_ref[0] * moe
    if final:
        hn = hn * lax.rsqrt(jnp.mean(hn * hn, axis=-1, keepdims=True) + EPS) * fg_ref[...]
    o_ref[...] = hn


def _combine(dest, y, h, mod3, ew, final_g, S, final):
    T, D = h.shape
    tc = 256
    per_b = S // tc
    return pl.pallas_call(
        functools.partial(_combine_kernel, final),
        grid_spec=pltpu.PrefetchScalarGridSpec(
            num_scalar_prefetch=2,
            grid=(T // tc,),
            in_specs=[
                pl.BlockSpec(memory_space=pl.ANY),
                pl.BlockSpec((tc, D), lambda i, d0, d1: (i, 0)),
                pl.BlockSpec((1, 1, D), lambda i, d0, d1: (i // per_b, 0, 5)),
                pl.BlockSpec((tc, LANES), lambda i, d0, d1: (i, 0)),
                pl.BlockSpec((1, D), lambda i, d0, d1: (0, 0)),
            ],
            out_specs=pl.BlockSpec((tc, D), lambda i, d0, d1: (i, 0)),
            scratch_shapes=[
                pltpu.VMEM((tc * SUBLANES, LANES), U32),
                pltpu.VMEM((tc * SUBLANES, LANES), U32),
                pltpu.SemaphoreType.DMA(()),
            ],
        ),
        out_shape=jax.ShapeDtypeStruct((T, D), F32),
        compiler_params=_params(("arbitrary",)),
    )(dest[:, 0], dest[:, 1], y, h, mod3, ew, final_g.reshape(1, D))


def kernel(x, c, ada_w, ada_b, norm1_g, w_in, gla_w_a2, gla_b_a, gla_norm_g, lru_conv_w, lru_conv_b,
           lru_wa, lru_ba, lru_wx, lru_bx, lru_lambda, diff_lq1, diff_lk1, diff_lq2, diff_lk2,
           diff_subln_g, rel_bias, w_out, norm2_g, router_g_w, router_g_b, router_e_w, router_e_b,
           moe_w1, moe_w3, moe_w2, final_g):
    B, S, D = x.shape
    T = B * S
    L = ada_w.shape[0]
    h = x.reshape(T, D)
    mod = _ada_mod(c, ada_w, ada_b)
    bias = _bias_tiles(rel_bias)
    for l in range(L):
        mod3 = mod[l][:, None, :]
        proj = _inproj(h, mod3, norm1_g[l], _permute_w_in(w_in[l]), S)
        wa2_pad = jnp.concatenate(
            [gla_w_a2[l], jnp.zeros((LANES - GLA_LOWRANK, GLA_KEY_WIDTH), F32)], axis=0).astype(BF16)
        o_gla = _gla(proj, wa2_pad, gla_b_a[l], gla_norm_g[l], B, S)
        w_gates = jnp.concatenate([_block_diag(lru_wa[l]), _block_diag(lru_wx[l])], axis=1).astype(BF16)
        b_gates = jnp.concatenate([lru_ba[l], lru_bx[l]]).reshape(1, 2 * LRU_WIDTH)
        o_lru = _lru(proj.reshape(B, S, PROJ_WIDTH), lru_conv_w[l], lru_conv_b[l], w_gates, b_gates,
                     lru_lambda[l]).reshape(T, LRU_WIDTH)
        lqk = jnp.stack([diff_lq1[l], diff_lk1[l], diff_lq2[l], diff_lk2[l]], axis=0)
        o_diff = _diff_attention(proj, bias, lqk, diff_subln_g[l], l, B, S)
        rw = jnp.concatenate(
            [router_g_w[l], router_e_w[l], jnp.zeros((D, LANES - N_GROUPS - N_EXPERTS), F32)], axis=1).astype(BF16)
        rb = jnp.concatenate(
            [router_g_b[l], router_e_b[l], jnp.zeros((LANES - N_GROUPS - N_EXPERTS,), F32)]).reshape(1, LANES)
        h, u2, eid, ew = _outproj(h, o_gla, o_lru, o_diff, w_out[l].astype(BF16), mod3, norm2_g[l], rw, rb, S)
        dest, zero_block, block_expert, n_used = _dispatch(eid[:, :TOP_K])
        xs = _scatter_rows(dest, zero_block, u2)
        y = _experts(block_expert, n_used, xs, moe_w1, moe_w3, moe_w2, l)
        h = _combine(dest, y, h, mod3, ew, final_g, S, final=(l == L - 1))
    return h.reshape(B, S, D)
```

```python
import functools
import math

import jax
import jax.numpy as jnp
from jax import lax
from jax.experimental import pallas as pl
from jax.experimental.pallas import tpu as pltpu

F32 = jnp.float32
BF16 = jnp.bfloat16
U32 = jnp.uint32

EPS = 1e-6
LOG2E = math.log2(math.e)
CHUNK = 64

GLA_DV = 128
GLA_DK = 64
GLA_HEADS = 6
GLA_WIDTH = GLA_HEADS * GLA_DV
GLA_KEY_WIDTH = GLA_HEADS * GLA_DK
GLA_LOWRANK = 16
GLA_TAU = 16.0

LRU_WIDTH = 512
LRU_BLOCKS = 8
LRU_BLOCK_DIM = LRU_WIDTH // LRU_BLOCKS
CONV_WIDTH = 4
LRU_C = 8.0

DIFF_DH = 64
DIFF_DV = 128
DIFF_HEADS = 6
DIFF_WIDTH = DIFF_HEADS * DIFF_DV

REL_BUCKETS = 32
REL_MAX_DIST = 128

N_GROUPS = 8
EXPERTS_PER_GROUP = 8
N_EXPERTS = 64
TOP_K = 2

LANES = 128
SUBLANES = 8
VMEM_LIMIT = 56 * 1024 * 1024

COL_GV = 0
COL_GOG = 768
COL_DQ = 1536
COL_DK = 2304
COL_DV = 3072
COL_GQ = 3840
COL_GK = 4224
COL_LY = 4608
COL_LX = 5120
COL_GA = 5632
PROJ_WIDTH = 5760

ATT_TILE = 512
GLA_TILE = 256
LRU_TILE = 256
MOE_BLK = 256
GATHER_ROWS = 512


def _params(sem, vmem=VMEM_LIMIT):
    return pltpu.CompilerParams(dimension_semantics=sem, vmem_limit_bytes=vmem)


def _sigmoid(x):
    return 0.5 * jnp.tanh(0.5 * x) + 0.5


def _softplus(x):
    return jnp.maximum(x, 0.0) + jnp.log1p(jnp.exp(-jnp.abs(x)))


def _dot(a, b):
    return jnp.dot(a, b, preferred_element_type=F32)


def _dot_nt(a, b):
    return lax.dot_general(a, b, (((1,), (1,)), ((), ())), preferred_element_type=F32)


def _dot_tn(a, b):
    return lax.dot_general(a, b, (((0,), (0,)), ((), ())), preferred_element_type=F32)


def _pack_bf16_pair(lo, hi):
    lo_bits = lax.bitcast_convert_type(lo.astype(BF16).astype(F32), U32)
    hi_bits = lax.bitcast_convert_type(hi.astype(BF16).astype(F32), U32)
    return (hi_bits & jnp.uint32(0xFFFF0000)) | (lo_bits >> 16)


def _unpack_bf16_pair(w):
    lo = lax.bitcast_convert_type(w << 16, F32)
    hi = lax.bitcast_convert_type(w & jnp.uint32(0xFFFF0000), F32)
    return lo, hi


def _store_row_tiles(ref, words):
    rows = words.shape[0]
    for s in range(SUBLANES):
        ref[pl.ds(s, rows, stride=SUBLANES), :] = words[:, s * LANES:(s + 1) * LANES]


def _load_row_tiles(ref):
    rows = ref.shape[0] // SUBLANES
    return jnp.concatenate([ref[pl.ds(s, rows, stride=SUBLANES), :] for s in range(SUBLANES)], axis=1)


def _ada_kernel(c_ref, w_ref, b_ref, o_ref):
    c = c_ref[...]
    s = c * _sigmoid(c)
    o_ref[0] = _dot(s.astype(BF16), w_ref[0].astype(BF16)) + b_ref[0]


def _ada_mod(c, ada_w, ada_b):
    L, D, N = ada_w.shape
    B = c.shape[0]
    tn = 1024
    return pl.pallas_call(
        _ada_kernel,
        grid=(L, N // tn),
        in_specs=[
            pl.BlockSpec((B, D), lambda l, j: (0, 0)),
            pl.BlockSpec((1, D, tn), lambda l, j: (l, 0, j)),
            pl.BlockSpec((1, 1, tn), lambda l, j: (l, 0, j)),
        ],
        out_specs=pl.BlockSpec((1, B, tn), lambda l, j: (l, 0, j)),
        out_shape=jax.ShapeDtypeStruct((L, B, N), F32),
        compiler_params=_params(("parallel", "parallel")),
    )(c, ada_w, ada_b.reshape(L, 1, N))


def _modulated_norm(x, g, sc, sh):
    ms = jnp.mean(x * x, axis=-1, keepdims=True)
    return (x * lax.rsqrt(ms + EPS) * g) * (1.0 + sc) + sh


def _inproj_kernel(h_ref, sh_ref, sc_ref, g_ref, w_ref, o_ref, u_scr):
    @pl.when(pl.program_id(1) == 0)
    def _():
        u = _modulated_norm(h_ref[...], g_ref[...], sc_ref[0], sh_ref[0])
        u_scr[...] = u.astype(BF16)

    o_ref[...] = _dot(u_scr[...], w_ref[...]).astype(o_ref.dtype)


def _inproj(h, mod3, norm_g, w_perm, S):
    T, D = h.shape
    N = w_perm.shape[1]
    tm, tn = 512, 1152
    per_b = S // tm
    return pl.pallas_call(
        _inproj_kernel,
        grid=(T // tm, N // tn),
        in_specs=[
            pl.BlockSpec((tm, D), lambda i, j: (i, 0)),
            pl.BlockSpec((1, 1, D), lambda i, j: (i // per_b, 0, 0)),
            pl.BlockSpec((1, 1, D), lambda i, j: (i // per_b, 0, 1)),
            pl.BlockSpec((1, D), lambda i, j: (0, 0)),
            pl.BlockSpec((D, tn), lambda i, j: (0, j)),
        ],
        out_specs=pl.BlockSpec((tm, tn), lambda i, j: (i, j)),
        out_shape=jax.ShapeDtypeStruct((T, N), BF16),
        scratch_shapes=[pltpu.VMEM((tm, D), BF16)],
        compiler_params=_params(("parallel", "arbitrary")),
    )(h, mod3, mod3, norm_g.reshape(1, D), w_perm)


def _permute_w_in(w):
    D = w.shape[0]
    gq, gk, gv, gog, ga, ly, lx, dq, dk, dv = jnp.split(
        w, [384, 768, 1536, 2304, 2320, 2832, 3344, 4112, 4880], axis=1)
    ga = jnp.concatenate([ga, jnp.zeros((D, LANES - GLA_LOWRANK), w.dtype)], axis=1)
    return jnp.concatenate([gv, gog, dq, dk, dv, gq, gk, ly, lx, ga], axis=1).astype(BF16)


def _gla_kernel(q_ref, k_ref, v_ref, og_ref, alr_ref, wa2_ref, ba_ref, ng_ref, o_ref, st_ref):
    tb = q_ref.shape[0]
    n_chunks = tb // CHUNK

    @pl.when(pl.program_id(1) == 0)
    def _():
        st_ref[...] = jnp.zeros_like(st_ref)

    row = lax.broadcasted_iota(jnp.int32, (tb, tb), 0)
    col = lax.broadcasted_iota(jnp.int32, (tb, tb), 1)
    same_chunk = (row // CHUNK) == (col // CHUNK)
    causal = col <= row
    tril = jnp.where(same_chunk & causal, 1.0, 0.0).astype(BF16)
    chunk_ones = jnp.where(same_chunk, 1.0, 0.0).astype(BF16)
    lane = lax.broadcasted_iota(jnp.int32, (1, LANES), 1)
    half_masks = (lane < GLA_DK, lane >= GLA_DK)

    alr = alr_ref[...]
    for p in range(GLA_HEADS // 2):
        cs = slice(p * LANES, (p + 1) * LANES)
        z = _dot(alr, wa2_ref[:, cs]) + ba_ref[:, cs]
        la = (jnp.minimum(z, 0.0) - jnp.log1p(jnp.exp(-jnp.abs(z)))) * (1.0 / GLA_TAU)
        la_hi = la.astype(BF16)
        la_lo = (la - la_hi.astype(F32)).astype(BF16)
        G = _dot(tril, la_hi) + _dot(tril, la_lo)
        Gl = _dot(chunk_ones, la_hi) + _dot(chunk_ones, la_lo)
        eG = jnp.exp(G)
        enG = jnp.exp(-G)
        q = q_ref[:, cs].astype(F32) * (GLA_DK ** -0.5)
        k = k_ref[:, cs].astype(F32)
        qf = q * eG
        qb = q * enG
        kf = (k * eG).astype(BF16)
        kb = (k * enG).astype(BF16)
        kd = k * jnp.exp(Gl - G)
        for hh in range(2):
            head = 2 * p + hh
            m = half_masks[hh]
            vs = slice(head * GLA_DV, (head + 1) * GLA_DV)
            qf_h = jnp.where(m, qf, 0.0).astype(BF16)
            qb_h = jnp.where(m, qb, 0.0).astype(BF16)
            kd_h = jnp.where(m, kd, 0.0).astype(BF16)
            v_h = v_ref[:, vs]
            a_f = _dot_nt(qf_h, kb)
            a_b = _dot_nt(qb_h, kf)
            attn = jnp.where(same_chunk, jnp.where(causal, a_f, a_b), 0.0)
            o_intra = _dot(attn.astype(BF16), v_h)
            st = st_ref[head]
            inter = []
            for c in range(n_chunks):
                rs = slice(c * CHUNK, (c + 1) * CHUNK)
                inter.append(_dot_nt(qf_h[rs], st.astype(BF16)))
                decay = jnp.exp(Gl[c * CHUNK:c * CHUNK + 1, :])
                st = st * decay + _dot_tn(v_h[rs], kd_h[rs])
            st_ref[head] = st
            o = o_intra + jnp.concatenate(inter, axis=0)
            o = o * lax.rsqrt(jnp.mean(o * o, axis=-1, keepdims=True) + EPS)
            og = og_ref[:, vs].astype(F32)
            o_ref[:, vs] = (o * ng_ref[:, vs] * (og * _sigmoid(og))).astype(o_ref.dtype)


def _gla(proj, wa2_pad, b_a, norm_g, B, S):
    T = proj.shape[0]
    tb = GLA_TILE
    nt = S // tb
    row = lambda b, i: b * nt + i
    return pl.pallas_call(
        _gla_kernel,
        grid=(B, nt),
        in_specs=[
            pl.BlockSpec((tb, GLA_KEY_WIDTH), lambda b, i: (row(b, i), COL_GQ // GLA_KEY_WIDTH)),
            pl.BlockSpec((tb, GLA_KEY_WIDTH), lambda b, i: (row(b, i), COL_GK // GLA_KEY_WIDTH)),
            pl.BlockSpec((tb, GLA_WIDTH), lambda b, i: (row(b, i), COL_GV // GLA_WIDTH)),
            pl.BlockSpec((tb, GLA_WIDTH), lambda b, i: (row(b, i), COL_GOG // GLA_WIDTH)),
            pl.BlockSpec((tb, LANES), lambda b, i: (row(b, i), COL_GA // LANES)),
            pl.BlockSpec((LANES, GLA_KEY_WIDTH), lambda b, i: (0, 0)),
            pl.BlockSpec((1, GLA_KEY_WIDTH), lambda b, i: (0, 0)),
            pl.BlockSpec((1, GLA_WIDTH), lambda b, i: (0, 0)),
        ],
        out_specs=pl.BlockSpec((tb, GLA_WIDTH), lambda b, i: (row(b, i), 0)),
        out_shape=jax.ShapeDtypeStruct((T, GLA_WIDTH), BF16),
        scratch_shapes=[pltpu.VMEM((GLA_HEADS, GLA_DV, LANES), F32)],
        compiler_params=_params(("parallel", "arbitrary")),
    )(proj, proj, proj, proj, proj, wa2_pad, b_a.reshape(1, -1), norm_g.reshape(1, -1))


def _lru_kernel(y_ref, x_ref, cw_ref, cb_ref, wg_ref, bg_ref, lam_ref, o_ref, *scratch):
    B, ts, W = x_ref.shape
    n_planes = W // LANES
    a_scr = scratch[0:n_planes]
    b_scr = scratch[n_planes:2 * n_planes]
    h_scr = scratch[2 * n_planes:3 * n_planes]
    xc_scr, tail_scr, carry_scr = scratch[3 * n_planes:]

    @pl.when(pl.program_id(0) == 0)
    def _():
        tail_scr[...] = jnp.zeros_like(tail_scr)
        carry_scr[...] = jnp.zeros_like(carry_scr)

    cw = cw_ref[...]
    cb = cb_ref[...]
    sp = _softplus(-lam_ref[...])
    row8 = lax.broadcasted_iota(jnp.int32, (8, W), 0)
    for b in range(B):
        x = x_ref[b].astype(F32)
        tail = tail_scr[b]
        xc = cb + cw[CONV_WIDTH - 1:CONV_WIDTH, :] * x
        head = cb + cw[CONV_WIDTH - 1:CONV_WIDTH, :] * x[0:8]
        for d in range(1, CONV_WIDTH):
            wd = cw[CONV_WIDTH - 1 - d:CONV_WIDTH - d, :]
            xr = pltpu.roll(x, d, 0)
            xc = xc + wd * xr
            head = head + wd * jnp.where(row8 < d, pltpu.roll(tail, d, 0), xr[0:8])
        tail_scr[b] = x[ts - 8:ts]
        xc_scr[...] = xc
        xc_scr[0:8] = head
        xc = xc_scr[...]
        gates = _sigmoid(_dot(xc.astype(BF16), wg_ref[...]) + bg_ref[...])
        r = gates[:, :W]
        ig = gates[:, W:]
        log_a = (-LRU_C) * r * sp
        a = jnp.exp(log_a)
        b_in = jnp.sqrt(-jnp.tanh(log_a) * (a * a + 1.0)) * (ig * xc)
        rows = slice(b * ts, (b + 1) * ts)
        for k in range(n_planes):
            a_scr[k][rows] = a[:, k * LANES:(k + 1) * LANES]
            b_scr[k][rows] = b_in[:, k * LANES:(k + 1) * LANES]

    def step(t, hs):
        idx = pl.ds(t, B, stride=ts)
        out = []
        for k in range(n_planes):
            hk = a_scr[k][idx, :] * hs[k] + b_scr[k][idx, :]
            h_scr[k][idx, :] = hk
            out.append(hk)
        return tuple(out)

    hs = lax.fori_loop(0, ts, step, tuple(carry_scr[k] for k in range(n_planes)), unroll=8)
    for k in range(n_planes):
        carry_scr[k] = hs[k]

    for b in range(B):
        rows = slice(b * ts, (b + 1) * ts)
        y = y_ref[b].astype(F32)
        gelu = 0.5 * y * (1.0 + jnp.tanh(math.sqrt(2.0 / math.pi) * (y + 0.044715 * (y * y * y))))
        h = jnp.concatenate([h_scr[k][rows] for k in range(n_planes)], axis=1)
        o_ref[b] = (h * gelu).astype(o_ref.dtype)


def _lru(proj3, conv_w, conv_b, w_gates, b_gates, lam):
    B, S, _ = proj3.shape
    W = LRU_WIDTH
    ts = LRU_TILE
    n_planes = W // LANES
    full = lambda shape: pl.BlockSpec(shape, lambda i: (0,) * len(shape))
    return pl.pallas_call(
        _lru_kernel,
        grid=(S // ts,),
        in_specs=[
            pl.BlockSpec((B, ts, W), lambda i: (0, i, COL_LY // W)),
            pl.BlockSpec((B, ts, W), lambda i: (0, i, COL_LX // W)),
            full((CONV_WIDTH, W)),
            full((1, W)),
            full((W, 2 * W)),
            full((1, 2 * W)),
            full((1, W)),
        ],
        out_specs=pl.BlockSpec((B, ts, W), lambda i: (0, i, 0)),
        out_shape=jax.ShapeDtypeStruct((B, S, W), BF16),
        scratch_shapes=(
            [pltpu.VMEM((B * ts, LANES), F32) for _ in range(3 * n_planes)]
            + [pltpu.VMEM((ts, W), F32), pltpu.VMEM((B, 8, W), F32), pltpu.VMEM((n_planes, B, LANES), F32)]),
        compiler_params=_params(("arbitrary",)),
    )(proj3, proj3, conv_w, conv_b.reshape(1, W), w_gates, b_gates, lam.reshape(1, W))


def _block_diag(w):
    n, d, _ = w.shape
    eye = jnp.eye(n, dtype=w.dtype)
    return (eye[:, None, :, None] * w[:, :, None, :]).reshape(n * d, n * d)


def _t5_bucket(rel):
    nb = REL_BUCKETS // 2
    ret = (rel > 0).astype(jnp.int32) * nb
    n = jnp.abs(rel)
    max_exact = nb // 2
    nf = jnp.maximum(n, 1).astype(jnp.float32)
    large = max_exact + (jnp.log(nf / max_exact) / math.log(REL_MAX_DIST / max_exact)
                         * (nb - max_exact)).astype(jnp.int32)
    large = jnp.minimum(large, nb - 1)
    return ret + jnp.where(n < max_exact, n, large)


def _bias_kernel(bucket_ref, table_ref, o_ref):
    h = pl.program_id(0)
    bucket = bucket_ref[0]
    acc = jnp.full(bucket.shape, -1e30, F32)
    for b in range(REL_BUCKETS):
        acc = jnp.where(bucket == b, table_ref[b, h] * LOG2E, acc)
    o_ref[0, 0] = acc


def _bias_tiles(rel_bias):
    t = ATT_TILE
    H = rel_bias.shape[1]
    qp = jnp.arange(t, dtype=jnp.int32)[:, None]
    kp = jnp.arange(t, dtype=jnp.int32)[None, :]
    mask = (kp // CHUNK) <= (qp // CHUNK)
    buckets = jnp.stack([jnp.where(mask, _t5_bucket(kp - qp), REL_BUCKETS), _t5_bucket(kp - t - qp)], axis=0)
    table = rel_bias.astype(F32)
    tiles = pl.pallas_call(
        _bias_kernel,
        grid=(H, 2),
        in_specs=[
            pl.BlockSpec((1, t, t), lambda h, k: (k, 0, 0)),
            pl.BlockSpec(memory_space=pltpu.SMEM),
        ],
        out_specs=pl.BlockSpec((1, 1, t, t), lambda h, k: (h, k, 0, 0)),
        out_shape=jax.ShapeDtypeStruct((H, 2, t, t), F32),
        compiler_params=_params(("parallel", "parallel")),
    )(buckets, table)
    far_bucket = _t5_bucket(jnp.full((1,), -t - 1, jnp.int32))
    far = jnp.sum(jnp.where(jnp.arange(REL_BUCKETS)[:, None] == far_bucket, table, 0.0), axis=0)
    return tiles, jnp.broadcast_to((far * LOG2E)[:, None, None], (H, 1, t))


def _diff_kernel(lam_init, q_ref, k_ref, v_ref, bias_ref, far_ref, lqk_ref, g_ref, o_ref,
                 qs_scr, m_scr, l_scr, acc_scr):
    i = pl.program_id(2)
    t = q_ref.shape[0]
    hq = t // 2
    lane = lax.broadcasted_iota(jnp.int32, (1, LANES), 1)
    q = q_ref[...].astype(F32) * (LOG2E * DIFF_DH ** -0.5)
    for half in range(2):
        qh = q[half * hq:(half + 1) * hq]
        qs_scr[(2 * half) * hq:(2 * half + 1) * hq] = jnp.where(lane < DIFF_DH, qh, 0.0).astype(BF16)
        qs_scr[(2 * half + 1) * hq:(2 * half + 2) * hq] = jnp.where(lane >= DIFF_DH, qh, 0.0).astype(BF16)
    m_scr[...] = jnp.full_like(m_scr, -1e30)
    l_scr[...] = jnp.zeros_like(l_scr)
    acc_scr[...] = jnp.zeros_like(acc_scr)

    def tile(rows, ks, bias):
        s = _dot_nt(qs_scr[rows, :], k_ref[ks, :]) + bias
        groups = [s[:, c * LANES:(c + 1) * LANES] for c in range(s.shape[1] // LANES)]
        mx = functools.reduce(jnp.maximum, groups)
        m_prev = m_scr[rows, :]
        m_new = jnp.maximum(m_prev, jnp.max(mx, axis=-1, keepdims=True))
        alpha = jnp.exp2(m_prev - m_new)
        ps = [jnp.exp2(g - m_new) for g in groups]
        l_scr[rows, :] = alpha * l_scr[rows, :] + functools.reduce(jnp.add, ps)
        p = jnp.concatenate(ps, axis=1).astype(BF16)
        acc_scr[rows, :] = alpha * acc_scr[rows, :] + _dot(p, v_ref[ks, :])
        m_scr[rows, :] = m_new

    def stacked(b, half):
        bh = b[half * hq:(half + 1) * hq]
        return [bh, bh]

    all_rows = slice(0, 2 * t)

    def far_body(j, carry):
        tile(all_rows, pl.ds(pl.multiple_of(j * t, t), t), far_ref[0])
        return carry

    lax.fori_loop(0, jnp.maximum(i - 1, 0), far_body, 0)

    @pl.when(i >= 1)
    def _():
        b = bias_ref[0, 1]
        tile(all_rows, pl.ds(pl.multiple_of((i - 1) * t, t), t),
             jnp.concatenate(stacked(b, 0) + stacked(b, 1), axis=0))

    b = bias_ref[0, 0]
    diag0 = pl.multiple_of(i * t, t)
    tile(slice(0, t), pl.ds(diag0, hq), jnp.concatenate(stacked(b[:, 0:hq], 0), axis=0))
    tile(slice(t, 2 * t), pl.ds(diag0, t), jnp.concatenate(stacked(b, 1), axis=0))

    lqk = lqk_ref[...]
    lam = (jnp.exp(jnp.sum(lqk[0:1] * lqk[1:2], axis=-1, keepdims=True))
           - jnp.exp(jnp.sum(lqk[2:3] * lqk[3:4], axis=-1, keepdims=True)) + lam_init)
    o = acc_scr[...] / jnp.sum(l_scr[...], axis=-1, keepdims=True)
    o = jnp.concatenate([o[0:hq] - lam * o[hq:t], o[t:t + hq] - lam * o[t + hq:2 * t]], axis=0)
    o = o * lax.rsqrt(jnp.mean(o * o, axis=-1, keepdims=True) + EPS)
    o_ref[...] = (o * g_ref[...] * (1.0 - lam_init)).astype(o_ref.dtype)


def _diff_attention(proj, bias, lqk, subln_g, layer_idx, B, S):
    T = proj.shape[0]
    t = ATT_TILE
    nq = S // t
    tiles, far = bias
    lam_init = 0.8 - 0.6 * math.exp(-0.3 * layer_idx)
    return pl.pallas_call(
        functools.partial(_diff_kernel, lam_init),
        grid=(B, DIFF_HEADS, nq),
        in_specs=[
            pl.BlockSpec((t, LANES), lambda b, h, i: (b * nq + i, COL_DQ // LANES + h)),
            pl.BlockSpec((S, LANES), lambda b, h, i: (b, COL_DK // LANES + h)),
            pl.BlockSpec((S, LANES), lambda b, h, i: (b, COL_DV // LANES + h)),
            pl.BlockSpec((1, 2, t, t), lambda b, h, i: (h, 0, 0, 0)),
            pl.BlockSpec((1, 1, t), lambda b, h, i: (h, 0, 0)),
            pl.BlockSpec((4, DIFF_DH), lambda b, h, i: (0, 0)),
            pl.BlockSpec((1, DIFF_DV), lambda b, h, i: (0, 0)),
        ],
        out_specs=pl.BlockSpec((t, LANES), lambda b, h, i: (b * nq + i, h)),
        out_shape=jax.ShapeDtypeStruct((T, DIFF_WIDTH), BF16),
        scratch_shapes=[
            pltpu.VMEM((2 * t, LANES), BF16),
            pltpu.VMEM((2 * t, LANES), F32),
            pltpu.VMEM((2 * t, LANES), F32),
            pltpu.VMEM((2 * t, DIFF_DV), F32),
        ],
        compiler_params=_params(("parallel", "parallel", "arbitrary")),
    )(proj, proj, proj, tiles, far, lqk, subln_g.reshape(1, DIFF_DV))


def _outproj_kernel(h_ref, og_ref, ol_ref, od_ref, w_ref, g1_ref, sh2_ref, sc2_ref, n2_ref, rw_ref, rb_ref,
                    hn_ref, u2_ref, eid_ref, ew_ref, cnt_ref):
    D = h_ref.shape[1]
    acc = _dot(og_ref[...], w_ref[0:GLA_WIDTH, :])
    acc += _dot(ol_ref[...], w_ref[GLA_WIDTH:GLA_WIDTH + LRU_WIDTH, :])
    acc += _dot(od_ref[...], w_ref[GLA_WIDTH + LRU_WIDTH:, :])
    hn = h_ref[...] + g1_ref[0] * acc
    hn_ref[...] = hn
    u2 = _modulated_norm(hn, n2_ref[...], sc2_ref[0], sh2_ref[0])
    _store_row_tiles(u2_ref, _pack_bf16_pair(u2[:, :D // 2], u2[:, D // 2:]))

    logits = _dot(u2.astype(BF16), rw_ref[...]) + rb_ref[...]
    lane = lax.broadcasted_iota(jnp.int32, logits.shape, 1)
    lane_f = lane.astype(F32)
    neg = jnp.float32(-jnp.inf)
    gmask = lane < N_GROUPS
    gl = jnp.where(gmask, logits, neg)
    gmax = jnp.max(gl, axis=-1, keepdims=True)
    gidx = jnp.min(jnp.where(gl == gmax, lane_f, float(LANES)), axis=-1, keepdims=True)
    g_w = 1.0 / jnp.sum(jnp.where(gmask, jnp.exp(gl - gmax), 0.0), axis=-1, keepdims=True)
    egroup = ((lane - N_GROUPS) >> 3).astype(F32)
    emask = (lane >= N_GROUPS) & (lane < N_GROUPS + N_EXPERTS) & (egroup == gidx)
    el = jnp.where(emask, logits, neg)
    v1 = jnp.max(el, axis=-1, keepdims=True)
    i1 = jnp.min(jnp.where(el == v1, lane_f, float(LANES)), axis=-1, keepdims=True)
    el2 = jnp.where(lane_f == i1, neg, el)
    v2 = jnp.max(el2, axis=-1, keepdims=True)
    i2 = jnp.min(jnp.where(el2 == v2, lane_f, float(LANES)), axis=-1, keepdims=True)
    e21 = jnp.exp(v2 - v1)
    w1 = g_w / (1.0 + e21)
    w2 = g_w * e21 / (1.0 + e21)
    ew_ref[...] = jnp.where(lane == 0, w1, jnp.where(lane == 1, w2, 0.0))

    @pl.when(pl.program_id(0) == 0)
    def _():
        cnt_ref[...] = jnp.zeros_like(cnt_ref)

    tm = logits.shape[0]
    oh1 = lane_f == i1
    oh2 = lane_f == i2
    both = jnp.where(oh1 | oh2, 1.0, 0.0).astype(BF16)
    row = lax.broadcasted_iota(jnp.int32, (tm, tm), 0)
    col = lax.broadcasted_iota(jnp.int32, (tm, tm), 1)
    earlier = _dot(jnp.where(col < row, 1.0, 0.0).astype(BF16), both) + cnt_ref[0:1, :]
    rank1 = jnp.sum(jnp.where(oh1, earlier, 0.0), axis=-1, keepdims=True)
    rank2 = jnp.sum(jnp.where(oh2, earlier, 0.0), axis=-1, keepdims=True)
    cnt_ref[0:1, :] = cnt_ref[0:1, :] + jnp.sum(both.astype(F32), axis=0, keepdims=True)
    info = jnp.where(lane == 0, i1 - float(N_GROUPS),
                     jnp.where(lane == 1, i2 - float(N_GROUPS),
                               jnp.where(lane == 2, rank1, jnp.where(lane == 3, rank2, 0.0))))
    eid_ref[...] = info.astype(jnp.int32)


def _outproj(h, o_gla, o_lru, o_diff, w_out, mod3, norm2_g, rw, rb, S):
    T, D = h.shape
    tm = 256
    per_b = S // tm
    rowblk = lambda width: pl.BlockSpec((tm, width), lambda i: (i, 0))
    modblk = lambda k: pl.BlockSpec((1, 1, D), lambda i: (i // per_b, 0, k))
    full = lambda shape: pl.BlockSpec(shape, lambda i: (0,) * len(shape))
    return pl.pallas_call(
        _outproj_kernel,
        grid=(T // tm,),
        in_specs=[
            rowblk(D), rowblk(GLA_WIDTH), rowblk(LRU_WIDTH), rowblk(DIFF_WIDTH),
            full((D, D)),
            modblk(2), modblk(3), modblk(4),
            full((1, D)),
            full((D, LANES)),
            full((1, LANES)),
        ],
        out_specs=[rowblk(D), pl.BlockSpec((tm * SUBLANES, LANES), lambda i: (i, 0)), rowblk(LANES), rowblk(LANES),
                   full((SUBLANES, LANES))],
        out_shape=[
            jax.ShapeDtypeStruct((T, D), F32),
            jax.ShapeDtypeStruct((T * SUBLANES, LANES), U32),
            jax.ShapeDtypeStruct((T, LANES), jnp.int32),
            jax.ShapeDtypeStruct((T, LANES), F32),
            jax.ShapeDtypeStruct((SUBLANES, LANES), F32),
        ],
        compiler_params=_params(("arbitrary",)),
    )(h, o_gla, o_lru, o_diff, w_out, mod3, mod3, mod3, norm2_g.reshape(1, D), rw, rb)


def _dispatch(info, counts):
    T = info.shape[0]
    blk = MOE_BLK
    n_blocks = (T * TOP_K) // blk + N_EXPERTS
    expert = info[:, 0:TOP_K]
    rank = info[:, TOP_K:2 * TOP_K]
    padded = (counts + blk - 1) // blk * blk
    pends = jnp.cumsum(padded)
    pstarts = pends - padded
    onehot = expert[:, :, None] == jnp.arange(N_EXPERTS, dtype=jnp.int32)[None, None, :]
    dest = rank + jnp.sum(jnp.where(onehot, pstarts[None, None, :], 0), axis=-1)
    n_used = (pends[-1] // blk).astype(jnp.int32)
    block_idx = jnp.arange(n_blocks, dtype=jnp.int32)
    block_expert = jnp.minimum(jnp.searchsorted(pends, block_idx * blk, side='right'),
                               N_EXPERTS - 1).astype(jnp.int32)
    last_used = block_expert[jnp.maximum(n_used - 1, 0)]
    block_expert = jnp.where(block_idx < n_used, block_expert, last_used)
    following = jnp.concatenate([block_expert[1:], jnp.full((1,), -1, jnp.int32)])
    zero_block = ((block_idx >= n_used - 1) | (following != block_expert)).astype(jnp.int32)
    return dest.astype(jnp.int32), zero_block, block_expert, n_used.reshape(1)


def _scatter_kernel(d0_ref, d1_ref, zb_ref, src_ref, o_ref, zero_buf, sem, zero_sem):
    rows = src_ref.shape[0] // SUBLANES
    base = pl.program_id(0) * rows
    fill_rows = zero_buf.shape[0]
    n_blocks = o_ref.shape[0] // fill_rows

    @pl.when(pl.program_id(0) == 0)
    def _():
        zero_buf[...] = jnp.zeros_like(zero_buf)

        def for_each_fill(fn):
            def body(j, carry):
                @pl.when(zb_ref[j] == 1)
                def _():
                    fn(pltpu.make_async_copy(
                        zero_buf, o_ref.at[pl.ds(pl.multiple_of(j * fill_rows, fill_rows), fill_rows)], zero_sem))
                return carry
            lax.fori_loop(0, n_blocks, body, 0)

        for_each_fill(lambda copy: copy.start())
        for_each_fill(lambda copy: copy.wait())

    def row_copy(r, slot):
        return pltpu.make_async_copy(src_ref.at[pl.ds(pl.multiple_of(r * SUBLANES, SUBLANES), SUBLANES)],
                                     o_ref.at[pl.ds(pl.multiple_of(slot * SUBLANES, SUBLANES), SUBLANES)], sem)

    def issue(r, carry):
        row_copy(r, d0_ref[base + r]).start()
        row_copy(r, d1_ref[base + r]).start()
        return carry

    lax.fori_loop(0, rows, issue, 0, unroll=8)
    for _ in range(TOP_K):
        pltpu.make_async_copy(src_ref, o_ref.at[pl.ds(0, rows * SUBLANES)], sem).wait()


def _scatter_rows(dest, zero_block, src):
    T = dest.shape[0]
    P = T * TOP_K + N_EXPERTS * MOE_BLK
    rows = GATHER_ROWS
    return pl.pallas_call(
        _scatter_kernel,
        grid_spec=pltpu.PrefetchScalarGridSpec(
            num_scalar_prefetch=3,
            grid=(T // rows,),
            in_specs=[pl.BlockSpec((rows * SUBLANES, LANES), lambda i, d0, d1, zb: (i, 0))],
            out_specs=pl.BlockSpec(memory_space=pl.ANY),
            scratch_shapes=[
                pltpu.VMEM((MOE_BLK * SUBLANES, LANES), src.dtype),
                pltpu.SemaphoreType.DMA(()),
                pltpu.SemaphoreType.DMA(()),
            ],
        ),
        out_shape=jax.ShapeDtypeStruct((P * SUBLANES, LANES), src.dtype),
        compiler_params=_params(("arbitrary",)),
    )(dest[:, 0], dest[:, 1], zero_block, src)


def _expert_kernel(layer, be_ref, first_ref, next_ref, slot_ref, nu_ref, xs_ref, w1_hbm, w3_hbm, w2_hbm, y_ref,
                   w1f, w3f, w2f, w1b, w3b, w2b, sems):
    i = pl.program_id(0)
    D = w1b.shape[0]

    def weight_copies(e, slot):
        return (pltpu.make_async_copy(w1_hbm.at[layer, e], w1f.at[slot], sems.at[slot, 0]),
                pltpu.make_async_copy(w3_hbm.at[layer, e], w3f.at[slot], sems.at[slot, 1]),
                pltpu.make_async_copy(w2_hbm.at[layer, e], w2f.at[slot], sems.at[slot, 2]))

    @pl.when(i == 0)
    def _():
        for c in weight_copies(be_ref[0], 0):
            c.start()

    @pl.when(first_ref[i] == 1)
    def _():
        slot = slot_ref[i]
        for c in weight_copies(be_ref[i], slot):
            c.wait()

        @pl.when(next_ref[i] >= 0)
        def _():
            for c in weight_copies(next_ref[i], 1 - slot):
                c.start()

        w1b[...] = w1f[slot].astype(BF16)
        w3b[...] = w3f[slot].astype(BF16)
        w2b[...] = w2f[slot].astype(BF16)

    @pl.when(i < nu_ref[0])
    def _():
        lo, hi = _unpack_bf16_pair(_load_row_tiles(xs_ref))
        lo = lo.astype(BF16)
        hi = hi.astype(BF16)
        a = _dot(lo, w1b[0:D // 2, :]) + _dot(hi, w1b[D // 2:, :])
        g = _dot(lo, w3b[0:D // 2, :]) + _dot(hi, w3b[D // 2:, :])
        hid = ((a * _sigmoid(a)) * g).astype(BF16)
        _store_row_tiles(y_ref, _pack_bf16_pair(_dot(hid, w2b[:, 0:D // 2]), _dot(hid, w2b[:, D // 2:])))

    @pl.when(i >= nu_ref[0])
    def _():
        y_ref[...] = jnp.zeros_like(y_ref)


def _segment_plan(block_expert, n_used):
    n = block_expert.shape[0]
    idx = jnp.arange(n, dtype=jnp.int32)
    prev = jnp.concatenate([jnp.full((1,), -1, jnp.int32), block_expert[:-1]])
    first = ((block_expert != prev) & (idx < n_used[0])).astype(jnp.int32)
    slot = (jnp.cumsum(first) - 1) % 2
    later_first = jnp.where(first == 1, idx, n)
    next_idx = lax.cummin(jnp.concatenate([later_first[1:], jnp.full((1,), n, jnp.int32)]), reverse=True)
    next_expert = jnp.where(next_idx < n, block_expert[jnp.minimum(next_idx, n - 1)], -1)
    return first, next_expert.astype(jnp.int32), slot.astype(jnp.int32)


def _experts(block_expert, n_used, xs, w1, w3, w2, layer):
    _, _, D, DE = w1.shape
    blk = MOE_BLK
    n_blocks = block_expert.shape[0]
    first, next_expert, slot = _segment_plan(block_expert, n_used)
    rowmap = lambda i, *refs: (jnp.minimum(i, jnp.maximum(refs[4][0] - 1, 0)), 0)
    hbm = pl.BlockSpec(memory_space=pl.ANY)
    return pl.pallas_call(
        functools.partial(_expert_kernel, layer),
        grid_spec=pltpu.PrefetchScalarGridSpec(
            num_scalar_prefetch=5,
            grid=(n_blocks,),
            in_specs=[pl.BlockSpec((blk * SUBLANES, LANES), rowmap), hbm, hbm, hbm],
            out_specs=pl.BlockSpec((blk * SUBLANES, LANES), lambda i, *refs: (i, 0)),
            scratch_shapes=[
                pltpu.VMEM((2, D, DE), F32),
                pltpu.VMEM((2, D, DE), F32),
                pltpu.VMEM((2, DE, D), F32),
                pltpu.VMEM((D, DE), BF16),
                pltpu.VMEM((D, DE), BF16),
                pltpu.VMEM((DE, D), BF16),
                pltpu.SemaphoreType.DMA((2, 3)),
            ],
        ),
        out_shape=jax.ShapeDtypeStruct((n_blocks * blk * SUBLANES, LANES), U32),
        compiler_params=_params(("arbitrary",)),
    )(block_expert, first, next_expert, slot, n_used, xs, w1, w3, w2)


def _combine_kernel(final, d0_ref, d1_ref, y_ref, h_ref, g2_ref, ew_ref, fg_ref, o_ref, buf0, buf1, sem):
    tc, D = h_ref.shape
    base = pl.program_id(0) * tc

    def tile_copy(slot, buf, r):
        return pltpu.make_async_copy(y_ref.at[pl.ds(pl.multiple_of(slot * SUBLANES, SUBLANES), SUBLANES)],
                                     buf.at[pl.ds(pl.multiple_of(r * SUBLANES, SUBLANES), SUBLANES)], sem)

    def issue(r, carry):
        tile_copy(d0_ref[base + r], buf0, r).start()
        tile_copy(d1_ref[base + r], buf1, r).start()
        return carry

    lax.fori_loop(0, tc, issue, 0, unroll=8)
    pltpu.make_async_copy(y_ref.at[pl.ds(0, tc * SUBLANES)], buf0, sem).wait()
    pltpu.make_async_copy(y_ref.at[pl.ds(0, tc * SUBLANES)], buf1, sem).wait()

    ew = ew_ref[...]
    w0 = ew[:, 0:1]
    w1 = ew[:, 1:2]
    lo0, hi0 = _unpack_bf16_pair(_load_row_tiles(buf0))
    lo1, hi1 = _unpack_bf16_pair(_load_row_tiles(buf1))
    moe = jnp.concatenate([w0 * lo0 + w1 * lo1, w0 * hi0 + w1 * hi1], axis=1)
    hn = h_ref[...] + g2_ref[0] * moe
    if final:
        hn = hn * lax.rsqrt(jnp.mean(hn * hn, axis=-1, keepdims=True) + EPS) * fg_ref[...]
    o_ref[...] = hn


def _combine(dest, y, h, mod3, ew, final_g, S, final):
    T, D = h.shape
    tc = 256
    per_b = S // tc
    return pl.pallas_call(
        functools.partial(_combine_kernel, final),
        grid_spec=pltpu.PrefetchScalarGridSpec(
            num_scalar_prefetch=2,
            grid=(T // tc,),
            in_specs=[
                pl.BlockSpec(memory_space=pl.ANY),
                pl.BlockSpec((tc, D), lambda i, d0, d1: (i, 0)),
                pl.BlockSpec((1, 1, D), lambda i, d0, d1: (i // per_b, 0, 5)),
                pl.BlockSpec((tc, LANES), lambda i, d0, d1: (i, 0)),
                pl.BlockSpec((1, D), lambda i, d0, d1: (0, 0)),
            ],
            out_specs=pl.BlockSpec((tc, D), lambda i, d0, d1: (i, 0)),
            scratch_shapes=[
                pltpu.VMEM((tc * SUBLANES, LANES), U32),
                pltpu.VMEM((tc * SUBLANES, LANES), U32),
                pltpu.SemaphoreType.DMA(()),
            ],
        ),
        out_shape=jax.ShapeDtypeStruct((T, D), F32),
        compiler_params=_params(("arbitrary",)),
    )(dest[:, 0], dest[:, 1], y, h, mod3, ew, final_g.reshape(1, D))


def kernel(x, c, ada_w, ada_b, norm1_g, w_in, gla_w_a2, gla_b_a, gla_norm_g, lru_conv_w, lru_conv_b,
           lru_wa, lru_ba, lru_wx, lru_bx, lru_lambda, diff_lq1, diff_lk1, diff_lq2, diff_lk2,
           diff_subln_g, rel_bias, w_out, norm2_g, router_g_w, router_g_b, router_e_w, router_e_b,
           moe_w1, moe_w3, moe_w2, final_g):
    B, S, D = x.shape
    T = B * S
    L = ada_w.shape[0]
    h = x.reshape(T, D)
    mod = _ada_mod(c, ada_w, ada_b)
    bias = _bias_tiles(rel_bias)
    for l in range(L):
        mod3 = mod[l][:, None, :]
        proj = _inproj(h, mod3, norm1_g[l], _permute_w_in(w_in[l]), S)
        wa2_pad = jnp.concatenate(
            [gla_w_a2[l], jnp.zeros((LANES - GLA_LOWRANK, GLA_KEY_WIDTH), F32)], axis=0).astype(BF16)
        o_gla = _gla(proj, wa2_pad, gla_b_a[l], gla_norm_g[l], B, S)
        w_gates = jnp.concatenate([_block_diag(lru_wa[l]), _block_diag(lru_wx[l])], axis=1).astype(BF16)
        b_gates = jnp.concatenate([lru_ba[l], lru_bx[l]]).reshape(1, 2 * LRU_WIDTH)
        o_lru = _lru(proj.reshape(B, S, PROJ_WIDTH), lru_conv_w[l], lru_conv_b[l], w_gates, b_gates,
                     lru_lambda[l]).reshape(T, LRU_WIDTH)
        lqk = jnp.stack([diff_lq1[l], diff_lk1[l], diff_lq2[l], diff_lk2[l]], axis=0)
        o_diff = _diff_attention(proj, bias, lqk, diff_subln_g[l], l, B, S)
        rw = jnp.concatenate(
            [router_g_w[l], router_e_w[l], jnp.zeros((D, LANES - N_GROUPS - N_EXPERTS), F32)], axis=1).astype(BF16)
        rb = jnp.concatenate(
            [router_g_b[l], router_e_b[l], jnp.zeros((LANES - N_GROUPS - N_EXPERTS,), F32)]).reshape(1, LANES)
        h, u2, info, ew, cnt = _outproj(h, o_gla, o_lru, o_diff, w_out[l].astype(BF16), mod3, norm2_g[l], rw, rb, S)
        counts = cnt[0, N_GROUPS:N_GROUPS + N_EXPERTS].astype(jnp.int32)
        dest, zero_block, block_expert, n_used = _dispatch(info, counts)
        xs = _scatter_rows(dest, zero_block, u2)
        y = _experts(block_expert, n_used, xs, moe_w1, moe_w3, moe_w2, l)
        h = _combine(dest, y, h, mod3, ew, final_g, S, final=(l == L - 1))
    return h.reshape(B, S, D)
```

```python
import functools
import math

import jax
import jax.numpy as jnp
from jax import lax
from jax.experimental import pallas as pl
from jax.experimental.pallas import tpu as pltpu

F32 = jnp.float32
BF16 = jnp.bfloat16
U32 = jnp.uint32

EPS = 1e-6
LOG2E = math.log2(math.e)
CHUNK = 64

GLA_DV = 128
GLA_DK = 64
GLA_HEADS = 6
GLA_WIDTH = GLA_HEADS * GLA_DV
GLA_KEY_WIDTH = GLA_HEADS * GLA_DK
GLA_LOWRANK = 16
GLA_TAU = 16.0

LRU_WIDTH = 512
LRU_BLOCKS = 8
LRU_BLOCK_DIM = LRU_WIDTH // LRU_BLOCKS
CONV_WIDTH = 4
LRU_C = 8.0

DIFF_DH = 64
DIFF_DV = 128
DIFF_HEADS = 6
DIFF_WIDTH = DIFF_HEADS * DIFF_DV

REL_BUCKETS = 32
REL_MAX_DIST = 128

N_GROUPS = 8
EXPERTS_PER_GROUP = 8
N_EXPERTS = 64
TOP_K = 2

LANES = 128
SUBLANES = 8
VMEM_LIMIT = 56 * 1024 * 1024

COL_GV = 0
COL_GOG = 768
COL_DQ = 1536
COL_DK = 2304
COL_DV = 3072
COL_GQ = 3840
COL_GK = 4224
COL_LY = 4608
COL_LX = 5120
COL_GA = 5632
PROJ_WIDTH = 5760

ATT_TILE = 512
GLA_TILE = 256
LRU_TILE = 256
MOE_BLK = 256
GATHER_ROWS = 512


def _params(sem, vmem=VMEM_LIMIT):
    return pltpu.CompilerParams(dimension_semantics=sem, vmem_limit_bytes=vmem)


def _sigmoid(x):
    return 0.5 * jnp.tanh(0.5 * x) + 0.5


def _softplus(x):
    return jnp.maximum(x, 0.0) + jnp.log1p(jnp.exp(-jnp.abs(x)))


def _dot(a, b):
    return jnp.dot(a, b, preferred_element_type=F32)


def _dot_nt(a, b):
    return lax.dot_general(a, b, (((1,), (1,)), ((), ())), preferred_element_type=F32)


def _dot_tn(a, b):
    return lax.dot_general(a, b, (((0,), (0,)), ((), ())), preferred_element_type=F32)


def _pack_bf16_pair(lo, hi):
    lo_bits = lax.bitcast_convert_type(lo.astype(BF16).astype(F32), U32)
    hi_bits = lax.bitcast_convert_type(hi.astype(BF16).astype(F32), U32)
    return (hi_bits & jnp.uint32(0xFFFF0000)) | (lo_bits >> 16)


def _unpack_bf16_pair(w):
    lo = lax.bitcast_convert_type(w << 16, F32)
    hi = lax.bitcast_convert_type(w & jnp.uint32(0xFFFF0000), F32)
    return lo, hi


def _store_row_tiles(ref, words):
    rows = words.shape[0]
    for s in range(SUBLANES):
        ref[pl.ds(s, rows, stride=SUBLANES), :] = words[:, s * LANES:(s + 1) * LANES]


def _load_row_tiles(ref):
    rows = ref.shape[0] // SUBLANES
    return jnp.concatenate([ref[pl.ds(s, rows, stride=SUBLANES), :] for s in range(SUBLANES)], axis=1)


def _ada_kernel(c_ref, w_ref, b_ref, o_ref):
    c = c_ref[...]
    s = c * _sigmoid(c)
    o_ref[0] = _dot(s.astype(BF16), w_ref[0].astype(BF16)) + b_ref[0]


def _ada_mod(c, ada_w, ada_b):
    L, D, N = ada_w.shape
    B = c.shape[0]
    tn = 1024
    return pl.pallas_call(
        _ada_kernel,
        grid=(L, N // tn),
        in_specs=[
            pl.BlockSpec((B, D), lambda l, j: (0, 0)),
            pl.BlockSpec((1, D, tn), lambda l, j: (l, 0, j)),
            pl.BlockSpec((1, 1, tn), lambda l, j: (l, 0, j)),
        ],
        out_specs=pl.BlockSpec((1, B, tn), lambda l, j: (l, 0, j)),
        out_shape=jax.ShapeDtypeStruct((L, B, N), F32),
        compiler_params=_params(("parallel", "parallel")),
    )(c, ada_w, ada_b.reshape(L, 1, N))


def _modulated_norm(x, g, sc, sh):
    ms = jnp.mean(x * x, axis=-1, keepdims=True)
    return (x * lax.rsqrt(ms + EPS) * g) * (1.0 + sc) + sh


def _inproj_kernel(h_ref, sh_ref, sc_ref, g_ref, w_ref, o_ref, u_scr):
    @pl.when(pl.program_id(1) == 0)
    def _():
        u = _modulated_norm(h_ref[...], g_ref[...], sc_ref[0], sh_ref[0])
        u_scr[...] = u.astype(BF16)

    o_ref[...] = _dot(u_scr[...], w_ref[...]).astype(o_ref.dtype)


def _inproj(h, mod3, norm_g, w_perm, S):
    T, D = h.shape
    N = w_perm.shape[1]
    tm, tn = 512, 1152
    per_b = S // tm
    return pl.pallas_call(
        _inproj_kernel,
        grid=(T // tm, N // tn),
        in_specs=[
            pl.BlockSpec((tm, D), lambda i, j: (i, 0)),
            pl.BlockSpec((1, 1, D), lambda i, j: (i // per_b, 0, 0)),
            pl.BlockSpec((1, 1, D), lambda i, j: (i // per_b, 0, 1)),
            pl.BlockSpec((1, D), lambda i, j: (0, 0)),
            pl.BlockSpec((D, tn), lambda i, j: (0, j)),
        ],
        out_specs=pl.BlockSpec((tm, tn), lambda i, j: (i, j)),
        out_shape=jax.ShapeDtypeStruct((T, N), BF16),
        scratch_shapes=[pltpu.VMEM((tm, D), BF16)],
        compiler_params=_params(("parallel", "arbitrary")),
    )(h, mod3, mod3, norm_g.reshape(1, D), w_perm)


def _permute_w_in(w):
    D = w.shape[0]
    gq, gk, gv, gog, ga, ly, lx, dq, dk, dv = jnp.split(
        w, [384, 768, 1536, 2304, 2320, 2832, 3344, 4112, 4880], axis=1)
    ga = jnp.concatenate([ga, jnp.zeros((D, LANES - GLA_LOWRANK), w.dtype)], axis=1)
    return jnp.concatenate([gv, gog, dq, dk, dv, gq, gk, ly, lx, ga], axis=1).astype(BF16)


def _gla_kernel(q_ref, k_ref, v_ref, og_ref, alr_ref, wa2_ref, ba_ref, ng_ref, o_ref, st_ref):
    tb = q_ref.shape[0]
    n_chunks = tb // CHUNK

    @pl.when(pl.program_id(1) == 0)
    def _():
        st_ref[...] = jnp.zeros_like(st_ref)

    row = lax.broadcasted_iota(jnp.int32, (tb, tb), 0)
    col = lax.broadcasted_iota(jnp.int32, (tb, tb), 1)
    same_chunk = (row // CHUNK) == (col // CHUNK)
    causal = col <= row
    tril = jnp.where(same_chunk & causal, 1.0, 0.0).astype(BF16)
    chunk_ones = jnp.where(same_chunk, 1.0, 0.0).astype(BF16)
    lane = lax.broadcasted_iota(jnp.int32, (1, LANES), 1)
    half_masks = (lane < GLA_DK, lane >= GLA_DK)

    alr = alr_ref[...]
    for p in range(GLA_HEADS // 2):
        cs = slice(p * LANES, (p + 1) * LANES)
        z = _dot(alr, wa2_ref[:, cs]) + ba_ref[:, cs]
        la = (jnp.minimum(z, 0.0) - jnp.log1p(jnp.exp(-jnp.abs(z)))) * (1.0 / GLA_TAU)
        la_hi = la.astype(BF16)
        la_lo = (la - la_hi.astype(F32)).astype(BF16)
        G = _dot(tril, la_hi) + _dot(tril, la_lo)
        Gl = _dot(chunk_ones, la_hi) + _dot(chunk_ones, la_lo)
        eG = jnp.exp(G)
        enG = jnp.exp(-G)
        q = q_ref[:, cs].astype(F32) * (GLA_DK ** -0.5)
        k = k_ref[:, cs].astype(F32)
        qf = q * eG
        qb = q * enG
        kf = (k * eG).astype(BF16)
        kb = (k * enG).astype(BF16)
        kd = k * jnp.exp(Gl - G)
        for hh in range(2):
            head = 2 * p + hh
            m = half_masks[hh]
            vs = slice(head * GLA_DV, (head + 1) * GLA_DV)
            qf_h = jnp.where(m, qf, 0.0).astype(BF16)
            qb_h = jnp.where(m, qb, 0.0).astype(BF16)
            kd_h = jnp.where(m, kd, 0.0).astype(BF16)
            v_h = v_ref[:, vs]
            a_f = _dot_nt(qf_h, kb)
            a_b = _dot_nt(qb_h, kf)
            attn = jnp.where(same_chunk, jnp.where(causal, a_f, a_b), 0.0)
            o_intra = _dot(attn.astype(BF16), v_h)
            st = st_ref[head]
            inter = []
            for c in range(n_chunks):
                rs = slice(c * CHUNK, (c + 1) * CHUNK)
                inter.append(_dot_nt(qf_h[rs], st.astype(BF16)))
                decay = jnp.exp(Gl[c * CHUNK:c * CHUNK + 1, :])
                st = st * decay + _dot_tn(v_h[rs], kd_h[rs])
            st_ref[head] = st
            o = o_intra + jnp.concatenate(inter, axis=0)
            o = o * lax.rsqrt(jnp.mean(o * o, axis=-1, keepdims=True) + EPS)
            og = og_ref[:, vs].astype(F32)
            o_ref[:, vs] = (o * ng_ref[:, vs] * (og * _sigmoid(og))).astype(o_ref.dtype)


def _gla(proj, wa2_pad, b_a, norm_g, B, S):
    T = proj.shape[0]
    tb = GLA_TILE
    nt = S // tb
    row = lambda b, i: b * nt + i
    return pl.pallas_call(
        _gla_kernel,
        grid=(B, nt),
        in_specs=[
            pl.BlockSpec((tb, GLA_KEY_WIDTH), lambda b, i: (row(b, i), COL_GQ // GLA_KEY_WIDTH)),
            pl.BlockSpec((tb, GLA_KEY_WIDTH), lambda b, i: (row(b, i), COL_GK // GLA_KEY_WIDTH)),
            pl.BlockSpec((tb, GLA_WIDTH), lambda b, i: (row(b, i), COL_GV // GLA_WIDTH)),
            pl.BlockSpec((tb, GLA_WIDTH), lambda b, i: (row(b, i), COL_GOG // GLA_WIDTH)),
            pl.BlockSpec((tb, LANES), lambda b, i: (row(b, i), COL_GA // LANES)),
            pl.BlockSpec((LANES, GLA_KEY_WIDTH), lambda b, i: (0, 0)),
            pl.BlockSpec((1, GLA_KEY_WIDTH), lambda b, i: (0, 0)),
            pl.BlockSpec((1, GLA_WIDTH), lambda b, i: (0, 0)),
        ],
        out_specs=pl.BlockSpec((tb, GLA_WIDTH), lambda b, i: (row(b, i), 0)),
        out_shape=jax.ShapeDtypeStruct((T, GLA_WIDTH), BF16),
        scratch_shapes=[pltpu.VMEM((GLA_HEADS, GLA_DV, LANES), F32)],
        compiler_params=_params(("parallel", "arbitrary")),
    )(proj, proj, proj, proj, proj, wa2_pad, b_a.reshape(1, -1), norm_g.reshape(1, -1))


def _lru_kernel(y_ref, x_ref, cw_ref, cb_ref, wg_ref, bg_ref, lam_ref, o_ref, *scratch):
    B, ts, W = x_ref.shape
    n_planes = W // LANES
    a_scr = scratch[0:n_planes]
    b_scr = scratch[n_planes:2 * n_planes]
    h_scr = scratch[2 * n_planes:3 * n_planes]
    xc_scr, tail_scr, carry_scr = scratch[3 * n_planes:]

    @pl.when(pl.program_id(0) == 0)
    def _():
        tail_scr[...] = jnp.zeros_like(tail_scr)
        carry_scr[...] = jnp.zeros_like(carry_scr)

    cw = cw_ref[...]
    cb = cb_ref[...]
    sp = _softplus(-lam_ref[...])
    row8 = lax.broadcasted_iota(jnp.int32, (8, W), 0)
    for b in range(B):
        x = x_ref[b].astype(F32)
        tail = tail_scr[b]
        xc = cb + cw[CONV_WIDTH - 1:CONV_WIDTH, :] * x
        head = cb + cw[CONV_WIDTH - 1:CONV_WIDTH, :] * x[0:8]
        for d in range(1, CONV_WIDTH):
            wd = cw[CONV_WIDTH - 1 - d:CONV_WIDTH - d, :]
            xr = pltpu.roll(x, d, 0)
            xc = xc + wd * xr
            head = head + wd * jnp.where(row8 < d, pltpu.roll(tail, d, 0), xr[0:8])
        tail_scr[b] = x[ts - 8:ts]
        xc_scr[...] = xc
        xc_scr[0:8] = head
        xc = xc_scr[...]
        gates = _sigmoid(_dot(xc.astype(BF16), wg_ref[...]) + bg_ref[...])
        r = gates[:, :W]
        ig = gates[:, W:]
        log_a = (-LRU_C) * r * sp
        a = jnp.exp(log_a)
        b_in = jnp.sqrt(-jnp.tanh(log_a) * (a * a + 1.0)) * (ig * xc)
        rows = slice(b * ts, (b + 1) * ts)
        for k in range(n_planes):
            a_scr[k][rows] = a[:, k * LANES:(k + 1) * LANES]
            b_scr[k][rows] = b_in[:, k * LANES:(k + 1) * LANES]

    def step(t, hs):
        idx = pl.ds(t, B, stride=ts)
        out = []
        for k in range(n_planes):
            hk = a_scr[k][idx, :] * hs[k] + b_scr[k][idx, :]
            h_scr[k][idx, :] = hk
            out.append(hk)
        return tuple(out)

    hs = lax.fori_loop(0, ts, step, tuple(carry_scr[k] for k in range(n_planes)), unroll=8)
    for k in range(n_planes):
        carry_scr[k] = hs[k]

    for b in range(B):
        rows = slice(b * ts, (b + 1) * ts)
        y = y_ref[b].astype(F32)
        gelu = 0.5 * y * (1.0 + jnp.tanh(math.sqrt(2.0 / math.pi) * (y + 0.044715 * (y * y * y))))
        h = jnp.concatenate([h_scr[k][rows] for k in range(n_planes)], axis=1)
        o_ref[b] = (h * gelu).astype(o_ref.dtype)


def _lru(proj3, conv_w, conv_b, w_gates, b_gates, lam):
    B, S, _ = proj3.shape
    W = LRU_WIDTH
    ts = LRU_TILE
    n_planes = W // LANES
    full = lambda shape: pl.BlockSpec(shape, lambda i: (0,) * len(shape))
    return pl.pallas_call(
        _lru_kernel,
        grid=(S // ts,),
        in_specs=[
            pl.BlockSpec((B, ts, W), lambda i: (0, i, COL_LY // W)),
            pl.BlockSpec((B, ts, W), lambda i: (0, i, COL_LX // W)),
            full((CONV_WIDTH, W)),
            full((1, W)),
            full((W, 2 * W)),
            full((1, 2 * W)),
            full((1, W)),
        ],
        out_specs=pl.BlockSpec((B, ts, W), lambda i: (0, i, 0)),
        out_shape=jax.ShapeDtypeStruct((B, S, W), BF16),
        scratch_shapes=(
            [pltpu.VMEM((B * ts, LANES), F32) for _ in range(3 * n_planes)]
            + [pltpu.VMEM((ts, W), F32), pltpu.VMEM((B, 8, W), F32), pltpu.VMEM((n_planes, B, LANES), F32)]),
        compiler_params=_params(("arbitrary",)),
    )(proj3, proj3, conv_w, conv_b.reshape(1, W), w_gates, b_gates, lam.reshape(1, W))


def _block_diag(w):
    n, d, _ = w.shape
    eye = jnp.eye(n, dtype=w.dtype)
    return (eye[:, None, :, None] * w[:, :, None, :]).reshape(n * d, n * d)


def _t5_bucket(rel):
    nb = REL_BUCKETS // 2
    ret = (rel > 0).astype(jnp.int32) * nb
    n = jnp.abs(rel)
    max_exact = nb // 2
    nf = jnp.maximum(n, 1).astype(jnp.float32)
    large = max_exact + (jnp.log(nf / max_exact) / math.log(REL_MAX_DIST / max_exact)
                         * (nb - max_exact)).astype(jnp.int32)
    large = jnp.minimum(large, nb - 1)
    return ret + jnp.where(n < max_exact, n, large)


def _bias_kernel(bucket_ref, table_ref, o_ref):
    h = pl.program_id(0)
    bucket = bucket_ref[0]
    acc = jnp.full(bucket.shape, -1e30, F32)
    for b in range(REL_BUCKETS):
        acc = jnp.where(bucket == b, table_ref[b, h] * LOG2E, acc)
    o_ref[0, 0] = acc


def _bias_tiles(rel_bias):
    t = ATT_TILE
    H = rel_bias.shape[1]
    qp = jnp.arange(t, dtype=jnp.int32)[:, None]
    kp = jnp.arange(t, dtype=jnp.int32)[None, :]
    mask = (kp // CHUNK) <= (qp // CHUNK)
    half = REL_BUCKETS // 2
    per_distance = _t5_bucket(-jnp.arange(2 * t, dtype=jnp.int32))
    edges = jnp.sum((per_distance[None, :] < jnp.arange(1, half, dtype=jnp.int32)[:, None]).astype(jnp.int32), axis=1)

    def bucket_2d(rel):
        passed = jnp.sum((jnp.abs(rel)[None] >= edges[:, None, None]).astype(jnp.int32), axis=0)
        return (rel > 0).astype(jnp.int32) * half + passed

    buckets = jnp.stack([jnp.where(mask, bucket_2d(kp - qp), REL_BUCKETS), bucket_2d(kp - t - qp)], axis=0)
    table = rel_bias.astype(F32)
    tiles = pl.pallas_call(
        _bias_kernel,
        grid=(H, 2),
        in_specs=[
            pl.BlockSpec((1, t, t), lambda h, k: (k, 0, 0)),
            pl.BlockSpec(memory_space=pltpu.SMEM),
        ],
        out_specs=pl.BlockSpec((1, 1, t, t), lambda h, k: (h, k, 0, 0)),
        out_shape=jax.ShapeDtypeStruct((H, 2, t, t), F32),
        compiler_params=_params(("parallel", "parallel")),
    )(buckets, table)
    far_bucket = _t5_bucket(jnp.full((1,), -t - 1, jnp.int32))
    far = jnp.sum(jnp.where(jnp.arange(REL_BUCKETS)[:, None] == far_bucket, table, 0.0), axis=0)
    return tiles, jnp.broadcast_to((far * LOG2E)[:, None, None], (H, 1, t))


def _diff_kernel(lam_init, q_ref, k_ref, v_ref, bias_ref, far_ref, lqk_ref, g_ref, o_ref,
                 qs_scr, m_scr, l_scr, acc_scr):
    i = pl.program_id(2)
    t = q_ref.shape[0]
    hq = t // 2
    lane = lax.broadcasted_iota(jnp.int32, (1, LANES), 1)
    q = q_ref[...].astype(F32) * (LOG2E * DIFF_DH ** -0.5)
    for half in range(2):
        qh = q[half * hq:(half + 1) * hq]
        qs_scr[(2 * half) * hq:(2 * half + 1) * hq] = jnp.where(lane < DIFF_DH, qh, 0.0).astype(BF16)
        qs_scr[(2 * half + 1) * hq:(2 * half + 2) * hq] = jnp.where(lane >= DIFF_DH, qh, 0.0).astype(BF16)
    m_scr[...] = jnp.full_like(m_scr, -1e30)
    l_scr[...] = jnp.zeros_like(l_scr)
    acc_scr[...] = jnp.zeros_like(acc_scr)

    def tile(rows, ks, bias):
        s = _dot_nt(qs_scr[rows, :], k_ref[ks, :]) + bias
        groups = [s[:, c * LANES:(c + 1) * LANES] for c in range(s.shape[1] // LANES)]
        mx = functools.reduce(jnp.maximum, groups)
        m_prev = m_scr[rows, :]
        m_new = jnp.maximum(m_prev, jnp.max(mx, axis=-1, keepdims=True))
        alpha = jnp.exp2(m_prev - m_new)
        ps = [jnp.exp2(g - m_new) for g in groups]
        l_scr[rows, :] = alpha * l_scr[rows, :] + functools.reduce(jnp.add, ps)
        p = jnp.concatenate(ps, axis=1).astype(BF16)
        acc_scr[rows, :] = alpha * acc_scr[rows, :] + _dot(p, v_ref[ks, :])
        m_scr[rows, :] = m_new

    def stacked(b, half):
        bh = b[half * hq:(half + 1) * hq]
        return [bh, bh]

    all_rows = slice(0, 2 * t)

    def far_body(j, carry):
        tile(all_rows, pl.ds(pl.multiple_of(j * t, t), t), far_ref[0])
        return carry

    lax.fori_loop(0, jnp.maximum(i - 1, 0), far_body, 0)

    @pl.when(i >= 1)
    def _():
        b = bias_ref[0, 1]
        tile(all_rows, pl.ds(pl.multiple_of((i - 1) * t, t), t),
             jnp.concatenate(stacked(b, 0) + stacked(b, 1), axis=0))

    b = bias_ref[0, 0]
    diag0 = pl.multiple_of(i * t, t)
    tile(slice(0, t), pl.ds(diag0, hq), jnp.concatenate(stacked(b[:, 0:hq], 0), axis=0))
    tile(slice(t, 2 * t), pl.ds(diag0, t), jnp.concatenate(stacked(b, 1), axis=0))

    lqk = lqk_ref[...]
    lam = (jnp.exp(jnp.sum(lqk[0:1] * lqk[1:2], axis=-1, keepdims=True))
           - jnp.exp(jnp.sum(lqk[2:3] * lqk[3:4], axis=-1, keepdims=True)) + lam_init)
    o = acc_scr[...] / jnp.sum(l_scr[...], axis=-1, keepdims=True)
    o = jnp.concatenate([o[0:hq] - lam * o[hq:t], o[t:t + hq] - lam * o[t + hq:2 * t]], axis=0)
    o = o * lax.rsqrt(jnp.mean(o * o, axis=-1, keepdims=True) + EPS)
    o_ref[...] = (o * g_ref[...] * (1.0 - lam_init)).astype(o_ref.dtype)


def _diff_attention(proj, bias, lqk, subln_g, layer_idx, B, S):
    T = proj.shape[0]
    t = ATT_TILE
    nq = S // t
    tiles, far = bias
    lam_init = 0.8 - 0.6 * math.exp(-0.3 * layer_idx)
    return pl.pallas_call(
        functools.partial(_diff_kernel, lam_init),
        grid=(B, DIFF_HEADS, nq),
        in_specs=[
            pl.BlockSpec((t, LANES), lambda b, h, i: (b * nq + i, COL_DQ // LANES + h)),
            pl.BlockSpec((S, LANES), lambda b, h, i: (b, COL_DK // LANES + h)),
            pl.BlockSpec((S, LANES), lambda b, h, i: (b, COL_DV // LANES + h)),
            pl.BlockSpec((1, 2, t, t), lambda b, h, i: (h, 0, 0, 0)),
            pl.BlockSpec((1, 1, t), lambda b, h, i: (h, 0, 0)),
            pl.BlockSpec((4, DIFF_DH), lambda b, h, i: (0, 0)),
            pl.BlockSpec((1, DIFF_DV), lambda b, h, i: (0, 0)),
        ],
        out_specs=pl.BlockSpec((t, LANES), lambda b, h, i: (b * nq + i, h)),
        out_shape=jax.ShapeDtypeStruct((T, DIFF_WIDTH), BF16),
        scratch_shapes=[
            pltpu.VMEM((2 * t, LANES), BF16),
            pltpu.VMEM((2 * t, LANES), F32),
            pltpu.VMEM((2 * t, LANES), F32),
            pltpu.VMEM((2 * t, DIFF_DV), F32),
        ],
        compiler_params=_params(("parallel", "parallel", "arbitrary")),
    )(proj, proj, proj, tiles, far, lqk, subln_g.reshape(1, DIFF_DV))


def _outproj_kernel(h_ref, og_ref, ol_ref, od_ref, w_ref, g1_ref, sh2_ref, sc2_ref, n2_ref, rw_ref, rb_ref,
                    hn_ref, u2_ref, eid_ref, ew_ref, cnt_ref):
    D = h_ref.shape[1]
    acc = _dot(og_ref[...], w_ref[0:GLA_WIDTH, :])
    acc += _dot(ol_ref[...], w_ref[GLA_WIDTH:GLA_WIDTH + LRU_WIDTH, :])
    acc += _dot(od_ref[...], w_ref[GLA_WIDTH + LRU_WIDTH:, :])
    hn = h_ref[...] + g1_ref[0] * acc
    hn_ref[...] = hn
    u2 = _modulated_norm(hn, n2_ref[...], sc2_ref[0], sh2_ref[0])
    _store_row_tiles(u2_ref, _pack_bf16_pair(u2[:, :D // 2], u2[:, D // 2:]))

    logits = _dot(u2.astype(BF16), rw_ref[...]) + rb_ref[...]
    lane = lax.broadcasted_iota(jnp.int32, logits.shape, 1)
    lane_f = lane.astype(F32)
    neg = jnp.float32(-jnp.inf)
    gmask = lane < N_GROUPS
    gl = jnp.where(gmask, logits, neg)
    gmax = jnp.max(gl, axis=-1, keepdims=True)
    gidx = jnp.min(jnp.where(gl == gmax, lane_f, float(LANES)), axis=-1, keepdims=True)
    g_w = 1.0 / jnp.sum(jnp.where(gmask, jnp.exp(gl - gmax), 0.0), axis=-1, keepdims=True)
    egroup = ((lane - N_GROUPS) >> 3).astype(F32)
    emask = (lane >= N_GROUPS) & (lane < N_GROUPS + N_EXPERTS) & (egroup == gidx)
    el = jnp.where(emask, logits, neg)
    v1 = jnp.max(el, axis=-1, keepdims=True)
    i1 = jnp.min(jnp.where(el == v1, lane_f, float(LANES)), axis=-1, keepdims=True)
    el2 = jnp.where(lane_f == i1, neg, el)
    v2 = jnp.max(el2, axis=-1, keepdims=True)
    i2 = jnp.min(jnp.where(el2 == v2, lane_f, float(LANES)), axis=-1, keepdims=True)
    e21 = jnp.exp(v2 - v1)
    w1 = g_w / (1.0 + e21)
    w2 = g_w * e21 / (1.0 + e21)
    ew_ref[...] = jnp.where(lane == 0, w1, jnp.where(lane == 1, w2, 0.0))

    @pl.when(pl.program_id(0) == 0)
    def _():
        cnt_ref[...] = jnp.zeros_like(cnt_ref)

    tm = logits.shape[0]
    oh1 = lane_f == i1
    oh2 = lane_f == i2
    both = jnp.where(oh1 | oh2, 1.0, 0.0).astype(BF16)
    row = lax.broadcasted_iota(jnp.int32, (tm, tm), 0)
    col = lax.broadcasted_iota(jnp.int32, (tm, tm), 1)
    earlier = _dot(jnp.where(col < row, 1.0, 0.0).astype(BF16), both) + cnt_ref[0:1, :]
    rank1 = jnp.sum(jnp.where(oh1, earlier, 0.0), axis=-1, keepdims=True)
    rank2 = jnp.sum(jnp.where(oh2, earlier, 0.0), axis=-1, keepdims=True)
    cnt_ref[0:1, :] = cnt_ref[0:1, :] + jnp.sum(both.astype(F32), axis=0, keepdims=True)
    info = jnp.where(lane == 0, i1 - float(N_GROUPS),
                     jnp.where(lane == 1, i2 - float(N_GROUPS),
                               jnp.where(lane == 2, rank1, jnp.where(lane == 3, rank2, 0.0))))
    eid_ref[...] = info.astype(jnp.int32)


def _outproj(h, o_gla, o_lru, o_diff, w_out, mod3, norm2_g, rw, rb, S):
    T, D = h.shape
    tm = 256
    per_b = S // tm
    rowblk = lambda width: pl.BlockSpec((tm, width), lambda i: (i, 0))
    modblk = lambda k: pl.BlockSpec((1, 1, D), lambda i: (i // per_b, 0, k))
    full = lambda shape: pl.BlockSpec(shape, lambda i: (0,) * len(shape))
    return pl.pallas_call(
        _outproj_kernel,
        grid=(T // tm,),
        in_specs=[
            rowblk(D), rowblk(GLA_WIDTH), rowblk(LRU_WIDTH), rowblk(DIFF_WIDTH),
            full((D, D)),
            modblk(2), modblk(3), modblk(4),
            full((1, D)),
            full((D, LANES)),
            full((1, LANES)),
        ],
        out_specs=[rowblk(D), pl.BlockSpec((tm * SUBLANES, LANES), lambda i: (i, 0)), rowblk(LANES), rowblk(LANES),
                   full((SUBLANES, LANES))],
        out_shape=[
            jax.ShapeDtypeStruct((T, D), F32),
            jax.ShapeDtypeStruct((T * SUBLANES, LANES), U32),
            jax.ShapeDtypeStruct((T, LANES), jnp.int32),
            jax.ShapeDtypeStruct((T, LANES), F32),
            jax.ShapeDtypeStruct((SUBLANES, LANES), F32),
        ],
        compiler_params=_params(("arbitrary",)),
    )(h, o_gla, o_lru, o_diff, w_out, mod3, mod3, mod3, norm2_g.reshape(1, D), rw, rb)


def _dispatch(info, counts):
    T = info.shape[0]
    blk = MOE_BLK
    n_blocks = (T * TOP_K) // blk + N_EXPERTS
    expert = info[:, 0:TOP_K]
    rank = info[:, TOP_K:2 * TOP_K]
    padded = (counts + blk - 1) // blk * blk
    pends = jnp.cumsum(padded)
    pstarts = pends - padded
    onehot = expert[:, :, None] == jnp.arange(N_EXPERTS, dtype=jnp.int32)[None, None, :]
    dest = rank + jnp.sum(jnp.where(onehot, pstarts[None, None, :], 0), axis=-1)
    n_used = (pends[-1] // blk).astype(jnp.int32)
    block_idx = jnp.arange(n_blocks, dtype=jnp.int32)
    block_expert = jnp.minimum(jnp.sum((pends[None, :] <= (block_idx * blk)[:, None]).astype(jnp.int32), axis=1),
                               N_EXPERTS - 1)
    last_used = jnp.sum(jnp.where(block_idx == jnp.maximum(n_used - 1, 0), block_expert, 0))
    block_expert = jnp.where(block_idx < n_used, block_expert, last_used).astype(jnp.int32)
    following = jnp.concatenate([block_expert[1:], jnp.full((1,), -1, jnp.int32)])
    zero_block = ((block_idx >= n_used - 1) | (following != block_expert)).astype(jnp.int32)
    owner = block_expert[:, None] == jnp.arange(N_EXPERTS, dtype=jnp.int32)[None, :]
    seg_end = jnp.sum(jnp.where(owner, (pstarts + counts)[None, :], 0), axis=1)
    n_valid = jnp.where(block_idx < n_used, jnp.clip(seg_end - block_idx * blk, 0, blk), 0).astype(jnp.int32)
    total = T * TOP_K + N_EXPERTS * blk
    pad_lo = jnp.concatenate([pstarts + counts, pends[-1:]]).astype(jnp.int32)
    pad_hi = jnp.concatenate([pends, jnp.full((1,), total, pends.dtype)]).astype(jnp.int32)
    return dict(dest=dest.astype(jnp.int32), zero_block=zero_block, block_expert=block_expert,
                n_used=n_used.reshape(1), n_valid=n_valid, pad_lo=pad_lo, pad_hi=pad_hi)


def _scatter_kernel(d0_ref, d1_ref, zb_ref, plo_ref, phi_ref, src_ref, o_ref, inv_ref, zero_buf, sem, zero_sem):
    rows = src_ref.shape[0] // SUBLANES
    base = pl.program_id(0) * rows
    fill_rows = zero_buf.shape[0]
    n_blocks = o_ref.shape[0] // fill_rows

    @pl.when(pl.program_id(0) == 0)
    def _():
        zero_buf[...] = jnp.zeros_like(zero_buf)

        def for_each_fill(fn):
            def body(j, carry):
                @pl.when(zb_ref[j] == 1)
                def _():
                    fn(pltpu.make_async_copy(
                        zero_buf, o_ref.at[pl.ds(pl.multiple_of(j * fill_rows, fill_rows), fill_rows)], zero_sem))
                return carry
            lax.fori_loop(0, n_blocks, body, 0)

        for_each_fill(lambda copy: copy.start())

        def mark_range(e, carry):
            def mark(s, c):
                inv_ref[s] = -1
                return c
            lax.fori_loop(plo_ref[e], phi_ref[e], mark, 0)
            return carry

        lax.fori_loop(0, plo_ref.shape[0], mark_range, 0)
        for_each_fill(lambda copy: copy.wait())

    def row_copy(r, slot):
        return pltpu.make_async_copy(src_ref.at[pl.ds(pl.multiple_of(r * SUBLANES, SUBLANES), SUBLANES)],
                                     o_ref.at[pl.ds(pl.multiple_of(slot * SUBLANES, SUBLANES), SUBLANES)], sem)

    def issue(r, carry):
        t = base + r
        s0 = d0_ref[t]
        s1 = d1_ref[t]
        row_copy(r, s0).start()
        row_copy(r, s1).start()
        inv_ref[s0] = t * TOP_K
        inv_ref[s1] = t * TOP_K + 1
        return carry

    lax.fori_loop(0, rows, issue, 0, unroll=8)
    for _ in range(TOP_K):
        pltpu.make_async_copy(src_ref, o_ref.at[pl.ds(0, rows * SUBLANES)], sem).wait()


def _scatter_rows(plan, src):
    dest = plan["dest"]
    T = dest.shape[0]
    P = T * TOP_K + N_EXPERTS * MOE_BLK
    rows = GATHER_ROWS
    return pl.pallas_call(
        _scatter_kernel,
        grid_spec=pltpu.PrefetchScalarGridSpec(
            num_scalar_prefetch=5,
            grid=(T // rows,),
            in_specs=[pl.BlockSpec((rows * SUBLANES, LANES), lambda i, *refs: (i, 0))],
            out_specs=[pl.BlockSpec(memory_space=pl.ANY), pl.BlockSpec(memory_space=pltpu.SMEM)],
            scratch_shapes=[
                pltpu.VMEM((MOE_BLK * SUBLANES, LANES), src.dtype),
                pltpu.SemaphoreType.DMA(()),
                pltpu.SemaphoreType.DMA(()),
            ],
        ),
        out_shape=[jax.ShapeDtypeStruct((P * SUBLANES, LANES), src.dtype),
                   jax.ShapeDtypeStruct((P,), jnp.int32)],
        compiler_params=_params(("arbitrary",)),
    )(dest[:, 0], dest[:, 1], plan["zero_block"], plan["pad_lo"], plan["pad_hi"], src)


def _expert_kernel(layer, be_ref, first_ref, next_ref, slot_ref, nu_ref, nv_ref, inv_ref,
                   xs_ref, w1_hbm, w3_hbm, w2_hbm, yt_ref,
                   w1f, w3f, w2f, w1b, w3b, w2b, ybuf, sems, ysems):
    i = pl.program_id(0)
    D = w1b.shape[0]
    blk = xs_ref.shape[0] // SUBLANES

    def drain_rows(j):
        n = nv_ref[j]
        b = j % 2
        for bit in range(blk.bit_length()):
            size = (1 << bit) * SUBLANES

            @pl.when((n >> bit) & 1 == 1)
            def _():
                pltpu.make_async_copy(ybuf.at[b, pl.ds(0, size)], yt_ref.at[pl.ds(0, size)], ysems.at[b]).wait()

    def weight_copies(e, slot):
        return (pltpu.make_async_copy(w1_hbm.at[layer, e], w1f.at[slot], sems.at[slot, 0]),
                pltpu.make_async_copy(w3_hbm.at[layer, e], w3f.at[slot], sems.at[slot, 1]),
                pltpu.make_async_copy(w2_hbm.at[layer, e], w2f.at[slot], sems.at[slot, 2]))

    @pl.when(i == 0)
    def _():
        for c in weight_copies(be_ref[0], 0):
            c.start()

    @pl.when(first_ref[i] == 1)
    def _():
        slot = slot_ref[i]
        for c in weight_copies(be_ref[i], slot):
            c.wait()

        @pl.when(next_ref[i] >= 0)
        def _():
            for c in weight_copies(next_ref[i], 1 - slot):
                c.start()

        w1b[...] = w1f[slot].astype(BF16)
        w3b[...] = w3f[slot].astype(BF16)
        w2b[...] = w2f[slot].astype(BF16)

    @pl.when(i >= 2)
    def _():
        drain_rows(i - 2)

    @pl.when(i < nu_ref[0])
    def _():
        lo, hi = _unpack_bf16_pair(_load_row_tiles(xs_ref))
        lo = lo.astype(BF16)
        hi = hi.astype(BF16)
        a = _dot(lo, w1b[0:D // 2, :]) + _dot(hi, w1b[D // 2:, :])
        g = _dot(lo, w3b[0:D // 2, :]) + _dot(hi, w3b[D // 2:, :])
        hid = ((a * _sigmoid(a)) * g).astype(BF16)
        b = i % 2
        _store_row_tiles(ybuf.at[b], _pack_bf16_pair(_dot(hid, w2b[:, 0:D // 2]), _dot(hid, w2b[:, D // 2:])))

        def issue(r, carry):
            a_idx = inv_ref[i * blk + r]
            pltpu.make_async_copy(ybuf.at[b, pl.ds(pl.multiple_of(r * SUBLANES, SUBLANES), SUBLANES)],
                                  yt_ref.at[pl.ds(pl.multiple_of(a_idx * SUBLANES, SUBLANES), SUBLANES)],
                                  ysems.at[b]).start()
            return carry

        lax.fori_loop(0, nv_ref[i], issue, 0)

    @pl.when(i == pl.num_programs(0) - 1)
    def _():
        drain_rows(i - 1)
        drain_rows(i)


def _segment_plan(block_expert, n_used):
    n = block_expert.shape[0]
    idx = jnp.arange(n, dtype=jnp.int32)
    prev = jnp.concatenate([jnp.full((1,), -1, jnp.int32), block_expert[:-1]])
    first = ((block_expert != prev) & (idx < n_used[0])).astype(jnp.int32)
    slot = (jnp.cumsum(first) - 1) % 2
    later_first = jnp.where(first == 1, idx, n)
    next_idx = lax.cummin(jnp.concatenate([later_first[1:], jnp.full((1,), n, jnp.int32)]), reverse=True)
    next_expert = jnp.where(next_idx < n, block_expert[jnp.minimum(next_idx, n - 1)], -1)
    return first, next_expert.astype(jnp.int32), slot.astype(jnp.int32)


def _experts(plan, inv, xs, w1, w3, w2, layer):
    _, _, D, DE = w1.shape
    blk = MOE_BLK
    block_expert, n_used = plan["block_expert"], plan["n_used"]
    n_blocks = block_expert.shape[0]
    n_assign = plan["dest"].shape[0] * TOP_K
    first, next_expert, slot = _segment_plan(block_expert, n_used)
    rowmap = lambda i, *refs: (jnp.minimum(i, jnp.maximum(refs[4][0] - 1, 0)), 0)
    hbm = pl.BlockSpec(memory_space=pl.ANY)
    return pl.pallas_call(
        functools.partial(_expert_kernel, layer),
        grid_spec=pltpu.PrefetchScalarGridSpec(
            num_scalar_prefetch=7,
            grid=(n_blocks,),
            in_specs=[pl.BlockSpec((blk * SUBLANES, LANES), rowmap), hbm, hbm, hbm],
            out_specs=hbm,
            scratch_shapes=[
                pltpu.VMEM((2, D, DE), F32),
                pltpu.VMEM((2, D, DE), F32),
                pltpu.VMEM((2, DE, D), F32),
                pltpu.VMEM((D, DE), BF16),
                pltpu.VMEM((D, DE), BF16),
                pltpu.VMEM((DE, D), BF16),
                pltpu.VMEM((2, blk * SUBLANES, LANES), U32),
                pltpu.SemaphoreType.DMA((2, 3)),
                pltpu.SemaphoreType.DMA((2,)),
            ],
        ),
        out_shape=jax.ShapeDtypeStruct((n_assign * SUBLANES, LANES), U32),
        compiler_params=_params(("arbitrary",)),
    )(block_expert, first, next_expert, slot, n_used, plan["n_valid"], inv, xs, w1, w3, w2)


def _combine_kernel(final, yt_ref, h_ref, g2_ref, ew_ref, fg_ref, o_ref):
    tc, D = h_ref.shape

    def expert_rows(k):
        return jnp.concatenate([yt_ref[pl.ds(k * SUBLANES + s, tc, stride=TOP_K * SUBLANES), :]
                                for s in range(SUBLANES)], axis=1)

    ew = ew_ref[...]
    w0 = ew[:, 0:1]
    w1 = ew[:, 1:2]
    lo0, hi0 = _unpack_bf16_pair(expert_rows(0))
    lo1, hi1 = _unpack_bf16_pair(expert_rows(1))
    moe = jnp.concatenate([w0 * lo0 + w1 * lo1, w0 * hi0 + w1 * hi1], axis=1)
    hn = h_ref[...] + g2_ref[0] * moe
    if final:
        hn = hn * lax.rsqrt(jnp.mean(hn * hn, axis=-1, keepdims=True) + EPS) * fg_ref[...]
    o_ref[...] = hn


def _combine(yt, h, mod3, ew, final_g, S, final):
    T, D = h.shape
    tc = 256
    per_b = S // tc
    return pl.pallas_call(
        functools.partial(_combine_kernel, final),
        grid=(T // tc,),
        in_specs=[
            pl.BlockSpec((tc * TOP_K * SUBLANES, LANES), lambda i: (i, 0)),
            pl.BlockSpec((tc, D), lambda i: (i, 0)),
            pl.BlockSpec((1, 1, D), lambda i: (i // per_b, 0, 5)),
            pl.BlockSpec((tc, LANES), lambda i: (i, 0)),
            pl.BlockSpec((1, D), lambda i: (0, 0)),
        ],
        out_specs=pl.BlockSpec((tc, D), lambda i: (i, 0)),
        out_shape=jax.ShapeDtypeStruct((T, D), F32),
        compiler_params=_params(("parallel",)),
    )(yt, h, mod3, ew, final_g.reshape(1, D))


def kernel(x, c, ada_w, ada_b, norm1_g, w_in, gla_w_a2, gla_b_a, gla_norm_g, lru_conv_w, lru_conv_b,
           lru_wa, lru_ba, lru_wx, lru_bx, lru_lambda, diff_lq1, diff_lk1, diff_lq2, diff_lk2,
           diff_subln_g, rel_bias, w_out, norm2_g, router_g_w, router_g_b, router_e_w, router_e_b,
           moe_w1, moe_w3, moe_w2, final_g):
    B, S, D = x.shape
    T = B * S
    L = ada_w.shape[0]
    h = x.reshape(T, D)
    mod = _ada_mod(c, ada_w, ada_b)
    bias = _bias_tiles(rel_bias)
    for l in range(L):
        mod3 = mod[l][:, None, :]
        proj = _inproj(h, mod3, norm1_g[l], _permute_w_in(w_in[l]), S)
        wa2_pad = jnp.concatenate(
            [gla_w_a2[l], jnp.zeros((LANES - GLA_LOWRANK, GLA_KEY_WIDTH), F32)], axis=0).astype(BF16)
        o_gla = _gla(proj, wa2_pad, gla_b_a[l], gla_norm_g[l], B, S)
        w_gates = jnp.concatenate([_block_diag(lru_wa[l]), _block_diag(lru_wx[l])], axis=1).astype(BF16)
        b_gates = jnp.concatenate([lru_ba[l], lru_bx[l]]).reshape(1, 2 * LRU_WIDTH)
        o_lru = _lru(proj.reshape(B, S, PROJ_WIDTH), lru_conv_w[l], lru_conv_b[l], w_gates, b_gates,
                     lru_lambda[l]).reshape(T, LRU_WIDTH)
        lqk = jnp.stack([diff_lq1[l], diff_lk1[l], diff_lq2[l], diff_lk2[l]], axis=0)
        o_diff = _diff_attention(proj, bias, lqk, diff_subln_g[l], l, B, S)
        rw = jnp.concatenate(
            [router_g_w[l], router_e_w[l], jnp.zeros((D, LANES - N_GROUPS - N_EXPERTS), F32)], axis=1).astype(BF16)
        rb = jnp.concatenate(
            [router_g_b[l], router_e_b[l], jnp.zeros((LANES - N_GROUPS - N_EXPERTS,), F32)]).reshape(1, LANES)
        h, u2, info, ew, cnt = _outproj(h, o_gla, o_lru, o_diff, w_out[l].astype(BF16), mod3, norm2_g[l], rw, rb, S)
        counts = cnt[0, N_GROUPS:N_GROUPS + N_EXPERTS].astype(jnp.int32)
        plan = _dispatch(info, counts)
        xs, inv = _scatter_rows(plan, u2)
        yt = _experts(plan, inv, xs, moe_w1, moe_w3, moe_w2, l)
        h = _combine(yt, h, mod3, ew, final_g, S, final=(l == L - 1))
    return h.reshape(B, S, D)
```

```python
import functools
import math

import jax
import jax.numpy as jnp
from jax import lax
from jax.experimental import pallas as pl
from jax.experimental.pallas import tpu as pltpu

F32 = jnp.float32
BF16 = jnp.bfloat16
U32 = jnp.uint32

EPS = 1e-6
LOG2E = math.log2(math.e)
CHUNK = 64

GLA_DV = 128
GLA_DK = 64
GLA_HEADS = 6
GLA_WIDTH = GLA_HEADS * GLA_DV
GLA_KEY_WIDTH = GLA_HEADS * GLA_DK
GLA_LOWRANK = 16
GLA_TAU = 16.0

LRU_WIDTH = 512
LRU_BLOCKS = 8
LRU_BLOCK_DIM = LRU_WIDTH // LRU_BLOCKS
CONV_WIDTH = 4
LRU_C = 8.0

DIFF_DH = 64
DIFF_DV = 128
DIFF_HEADS = 6
DIFF_WIDTH = DIFF_HEADS * DIFF_DV

REL_BUCKETS = 32
REL_MAX_DIST = 128

N_GROUPS = 8
EXPERTS_PER_GROUP = 8
N_EXPERTS = 64
TOP_K = 2

LANES = 128
SUBLANES = 8
VMEM_LIMIT = 56 * 1024 * 1024

COL_GV = 0
COL_GOG = 768
COL_DQ = 1536
COL_DK = 2304
COL_DV = 3072
COL_GQ = 3840
COL_GK = 4224
COL_LY = 4608
COL_LX = 5120
COL_GA = 5632
PROJ_WIDTH = 5760

ATT_TILE = 512
GLA_TILE = 256
LRU_TILE = 256
MOE_BLK = 256
GATHER_ROWS = 512


def _params(sem, vmem=VMEM_LIMIT):
    return pltpu.CompilerParams(dimension_semantics=sem, vmem_limit_bytes=vmem)


def _sigmoid(x):
    return 0.5 * jnp.tanh(0.5 * x) + 0.5


def _softplus(x):
    return jnp.maximum(x, 0.0) + jnp.log1p(jnp.exp(-jnp.abs(x)))


def _dot(a, b):
    return jnp.dot(a, b, preferred_element_type=F32)


def _dot_nt(a, b):
    return lax.dot_general(a, b, (((1,), (1,)), ((), ())), preferred_element_type=F32)


def _dot_tn(a, b):
    return lax.dot_general(a, b, (((0,), (0,)), ((), ())), preferred_element_type=F32)


def _pack_bf16_pair(lo, hi):
    lo_bits = lax.bitcast_convert_type(lo.astype(BF16).astype(F32), U32)
    hi_bits = lax.bitcast_convert_type(hi.astype(BF16).astype(F32), U32)
    return (hi_bits & jnp.uint32(0xFFFF0000)) | (lo_bits >> 16)


def _unpack_bf16_pair(w):
    lo = lax.bitcast_convert_type(w << 16, F32)
    hi = lax.bitcast_convert_type(w & jnp.uint32(0xFFFF0000), F32)
    return lo, hi


def _store_row_tiles(ref, words):
    rows = words.shape[0]
    for s in range(SUBLANES):
        ref[pl.ds(s, rows, stride=SUBLANES), :] = words[:, s * LANES:(s + 1) * LANES]


def _load_row_tiles(ref):
    rows = ref.shape[0] // SUBLANES
    return jnp.concatenate([ref[pl.ds(s, rows, stride=SUBLANES), :] for s in range(SUBLANES)], axis=1)


def _ada_kernel(c_ref, w_ref, b_ref, o_ref):
    c = c_ref[...]
    s = c * _sigmoid(c)
    o_ref[0] = _dot(s.astype(BF16), w_ref[0].astype(BF16)) + b_ref[0]


def _ada_mod(c, ada_w, ada_b):
    L, D, N = ada_w.shape
    B = c.shape[0]
    tn = 1024
    return pl.pallas_call(
        _ada_kernel,
        grid=(L, N // tn),
        in_specs=[
            pl.BlockSpec((B, D), lambda l, j: (0, 0)),
            pl.BlockSpec((1, D, tn), lambda l, j: (l, 0, j)),
            pl.BlockSpec((1, 1, tn), lambda l, j: (l, 0, j)),
        ],
        out_specs=pl.BlockSpec((1, B, tn), lambda l, j: (l, 0, j)),
        out_shape=jax.ShapeDtypeStruct((L, B, N), F32),
        compiler_params=_params(("parallel", "parallel")),
    )(c, ada_w, ada_b.reshape(L, 1, N))


def _modulated_norm(x, g, sc, sh):
    ms = jnp.mean(x * x, axis=-1, keepdims=True)
    return (x * lax.rsqrt(ms + EPS) * g) * (1.0 + sc) + sh


def _inproj_kernel(h_ref, sh_ref, sc_ref, g_ref, w_ref, o_ref, u_scr):
    @pl.when(pl.program_id(1) == 0)
    def _():
        u = _modulated_norm(h_ref[...], g_ref[...], sc_ref[0], sh_ref[0])
        u_scr[...] = u.astype(BF16)

    o_ref[...] = _dot(u_scr[...], w_ref[...]).astype(o_ref.dtype)


def _inproj(h, mod3, norm_g, w_perm, layer, S):
    T, D = h.shape
    N = w_perm.shape[2]
    tm, tn = 512, 1152
    per_b = S // tm
    return pl.pallas_call(
        _inproj_kernel,
        grid=(T // tm, N // tn),
        in_specs=[
            pl.BlockSpec((tm, D), lambda i, j: (i, 0)),
            pl.BlockSpec((1, 1, D), lambda i, j: (i // per_b, 0, 0)),
            pl.BlockSpec((1, 1, D), lambda i, j: (i // per_b, 0, 1)),
            pl.BlockSpec((1, D), lambda i, j: (0, 0)),
            pl.BlockSpec((None, D, tn), lambda i, j: (layer, 0, j)),
        ],
        out_specs=pl.BlockSpec((tm, tn), lambda i, j: (i, j)),
        out_shape=jax.ShapeDtypeStruct((T, N), BF16),
        scratch_shapes=[pltpu.VMEM((tm, D), BF16)],
        compiler_params=_params(("parallel", "arbitrary")),
    )(h, mod3, mod3, norm_g.reshape(1, D), w_perm)


W_IN_SEGMENTS = ((COL_GQ, 0, 384), (COL_GK, 384, 384), (COL_GV, 768, 768), (COL_GOG, 1536, 768),
                 (COL_GA, 2304, GLA_LOWRANK), (COL_LY, 2320, 512), (COL_LX, 2832, 512),
                 (COL_DQ, 3344, 768), (COL_DK, 4112, 768), (COL_DV, 4880, 768))


def _relayout_kernel(w_ref, o_ref):
    x = w_ref[0]
    for dst, src, width in W_IN_SEGMENTS:
        o_ref[0, :, dst:dst + width] = x[:, src:src + width].astype(BF16)
    pad = slice(COL_GA + GLA_LOWRANK, COL_GA + LANES)
    o_ref[0, :, pad] = jnp.zeros((x.shape[0], LANES - GLA_LOWRANK), BF16)


def _permute_w_in(w):
    L, D, N = w.shape
    rt = 256
    return pl.pallas_call(
        _relayout_kernel,
        grid=(L, D // rt),
        in_specs=[pl.BlockSpec((1, rt, N), lambda l, i: (l, i, 0))],
        out_specs=pl.BlockSpec((1, rt, PROJ_WIDTH), lambda l, i: (l, i, 0)),
        out_shape=jax.ShapeDtypeStruct((L, D, PROJ_WIDTH), BF16),
        compiler_params=_params(("parallel", "parallel")),
    )(w)


def _gla_kernel(q_ref, k_ref, v_ref, og_ref, alr_ref, wa2_ref, ba_ref, ng_ref, o_ref, st_ref):
    tb = q_ref.shape[0]
    n_chunks = tb // CHUNK

    @pl.when(pl.program_id(1) == 0)
    def _():
        st_ref[...] = jnp.zeros_like(st_ref)

    row = lax.broadcasted_iota(jnp.int32, (tb, tb), 0)
    col = lax.broadcasted_iota(jnp.int32, (tb, tb), 1)
    same_chunk = (row // CHUNK) == (col // CHUNK)
    causal = col <= row
    tril = jnp.where(same_chunk & causal, 1.0, 0.0).astype(BF16)
    chunk_ones = jnp.where(same_chunk, 1.0, 0.0).astype(BF16)
    lane = lax.broadcasted_iota(jnp.int32, (1, LANES), 1)
    half_masks = (lane < GLA_DK, lane >= GLA_DK)

    alr = alr_ref[...]
    for p in range(GLA_HEADS // 2):
        cs = slice(p * LANES, (p + 1) * LANES)
        z = _dot(alr, wa2_ref[:, cs]) + ba_ref[:, cs]
        la = (jnp.minimum(z, 0.0) - jnp.log1p(jnp.exp(-jnp.abs(z)))) * (1.0 / GLA_TAU)
        la_hi = la.astype(BF16)
        la_lo = (la - la_hi.astype(F32)).astype(BF16)
        G = _dot(tril, la_hi) + _dot(tril, la_lo)
        Gl = _dot(chunk_ones, la_hi) + _dot(chunk_ones, la_lo)
        eG = jnp.exp(G)
        enG = jnp.exp(-G)
        q = q_ref[:, cs].astype(F32) * (GLA_DK ** -0.5)
        k = k_ref[:, cs].astype(F32)
        qf = q * eG
        qb = q * enG
        kf = (k * eG).astype(BF16)
        kb = (k * enG).astype(BF16)
        kd = k * jnp.exp(Gl - G)
        for hh in range(2):
            head = 2 * p + hh
            m = half_masks[hh]
            vs = slice(head * GLA_DV, (head + 1) * GLA_DV)
            qf_h = jnp.where(m, qf, 0.0).astype(BF16)
            qb_h = jnp.where(m, qb, 0.0).astype(BF16)
            kd_h = jnp.where(m, kd, 0.0).astype(BF16)
            v_h = v_ref[:, vs]
            a_f = _dot_nt(qf_h, kb)
            a_b = _dot_nt(qb_h, kf)
            attn = jnp.where(same_chunk, jnp.where(causal, a_f, a_b), 0.0)
            o_intra = _dot(attn.astype(BF16), v_h)
            st = st_ref[head]
            inter = []
            for c in range(n_chunks):
                rs = slice(c * CHUNK, (c + 1) * CHUNK)
                inter.append(_dot_nt(qf_h[rs], st.astype(BF16)))
                decay = jnp.exp(Gl[c * CHUNK:c * CHUNK + 1, :])
                st = st * decay + _dot_tn(v_h[rs], kd_h[rs])
            st_ref[head] = st
            o = o_intra + jnp.concatenate(inter, axis=0)
            o = o * lax.rsqrt(jnp.mean(o * o, axis=-1, keepdims=True) + EPS)
            og = og_ref[:, vs].astype(F32)
            o_ref[:, vs] = (o * ng_ref[:, vs] * (og * _sigmoid(og))).astype(o_ref.dtype)


def _gla(proj, wa2_pad, b_a, norm_g, B, S):
    T = proj.shape[0]
    tb = GLA_TILE
    nt = S // tb
    row = lambda b, i: b * nt + i
    return pl.pallas_call(
        _gla_kernel,
        grid=(B, nt),
        in_specs=[
            pl.BlockSpec((tb, GLA_KEY_WIDTH), lambda b, i: (row(b, i), COL_GQ // GLA_KEY_WIDTH)),
            pl.BlockSpec((tb, GLA_KEY_WIDTH), lambda b, i: (row(b, i), COL_GK // GLA_KEY_WIDTH)),
            pl.BlockSpec((tb, GLA_WIDTH), lambda b, i: (row(b, i), COL_GV // GLA_WIDTH)),
            pl.BlockSpec((tb, GLA_WIDTH), lambda b, i: (row(b, i), COL_GOG // GLA_WIDTH)),
            pl.BlockSpec((tb, LANES), lambda b, i: (row(b, i), COL_GA // LANES)),
            pl.BlockSpec((LANES, GLA_KEY_WIDTH), lambda b, i: (0, 0)),
            pl.BlockSpec((1, GLA_KEY_WIDTH), lambda b, i: (0, 0)),
            pl.BlockSpec((1, GLA_WIDTH), lambda b, i: (0, 0)),
        ],
        out_specs=pl.BlockSpec((tb, GLA_WIDTH), lambda b, i: (row(b, i), 0)),
        out_shape=jax.ShapeDtypeStruct((T, GLA_WIDTH), BF16),
        scratch_shapes=[pltpu.VMEM((GLA_HEADS, GLA_DV, LANES), F32)],
        compiler_params=_params(("parallel", "arbitrary")),
    )(proj, proj, proj, proj, proj, wa2_pad, b_a.reshape(1, -1), norm_g.reshape(1, -1))


def _lru_kernel(y_ref, x_ref, cw_ref, cb_ref, wg_ref, bg_ref, lam_ref, o_ref, *scratch):
    B, ts, W = x_ref.shape
    n_planes = W // LANES
    a_scr = scratch[0:n_planes]
    b_scr = scratch[n_planes:2 * n_planes]
    h_scr = scratch[2 * n_planes:3 * n_planes]
    xc_scr, tail_scr, carry_scr = scratch[3 * n_planes:]

    @pl.when(pl.program_id(0) == 0)
    def _():
        tail_scr[...] = jnp.zeros_like(tail_scr)
        carry_scr[...] = jnp.zeros_like(carry_scr)

    cw = cw_ref[...]
    cb = cb_ref[...]
    sp = _softplus(-lam_ref[...])
    row8 = lax.broadcasted_iota(jnp.int32, (8, W), 0)
    for b in range(B):
        x = x_ref[b].astype(F32)
        tail = tail_scr[b]
        xc = cb + cw[CONV_WIDTH - 1:CONV_WIDTH, :] * x
        head = cb + cw[CONV_WIDTH - 1:CONV_WIDTH, :] * x[0:8]
        for d in range(1, CONV_WIDTH):
            wd = cw[CONV_WIDTH - 1 - d:CONV_WIDTH - d, :]
            xr = pltpu.roll(x, d, 0)
            xc = xc + wd * xr
            head = head + wd * jnp.where(row8 < d, pltpu.roll(tail, d, 0), xr[0:8])
        tail_scr[b] = x[ts - 8:ts]
        xc_scr[...] = xc
        xc_scr[0:8] = head
        xc = xc_scr[...]
        gates = _sigmoid(_dot(xc.astype(BF16), wg_ref[...]) + bg_ref[...])
        r = gates[:, :W]
        ig = gates[:, W:]
        log_a = (-LRU_C) * r * sp
        a = jnp.exp(log_a)
        b_in = jnp.sqrt(-jnp.tanh(log_a) * (a * a + 1.0)) * (ig * xc)
        rows = slice(b * ts, (b + 1) * ts)
        for k in range(n_planes):
            a_scr[k][rows] = a[:, k * LANES:(k + 1) * LANES]
            b_scr[k][rows] = b_in[:, k * LANES:(k + 1) * LANES]

    def step(t, hs):
        idx = pl.ds(t, B, stride=ts)
        out = []
        for k in range(n_planes):
            hk = a_scr[k][idx, :] * hs[k] + b_scr[k][idx, :]
            h_scr[k][idx, :] = hk
            out.append(hk)
        return tuple(out)

    hs = lax.fori_loop(0, ts, step, tuple(carry_scr[k] for k in range(n_planes)), unroll=8)
    for k in range(n_planes):
        carry_scr[k] = hs[k]

    for b in range(B):
        rows = slice(b * ts, (b + 1) * ts)
        y = y_ref[b].astype(F32)
        gelu = 0.5 * y * (1.0 + jnp.tanh(math.sqrt(2.0 / math.pi) * (y + 0.044715 * (y * y * y))))
        h = jnp.concatenate([h_scr[k][rows] for k in range(n_planes)], axis=1)
        o_ref[b] = (h * gelu).astype(o_ref.dtype)


def _lru(proj3, conv_w, conv_b, w_gates, b_gates, lam):
    B, S, _ = proj3.shape
    W = LRU_WIDTH
    ts = LRU_TILE
    n_planes = W // LANES
    full = lambda shape: pl.BlockSpec(shape, lambda i: (0,) * len(shape))
    return pl.pallas_call(
        _lru_kernel,
        grid=(S // ts,),
        in_specs=[
            pl.BlockSpec((B, ts, W), lambda i: (0, i, COL_LY // W)),
            pl.BlockSpec((B, ts, W), lambda i: (0, i, COL_LX // W)),
            full((CONV_WIDTH, W)),
            full((1, W)),
            full((W, 2 * W)),
            full((1, 2 * W)),
            full((1, W)),
        ],
        out_specs=pl.BlockSpec((B, ts, W), lambda i: (0, i, 0)),
        out_shape=jax.ShapeDtypeStruct((B, S, W), BF16),
        scratch_shapes=(
            [pltpu.VMEM((B * ts, LANES), F32) for _ in range(3 * n_planes)]
            + [pltpu.VMEM((ts, W), F32), pltpu.VMEM((B, 8, W), F32), pltpu.VMEM((n_planes, B, LANES), F32)]),
        compiler_params=_params(("arbitrary",)),
    )(proj3, proj3, conv_w, conv_b.reshape(1, W), w_gates, b_gates, lam.reshape(1, W))


def _block_diag(w):
    n, d, _ = w.shape
    eye = jnp.eye(n, dtype=w.dtype)
    return (eye[:, None, :, None] * w[:, :, None, :]).reshape(n * d, n * d)


def _t5_bucket(rel):
    nb = REL_BUCKETS // 2
    ret = (rel > 0).astype(jnp.int32) * nb
    n = jnp.abs(rel)
    max_exact = nb // 2
    nf = jnp.maximum(n, 1).astype(jnp.float32)
    large = max_exact + (jnp.log(nf / max_exact) / math.log(REL_MAX_DIST / max_exact)
                         * (nb - max_exact)).astype(jnp.int32)
    large = jnp.minimum(large, nb - 1)
    return ret + jnp.where(n < max_exact, n, large)


def _bias_kernel(bucket_ref, table_ref, o_ref):
    h = pl.program_id(0)
    bucket = bucket_ref[0]
    acc = jnp.full(bucket.shape, -1e30, F32)
    for b in range(REL_BUCKETS):
        acc = jnp.where(bucket == b, table_ref[b, h] * LOG2E, acc)
    o_ref[0, 0] = acc


def _bias_tiles(rel_bias):
    t = ATT_TILE
    H = rel_bias.shape[1]
    qp = jnp.arange(t, dtype=jnp.int32)[:, None]
    kp = jnp.arange(t, dtype=jnp.int32)[None, :]
    mask = (kp // CHUNK) <= (qp // CHUNK)
    half = REL_BUCKETS // 2
    per_distance = _t5_bucket(-jnp.arange(2 * t, dtype=jnp.int32))
    edges = jnp.sum((per_distance[None, :] < jnp.arange(1, half, dtype=jnp.int32)[:, None]).astype(jnp.int32), axis=1)

    def bucket_2d(rel):
        passed = jnp.sum((jnp.abs(rel)[None] >= edges[:, None, None]).astype(jnp.int32), axis=0)
        return (rel > 0).astype(jnp.int32) * half + passed

    buckets = jnp.stack([jnp.where(mask, bucket_2d(kp - qp), REL_BUCKETS), bucket_2d(kp - t - qp)], axis=0)
    table = rel_bias.astype(F32)
    tiles = pl.pallas_call(
        _bias_kernel,
        grid=(H, 2),
        in_specs=[
            pl.BlockSpec((1, t, t), lambda h, k: (k, 0, 0)),
            pl.BlockSpec(memory_space=pltpu.SMEM),
        ],
        out_specs=pl.BlockSpec((1, 1, t, t), lambda h, k: (h, k, 0, 0)),
        out_shape=jax.ShapeDtypeStruct((H, 2, t, t), F32),
        compiler_params=_params(("parallel", "parallel")),
    )(buckets, table)
    far_bucket = _t5_bucket(jnp.full((1,), -t - 1, jnp.int32))
    far = jnp.sum(jnp.where(jnp.arange(REL_BUCKETS)[:, None] == far_bucket, table, 0.0), axis=0)
    return tiles, jnp.broadcast_to((far * LOG2E)[:, None, None], (H, 1, t))


def _diff_kernel(lam_init, q_ref, k_ref, v_ref, bias_ref, far_ref, lqk_ref, g_ref, o_ref,
                 qs_scr, m_scr, l_scr, acc_scr):
    i = pl.program_id(2)
    t = q_ref.shape[0]
    hq = t // 2
    lane = lax.broadcasted_iota(jnp.int32, (1, LANES), 1)
    q = q_ref[...].astype(F32) * (LOG2E * DIFF_DH ** -0.5)
    for half in range(2):
        qh = q[half * hq:(half + 1) * hq]
        qs_scr[(2 * half) * hq:(2 * half + 1) * hq] = jnp.where(lane < DIFF_DH, qh, 0.0).astype(BF16)
        qs_scr[(2 * half + 1) * hq:(2 * half + 2) * hq] = jnp.where(lane >= DIFF_DH, qh, 0.0).astype(BF16)
    m_scr[...] = jnp.full_like(m_scr, -1e30)
    l_scr[...] = jnp.zeros_like(l_scr)
    acc_scr[...] = jnp.zeros_like(acc_scr)

    def tile(rows, ks, bias):
        s = _dot_nt(qs_scr[rows, :], k_ref[ks, :]) + bias
        groups = [s[:, c * LANES:(c + 1) * LANES] for c in range(s.shape[1] // LANES)]
        mx = functools.reduce(jnp.maximum, groups)
        m_prev = m_scr[rows, :]
        m_new = jnp.maximum(m_prev, jnp.max(mx, axis=-1, keepdims=True))
        alpha = jnp.exp2(m_prev - m_new)
        ps = [jnp.exp2(g - m_new) for g in groups]
        l_scr[rows, :] = alpha * l_scr[rows, :] + functools.reduce(jnp.add, ps)
        p = jnp.concatenate(ps, axis=1).astype(BF16)
        acc_scr[rows, :] = alpha * acc_scr[rows, :] + _dot(p, v_ref[ks, :])
        m_scr[rows, :] = m_new

    def stacked(b, half):
        bh = b[half * hq:(half + 1) * hq]
        return [bh, bh]

    all_rows = slice(0, 2 * t)

    def far_body(j, carry):
        tile(all_rows, pl.ds(pl.multiple_of(j * t, t), t), far_ref[0])
        return carry

    lax.fori_loop(0, jnp.maximum(i - 1, 0), far_body, 0)

    @pl.when(i >= 1)
    def _():
        b = bias_ref[0, 1]
        tile(all_rows, pl.ds(pl.multiple_of((i - 1) * t, t), t),
             jnp.concatenate(stacked(b, 0) + stacked(b, 1), axis=0))

    b = bias_ref[0, 0]
    diag0 = pl.multiple_of(i * t, t)
    tile(slice(0, t), pl.ds(diag0, hq), jnp.concatenate(stacked(b[:, 0:hq], 0), axis=0))
    tile(slice(t, 2 * t), pl.ds(diag0, t), jnp.concatenate(stacked(b, 1), axis=0))

    lqk = lqk_ref[...]
    lam = (jnp.exp(jnp.sum(lqk[0:1] * lqk[1:2], axis=-1, keepdims=True))
           - jnp.exp(jnp.sum(lqk[2:3] * lqk[3:4], axis=-1, keepdims=True)) + lam_init)
    o = acc_scr[...] / jnp.sum(l_scr[...], axis=-1, keepdims=True)
    o = jnp.concatenate([o[0:hq] - lam * o[hq:t], o[t:t + hq] - lam * o[t + hq:2 * t]], axis=0)
    o = o * lax.rsqrt(jnp.mean(o * o, axis=-1, keepdims=True) + EPS)
    o_ref[...] = (o * g_ref[...] * (1.0 - lam_init)).astype(o_ref.dtype)


def _diff_attention(proj, bias, lqk, subln_g, layer_idx, B, S):
    T = proj.shape[0]
    t = ATT_TILE
    nq = S // t
    tiles, far = bias
    lam_init = 0.8 - 0.6 * math.exp(-0.3 * layer_idx)
    return pl.pallas_call(
        functools.partial(_diff_kernel, lam_init),
        grid=(B, DIFF_HEADS, nq),
        in_specs=[
            pl.BlockSpec((t, LANES), lambda b, h, i: (b * nq + i, COL_DQ // LANES + h)),
            pl.BlockSpec((S, LANES), lambda b, h, i: (b, COL_DK // LANES + h)),
            pl.BlockSpec((S, LANES), lambda b, h, i: (b, COL_DV // LANES + h)),
            pl.BlockSpec((1, 2, t, t), lambda b, h, i: (h, 0, 0, 0)),
            pl.BlockSpec((1, 1, t), lambda b, h, i: (h, 0, 0)),
            pl.BlockSpec((4, DIFF_DH), lambda b, h, i: (0, 0)),
            pl.BlockSpec((1, DIFF_DV), lambda b, h, i: (0, 0)),
        ],
        out_specs=pl.BlockSpec((t, LANES), lambda b, h, i: (b * nq + i, h)),
        out_shape=jax.ShapeDtypeStruct((T, DIFF_WIDTH), BF16),
        scratch_shapes=[
            pltpu.VMEM((2 * t, LANES), BF16),
            pltpu.VMEM((2 * t, LANES), F32),
            pltpu.VMEM((2 * t, LANES), F32),
            pltpu.VMEM((2 * t, DIFF_DV), F32),
        ],
        compiler_params=_params(("parallel", "parallel", "arbitrary")),
    )(proj, proj, proj, tiles, far, lqk, subln_g.reshape(1, DIFF_DV))


def _outproj_kernel(h_ref, og_ref, ol_ref, od_ref, w_ref, g1_ref, sh2_ref, sc2_ref, n2_ref, rw_ref, rb_ref,
                    hn_ref, u2_ref, eid_ref, ew_ref, cnt_ref):
    D = h_ref.shape[1]
    acc = _dot(og_ref[...], w_ref[0:GLA_WIDTH, :])
    acc += _dot(ol_ref[...], w_ref[GLA_WIDTH:GLA_WIDTH + LRU_WIDTH, :])
    acc += _dot(od_ref[...], w_ref[GLA_WIDTH + LRU_WIDTH:, :])
    hn = h_ref[...] + g1_ref[0] * acc
    hn_ref[...] = hn
    u2 = _modulated_norm(hn, n2_ref[...], sc2_ref[0], sh2_ref[0])
    _store_row_tiles(u2_ref, _pack_bf16_pair(u2[:, :D // 2], u2[:, D // 2:]))

    logits = _dot(u2.astype(BF16), rw_ref[...]) + rb_ref[...]
    lane = lax.broadcasted_iota(jnp.int32, logits.shape, 1)
    lane_f = lane.astype(F32)
    neg = jnp.float32(-jnp.inf)
    gmask = lane < N_GROUPS
    gl = jnp.where(gmask, logits, neg)
    gmax = jnp.max(gl, axis=-1, keepdims=True)
    gidx = jnp.min(jnp.where(gl == gmax, lane_f, float(LANES)), axis=-1, keepdims=True)
    g_w = 1.0 / jnp.sum(jnp.where(gmask, jnp.exp(gl - gmax), 0.0), axis=-1, keepdims=True)
    egroup = ((lane - N_GROUPS) >> 3).astype(F32)
    emask = (lane >= N_GROUPS) & (lane < N_GROUPS + N_EXPERTS) & (egroup == gidx)
    el = jnp.where(emask, logits, neg)
    v1 = jnp.max(el, axis=-1, keepdims=True)
    i1 = jnp.min(jnp.where(el == v1, lane_f, float(LANES)), axis=-1, keepdims=True)
    el2 = jnp.where(lane_f == i1, neg, el)
    v2 = jnp.max(el2, axis=-1, keepdims=True)
    i2 = jnp.min(jnp.where(el2 == v2, lane_f, float(LANES)), axis=-1, keepdims=True)
    e21 = jnp.exp(v2 - v1)
    w1 = g_w / (1.0 + e21)
    w2 = g_w * e21 / (1.0 + e21)
    ew_ref[...] = jnp.where(lane == 0, w1, jnp.where(lane == 1, w2, 0.0))

    @pl.when(pl.program_id(0) == 0)
    def _():
        cnt_ref[...] = jnp.zeros_like(cnt_ref)

    tm = logits.shape[0]
    oh1 = lane_f == i1
    oh2 = lane_f == i2
    both = jnp.where(oh1 | oh2, 1.0, 0.0).astype(BF16)
    row = lax.broadcasted_iota(jnp.int32, (tm, tm), 0)
    col = lax.broadcasted_iota(jnp.int32, (tm, tm), 1)
    earlier = _dot(jnp.where(col < row, 1.0, 0.0).astype(BF16), both) + cnt_ref[0:1, :]
    rank1 = jnp.sum(jnp.where(oh1, earlier, 0.0), axis=-1, keepdims=True)
    rank2 = jnp.sum(jnp.where(oh2, earlier, 0.0), axis=-1, keepdims=True)
    cnt_ref[0:1, :] = cnt_ref[0:1, :] + jnp.sum(both.astype(F32), axis=0, keepdims=True)
    info = jnp.where(lane == 0, i1 - float(N_GROUPS),
                     jnp.where(lane == 1, i2 - float(N_GROUPS),
                               jnp.where(lane == 2, rank1, jnp.where(lane == 3, rank2, 0.0))))
    eid_ref[...] = info.astype(jnp.int32)


def _outproj(h, o_gla, o_lru, o_diff, w_out, layer, mod3, norm2_g, rw, rb, S):
    T, D = h.shape
    tm = 256
    per_b = S // tm
    rowblk = lambda width: pl.BlockSpec((tm, width), lambda i: (i, 0))
    modblk = lambda k: pl.BlockSpec((1, 1, D), lambda i: (i // per_b, 0, k))
    full = lambda shape: pl.BlockSpec(shape, lambda i: (0,) * len(shape))
    return pl.pallas_call(
        _outproj_kernel,
        grid=(T // tm,),
        in_specs=[
            rowblk(D), rowblk(GLA_WIDTH), rowblk(LRU_WIDTH), rowblk(DIFF_WIDTH),
            pl.BlockSpec((None, D, D), lambda i: (layer, 0, 0)),
            modblk(2), modblk(3), modblk(4),
            full((1, D)),
            full((D, LANES)),
            full((1, LANES)),
        ],
        out_specs=[rowblk(D), pl.BlockSpec((tm * SUBLANES, LANES), lambda i: (i, 0)), rowblk(LANES), rowblk(LANES),
                   full((SUBLANES, LANES))],
        out_shape=[
            jax.ShapeDtypeStruct((T, D), F32),
            jax.ShapeDtypeStruct((T * SUBLANES, LANES), U32),
            jax.ShapeDtypeStruct((T, LANES), jnp.int32),
            jax.ShapeDtypeStruct((T, LANES), F32),
            jax.ShapeDtypeStruct((SUBLANES, LANES), F32),
        ],
        compiler_params=_params(("arbitrary",)),
    )(h, o_gla, o_lru, o_diff, w_out, mod3, mod3, mod3, norm2_g.reshape(1, D), rw, rb)


def _dispatch(info, counts):
    T = info.shape[0]
    blk = MOE_BLK
    n_blocks = (T * TOP_K) // blk + N_EXPERTS
    expert = info[:, 0:TOP_K]
    rank = info[:, TOP_K:2 * TOP_K]
    padded = (counts + blk - 1) // blk * blk
    pends = jnp.cumsum(padded)
    pstarts = pends - padded
    ustarts = jnp.cumsum(counts) - counts
    onehot = expert[:, :, None] == jnp.arange(N_EXPERTS, dtype=jnp.int32)[None, None, :]
    dest = rank + jnp.sum(jnp.where(onehot, pstarts[None, None, :], 0), axis=-1)
    packed = rank + jnp.sum(jnp.where(onehot, ustarts[None, None, :], 0), axis=-1)
    n_used = (pends[-1] // blk).astype(jnp.int32)
    block_idx = jnp.arange(n_blocks, dtype=jnp.int32)
    block_expert = jnp.minimum(jnp.sum((pends[None, :] <= (block_idx * blk)[:, None]).astype(jnp.int32), axis=1),
                               N_EXPERTS - 1)
    last_used = jnp.sum(jnp.where(block_idx == jnp.maximum(n_used - 1, 0), block_expert, 0))
    block_expert = jnp.where(block_idx < n_used, block_expert, last_used).astype(jnp.int32)
    following = jnp.concatenate([block_expert[1:], jnp.full((1,), -1, jnp.int32)])
    zero_block = ((block_idx >= n_used - 1) | (following != block_expert)).astype(jnp.int32)
    owner = block_expert[:, None] == jnp.arange(N_EXPERTS, dtype=jnp.int32)[None, :]
    seg_end = jnp.sum(jnp.where(owner, (pstarts + counts)[None, :], 0), axis=1)
    n_valid = jnp.where(block_idx < n_used, jnp.clip(seg_end - block_idx * blk, 0, blk), 0).astype(jnp.int32)
    packed_base = (jnp.sum(jnp.where(owner, (ustarts - pstarts)[None, :], 0), axis=1) + block_idx * blk)
    packed_base = jnp.where(n_valid > 0, packed_base, 0).astype(jnp.int32)
    return dict(dest=dest.astype(jnp.int32), packed=packed.astype(jnp.int32), zero_block=zero_block,
                block_expert=block_expert, n_used=n_used.reshape(1), n_valid=n_valid, packed_base=packed_base)


def _scatter_kernel(d0_ref, d1_ref, p0_ref, p1_ref, zb_ref, src_ref, o_ref, inv_ref, zero_buf, sem, zero_sem):
    rows = src_ref.shape[0] // SUBLANES
    base = pl.program_id(0) * rows
    fill_rows = zero_buf.shape[0]
    n_blocks = o_ref.shape[0] // fill_rows

    @pl.when(pl.program_id(0) == 0)
    def _():
        zero_buf[...] = jnp.zeros_like(zero_buf)

        def for_each_fill(fn):
            def body(j, carry):
                @pl.when(zb_ref[j] == 1)
                def _():
                    fn(pltpu.make_async_copy(
                        zero_buf, o_ref.at[pl.ds(pl.multiple_of(j * fill_rows, fill_rows), fill_rows)], zero_sem))
                return carry
            lax.fori_loop(0, n_blocks, body, 0)

        for_each_fill(lambda copy: copy.start())
        for_each_fill(lambda copy: copy.wait())

    def row_copy(r, slot):
        return pltpu.make_async_copy(src_ref.at[pl.ds(pl.multiple_of(r * SUBLANES, SUBLANES), SUBLANES)],
                                     o_ref.at[pl.ds(pl.multiple_of(slot * SUBLANES, SUBLANES), SUBLANES)], sem)

    def issue(r, carry):
        t = base + r
        s0 = d0_ref[t]
        s1 = d1_ref[t]
        row_copy(r, s0).start()
        row_copy(r, s1).start()
        inv_ref[p0_ref[t]] = t * TOP_K
        inv_ref[p1_ref[t]] = t * TOP_K + 1
        return carry

    lax.fori_loop(0, rows, issue, 0, unroll=8)
    for _ in range(TOP_K):
        pltpu.make_async_copy(src_ref, o_ref.at[pl.ds(0, rows * SUBLANES)], sem).wait()


def _scatter_rows(plan, src):
    dest, packed = plan["dest"], plan["packed"]
    T = dest.shape[0]
    P = T * TOP_K + N_EXPERTS * MOE_BLK
    rows = GATHER_ROWS
    return pl.pallas_call(
        _scatter_kernel,
        grid_spec=pltpu.PrefetchScalarGridSpec(
            num_scalar_prefetch=5,
            grid=(T // rows,),
            in_specs=[pl.BlockSpec((rows * SUBLANES, LANES), lambda i, *refs: (i, 0))],
            out_specs=[pl.BlockSpec(memory_space=pl.ANY), pl.BlockSpec(memory_space=pltpu.SMEM)],
            scratch_shapes=[
                pltpu.VMEM((MOE_BLK * SUBLANES, LANES), src.dtype),
                pltpu.SemaphoreType.DMA(()),
                pltpu.SemaphoreType.DMA(()),
            ],
        ),
        out_shape=[jax.ShapeDtypeStruct((P * SUBLANES, LANES), src.dtype),
                   jax.ShapeDtypeStruct((T * TOP_K,), jnp.int32)],
        compiler_params=_params(("arbitrary",)),
    )(dest[:, 0], dest[:, 1], packed[:, 0], packed[:, 1], plan["zero_block"], src)


def _expert_kernel(layer, be_ref, first_ref, next_ref, slot_ref, nu_ref, nv_ref, pb_ref, inv_ref,
                   xs_ref, w1_hbm, w3_hbm, w2_hbm, yt_ref,
                   w1f, w3f, w2f, w1b, w3b, w2b, ybuf, sems, ysems):
    i = pl.program_id(0)
    D = w1b.shape[0]
    blk = xs_ref.shape[0] // SUBLANES

    def drain_rows(j):
        n = nv_ref[j]
        b = j % 2
        for bit in range(blk.bit_length()):
            size = (1 << bit) * SUBLANES

            @pl.when((n >> bit) & 1 == 1)
            def _():
                pltpu.make_async_copy(ybuf.at[b, pl.ds(0, size)], yt_ref.at[pl.ds(0, size)], ysems.at[b]).wait()

    def weight_copies(e, slot):
        return (pltpu.make_async_copy(w1_hbm.at[layer, e], w1f.at[slot], sems.at[slot, 0]),
                pltpu.make_async_copy(w3_hbm.at[layer, e], w3f.at[slot], sems.at[slot, 1]),
                pltpu.make_async_copy(w2_hbm.at[layer, e], w2f.at[slot], sems.at[slot, 2]))

    @pl.when(i == 0)
    def _():
        for c in weight_copies(be_ref[0], 0):
            c.start()

    @pl.when(first_ref[i] == 1)
    def _():
        slot = slot_ref[i]
        for c in weight_copies(be_ref[i], slot):
            c.wait()

        @pl.when(next_ref[i] >= 0)
        def _():
            for c in weight_copies(next_ref[i], 1 - slot):
                c.start()

        w1b[...] = w1f[slot].astype(BF16)
        w3b[...] = w3f[slot].astype(BF16)
        w2b[...] = w2f[slot].astype(BF16)

    @pl.when(i >= 2)
    def _():
        drain_rows(i - 2)

    @pl.when(i < nu_ref[0])
    def _():
        lo, hi = _unpack_bf16_pair(_load_row_tiles(xs_ref))
        lo = lo.astype(BF16)
        hi = hi.astype(BF16)
        a = _dot(lo, w1b[0:D // 2, :]) + _dot(hi, w1b[D // 2:, :])
        g = _dot(lo, w3b[0:D // 2, :]) + _dot(hi, w3b[D // 2:, :])
        hid = ((a * _sigmoid(a)) * g).astype(BF16)
        b = i % 2
        _store_row_tiles(ybuf.at[b], _pack_bf16_pair(_dot(hid, w2b[:, 0:D // 2]), _dot(hid, w2b[:, D // 2:])))

        n = nv_ref[i]
        first_pos = pb_ref[i]
        unroll = 8

        def issue_row(r):
            a_idx = inv_ref[first_pos + r]
            pltpu.make_async_copy(ybuf.at[b, pl.ds(pl.multiple_of(r * SUBLANES, SUBLANES), SUBLANES)],
                                  yt_ref.at[pl.ds(pl.multiple_of(a_idx * SUBLANES, SUBLANES), SUBLANES)],
                                  ysems.at[b]).start()

        def issue_group(g, carry):
            for u in range(unroll):
                issue_row(g * unroll + u)
            return carry

        def issue_one(r, carry):
            issue_row(r)
            return carry

        lax.fori_loop(0, n // unroll, issue_group, 0)
        lax.fori_loop((n // unroll) * unroll, n, issue_one, 0)

    @pl.when(i == pl.num_programs(0) - 1)
    def _():
        drain_rows(i - 1)
        drain_rows(i)


def _segment_plan(block_expert, n_used):
    n = block_expert.shape[0]
    idx = jnp.arange(n, dtype=jnp.int32)
    prev = jnp.concatenate([jnp.full((1,), -1, jnp.int32), block_expert[:-1]])
    first = ((block_expert != prev) & (idx < n_used[0])).astype(jnp.int32)
    slot = (jnp.cumsum(first) - 1) % 2
    later_first = jnp.where(first == 1, idx, n)
    next_idx = lax.cummin(jnp.concatenate([later_first[1:], jnp.full((1,), n, jnp.int32)]), reverse=True)
    next_expert = jnp.where(next_idx < n, block_expert[jnp.minimum(next_idx, n - 1)], -1)
    return first, next_expert.astype(jnp.int32), slot.astype(jnp.int32)


def _experts(plan, inv, xs, w1, w3, w2, layer):
    _, _, D, DE = w1.shape
    blk = MOE_BLK
    block_expert, n_used = plan["block_expert"], plan["n_used"]
    n_blocks = block_expert.shape[0]
    n_assign = plan["dest"].shape[0] * TOP_K
    first, next_expert, slot = _segment_plan(block_expert, n_used)
    rowmap = lambda i, *refs: (jnp.minimum(i, jnp.maximum(refs[4][0] - 1, 0)), 0)
    hbm = pl.BlockSpec(memory_space=pl.ANY)
    return pl.pallas_call(
        functools.partial(_expert_kernel, layer),
        grid_spec=pltpu.PrefetchScalarGridSpec(
            num_scalar_prefetch=8,
            grid=(n_blocks,),
            in_specs=[pl.BlockSpec((blk * SUBLANES, LANES), rowmap), hbm, hbm, hbm],
            out_specs=hbm,
            scratch_shapes=[
                pltpu.VMEM((2, D, DE), F32),
                pltpu.VMEM((2, D, DE), F32),
                pltpu.VMEM((2, DE, D), F32),
                pltpu.VMEM((D, DE), BF16),
                pltpu.VMEM((D, DE), BF16),
                pltpu.VMEM((DE, D), BF16),
                pltpu.VMEM((2, blk * SUBLANES, LANES), U32),
                pltpu.SemaphoreType.DMA((2, 3)),
                pltpu.SemaphoreType.DMA((2,)),
            ],
        ),
        out_shape=jax.ShapeDtypeStruct((n_assign * SUBLANES, LANES), U32),
        compiler_params=_params(("arbitrary",)),
    )(block_expert, first, next_expert, slot, n_used, plan["n_valid"], plan["packed_base"], inv, xs, w1, w3, w2)


def _combine_kernel(final, yt_ref, h_ref, g2_ref, ew_ref, fg_ref, o_ref):
    tc, D = h_ref.shape

    def expert_rows(k):
        return jnp.concatenate([yt_ref[pl.ds(k * SUBLANES + s, tc, stride=TOP_K * SUBLANES), :]
                                for s in range(SUBLANES)], axis=1)

    ew = ew_ref[...]
    w0 = ew[:, 0:1]
    w1 = ew[:, 1:2]
    lo0, hi0 = _unpack_bf16_pair(expert_rows(0))
    lo1, hi1 = _unpack_bf16_pair(expert_rows(1))
    moe = jnp.concatenate([w0 * lo0 + w1 * lo1, w0 * hi0 + w1 * hi1], axis=1)
    hn = h_ref[...] + g2_ref[0] * moe
    if final:
        hn = hn * lax.rsqrt(jnp.mean(hn * hn, axis=-1, keepdims=True) + EPS) * fg_ref[...]
    o_ref[...] = hn


def _combine(yt, h, mod3, ew, final_g, S, final):
    T, D = h.shape
    tc = 256
    per_b = S // tc
    return pl.pallas_call(
        functools.partial(_combine_kernel, final),
        grid=(T // tc,),
        in_specs=[
            pl.BlockSpec((tc * TOP_K * SUBLANES, LANES), lambda i: (i, 0)),
            pl.BlockSpec((tc, D), lambda i: (i, 0)),
            pl.BlockSpec((1, 1, D), lambda i: (i // per_b, 0, 5)),
            pl.BlockSpec((tc, LANES), lambda i: (i, 0)),
            pl.BlockSpec((1, D), lambda i: (0, 0)),
        ],
        out_specs=pl.BlockSpec((tc, D), lambda i: (i, 0)),
        out_shape=jax.ShapeDtypeStruct((T, D), F32),
        compiler_params=_params(("parallel",)),
    )(yt, h, mod3, ew, final_g.reshape(1, D))


def kernel(x, c, ada_w, ada_b, norm1_g, w_in, gla_w_a2, gla_b_a, gla_norm_g, lru_conv_w, lru_conv_b,
           lru_wa, lru_ba, lru_wx, lru_bx, lru_lambda, diff_lq1, diff_lk1, diff_lq2, diff_lk2,
           diff_subln_g, rel_bias, w_out, norm2_g, router_g_w, router_g_b, router_e_w, router_e_b,
           moe_w1, moe_w3, moe_w2, final_g):
    B, S, D = x.shape
    T = B * S
    L = ada_w.shape[0]
    h = x.reshape(T, D)
    mod = _ada_mod(c, ada_w, ada_b)
    bias = _bias_tiles(rel_bias)
    w_in_perm = _permute_w_in(w_in)
    w_out_bf16 = w_out.astype(BF16)
    for l in range(L):
        mod3 = mod[l][:, None, :]
        proj = _inproj(h, mod3, norm1_g[l], w_in_perm, l, S)
        wa2_pad = jnp.concatenate(
            [gla_w_a2[l], jnp.zeros((LANES - GLA_LOWRANK, GLA_KEY_WIDTH), F32)], axis=0).astype(BF16)
        o_gla = _gla(proj, wa2_pad, gla_b_a[l], gla_norm_g[l], B, S)
        w_gates = jnp.concatenate([_block_diag(lru_wa[l]), _block_diag(lru_wx[l])], axis=1).astype(BF16)
        b_gates = jnp.concatenate([lru_ba[l], lru_bx[l]]).reshape(1, 2 * LRU_WIDTH)
        o_lru = _lru(proj.reshape(B, S, PROJ_WIDTH), lru_conv_w[l], lru_conv_b[l], w_gates, b_gates,
                     lru_lambda[l]).reshape(T, LRU_WIDTH)
        lqk = jnp.stack([diff_lq1[l], diff_lk1[l], diff_lq2[l], diff_lk2[l]], axis=0)
        o_diff = _diff_attention(proj, bias, lqk, diff_subln_g[l], l, B, S)
        rw = jnp.concatenate(
            [router_g_w[l], router_e_w[l], jnp.zeros((D, LANES - N_GROUPS - N_EXPERTS), F32)], axis=1).astype(BF16)
        rb = jnp.concatenate(
            [router_g_b[l], router_e_b[l], jnp.zeros((LANES - N_GROUPS - N_EXPERTS,), F32)]).reshape(1, LANES)
        h, u2, info, ew, cnt = _outproj(h, o_gla, o_lru, o_diff, w_out_bf16, l, mod3, norm2_g[l], rw, rb, S)
        counts = cnt[0, N_GROUPS:N_GROUPS + N_EXPERTS].astype(jnp.int32)
        plan = _dispatch(info, counts)
        xs, inv = _scatter_rows(plan, u2)
        yt = _experts(plan, inv, xs, moe_w1, moe_w3, moe_w2, l)
        h = _combine(yt, h, mod3, ew, final_g, S, final=(l == L - 1))
    return h.reshape(B, S, D)
```

```python
import functools
import math

import jax
import jax.numpy as jnp
from jax import lax
from jax.experimental import pallas as pl
from jax.experimental.pallas import tpu as pltpu

F32 = jnp.float32
BF16 = jnp.bfloat16
U32 = jnp.uint32

EPS = 1e-6
LOG2E = math.log2(math.e)
CHUNK = 64

GLA_DV = 128
GLA_DK = 64
GLA_HEADS = 6
GLA_WIDTH = GLA_HEADS * GLA_DV
GLA_KEY_WIDTH = GLA_HEADS * GLA_DK
GLA_LOWRANK = 16
GLA_TAU = 16.0

LRU_WIDTH = 512
LRU_BLOCKS = 8
LRU_BLOCK_DIM = LRU_WIDTH // LRU_BLOCKS
CONV_WIDTH = 4
LRU_C = 8.0

DIFF_DH = 64
DIFF_DV = 128
DIFF_HEADS = 6
DIFF_WIDTH = DIFF_HEADS * DIFF_DV

REL_BUCKETS = 32
REL_MAX_DIST = 128

N_GROUPS = 8
EXPERTS_PER_GROUP = 8
N_EXPERTS = 64
TOP_K = 2

LANES = 128
SUBLANES = 8
VMEM_LIMIT = 56 * 1024 * 1024

COL_GV = 0
COL_GOG = 768
COL_DQ = 1536
COL_DK = 2304
COL_DV = 3072
COL_GQ = 3840
COL_GK = 4224
COL_LY = 4608
COL_LX = 5120
COL_GA = 5632
PROJ_WIDTH = 5760

ATT_TILE = 512
GLA_TILE = 256
LRU_TILE = 256
MOE_BLK = 256
GATHER_ROWS = 512


def _params(sem, vmem=VMEM_LIMIT):
    return pltpu.CompilerParams(dimension_semantics=sem, vmem_limit_bytes=vmem)


def _sigmoid(x):
    return 0.5 * jnp.tanh(0.5 * x) + 0.5


def _softplus(x):
    return jnp.maximum(x, 0.0) + jnp.log1p(jnp.exp(-jnp.abs(x)))


def _dot(a, b):
    return jnp.dot(a, b, preferred_element_type=F32)


def _dot_nt(a, b):
    return lax.dot_general(a, b, (((1,), (1,)), ((), ())), preferred_element_type=F32)


def _dot_tn(a, b):
    return lax.dot_general(a, b, (((0,), (0,)), ((), ())), preferred_element_type=F32)


def _pack_bf16_pair(lo, hi):
    lo_bits = lax.bitcast_convert_type(lo.astype(BF16).astype(F32), U32)
    hi_bits = lax.bitcast_convert_type(hi.astype(BF16).astype(F32), U32)
    return (hi_bits & jnp.uint32(0xFFFF0000)) | (lo_bits >> 16)


def _unpack_bf16_pair(w):
    lo = lax.bitcast_convert_type(w << 16, F32)
    hi = lax.bitcast_convert_type(w & jnp.uint32(0xFFFF0000), F32)
    return lo, hi


def _store_row_tiles(ref, words):
    rows = words.shape[0]
    for s in range(SUBLANES):
        ref[pl.ds(s, rows, stride=SUBLANES), :] = words[:, s * LANES:(s + 1) * LANES]


def _load_row_tiles(ref):
    rows = ref.shape[0] // SUBLANES
    return jnp.concatenate([ref[pl.ds(s, rows, stride=SUBLANES), :] for s in range(SUBLANES)], axis=1)


def _ada_kernel(c_ref, w_ref, b_ref, o_ref):
    c = c_ref[...]
    s = c * _sigmoid(c)
    o_ref[0] = _dot(s.astype(BF16), w_ref[0].astype(BF16)) + b_ref[0]


def _ada_mod(c, ada_w, ada_b):
    L, D, N = ada_w.shape
    B = c.shape[0]
    tn = 1024
    return pl.pallas_call(
        _ada_kernel,
        grid=(L, N // tn),
        in_specs=[
            pl.BlockSpec((B, D), lambda l, j: (0, 0)),
            pl.BlockSpec((1, D, tn), lambda l, j: (l, 0, j)),
            pl.BlockSpec((1, 1, tn), lambda l, j: (l, 0, j)),
        ],
        out_specs=pl.BlockSpec((1, B, tn), lambda l, j: (l, 0, j)),
        out_shape=jax.ShapeDtypeStruct((L, B, N), F32),
        compiler_params=_params(("parallel", "parallel")),
    )(c, ada_w, ada_b.reshape(L, 1, N))


def _modulated_norm(x, g, sc, sh):
    ms = jnp.mean(x * x, axis=-1, keepdims=True)
    return (x * lax.rsqrt(ms + EPS) * g) * (1.0 + sc) + sh


def _moe_mix(yt_ref, ew_ref):
    rows = ew_ref.shape[0]

    def expert_rows(k):
        return jnp.concatenate([yt_ref[pl.ds(k * SUBLANES + s, rows, stride=TOP_K * SUBLANES), :]
                                for s in range(SUBLANES)], axis=1)

    ew = ew_ref[...]
    w0 = ew[:, 0:1]
    w1 = ew[:, 1:2]
    lo0, hi0 = _unpack_bf16_pair(expert_rows(0))
    lo1, hi1 = _unpack_bf16_pair(expert_rows(1))
    return jnp.concatenate([w0 * lo0 + w1 * lo1, w0 * hi0 + w1 * hi1], axis=1)


def _inproj_kernel(h_ref, sh_ref, sc_ref, g_ref, w_ref, o_ref, u_scr):
    @pl.when(pl.program_id(1) == 0)
    def _():
        u = _modulated_norm(h_ref[...], g_ref[...], sc_ref[0], sh_ref[0])
        u_scr[...] = u.astype(BF16)

    o_ref[...] = _dot(u_scr[...], w_ref[...]).astype(o_ref.dtype)


def _inproj_after_moe_kernel(h_ref, yt_ref, ew_ref, g2_ref, sh_ref, sc_ref, g_ref, w_ref, o_ref, hn_ref, u_scr):
    @pl.when(pl.program_id(1) == 0)
    def _():
        hn = h_ref[...] + g2_ref[0] * _moe_mix(yt_ref, ew_ref)
        hn_ref[...] = hn
        u = _modulated_norm(hn, g_ref[...], sc_ref[0], sh_ref[0])
        u_scr[...] = u.astype(BF16)

    o_ref[...] = _dot(u_scr[...], w_ref[...]).astype(o_ref.dtype)


def _inproj(h, mod3, norm_g, w_perm, layer, S, pending_moe=None):
    T, D = h.shape
    N = w_perm.shape[2]
    tm, tn = 512, 1152
    per_b = S // tm
    rows = pl.BlockSpec((tm, D), lambda i, j: (i, 0))
    mod_specs = [
        pl.BlockSpec((1, 1, D), lambda i, j: (i // per_b, 0, 0)),
        pl.BlockSpec((1, 1, D), lambda i, j: (i // per_b, 0, 1)),
        pl.BlockSpec((1, D), lambda i, j: (0, 0)),
        pl.BlockSpec((None, D, tn), lambda i, j: (layer, 0, j)),
    ]
    proj_spec = pl.BlockSpec((tm, tn), lambda i, j: (i, j))
    proj_shape = jax.ShapeDtypeStruct((T, N), BF16)
    common = dict(grid=(T // tm, N // tn), scratch_shapes=[pltpu.VMEM((tm, D), BF16)],
                  compiler_params=_params(("parallel", "arbitrary")))
    if pending_moe is None:
        return pl.pallas_call(_inproj_kernel, in_specs=[rows] + mod_specs, out_specs=proj_spec,
                              out_shape=proj_shape, **common)(h, mod3, mod3, norm_g.reshape(1, D), w_perm)
    yt, ew, prev_mod3 = pending_moe
    moe_specs = [
        pl.BlockSpec((tm * TOP_K * SUBLANES, LANES), lambda i, j: (i, 0)),
        pl.BlockSpec((tm, LANES), lambda i, j: (i, 0)),
        pl.BlockSpec((1, 1, D), lambda i, j: (i // per_b, 0, 5)),
    ]
    return pl.pallas_call(
        _inproj_after_moe_kernel, in_specs=[rows] + moe_specs + mod_specs, out_specs=[proj_spec, rows],
        out_shape=[proj_shape, jax.ShapeDtypeStruct((T, D), F32)], **common,
    )(h, yt, ew, prev_mod3, mod3, mod3, norm_g.reshape(1, D), w_perm)


W_IN_SEGMENTS = ((COL_GQ, 0, 384), (COL_GK, 384, 384), (COL_GV, 768, 768), (COL_GOG, 1536, 768),
                 (COL_GA, 2304, GLA_LOWRANK), (COL_LY, 2320, 512), (COL_LX, 2832, 512),
                 (COL_DQ, 3344, 768), (COL_DK, 4112, 768), (COL_DV, 4880, 768))


def _relayout_kernel(w_ref, o_ref):
    x = w_ref[0]
    for dst, src, width in W_IN_SEGMENTS:
        o_ref[0, :, dst:dst + width] = x[:, src:src + width]
    pad = slice(COL_GA + GLA_LOWRANK, COL_GA + LANES)
    o_ref[0, :, pad] = jnp.zeros((x.shape[0], LANES - GLA_LOWRANK), BF16)


def _permute_w_in(w):
    L, D, N = w.shape
    rt = 256
    return pl.pallas_call(
        _relayout_kernel,
        grid=(L, D // rt),
        in_specs=[pl.BlockSpec((1, rt, N), lambda l, i: (l, i, 0))],
        out_specs=pl.BlockSpec((1, rt, PROJ_WIDTH), lambda l, i: (l, i, 0)),
        out_shape=jax.ShapeDtypeStruct((L, D, PROJ_WIDTH), BF16),
        compiler_params=_params(("parallel", "parallel")),
    )(w.astype(BF16))


def _gla_kernel(q_ref, k_ref, v_ref, og_ref, alr_ref, wa2_ref, ba_ref, ng_ref, o_ref, st_ref):
    tb = q_ref.shape[0]
    n_chunks = tb // CHUNK

    @pl.when(pl.program_id(1) == 0)
    def _():
        st_ref[...] = jnp.zeros_like(st_ref)

    row = lax.broadcasted_iota(jnp.int32, (tb, tb), 0)
    col = lax.broadcasted_iota(jnp.int32, (tb, tb), 1)
    same_chunk = (row // CHUNK) == (col // CHUNK)
    causal = col <= row
    tril = jnp.where(same_chunk & causal, 1.0, 0.0).astype(BF16)
    chunk_ones = jnp.where(same_chunk, 1.0, 0.0).astype(BF16)
    lane = lax.broadcasted_iota(jnp.int32, (1, LANES), 1)
    half_masks = (lane < GLA_DK, lane >= GLA_DK)

    alr = alr_ref[...]
    for p in range(GLA_HEADS // 2):
        cs = slice(p * LANES, (p + 1) * LANES)
        z = _dot(alr, wa2_ref[:, cs]) + ba_ref[:, cs]
        la = (jnp.minimum(z, 0.0) - jnp.log1p(jnp.exp(-jnp.abs(z)))) * (1.0 / GLA_TAU)
        la_hi = la.astype(BF16)
        la_lo = (la - la_hi.astype(F32)).astype(BF16)
        G = _dot(tril, la_hi) + _dot(tril, la_lo)
        Gl = _dot(chunk_ones, la_hi) + _dot(chunk_ones, la_lo)
        eG = jnp.exp(G)
        enG = jnp.exp(-G)
        q = q_ref[:, cs].astype(F32) * (GLA_DK ** -0.5)
        k = k_ref[:, cs].astype(F32)
        qf = q * eG
        qb = q * enG
        kf = (k * eG).astype(BF16)
        kb = (k * enG).astype(BF16)
        kd = k * jnp.exp(Gl - G)
        for hh in range(2):
            head = 2 * p + hh
            m = half_masks[hh]
            vs = slice(head * GLA_DV, (head + 1) * GLA_DV)
            qf_h = jnp.where(m, qf, 0.0).astype(BF16)
            qb_h = jnp.where(m, qb, 0.0).astype(BF16)
            kd_h = jnp.where(m, kd, 0.0).astype(BF16)
            v_h = v_ref[:, vs]
            a_f = _dot_nt(qf_h, kb)
            a_b = _dot_nt(qb_h, kf)
            attn = jnp.where(same_chunk, jnp.where(causal, a_f, a_b), 0.0)
            o_intra = _dot(attn.astype(BF16), v_h)
            st = st_ref[head]
            inter = []
            for c in range(n_chunks):
                rs = slice(c * CHUNK, (c + 1) * CHUNK)
                inter.append(_dot_nt(qf_h[rs], st.astype(BF16)))
                decay = jnp.exp(Gl[c * CHUNK:c * CHUNK + 1, :])
                st = st * decay + _dot_tn(v_h[rs], kd_h[rs])
            st_ref[head] = st
            o = o_intra + jnp.concatenate(inter, axis=0)
            o = o * lax.rsqrt(jnp.mean(o * o, axis=-1, keepdims=True) + EPS)
            og = og_ref[:, vs].astype(F32)
            o_ref[:, vs] = (o * ng_ref[:, vs] * (og * _sigmoid(og))).astype(o_ref.dtype)


def _gla(proj, wa2_pad, b_a, norm_g, B, S):
    T = proj.shape[0]
    tb = GLA_TILE
    nt = S // tb
    row = lambda b, i: b * nt + i
    return pl.pallas_call(
        _gla_kernel,
        grid=(B, nt),
        in_specs=[
            pl.BlockSpec((tb, GLA_KEY_WIDTH), lambda b, i: (row(b, i), COL_GQ // GLA_KEY_WIDTH)),
            pl.BlockSpec((tb, GLA_KEY_WIDTH), lambda b, i: (row(b, i), COL_GK // GLA_KEY_WIDTH)),
            pl.BlockSpec((tb, GLA_WIDTH), lambda b, i: (row(b, i), COL_GV // GLA_WIDTH)),
            pl.BlockSpec((tb, GLA_WIDTH), lambda b, i: (row(b, i), COL_GOG // GLA_WIDTH)),
            pl.BlockSpec((tb, LANES), lambda b, i: (row(b, i), COL_GA // LANES)),
            pl.BlockSpec((LANES, GLA_KEY_WIDTH), lambda b, i: (0, 0)),
            pl.BlockSpec((1, GLA_KEY_WIDTH), lambda b, i: (0, 0)),
            pl.BlockSpec((1, GLA_WIDTH), lambda b, i: (0, 0)),
        ],
        out_specs=pl.BlockSpec((tb, GLA_WIDTH), lambda b, i: (row(b, i), 0)),
        out_shape=jax.ShapeDtypeStruct((T, GLA_WIDTH), BF16),
        scratch_shapes=[pltpu.VMEM((GLA_HEADS, GLA_DV, LANES), F32)],
        compiler_params=_params(("parallel", "arbitrary")),
    )(proj, proj, proj, proj, proj, wa2_pad, b_a.reshape(1, -1), norm_g.reshape(1, -1))


def _lru_kernel(y_ref, x_ref, cw_ref, cb_ref, wg_ref, bg_ref, lam_ref, o_ref, *scratch):
    B, ts, W = x_ref.shape
    n_planes = W // LANES
    a_scr = scratch[0:n_planes]
    b_scr = scratch[n_planes:2 * n_planes]
    h_scr = scratch[2 * n_planes:3 * n_planes]
    xc_scr, tail_scr, carry_scr = scratch[3 * n_planes:]

    @pl.when(pl.program_id(0) == 0)
    def _():
        tail_scr[...] = jnp.zeros_like(tail_scr)
        carry_scr[...] = jnp.zeros_like(carry_scr)

    cw = cw_ref[...]
    cb = cb_ref[...]
    sp = _softplus(-lam_ref[...])
    row8 = lax.broadcasted_iota(jnp.int32, (8, W), 0)
    for b in range(B):
        x = x_ref[b].astype(F32)
        tail = tail_scr[b]
        xc = cb + cw[CONV_WIDTH - 1:CONV_WIDTH, :] * x
        head = cb + cw[CONV_WIDTH - 1:CONV_WIDTH, :] * x[0:8]
        for d in range(1, CONV_WIDTH):
            wd = cw[CONV_WIDTH - 1 - d:CONV_WIDTH - d, :]
            xr = pltpu.roll(x, d, 0)
            xc = xc + wd * xr
            head = head + wd * jnp.where(row8 < d, pltpu.roll(tail, d, 0), xr[0:8])
        tail_scr[b] = x[ts - 8:ts]
        xc_scr[...] = xc
        xc_scr[0:8] = head
        xc = xc_scr[...]
        gates = _sigmoid(_dot(xc.astype(BF16), wg_ref[...]) + bg_ref[...])
        r = gates[:, :W]
        ig = gates[:, W:]
        log_a = (-LRU_C) * r * sp
        a = jnp.exp(log_a)
        b_in = jnp.sqrt(-jnp.tanh(log_a) * (a * a + 1.0)) * (ig * xc)
        rows = slice(b * ts, (b + 1) * ts)
        for k in range(n_planes):
            a_scr[k][rows] = a[:, k * LANES:(k + 1) * LANES]
            b_scr[k][rows] = b_in[:, k * LANES:(k + 1) * LANES]

    def step(t, hs):
        idx = pl.ds(t, B, stride=ts)
        out = []
        for k in range(n_planes):
            hk = a_scr[k][idx, :] * hs[k] + b_scr[k][idx, :]
            h_scr[k][idx, :] = hk
            out.append(hk)
        return tuple(out)

    hs = lax.fori_loop(0, ts, step, tuple(carry_scr[k] for k in range(n_planes)), unroll=8)
    for k in range(n_planes):
        carry_scr[k] = hs[k]

    for b in range(B):
        rows = slice(b * ts, (b + 1) * ts)
        y = y_ref[b].astype(F32)
        gelu = 0.5 * y * (1.0 + jnp.tanh(math.sqrt(2.0 / math.pi) * (y + 0.044715 * (y * y * y))))
        h = jnp.concatenate([h_scr[k][rows] for k in range(n_planes)], axis=1)
        o_ref[b] = (h * gelu).astype(o_ref.dtype)


def _lru(proj3, conv_w, conv_b, w_gates, b_gates, lam):
    B, S, _ = proj3.shape
    W = LRU_WIDTH
    ts = LRU_TILE
    n_planes = W // LANES
    full = lambda shape: pl.BlockSpec(shape, lambda i: (0,) * len(shape))
    return pl.pallas_call(
        _lru_kernel,
        grid=(S // ts,),
        in_specs=[
            pl.BlockSpec((B, ts, W), lambda i: (0, i, COL_LY // W)),
            pl.BlockSpec((B, ts, W), lambda i: (0, i, COL_LX // W)),
            full((CONV_WIDTH, W)),
            full((1, W)),
            full((W, 2 * W)),
            full((1, 2 * W)),
            full((1, W)),
        ],
        out_specs=pl.BlockSpec((B, ts, W), lambda i: (0, i, 0)),
        out_shape=jax.ShapeDtypeStruct((B, S, W), BF16),
        scratch_shapes=(
            [pltpu.VMEM((B * ts, LANES), F32) for _ in range(3 * n_planes)]
            + [pltpu.VMEM((ts, W), F32), pltpu.VMEM((B, 8, W), F32), pltpu.VMEM((n_planes, B, LANES), F32)]),
        compiler_params=_params(("arbitrary",)),
    )(proj3, proj3, conv_w, conv_b.reshape(1, W), w_gates, b_gates, lam.reshape(1, W))


def _block_diag(w):
    n, d, _ = w.shape
    eye = jnp.eye(n, dtype=w.dtype)
    return (eye[:, None, :, None] * w[:, :, None, :]).reshape(n * d, n * d)


def _t5_bucket(rel):
    nb = REL_BUCKETS // 2
    ret = (rel > 0).astype(jnp.int32) * nb
    n = jnp.abs(rel)
    max_exact = nb // 2
    nf = jnp.maximum(n, 1).astype(jnp.float32)
    large = max_exact + (jnp.log(nf / max_exact) / math.log(REL_MAX_DIST / max_exact)
                         * (nb - max_exact)).astype(jnp.int32)
    large = jnp.minimum(large, nb - 1)
    return ret + jnp.where(n < max_exact, n, large)


def _bias_kernel(bucket_ref, table_ref, o_ref):
    h = pl.program_id(0)
    bucket = bucket_ref[0]
    acc = jnp.full(bucket.shape, -1e30, F32)
    for b in range(REL_BUCKETS):
        acc = jnp.where(bucket == b, table_ref[b, h] * LOG2E, acc)
    o_ref[0, 0] = acc


def _bias_tiles(rel_bias):
    t = ATT_TILE
    H = rel_bias.shape[1]
    qp = jnp.arange(t, dtype=jnp.int32)[:, None]
    kp = jnp.arange(t, dtype=jnp.int32)[None, :]
    mask = (kp // CHUNK) <= (qp // CHUNK)
    half = REL_BUCKETS // 2
    per_distance = _t5_bucket(-jnp.arange(2 * t, dtype=jnp.int32))
    edges = jnp.sum((per_distance[None, :] < jnp.arange(1, half, dtype=jnp.int32)[:, None]).astype(jnp.int32), axis=1)

    def bucket_2d(rel):
        passed = jnp.sum((jnp.abs(rel)[None] >= edges[:, None, None]).astype(jnp.int32), axis=0)
        return (rel > 0).astype(jnp.int32) * half + passed

    buckets = jnp.stack([jnp.where(mask, bucket_2d(kp - qp), REL_BUCKETS), bucket_2d(kp - t - qp)], axis=0)
    table = rel_bias.astype(F32)
    tiles = pl.pallas_call(
        _bias_kernel,
        grid=(H, 2),
        in_specs=[
            pl.BlockSpec((1, t, t), lambda h, k: (k, 0, 0)),
            pl.BlockSpec(memory_space=pltpu.SMEM),
        ],
        out_specs=pl.BlockSpec((1, 1, t, t), lambda h, k: (h, k, 0, 0)),
        out_shape=jax.ShapeDtypeStruct((H, 2, t, t), F32),
        compiler_params=_params(("parallel", "parallel")),
    )(buckets, table)
    far_bucket = _t5_bucket(jnp.full((1,), -t - 1, jnp.int32))
    far = jnp.sum(jnp.where(jnp.arange(REL_BUCKETS)[:, None] == far_bucket, table, 0.0), axis=0)
    return tiles, jnp.broadcast_to((far * LOG2E)[:, None, None], (H, 1, t))


def _diff_kernel(lam_init, q_ref, k_ref, v_ref, bias_ref, far_ref, lqk_ref, g_ref, o_ref,
                 qs_scr, m_scr, l_scr, acc_scr):
    i = pl.program_id(2)
    t = q_ref.shape[0]
    hq = t // 2
    lane = lax.broadcasted_iota(jnp.int32, (1, LANES), 1)
    q = q_ref[...].astype(F32) * (LOG2E * DIFF_DH ** -0.5)
    for half in range(2):
        qh = q[half * hq:(half + 1) * hq]
        qs_scr[(2 * half) * hq:(2 * half + 1) * hq] = jnp.where(lane < DIFF_DH, qh, 0.0).astype(BF16)
        qs_scr[(2 * half + 1) * hq:(2 * half + 2) * hq] = jnp.where(lane >= DIFF_DH, qh, 0.0).astype(BF16)
    m_scr[...] = jnp.full_like(m_scr, -1e30)
    l_scr[...] = jnp.zeros_like(l_scr)
    acc_scr[...] = jnp.zeros_like(acc_scr)

    def tile(rows, ks, bias):
        s = _dot_nt(qs_scr[rows, :], k_ref[ks, :]) + bias
        groups = [s[:, c * LANES:(c + 1) * LANES] for c in range(s.shape[1] // LANES)]
        mx = functools.reduce(jnp.maximum, groups)
        m_prev = m_scr[rows, :]
        m_new = jnp.maximum(m_prev, jnp.max(mx, axis=-1, keepdims=True))
        alpha = jnp.exp2(m_prev - m_new)
        ps = [jnp.exp2(g - m_new) for g in groups]
        l_scr[rows, :] = alpha * l_scr[rows, :] + functools.reduce(jnp.add, ps)
        p = jnp.concatenate(ps, axis=1).astype(BF16)
        acc_scr[rows, :] = alpha * acc_scr[rows, :] + _dot(p, v_ref[ks, :])
        m_scr[rows, :] = m_new

    def stacked(b, half):
        bh = b[half * hq:(half + 1) * hq]
        return [bh, bh]

    all_rows = slice(0, 2 * t)

    def far_body(j, carry):
        tile(all_rows, pl.ds(pl.multiple_of(j * t, t), t), far_ref[0])
        return carry

    lax.fori_loop(0, jnp.maximum(i - 1, 0), far_body, 0)

    @pl.when(i >= 1)
    def _():
        b = bias_ref[0, 1]
        tile(all_rows, pl.ds(pl.multiple_of((i - 1) * t, t), t),
             jnp.concatenate(stacked(b, 0) + stacked(b, 1), axis=0))

    b = bias_ref[0, 0]
    diag0 = pl.multiple_of(i * t, t)
    tile(slice(0, t), pl.ds(diag0, hq), jnp.concatenate(stacked(b[:, 0:hq], 0), axis=0))
    tile(slice(t, 2 * t), pl.ds(diag0, t), jnp.concatenate(stacked(b, 1), axis=0))

    lqk = lqk_ref[...]
    lam = (jnp.exp(jnp.sum(lqk[0:1] * lqk[1:2], axis=-1, keepdims=True))
           - jnp.exp(jnp.sum(lqk[2:3] * lqk[3:4], axis=-1, keepdims=True)) + lam_init)
    o = acc_scr[...] / jnp.sum(l_scr[...], axis=-1, keepdims=True)
    o = jnp.concatenate([o[0:hq] - lam * o[hq:t], o[t:t + hq] - lam * o[t + hq:2 * t]], axis=0)
    o = o * lax.rsqrt(jnp.mean(o * o, axis=-1, keepdims=True) + EPS)
    o_ref[...] = (o * g_ref[...] * (1.0 - lam_init)).astype(o_ref.dtype)


def _diff_attention(proj, bias, lqk, subln_g, layer_idx, B, S):
    T = proj.shape[0]
    t = ATT_TILE
    nq = S // t
    tiles, far = bias
    lam_init = 0.8 - 0.6 * math.exp(-0.3 * layer_idx)
    return pl.pallas_call(
        functools.partial(_diff_kernel, lam_init),
        grid=(B, DIFF_HEADS, nq),
        in_specs=[
            pl.BlockSpec((t, LANES), lambda b, h, i: (b * nq + i, COL_DQ // LANES + h)),
            pl.BlockSpec((S, LANES), lambda b, h, i: (b, COL_DK // LANES + h)),
            pl.BlockSpec((S, LANES), lambda b, h, i: (b, COL_DV // LANES + h)),
            pl.BlockSpec((1, 2, t, t), lambda b, h, i: (h, 0, 0, 0)),
            pl.BlockSpec((1, 1, t), lambda b, h, i: (h, 0, 0)),
            pl.BlockSpec((4, DIFF_DH), lambda b, h, i: (0, 0)),
            pl.BlockSpec((1, DIFF_DV), lambda b, h, i: (0, 0)),
        ],
        out_specs=pl.BlockSpec((t, LANES), lambda b, h, i: (b * nq + i, h)),
        out_shape=jax.ShapeDtypeStruct((T, DIFF_WIDTH), BF16),
        scratch_shapes=[
            pltpu.VMEM((2 * t, LANES), BF16),
            pltpu.VMEM((2 * t, LANES), F32),
            pltpu.VMEM((2 * t, LANES), F32),
            pltpu.VMEM((2 * t, DIFF_DV), F32),
        ],
        compiler_params=_params(("parallel", "parallel", "arbitrary")),
    )(proj, proj, proj, tiles, far, lqk, subln_g.reshape(1, DIFF_DV))


def _outproj_kernel(h_ref, og_ref, ol_ref, od_ref, w_ref, g1_ref, sh2_ref, sc2_ref, n2_ref, rw_ref, rb_ref,
                    hn_ref, u2_ref, eid_ref, ew_ref, cnt_ref):
    tm = h_ref.shape[0]

    @pl.when(pl.program_id(0) == 0)
    def _():
        cnt_ref[...] = jnp.zeros_like(cnt_ref)

    nr = tm // 2
    for half in range(2):
        _outproj_rows(slice(half * nr, (half + 1) * nr), h_ref, og_ref, ol_ref, od_ref, w_ref, g1_ref, sh2_ref,
                      sc2_ref, n2_ref, rw_ref, rb_ref, hn_ref, u2_ref, eid_ref, ew_ref, cnt_ref)


def _outproj_rows(rows, h_ref, og_ref, ol_ref, od_ref, w_ref, g1_ref, sh2_ref, sc2_ref, n2_ref, rw_ref, rb_ref,
                  hn_ref, u2_ref, eid_ref, ew_ref, cnt_ref):
    D = h_ref.shape[1]
    nr = rows.stop - rows.start
    acc = _dot(og_ref[rows, :], w_ref[0:GLA_WIDTH, :])
    acc += _dot(ol_ref[rows, :], w_ref[GLA_WIDTH:GLA_WIDTH + LRU_WIDTH, :])
    acc += _dot(od_ref[rows, :], w_ref[GLA_WIDTH + LRU_WIDTH:, :])
    hn = h_ref[rows, :] + g1_ref[0] * acc
    hn_ref[rows, :] = hn
    u2 = _modulated_norm(hn, n2_ref[...], sc2_ref[0], sh2_ref[0])
    _store_row_tiles(u2_ref.at[pl.ds(rows.start * SUBLANES, nr * SUBLANES)],
                     _pack_bf16_pair(u2[:, :D // 2], u2[:, D // 2:]))

    logits = _dot(u2.astype(BF16), rw_ref[...]) + rb_ref[...]
    lane = lax.broadcasted_iota(jnp.int32, logits.shape, 1)
    lane_f = lane.astype(F32)
    neg = jnp.float32(-jnp.inf)
    gmask = lane < N_GROUPS
    gl = jnp.where(gmask, logits, neg)
    gmax = jnp.max(gl, axis=-1, keepdims=True)
    gidx = jnp.min(jnp.where(gl == gmax, lane_f, float(LANES)), axis=-1, keepdims=True)
    g_w = 1.0 / jnp.sum(jnp.where(gmask, jnp.exp(gl - gmax), 0.0), axis=-1, keepdims=True)
    egroup = ((lane - N_GROUPS) >> 3).astype(F32)
    emask = (lane >= N_GROUPS) & (lane < N_GROUPS + N_EXPERTS) & (egroup == gidx)
    el = jnp.where(emask, logits, neg)
    v1 = jnp.max(el, axis=-1, keepdims=True)
    i1 = jnp.min(jnp.where(el == v1, lane_f, float(LANES)), axis=-1, keepdims=True)
    el2 = jnp.where(lane_f == i1, neg, el)
    v2 = jnp.max(el2, axis=-1, keepdims=True)
    i2 = jnp.min(jnp.where(el2 == v2, lane_f, float(LANES)), axis=-1, keepdims=True)
    e21 = jnp.exp(v2 - v1)
    w1 = g_w / (1.0 + e21)
    w2 = g_w * e21 / (1.0 + e21)
    ew_ref[rows, :] = jnp.where(lane == 0, w1, jnp.where(lane == 1, w2, 0.0))

    oh1 = lane_f == i1
    oh2 = lane_f == i2
    both = jnp.where(oh1 | oh2, 1.0, 0.0).astype(BF16)
    row = lax.broadcasted_iota(jnp.int32, (nr, nr), 0)
    col = lax.broadcasted_iota(jnp.int32, (nr, nr), 1)
    earlier = _dot(jnp.where(col < row, 1.0, 0.0).astype(BF16), both) + cnt_ref[0:1, :]
    rank1 = jnp.sum(jnp.where(oh1, earlier, 0.0), axis=-1, keepdims=True)
    rank2 = jnp.sum(jnp.where(oh2, earlier, 0.0), axis=-1, keepdims=True)
    cnt_ref[0:1, :] = cnt_ref[0:1, :] + jnp.sum(both.astype(F32), axis=0, keepdims=True)
    info = jnp.where(lane == 0, i1 - float(N_GROUPS),
                     jnp.where(lane == 1, i2 - float(N_GROUPS),
                               jnp.where(lane == 2, rank1, jnp.where(lane == 3, rank2, 0.0))))
    eid_ref[rows, :] = info.astype(jnp.int32)


def _outproj(h, o_gla, o_lru, o_diff, w_out, layer, mod3, norm2_g, rw, rb, S):
    T, D = h.shape
    tm = 512
    per_b = S // tm
    rowblk = lambda width: pl.BlockSpec((tm, width), lambda i: (i, 0))
    modblk = lambda k: pl.BlockSpec((1, 1, D), lambda i: (i // per_b, 0, k))
    full = lambda shape: pl.BlockSpec(shape, lambda i: (0,) * len(shape))
    return pl.pallas_call(
        _outproj_kernel,
        grid=(T // tm,),
        in_specs=[
            rowblk(D), rowblk(GLA_WIDTH), rowblk(LRU_WIDTH), rowblk(DIFF_WIDTH),
            pl.BlockSpec((None, D, D), lambda i: (layer, 0, 0)),
            modblk(2), modblk(3), modblk(4),
            full((1, D)),
            full((D, LANES)),
            full((1, LANES)),
        ],
        out_specs=[rowblk(D), pl.BlockSpec((tm * SUBLANES, LANES), lambda i: (i, 0)), rowblk(LANES), rowblk(LANES),
                   full((SUBLANES, LANES))],
        out_shape=[
            jax.ShapeDtypeStruct((T, D), F32),
            jax.ShapeDtypeStruct((T * SUBLANES, LANES), U32),
            jax.ShapeDtypeStruct((T, LANES), jnp.int32),
            jax.ShapeDtypeStruct((T, LANES), F32),
            jax.ShapeDtypeStruct((SUBLANES, LANES), F32),
        ],
        compiler_params=_params(("arbitrary",)),
    )(h, o_gla, o_lru, o_diff, w_out, mod3, mod3, mod3, norm2_g.reshape(1, D), rw, rb)


def _dispatch(info, counts):
    T = info.shape[0]
    blk = MOE_BLK
    n_blocks = (T * TOP_K) // blk + N_EXPERTS
    expert = info[:, 0:TOP_K]
    rank = info[:, TOP_K:2 * TOP_K]
    padded = (counts + blk - 1) // blk * blk
    pends = jnp.cumsum(padded)
    pstarts = pends - padded
    ustarts = jnp.cumsum(counts) - counts
    onehot = expert[:, :, None] == jnp.arange(N_EXPERTS, dtype=jnp.int32)[None, None, :]
    dest = rank + jnp.sum(jnp.where(onehot, pstarts[None, None, :], 0), axis=-1)
    packed = rank + jnp.sum(jnp.where(onehot, ustarts[None, None, :], 0), axis=-1)
    n_used = (pends[-1] // blk).astype(jnp.int32)
    block_idx = jnp.arange(n_blocks, dtype=jnp.int32)
    block_expert = jnp.minimum(jnp.sum((pends[None, :] <= (block_idx * blk)[:, None]).astype(jnp.int32), axis=1),
                               N_EXPERTS - 1)
    last_used = jnp.sum(jnp.where(block_idx == jnp.maximum(n_used - 1, 0), block_expert, 0))
    block_expert = jnp.where(block_idx < n_used, block_expert, last_used).astype(jnp.int32)
    following = jnp.concatenate([block_expert[1:], jnp.full((1,), -1, jnp.int32)])
    zero_block = ((block_idx >= n_used - 1) | (following != block_expert)).astype(jnp.int32)
    owner = block_expert[:, None] == jnp.arange(N_EXPERTS, dtype=jnp.int32)[None, :]
    seg_end = jnp.sum(jnp.where(owner, (pstarts + counts)[None, :], 0), axis=1)
    n_valid = jnp.where(block_idx < n_used, jnp.clip(seg_end - block_idx * blk, 0, blk), 0).astype(jnp.int32)
    packed_base = (jnp.sum(jnp.where(owner, (ustarts - pstarts)[None, :], 0), axis=1) + block_idx * blk)
    packed_base = jnp.where(n_valid > 0, packed_base, 0).astype(jnp.int32)
    return dict(dest=dest.astype(jnp.int32), packed=packed.astype(jnp.int32), zero_block=zero_block,
                block_expert=block_expert, n_used=n_used.reshape(1), n_valid=n_valid, packed_base=packed_base)


def _scatter_kernel(d0_ref, d1_ref, p0_ref, p1_ref, zb_ref, src_ref, o_ref, inv_ref, zero_buf, sem, zero_sem):
    rows = src_ref.shape[0] // SUBLANES
    base = pl.program_id(0) * rows
    fill_rows = zero_buf.shape[0]
    n_blocks = o_ref.shape[0] // fill_rows

    @pl.when(pl.program_id(0) == 0)
    def _():
        zero_buf[...] = jnp.zeros_like(zero_buf)

        def for_each_fill(fn):
            def body(j, carry):
                @pl.when(zb_ref[j] == 1)
                def _():
                    fn(pltpu.make_async_copy(
                        zero_buf, o_ref.at[pl.ds(pl.multiple_of(j * fill_rows, fill_rows), fill_rows)], zero_sem))
                return carry
            lax.fori_loop(0, n_blocks, body, 0)

        for_each_fill(lambda copy: copy.start())
        for_each_fill(lambda copy: copy.wait())

    def row_copy(r, slot):
        return pltpu.make_async_copy(src_ref.at[pl.ds(pl.multiple_of(r * SUBLANES, SUBLANES), SUBLANES)],
                                     o_ref.at[pl.ds(pl.multiple_of(slot * SUBLANES, SUBLANES), SUBLANES)], sem)

    def issue(r, carry):
        t = base + r
        s0 = d0_ref[t]
        s1 = d1_ref[t]
        row_copy(r, s0).start()
        row_copy(r, s1).start()
        inv_ref[p0_ref[t]] = t * TOP_K
        inv_ref[p1_ref[t]] = t * TOP_K + 1
        return carry

    lax.fori_loop(0, rows, issue, 0, unroll=8)
    for _ in range(TOP_K):
        pltpu.make_async_copy(src_ref, o_ref.at[pl.ds(0, rows * SUBLANES)], sem).wait()


def _scatter_rows(plan, src):
    dest, packed = plan["dest"], plan["packed"]
    T = dest.shape[0]
    P = T * TOP_K + N_EXPERTS * MOE_BLK
    rows = GATHER_ROWS
    return pl.pallas_call(
        _scatter_kernel,
        grid_spec=pltpu.PrefetchScalarGridSpec(
            num_scalar_prefetch=5,
            grid=(T // rows,),
            in_specs=[pl.BlockSpec((rows * SUBLANES, LANES), lambda i, *refs: (i, 0))],
            out_specs=[pl.BlockSpec(memory_space=pl.ANY), pl.BlockSpec(memory_space=pltpu.SMEM)],
            scratch_shapes=[
                pltpu.VMEM((MOE_BLK * SUBLANES, LANES), src.dtype),
                pltpu.SemaphoreType.DMA(()),
                pltpu.SemaphoreType.DMA(()),
            ],
        ),
        out_shape=[jax.ShapeDtypeStruct((P * SUBLANES, LANES), src.dtype),
                   jax.ShapeDtypeStruct((T * TOP_K,), jnp.int32)],
        compiler_params=_params(("arbitrary",)),
    )(dest[:, 0], dest[:, 1], packed[:, 0], packed[:, 1], plan["zero_block"], src)


def _expert_kernel(layer, be_ref, first_ref, next_ref, slot_ref, nu_ref, nv_ref, pb_ref, inv_ref,
                   xs_ref, w1_hbm, w3_hbm, w2_hbm, yt_ref,
                   w1f, w3f, w2f, w1b, w3b, w2b, ybuf, sems, ysems):
    i = pl.program_id(0)
    D = w1b.shape[0]
    blk = xs_ref.shape[0] // SUBLANES

    def drain_rows(j):
        n = nv_ref[j]
        b = j % 2
        for bit in range(blk.bit_length()):
            size = (1 << bit) * SUBLANES

            @pl.when((n >> bit) & 1 == 1)
            def _():
                pltpu.make_async_copy(ybuf.at[b, pl.ds(0, size)], yt_ref.at[pl.ds(0, size)], ysems.at[b]).wait()

    def weight_copies(e, slot):
        return (pltpu.make_async_copy(w1_hbm.at[layer, e], w1f.at[slot], sems.at[slot, 0]),
                pltpu.make_async_copy(w3_hbm.at[layer, e], w3f.at[slot], sems.at[slot, 1]),
                pltpu.make_async_copy(w2_hbm.at[layer, e], w2f.at[slot], sems.at[slot, 2]))

    @pl.when(i == 0)
    def _():
        for c in weight_copies(be_ref[0], 0):
            c.start()

    @pl.when(first_ref[i] == 1)
    def _():
        slot = slot_ref[i]
        for c in weight_copies(be_ref[i], slot):
            c.wait()

        @pl.when(next_ref[i] >= 0)
        def _():
            for c in weight_copies(next_ref[i], 1 - slot):
                c.start()

        w1b[...] = w1f[slot].astype(BF16)
        w3b[...] = w3f[slot].astype(BF16)
        w2b[...] = w2f[slot].astype(BF16)

    @pl.when(i >= 2)
    def _():
        drain_rows(i - 2)

    @pl.when(i < nu_ref[0])
    def _():
        lo, hi = _unpack_bf16_pair(_load_row_tiles(xs_ref))
        lo = lo.astype(BF16)
        hi = hi.astype(BF16)
        a = _dot(lo, w1b[0:D // 2, :]) + _dot(hi, w1b[D // 2:, :])
        g = _dot(lo, w3b[0:D // 2, :]) + _dot(hi, w3b[D // 2:, :])
        hid = ((a * _sigmoid(a)) * g).astype(BF16)
        b = i % 2
        _store_row_tiles(ybuf.at[b], _pack_bf16_pair(_dot(hid, w2b[:, 0:D // 2]), _dot(hid, w2b[:, D // 2:])))

        n = nv_ref[i]
        first_pos = pb_ref[i]
        unroll = 8

        def issue_row(r):
            a_idx = inv_ref[first_pos + r]
            pltpu.make_async_copy(ybuf.at[b, pl.ds(pl.multiple_of(r * SUBLANES, SUBLANES), SUBLANES)],
                                  yt_ref.at[pl.ds(pl.multiple_of(a_idx * SUBLANES, SUBLANES), SUBLANES)],
                                  ysems.at[b]).start()

        def issue_group(g, carry):
            for u in range(unroll):
                issue_row(g * unroll + u)
            return carry

        def issue_one(r, carry):
            issue_row(r)
            return carry

        lax.fori_loop(0, n // unroll, issue_group, 0)
        lax.fori_loop((n // unroll) * unroll, n, issue_one, 0)

    @pl.when(i == pl.num_programs(0) - 1)
    def _():
        drain_rows(i - 1)
        drain_rows(i)


def _segment_plan(block_expert, n_used):
    n = block_expert.shape[0]
    idx = jnp.arange(n, dtype=jnp.int32)
    prev = jnp.concatenate([jnp.full((1,), -1, jnp.int32), block_expert[:-1]])
    first = ((block_expert != prev) & (idx < n_used[0])).astype(jnp.int32)
    slot = (jnp.cumsum(first) - 1) % 2
    later_first = jnp.where(first == 1, idx, n)
    next_idx = lax.cummin(jnp.concatenate([later_first[1:], jnp.full((1,), n, jnp.int32)]), reverse=True)
    next_expert = jnp.where(next_idx < n, block_expert[jnp.minimum(next_idx, n - 1)], -1)
    return first, next_expert.astype(jnp.int32), slot.astype(jnp.int32)


def _experts(plan, inv, xs, w1, w3, w2, layer):
    _, _, D, DE = w1.shape
    blk = MOE_BLK
    block_expert, n_used = plan["block_expert"], plan["n_used"]
    n_blocks = block_expert.shape[0]
    n_assign = plan["dest"].shape[0] * TOP_K
    first, next_expert, slot = _segment_plan(block_expert, n_used)
    rowmap = lambda i, *refs: (jnp.minimum(i, jnp.maximum(refs[4][0] - 1, 0)), 0)
    hbm = pl.BlockSpec(memory_space=pl.ANY)
    return pl.pallas_call(
        functools.partial(_expert_kernel, layer),
        grid_spec=pltpu.PrefetchScalarGridSpec(
            num_scalar_prefetch=8,
            grid=(n_blocks,),
            in_specs=[pl.BlockSpec((blk * SUBLANES, LANES), rowmap), hbm, hbm, hbm],
            out_specs=hbm,
            scratch_shapes=[
                pltpu.VMEM((2, D, DE), F32),
                pltpu.VMEM((2, D, DE), F32),
                pltpu.VMEM((2, DE, D), F32),
                pltpu.VMEM((D, DE), BF16),
                pltpu.VMEM((D, DE), BF16),
                pltpu.VMEM((DE, D), BF16),
                pltpu.VMEM((2, blk * SUBLANES, LANES), U32),
                pltpu.SemaphoreType.DMA((2, 3)),
                pltpu.SemaphoreType.DMA((2,)),
            ],
        ),
        out_shape=jax.ShapeDtypeStruct((n_assign * SUBLANES, LANES), U32),
        compiler_params=_params(("arbitrary",)),
    )(block_expert, first, next_expert, slot, n_used, plan["n_valid"], plan["packed_base"], inv, xs, w1, w3, w2)


def _final_combine_kernel(yt_ref, h_ref, g2_ref, ew_ref, fg_ref, o_ref):
    hn = h_ref[...] + g2_ref[0] * _moe_mix(yt_ref, ew_ref)
    o_ref[...] = hn * lax.rsqrt(jnp.mean(hn * hn, axis=-1, keepdims=True) + EPS) * fg_ref[...]


def _final_combine(yt, h, mod3, ew, final_g, S):
    T, D = h.shape
    tc = 256
    per_b = S // tc
    return pl.pallas_call(
        _final_combine_kernel,
        grid=(T // tc,),
        in_specs=[
            pl.BlockSpec((tc * TOP_K * SUBLANES, LANES), lambda i: (i, 0)),
            pl.BlockSpec((tc, D), lambda i: (i, 0)),
            pl.BlockSpec((1, 1, D), lambda i: (i // per_b, 0, 5)),
            pl.BlockSpec((tc, LANES), lambda i: (i, 0)),
            pl.BlockSpec((1, D), lambda i: (0, 0)),
        ],
        out_specs=pl.BlockSpec((tc, D), lambda i: (i, 0)),
        out_shape=jax.ShapeDtypeStruct((T, D), F32),
        compiler_params=_params(("parallel",)),
    )(yt, h, mod3, ew, final_g.reshape(1, D))


def kernel(x, c, ada_w, ada_b, norm1_g, w_in, gla_w_a2, gla_b_a, gla_norm_g, lru_conv_w, lru_conv_b,
           lru_wa, lru_ba, lru_wx, lru_bx, lru_lambda, diff_lq1, diff_lk1, diff_lq2, diff_lk2,
           diff_subln_g, rel_bias, w_out, norm2_g, router_g_w, router_g_b, router_e_w, router_e_b,
           moe_w1, moe_w3, moe_w2, final_g):
    B, S, D = x.shape
    T = B * S
    L = ada_w.shape[0]
    h = x.reshape(T, D)
    mod = _ada_mod(c, ada_w, ada_b)
    bias = _bias_tiles(rel_bias)
    w_in_perm = _permute_w_in(w_in)
    w_out_bf16 = w_out.astype(BF16)
    pending_moe = None
    for l in range(L):
        mod3 = mod[l][:, None, :]
        if pending_moe is None:
            proj = _inproj(h, mod3, norm1_g[l], w_in_perm, l, S)
        else:
            proj, h = _inproj(h, mod3, norm1_g[l], w_in_perm, l, S, pending_moe)
        wa2_pad = jnp.concatenate(
            [gla_w_a2[l], jnp.zeros((LANES - GLA_LOWRANK, GLA_KEY_WIDTH), F32)], axis=0).astype(BF16)
        o_gla = _gla(proj, wa2_pad, gla_b_a[l], gla_norm_g[l], B, S)
        w_gates = jnp.concatenate([_block_diag(lru_wa[l]), _block_diag(lru_wx[l])], axis=1).astype(BF16)
        b_gates = jnp.concatenate([lru_ba[l], lru_bx[l]]).reshape(1, 2 * LRU_WIDTH)
        o_lru = _lru(proj.reshape(B, S, PROJ_WIDTH), lru_conv_w[l], lru_conv_b[l], w_gates, b_gates,
                     lru_lambda[l]).reshape(T, LRU_WIDTH)
        lqk = jnp.stack([diff_lq1[l], diff_lk1[l], diff_lq2[l], diff_lk2[l]], axis=0)
        o_diff = _diff_attention(proj, bias, lqk, diff_subln_g[l], l, B, S)
        rw = jnp.concatenate(
            [router_g_w[l], router_e_w[l], jnp.zeros((D, LANES - N_GROUPS - N_EXPERTS), F32)], axis=1).astype(BF16)
        rb = jnp.concatenate(
            [router_g_b[l], router_e_b[l], jnp.zeros((LANES - N_GROUPS - N_EXPERTS,), F32)]).reshape(1, LANES)
        h, u2, info, ew, cnt = _outproj(h, o_gla, o_lru, o_diff, w_out_bf16, l, mod3, norm2_g[l], rw, rb, S)
        counts = cnt[0, N_GROUPS:N_GROUPS + N_EXPERTS].astype(jnp.int32)
        plan = _dispatch(info, counts)
        xs, inv = _scatter_rows(plan, u2)
        yt = _experts(plan, inv, xs, moe_w1, moe_w3, moe_w2, l)
        pending_moe = (yt, ew, mod3)
    out = _final_combine(yt, h, mod3, ew, final_g, S)
    return out.reshape(B, S, D)
```

```python
import functools
import math

import jax
import jax.numpy as jnp
from jax import lax
from jax.experimental import pallas as pl
from jax.experimental.pallas import tpu as pltpu

F32 = jnp.float32
BF16 = jnp.bfloat16
U32 = jnp.uint32

EPS = 1e-6
LOG2E = math.log2(math.e)
CHUNK = 64

GLA_DV = 128
GLA_DK = 64
GLA_HEADS = 6
GLA_WIDTH = GLA_HEADS * GLA_DV
GLA_KEY_WIDTH = GLA_HEADS * GLA_DK
GLA_LOWRANK = 16
GLA_TAU = 16.0

LRU_WIDTH = 512
LRU_BLOCKS = 8
LRU_BLOCK_DIM = LRU_WIDTH // LRU_BLOCKS
CONV_WIDTH = 4
LRU_C = 8.0

DIFF_DH = 64
DIFF_DV = 128
DIFF_HEADS = 6
DIFF_WIDTH = DIFF_HEADS * DIFF_DV

REL_BUCKETS = 32
REL_MAX_DIST = 128

N_GROUPS = 8
EXPERTS_PER_GROUP = 8
N_EXPERTS = 64
TOP_K = 2

LANES = 128
SUBLANES = 8
VMEM_LIMIT = 56 * 1024 * 1024

COL_GV = 0
COL_GOG = 768
COL_DQ = 1536
COL_DK = 2304
COL_DV = 3072
COL_GQ = 3840
COL_GK = 4224
COL_LY = 4608
COL_LX = 5120
COL_GA = 5632
PROJ_WIDTH = 5760

ATT_TILE = 512
GLA_TILE = 256
LRU_TILE = 256
MOE_BLK = 256
GATHER_ROWS = 512


def _params(sem, vmem=VMEM_LIMIT):
    return pltpu.CompilerParams(dimension_semantics=sem, vmem_limit_bytes=vmem)


def _sigmoid(x):
    return 0.5 * jnp.tanh(0.5 * x) + 0.5


def _softplus(x):
    return jnp.maximum(x, 0.0) + jnp.log1p(jnp.exp(-jnp.abs(x)))


def _dot(a, b):
    return jnp.dot(a, b, preferred_element_type=F32)


def _dot_nt(a, b):
    return lax.dot_general(a, b, (((1,), (1,)), ((), ())), preferred_element_type=F32)


def _dot_tn(a, b):
    return lax.dot_general(a, b, (((0,), (0,)), ((), ())), preferred_element_type=F32)


def _pack_bf16_pair(lo, hi):
    lo_bits = lax.bitcast_convert_type(lo.astype(BF16).astype(F32), U32)
    hi_bits = lax.bitcast_convert_type(hi.astype(BF16).astype(F32), U32)
    return (hi_bits & jnp.uint32(0xFFFF0000)) | (lo_bits >> 16)


def _unpack_bf16_pair(w):
    lo = lax.bitcast_convert_type(w << 16, F32)
    hi = lax.bitcast_convert_type(w & jnp.uint32(0xFFFF0000), F32)
    return lo, hi


def _store_row_tiles(ref, words):
    rows = words.shape[0]
    for s in range(SUBLANES):
        ref[pl.ds(s, rows, stride=SUBLANES), :] = words[:, s * LANES:(s + 1) * LANES]


def _load_row_tiles(ref):
    rows = ref.shape[0] // SUBLANES
    return jnp.concatenate([ref[pl.ds(s, rows, stride=SUBLANES), :] for s in range(SUBLANES)], axis=1)


def _ada_kernel(c_ref, w_ref, b_ref, o_ref):
    c = c_ref[...]
    s = c * _sigmoid(c)
    o_ref[0] = _dot(s.astype(BF16), w_ref[0].astype(BF16)) + b_ref[0]


def _ada_mod(c, ada_w, ada_b):
    L, D, N = ada_w.shape
    B = c.shape[0]
    tn = 1024
    return pl.pallas_call(
        _ada_kernel,
        grid=(L, N // tn),
        in_specs=[
            pl.BlockSpec((B, D), lambda l, j: (0, 0)),
            pl.BlockSpec((1, D, tn), lambda l, j: (l, 0, j)),
            pl.BlockSpec((1, 1, tn), lambda l, j: (l, 0, j)),
        ],
        out_specs=pl.BlockSpec((1, B, tn), lambda l, j: (l, 0, j)),
        out_shape=jax.ShapeDtypeStruct((L, B, N), F32),
        compiler_params=_params(("parallel", "parallel")),
    )(c, ada_w, ada_b.reshape(L, 1, N))


def _modulated_norm(x, g, sc, sh):
    ms = jnp.mean(x * x, axis=-1, keepdims=True)
    return (x * lax.rsqrt(ms + EPS) * g) * (1.0 + sc) + sh


def _moe_mix(yt_ref, ew_ref):
    rows = ew_ref.shape[0]

    def expert_rows(k):
        return jnp.concatenate([yt_ref[pl.ds(k * SUBLANES + s, rows, stride=TOP_K * SUBLANES), :]
                                for s in range(SUBLANES)], axis=1)

    ew = ew_ref[...]
    w0 = ew[:, 0:1]
    w1 = ew[:, 1:2]
    lo0, hi0 = _unpack_bf16_pair(expert_rows(0))
    lo1, hi1 = _unpack_bf16_pair(expert_rows(1))
    return jnp.concatenate([w0 * lo0 + w1 * lo1, w0 * hi0 + w1 * hi1], axis=1)


def _inproj_kernel(h_ref, sh_ref, sc_ref, g_ref, w_ref, o_ref, u_scr):
    @pl.when(pl.program_id(1) == 0)
    def _():
        u = _modulated_norm(h_ref[...], g_ref[...], sc_ref[0], sh_ref[0])
        u_scr[...] = u.astype(BF16)

    o_ref[...] = _dot(u_scr[...], w_ref[...]).astype(o_ref.dtype)


def _inproj_after_moe_kernel(h_ref, yt_ref, ew_ref, g2_ref, sh_ref, sc_ref, g_ref, w_ref, o_ref, hn_ref, u_scr):
    @pl.when(pl.program_id(1) == 0)
    def _():
        hn = h_ref[...] + g2_ref[0] * _moe_mix(yt_ref, ew_ref)
        hn_ref[...] = hn
        u = _modulated_norm(hn, g_ref[...], sc_ref[0], sh_ref[0])
        u_scr[...] = u.astype(BF16)

    o_ref[...] = _dot(u_scr[...], w_ref[...]).astype(o_ref.dtype)


def _inproj(h, mod3, norm_g, w_perm, layer, S, pending_moe=None):
    T, D = h.shape
    N = w_perm.shape[2]
    tm, tn = 512, 1152
    per_b = S // tm
    rows = pl.BlockSpec((tm, D), lambda i, j: (i, 0))
    mod_specs = [
        pl.BlockSpec((1, 1, D), lambda i, j: (i // per_b, 0, 0)),
        pl.BlockSpec((1, 1, D), lambda i, j: (i // per_b, 0, 1)),
        pl.BlockSpec((1, D), lambda i, j: (0, 0)),
        pl.BlockSpec((None, D, tn), lambda i, j: (layer, 0, j)),
    ]
    proj_spec = pl.BlockSpec((tm, tn), lambda i, j: (i, j))
    proj_shape = jax.ShapeDtypeStruct((T, N), BF16)
    common = dict(grid=(T // tm, N // tn), scratch_shapes=[pltpu.VMEM((tm, D), BF16)],
                  compiler_params=_params(("parallel", "arbitrary")))
    if pending_moe is None:
        return pl.pallas_call(_inproj_kernel, in_specs=[rows] + mod_specs, out_specs=proj_spec,
                              out_shape=proj_shape, **common)(h, mod3, mod3, norm_g.reshape(1, D), w_perm)
    yt, ew, prev_mod3 = pending_moe
    moe_specs = [
        pl.BlockSpec((tm * TOP_K * SUBLANES, LANES), lambda i, j: (i, 0)),
        pl.BlockSpec((tm, LANES), lambda i, j: (i, 0)),
        pl.BlockSpec((1, 1, D), lambda i, j: (i // per_b, 0, 5)),
    ]
    return pl.pallas_call(
        _inproj_after_moe_kernel, in_specs=[rows] + moe_specs + mod_specs, out_specs=[proj_spec, rows],
        out_shape=[proj_shape, jax.ShapeDtypeStruct((T, D), F32)], **common,
    )(h, yt, ew, prev_mod3, mod3, mod3, norm_g.reshape(1, D), w_perm)


W_IN_SEGMENTS = ((COL_GQ, 0, 384), (COL_GK, 384, 384), (COL_GV, 768, 768), (COL_GOG, 1536, 768),
                 (COL_GA, 2304, GLA_LOWRANK), (COL_LY, 2320, 512), (COL_LX, 2832, 512),
                 (COL_DQ, 3344, 768), (COL_DK, 4112, 768), (COL_DV, 4880, 768))


def _relayout_kernel(w_ref, o_ref):
    x = w_ref[0]
    for dst, src, width in W_IN_SEGMENTS:
        o_ref[0, :, dst:dst + width] = x[:, src:src + width]
    pad = slice(COL_GA + GLA_LOWRANK, COL_GA + LANES)
    o_ref[0, :, pad] = jnp.zeros((x.shape[0], LANES - GLA_LOWRANK), BF16)


def _permute_w_in(w):
    L, D, N = w.shape
    rt = 256
    return pl.pallas_call(
        _relayout_kernel,
        grid=(L, D // rt),
        in_specs=[pl.BlockSpec((1, rt, N), lambda l, i: (l, i, 0))],
        out_specs=pl.BlockSpec((1, rt, PROJ_WIDTH), lambda l, i: (l, i, 0)),
        out_shape=jax.ShapeDtypeStruct((L, D, PROJ_WIDTH), BF16),
        compiler_params=_params(("parallel", "parallel")),
    )(w.astype(BF16))


def _gla_kernel(q_ref, k_ref, v_ref, og_ref, alr_ref, wa2_ref, ba_ref, ng_ref, o_ref, st_ref):
    tb = q_ref.shape[0]
    n_chunks = tb // CHUNK

    @pl.when(pl.program_id(1) == 0)
    def _():
        st_ref[...] = jnp.zeros_like(st_ref)

    row = lax.broadcasted_iota(jnp.int32, (tb, tb), 0)
    col = lax.broadcasted_iota(jnp.int32, (tb, tb), 1)
    same_chunk = (row // CHUNK) == (col // CHUNK)
    causal = col <= row
    tril = jnp.where(same_chunk & causal, 1.0, 0.0).astype(BF16)
    lane = lax.broadcasted_iota(jnp.int32, (1, LANES), 1)
    half_masks = (lane < GLA_DK, lane >= GLA_DK)

    alr = alr_ref[...]
    for p in range(GLA_HEADS // 2):
        cs = slice(p * LANES, (p + 1) * LANES)
        z = _dot(alr, wa2_ref[:, cs]) + ba_ref[:, cs]
        la = (jnp.minimum(z, 0.0) - jnp.log1p(jnp.exp(-jnp.abs(z)))) * (1.0 / GLA_TAU)
        la_hi = la.astype(BF16)
        la_lo = (la - la_hi.astype(F32)).astype(BF16)
        G = _dot(tril, la_hi) + _dot(tril, la_lo)
        Gl = jnp.concatenate(
            [jnp.broadcast_to(G[(c + 1) * CHUNK - 1:(c + 1) * CHUNK, :], (CHUNK, LANES)) for c in range(n_chunks)],
            axis=0)
        eG = jnp.exp(G)
        enG = jnp.exp(-G)
        q = q_ref[:, cs].astype(F32) * (GLA_DK ** -0.5)
        k = k_ref[:, cs].astype(F32)
        qf = q * eG
        qb = q * enG
        kf = (k * eG).astype(BF16)
        kb = (k * enG).astype(BF16)
        kd = k * jnp.exp(Gl - G)
        for hh in range(2):
            head = 2 * p + hh
            m = half_masks[hh]
            vs = slice(head * GLA_DV, (head + 1) * GLA_DV)
            qf_h = jnp.where(m, qf, 0.0).astype(BF16)
            qb_h = jnp.where(m, qb, 0.0).astype(BF16)
            kd_h = jnp.where(m, kd, 0.0).astype(BF16)
            v_h = v_ref[:, vs]
            a_f = _dot_nt(qf_h, kb)
            a_b = _dot_nt(qb_h, kf)
            attn = jnp.where(same_chunk, jnp.where(causal, a_f, a_b), 0.0)
            o_intra = _dot(attn.astype(BF16), v_h)
            st = st_ref[head]
            inter = []
            for c in range(n_chunks):
                rs = slice(c * CHUNK, (c + 1) * CHUNK)
                inter.append(_dot_nt(qf_h[rs], st.astype(BF16)))
                decay = jnp.exp(Gl[c * CHUNK:c * CHUNK + 1, :])
                st = st * decay + _dot_tn(v_h[rs], kd_h[rs])
            st_ref[head] = st
            o = o_intra + jnp.concatenate(inter, axis=0)
            o = o * lax.rsqrt(jnp.mean(o * o, axis=-1, keepdims=True) + EPS)
            og = og_ref[:, vs].astype(F32)
            o_ref[:, vs] = (o * ng_ref[:, vs] * (og * _sigmoid(og))).astype(o_ref.dtype)


def _gla(proj, wa2_pad, b_a, norm_g, B, S):
    T = proj.shape[0]
    tb = GLA_TILE
    nt = S // tb
    row = lambda b, i: b * nt + i
    return pl.pallas_call(
        _gla_kernel,
        grid=(B, nt),
        in_specs=[
            pl.BlockSpec((tb, GLA_KEY_WIDTH), lambda b, i: (row(b, i), COL_GQ // GLA_KEY_WIDTH)),
            pl.BlockSpec((tb, GLA_KEY_WIDTH), lambda b, i: (row(b, i), COL_GK // GLA_KEY_WIDTH)),
            pl.BlockSpec((tb, GLA_WIDTH), lambda b, i: (row(b, i), COL_GV // GLA_WIDTH)),
            pl.BlockSpec((tb, GLA_WIDTH), lambda b, i: (row(b, i), COL_GOG // GLA_WIDTH)),
            pl.BlockSpec((tb, LANES), lambda b, i: (row(b, i), COL_GA // LANES)),
            pl.BlockSpec((LANES, GLA_KEY_WIDTH), lambda b, i: (0, 0)),
            pl.BlockSpec((1, GLA_KEY_WIDTH), lambda b, i: (0, 0)),
            pl.BlockSpec((1, GLA_WIDTH), lambda b, i: (0, 0)),
        ],
        out_specs=pl.BlockSpec((tb, GLA_WIDTH), lambda b, i: (row(b, i), 0)),
        out_shape=jax.ShapeDtypeStruct((T, GLA_WIDTH), BF16),
        scratch_shapes=[pltpu.VMEM((GLA_HEADS, GLA_DV, LANES), F32)],
        compiler_params=_params(("parallel", "arbitrary")),
    )(proj, proj, proj, proj, proj, wa2_pad, b_a.reshape(1, -1), norm_g.reshape(1, -1))


def _lru_kernel(y_ref, x_ref, cw_ref, cb_ref, wg_ref, bg_ref, lam_ref, o_ref, *scratch):
    B, ts, W = x_ref.shape
    n_planes = W // LANES
    a_scr = scratch[0:n_planes]
    b_scr = scratch[n_planes:2 * n_planes]
    h_scr = scratch[2 * n_planes:3 * n_planes]
    xc_scr, tail_scr, carry_scr = scratch[3 * n_planes:]

    @pl.when(pl.program_id(0) == 0)
    def _():
        tail_scr[...] = jnp.zeros_like(tail_scr)
        carry_scr[...] = jnp.zeros_like(carry_scr)

    cw = cw_ref[...]
    cb = cb_ref[...]
    sp = _softplus(-lam_ref[...])
    row8 = lax.broadcasted_iota(jnp.int32, (8, W), 0)
    for b in range(B):
        x = x_ref[b].astype(F32)
        tail = tail_scr[b]
        xc = cb + cw[CONV_WIDTH - 1:CONV_WIDTH, :] * x
        head = cb + cw[CONV_WIDTH - 1:CONV_WIDTH, :] * x[0:8]
        for d in range(1, CONV_WIDTH):
            wd = cw[CONV_WIDTH - 1 - d:CONV_WIDTH - d, :]
            xr = pltpu.roll(x, d, 0)
            xc = xc + wd * xr
            head = head + wd * jnp.where(row8 < d, pltpu.roll(tail, d, 0), xr[0:8])
        tail_scr[b] = x[ts - 8:ts]
        xc_scr[...] = xc
        xc_scr[0:8] = head
        xc = xc_scr[...]
        gates = _sigmoid(_dot(xc.astype(BF16), wg_ref[...]) + bg_ref[...])
        r = gates[:, :W]
        ig = gates[:, W:]
        log_a = (-LRU_C) * r * sp
        a = jnp.exp(log_a)
        b_in = jnp.sqrt(-jnp.tanh(log_a) * (a * a + 1.0)) * (ig * xc)
        rows = slice(b * ts, (b + 1) * ts)
        for k in range(n_planes):
            a_scr[k][rows] = a[:, k * LANES:(k + 1) * LANES]
            b_scr[k][rows] = b_in[:, k * LANES:(k + 1) * LANES]

    def step(t, hs):
        idx = pl.ds(t, B, stride=ts)
        out = []
        for k in range(n_planes):
            hk = a_scr[k][idx, :] * hs[k] + b_scr[k][idx, :]
            h_scr[k][idx, :] = hk
            out.append(hk)
        return tuple(out)

    hs = lax.fori_loop(0, ts, step, tuple(carry_scr[k] for k in range(n_planes)), unroll=8)
    for k in range(n_planes):
        carry_scr[k] = hs[k]

    for b in range(B):
        rows = slice(b * ts, (b + 1) * ts)
        y = y_ref[b].astype(F32)
        gelu = 0.5 * y * (1.0 + jnp.tanh(math.sqrt(2.0 / math.pi) * (y + 0.044715 * (y * y * y))))
        h = jnp.concatenate([h_scr[k][rows] for k in range(n_planes)], axis=1)
        o_ref[b] = (h * gelu).astype(o_ref.dtype)


def _lru(proj3, conv_w, conv_b, w_gates, b_gates, lam):
    B, S, _ = proj3.shape
    W = LRU_WIDTH
    ts = LRU_TILE
    n_planes = W // LANES
    full = lambda shape: pl.BlockSpec(shape, lambda i: (0,) * len(shape))
    return pl.pallas_call(
        _lru_kernel,
        grid=(S // ts,),
        in_specs=[
            pl.BlockSpec((B, ts, W), lambda i: (0, i, COL_LY // W)),
            pl.BlockSpec((B, ts, W), lambda i: (0, i, COL_LX // W)),
            full((CONV_WIDTH, W)),
            full((1, W)),
            full((W, 2 * W)),
            full((1, 2 * W)),
            full((1, W)),
        ],
        out_specs=pl.BlockSpec((B, ts, W), lambda i: (0, i, 0)),
        out_shape=jax.ShapeDtypeStruct((B, S, W), BF16),
        scratch_shapes=(
            [pltpu.VMEM((B * ts, LANES), F32) for _ in range(3 * n_planes)]
            + [pltpu.VMEM((ts, W), F32), pltpu.VMEM((B, 8, W), F32), pltpu.VMEM((n_planes, B, LANES), F32)]),
        compiler_params=_params(("arbitrary",)),
    )(proj3, proj3, conv_w, conv_b.reshape(1, W), w_gates, b_gates, lam.reshape(1, W))


def _block_diag(w):
    n, d, _ = w.shape
    eye = jnp.eye(n, dtype=w.dtype)
    return (eye[:, None, :, None] * w[:, :, None, :]).reshape(n * d, n * d)


def _t5_bucket(rel):
    nb = REL_BUCKETS // 2
    ret = (rel > 0).astype(jnp.int32) * nb
    n = jnp.abs(rel)
    max_exact = nb // 2
    nf = jnp.maximum(n, 1).astype(jnp.float32)
    large = max_exact + (jnp.log(nf / max_exact) / math.log(REL_MAX_DIST / max_exact)
                         * (nb - max_exact)).astype(jnp.int32)
    large = jnp.minimum(large, nb - 1)
    return ret + jnp.where(n < max_exact, n, large)


def _bias_kernel(bucket_ref, table_ref, o_ref):
    h = pl.program_id(0)
    bucket = bucket_ref[0]
    acc = jnp.full(bucket.shape, -1e30, F32)
    for b in range(REL_BUCKETS):
        acc = jnp.where(bucket == b, table_ref[b, h] * LOG2E, acc)
    o_ref[0, 0] = acc


def _bias_tiles(rel_bias):
    t = ATT_TILE
    H = rel_bias.shape[1]
    qp = jnp.arange(t, dtype=jnp.int32)[:, None]
    kp = jnp.arange(t, dtype=jnp.int32)[None, :]
    mask = (kp // CHUNK) <= (qp // CHUNK)
    half = REL_BUCKETS // 2
    per_distance = _t5_bucket(-jnp.arange(2 * t, dtype=jnp.int32))
    edges = jnp.sum((per_distance[None, :] < jnp.arange(1, half, dtype=jnp.int32)[:, None]).astype(jnp.int32), axis=1)

    def bucket_2d(rel):
        passed = jnp.sum((jnp.abs(rel)[None] >= edges[:, None, None]).astype(jnp.int32), axis=0)
        return (rel > 0).astype(jnp.int32) * half + passed

    buckets = jnp.stack([jnp.where(mask, bucket_2d(kp - qp), REL_BUCKETS), bucket_2d(kp - t - qp)], axis=0)
    table = rel_bias.astype(F32)
    tiles = pl.pallas_call(
        _bias_kernel,
        grid=(H, 2),
        in_specs=[
            pl.BlockSpec((1, t, t), lambda h, k: (k, 0, 0)),
            pl.BlockSpec(memory_space=pltpu.SMEM),
        ],
        out_specs=pl.BlockSpec((1, 1, t, t), lambda h, k: (h, k, 0, 0)),
        out_shape=jax.ShapeDtypeStruct((H, 2, t, t), F32),
        compiler_params=_params(("parallel", "parallel")),
    )(buckets, table)
    far_bucket = _t5_bucket(jnp.full((1,), -t - 1, jnp.int32))
    far = jnp.sum(jnp.where(jnp.arange(REL_BUCKETS)[:, None] == far_bucket, table, 0.0), axis=0)
    return tiles, jnp.broadcast_to((far * LOG2E)[:, None, None], (H, 1, t))


def _diff_kernel(lam_init, q_ref, k_ref, v_ref, bias_ref, far_ref, lqk_ref, g_ref, o_ref,
                 qs_scr, m_scr, l_scr, acc_scr):
    i = pl.program_id(2)
    t = q_ref.shape[0]
    hq = t // 2
    lane = lax.broadcasted_iota(jnp.int32, (1, LANES), 1)
    q = q_ref[...].astype(F32) * (LOG2E * DIFF_DH ** -0.5)
    for half in range(2):
        qh = q[half * hq:(half + 1) * hq]
        qs_scr[(2 * half) * hq:(2 * half + 1) * hq] = jnp.where(lane < DIFF_DH, qh, 0.0).astype(BF16)
        qs_scr[(2 * half + 1) * hq:(2 * half + 2) * hq] = jnp.where(lane >= DIFF_DH, qh, 0.0).astype(BF16)
    m_scr[...] = jnp.full_like(m_scr, -1e30)
    l_scr[...] = jnp.zeros_like(l_scr)
    acc_scr[...] = jnp.zeros_like(acc_scr)

    def tile(rows, ks, bias):
        s = _dot_nt(qs_scr[rows, :], k_ref[ks, :]) + bias
        groups = [s[:, c * LANES:(c + 1) * LANES] for c in range(s.shape[1] // LANES)]
        mx = functools.reduce(jnp.maximum, groups)
        m_prev = m_scr[rows, :]
        m_new = jnp.maximum(m_prev, jnp.max(mx, axis=-1, keepdims=True))
        alpha = jnp.exp2(m_prev - m_new)
        ps = [jnp.exp2(g - m_new) for g in groups]
        l_scr[rows, :] = alpha * l_scr[rows, :] + functools.reduce(jnp.add, ps)
        p = jnp.concatenate(ps, axis=1).astype(BF16)
        acc_scr[rows, :] = alpha * acc_scr[rows, :] + _dot(p, v_ref[ks, :])
        m_scr[rows, :] = m_new

    def stacked(b, half):
        bh = b[half * hq:(half + 1) * hq]
        return [bh, bh]

    all_rows = slice(0, 2 * t)

    def far_body(j, carry):
        tile(all_rows, pl.ds(pl.multiple_of(j * t, t), t), far_ref[0])
        return carry

    lax.fori_loop(0, jnp.maximum(i - 1, 0), far_body, 0)

    @pl.when(i >= 1)
    def _():
        b = bias_ref[0, 1]
        tile(all_rows, pl.ds(pl.multiple_of((i - 1) * t, t), t),
             jnp.concatenate(stacked(b, 0) + stacked(b, 1), axis=0))

    b = bias_ref[0, 0]
    diag0 = pl.multiple_of(i * t, t)
    tile(slice(0, t), pl.ds(diag0, hq), jnp.concatenate(stacked(b[:, 0:hq], 0), axis=0))
    tile(slice(t, 2 * t), pl.ds(diag0, t), jnp.concatenate(stacked(b, 1), axis=0))

    lqk = lqk_ref[...]
    lam = (jnp.exp(jnp.sum(lqk[0:1] * lqk[1:2], axis=-1, keepdims=True))
           - jnp.exp(jnp.sum(lqk[2:3] * lqk[3:4], axis=-1, keepdims=True)) + lam_init)
    o = acc_scr[...] / jnp.sum(l_scr[...], axis=-1, keepdims=True)
    o = jnp.concatenate([o[0:hq] - lam * o[hq:t], o[t:t + hq] - lam * o[t + hq:2 * t]], axis=0)
    o = o * lax.rsqrt(jnp.mean(o * o, axis=-1, keepdims=True) + EPS)
    o_ref[...] = (o * g_ref[...] * (1.0 - lam_init)).astype(o_ref.dtype)


def _diff_attention(proj, bias, lqk, subln_g, layer_idx, B, S):
    T = proj.shape[0]
    t = ATT_TILE
    nq = S // t
    tiles, far = bias
    lam_init = 0.8 - 0.6 * math.exp(-0.3 * layer_idx)
    return pl.pallas_call(
        functools.partial(_diff_kernel, lam_init),
        grid=(B, DIFF_HEADS, nq),
        in_specs=[
            pl.BlockSpec((t, LANES), lambda b, h, i: (b * nq + i, COL_DQ // LANES + h)),
            pl.BlockSpec((S, LANES), lambda b, h, i: (b, COL_DK // LANES + h)),
            pl.BlockSpec((S, LANES), lambda b, h, i: (b, COL_DV // LANES + h)),
            pl.BlockSpec((1, 2, t, t), lambda b, h, i: (h, 0, 0, 0)),
            pl.BlockSpec((1, 1, t), lambda b, h, i: (h, 0, 0)),
            pl.BlockSpec((4, DIFF_DH), lambda b, h, i: (0, 0)),
            pl.BlockSpec((1, DIFF_DV), lambda b, h, i: (0, 0)),
        ],
        out_specs=pl.BlockSpec((t, LANES), lambda b, h, i: (b * nq + i, h)),
        out_shape=jax.ShapeDtypeStruct((T, DIFF_WIDTH), BF16),
        scratch_shapes=[
            pltpu.VMEM((2 * t, LANES), BF16),
            pltpu.VMEM((2 * t, LANES), F32),
            pltpu.VMEM((2 * t, LANES), F32),
            pltpu.VMEM((2 * t, DIFF_DV), F32),
        ],
        compiler_params=_params(("parallel", "parallel", "arbitrary")),
    )(proj, proj, proj, tiles, far, lqk, subln_g.reshape(1, DIFF_DV))


def _outproj_kernel(h_ref, og_ref, ol_ref, od_ref, w_ref, g1_ref, sh2_ref, sc2_ref, n2_ref, rw_ref, rb_ref,
                    hn_ref, u2_ref, eid_ref, ew_ref, cnt_ref):
    tm = h_ref.shape[0]

    @pl.when(pl.program_id(0) == 0)
    def _():
        cnt_ref[...] = jnp.zeros_like(cnt_ref)

    nr = tm // 2
    for half in range(2):
        _outproj_rows(slice(half * nr, (half + 1) * nr), h_ref, og_ref, ol_ref, od_ref, w_ref, g1_ref, sh2_ref,
                      sc2_ref, n2_ref, rw_ref, rb_ref, hn_ref, u2_ref, eid_ref, ew_ref, cnt_ref)


def _outproj_rows(rows, h_ref, og_ref, ol_ref, od_ref, w_ref, g1_ref, sh2_ref, sc2_ref, n2_ref, rw_ref, rb_ref,
                  hn_ref, u2_ref, eid_ref, ew_ref, cnt_ref):
    D = h_ref.shape[1]
    nr = rows.stop - rows.start
    acc = _dot(og_ref[rows, :], w_ref[0:GLA_WIDTH, :])
    acc += _dot(ol_ref[rows, :], w_ref[GLA_WIDTH:GLA_WIDTH + LRU_WIDTH, :])
    acc += _dot(od_ref[rows, :], w_ref[GLA_WIDTH + LRU_WIDTH:, :])
    hn = h_ref[rows, :] + g1_ref[0] * acc
    hn_ref[rows, :] = hn
    u2 = _modulated_norm(hn, n2_ref[...], sc2_ref[0], sh2_ref[0])
    _store_row_tiles(u2_ref.at[pl.ds(rows.start * SUBLANES, nr * SUBLANES)],
                     _pack_bf16_pair(u2[:, :D // 2], u2[:, D // 2:]))

    logits = _dot(u2.astype(BF16), rw_ref[...]) + rb_ref[...]
    lane = lax.broadcasted_iota(jnp.int32, logits.shape, 1)
    lane_f = lane.astype(F32)
    neg = jnp.float32(-jnp.inf)
    gmask = lane < N_GROUPS
    gl = jnp.where(gmask, logits, neg)
    gmax = jnp.max(gl, axis=-1, keepdims=True)
    gidx = jnp.min(jnp.where(gl == gmax, lane_f, float(LANES)), axis=-1, keepdims=True)
    g_w = 1.0 / jnp.sum(jnp.where(gmask, jnp.exp(gl - gmax), 0.0), axis=-1, keepdims=True)
    egroup = ((lane - N_GROUPS) >> 3).astype(F32)
    emask = (lane >= N_GROUPS) & (lane < N_GROUPS + N_EXPERTS) & (egroup == gidx)
    el = jnp.where(emask, logits, neg)
    v1 = jnp.max(el, axis=-1, keepdims=True)
    i1 = jnp.min(jnp.where(el == v1, lane_f, float(LANES)), axis=-1, keepdims=True)
    el2 = jnp.where(lane_f == i1, neg, el)
    v2 = jnp.max(el2, axis=-1, keepdims=True)
    i2 = jnp.min(jnp.where(el2 == v2, lane_f, float(LANES)), axis=-1, keepdims=True)
    e21 = jnp.exp(v2 - v1)
    w1 = g_w / (1.0 + e21)
    w2 = g_w * e21 / (1.0 + e21)
    ew_ref[rows, :] = jnp.where(lane == 0, w1, jnp.where(lane == 1, w2, 0.0))

    oh1 = lane_f == i1
    oh2 = lane_f == i2
    both = jnp.where(oh1 | oh2, 1.0, 0.0).astype(BF16)
    row = lax.broadcasted_iota(jnp.int32, (nr, nr), 0)
    col = lax.broadcasted_iota(jnp.int32, (nr, nr), 1)
    earlier = _dot(jnp.where(col < row, 1.0, 0.0).astype(BF16), both) + cnt_ref[0:1, :]
    rank1 = jnp.sum(jnp.where(oh1, earlier, 0.0), axis=-1, keepdims=True)
    rank2 = jnp.sum(jnp.where(oh2, earlier, 0.0), axis=-1, keepdims=True)
    cnt_ref[0:1, :] = cnt_ref[0:1, :] + jnp.sum(both.astype(F32), axis=0, keepdims=True)
    info = jnp.where(lane == 0, i1 - float(N_GROUPS),
                     jnp.where(lane == 1, i2 - float(N_GROUPS),
                               jnp.where(lane == 2, rank1, jnp.where(lane == 3, rank2, 0.0))))
    eid_ref[rows, :] = info.astype(jnp.int32)


def _outproj(h, o_gla, o_lru, o_diff, w_out, layer, mod3, norm2_g, rw, rb, S):
    T, D = h.shape
    tm = 512
    per_b = S // tm
    rowblk = lambda width: pl.BlockSpec((tm, width), lambda i: (i, 0))
    modblk = lambda k: pl.BlockSpec((1, 1, D), lambda i: (i // per_b, 0, k))
    full = lambda shape: pl.BlockSpec(shape, lambda i: (0,) * len(shape))
    return pl.pallas_call(
        _outproj_kernel,
        grid=(T // tm,),
        in_specs=[
            rowblk(D), rowblk(GLA_WIDTH), rowblk(LRU_WIDTH), rowblk(DIFF_WIDTH),
            pl.BlockSpec((None, D, D), lambda i: (layer, 0, 0)),
            modblk(2), modblk(3), modblk(4),
            full((1, D)),
            full((D, LANES)),
            full((1, LANES)),
        ],
        out_specs=[rowblk(D), pl.BlockSpec((tm * SUBLANES, LANES), lambda i: (i, 0)), rowblk(LANES), rowblk(LANES),
                   full((SUBLANES, LANES))],
        out_shape=[
            jax.ShapeDtypeStruct((T, D), F32),
            jax.ShapeDtypeStruct((T * SUBLANES, LANES), U32),
            jax.ShapeDtypeStruct((T, LANES), jnp.int32),
            jax.ShapeDtypeStruct((T, LANES), F32),
            jax.ShapeDtypeStruct((SUBLANES, LANES), F32),
        ],
        compiler_params=_params(("arbitrary",)),
    )(h, o_gla, o_lru, o_diff, w_out, mod3, mod3, mod3, norm2_g.reshape(1, D), rw, rb)


def _dispatch(info, counts):
    T = info.shape[0]
    blk = MOE_BLK
    n_blocks = (T * TOP_K) // blk + N_EXPERTS
    expert = info[:, 0:TOP_K]
    rank = info[:, TOP_K:2 * TOP_K]
    padded = (counts + blk - 1) // blk * blk
    pends = jnp.cumsum(padded)
    pstarts = pends - padded
    ustarts = jnp.cumsum(counts) - counts
    onehot = expert[:, :, None] == jnp.arange(N_EXPERTS, dtype=jnp.int32)[None, None, :]
    dest = rank + jnp.sum(jnp.where(onehot, pstarts[None, None, :], 0), axis=-1)
    packed = rank + jnp.sum(jnp.where(onehot, ustarts[None, None, :], 0), axis=-1)
    n_used = (pends[-1] // blk).astype(jnp.int32)
    block_idx = jnp.arange(n_blocks, dtype=jnp.int32)
    block_expert = jnp.minimum(jnp.sum((pends[None, :] <= (block_idx * blk)[:, None]).astype(jnp.int32), axis=1),
                               N_EXPERTS - 1)
    last_used = jnp.sum(jnp.where(block_idx == jnp.maximum(n_used - 1, 0), block_expert, 0))
    block_expert = jnp.where(block_idx < n_used, block_expert, last_used).astype(jnp.int32)
    following = jnp.concatenate([block_expert[1:], jnp.full((1,), -1, jnp.int32)])
    zero_block = ((block_idx >= n_used - 1) | (following != block_expert)).astype(jnp.int32)
    owner = block_expert[:, None] == jnp.arange(N_EXPERTS, dtype=jnp.int32)[None, :]
    seg_end = jnp.sum(jnp.where(owner, (pstarts + counts)[None, :], 0), axis=1)
    n_valid = jnp.where(block_idx < n_used, jnp.clip(seg_end - block_idx * blk, 0, blk), 0).astype(jnp.int32)
    packed_base = (jnp.sum(jnp.where(owner, (ustarts - pstarts)[None, :], 0), axis=1) + block_idx * blk)
    packed_base = jnp.where(n_valid > 0, packed_base, 0).astype(jnp.int32)
    return dict(dest=dest.astype(jnp.int32), packed=packed.astype(jnp.int32), zero_block=zero_block,
                block_expert=block_expert, n_used=n_used.reshape(1), n_valid=n_valid, packed_base=packed_base)


def _scatter_kernel(d0_ref, d1_ref, p0_ref, p1_ref, zb_ref, src_ref, o_ref, inv_ref, zero_buf, sem, zero_sem):
    rows = src_ref.shape[0] // SUBLANES
    base = pl.program_id(0) * rows
    fill_rows = zero_buf.shape[0]
    n_blocks = o_ref.shape[0] // fill_rows

    @pl.when(pl.program_id(0) == 0)
    def _():
        zero_buf[...] = jnp.zeros_like(zero_buf)

        def for_each_fill(fn):
            def body(j, carry):
                @pl.when(zb_ref[j] == 1)
                def _():
                    fn(pltpu.make_async_copy(
                        zero_buf, o_ref.at[pl.ds(pl.multiple_of(j * fill_rows, fill_rows), fill_rows)], zero_sem))
                return carry
            lax.fori_loop(0, n_blocks, body, 0)

        for_each_fill(lambda copy: copy.start())
        for_each_fill(lambda copy: copy.wait())

    def row_copy(r, slot):
        return pltpu.make_async_copy(src_ref.at[pl.ds(pl.multiple_of(r * SUBLANES, SUBLANES), SUBLANES)],
                                     o_ref.at[pl.ds(pl.multiple_of(slot * SUBLANES, SUBLANES), SUBLANES)], sem)

    def issue(r, carry):
        t = base + r
        s0 = d0_ref[t]
        s1 = d1_ref[t]
        row_copy(r, s0).start()
        row_copy(r, s1).start()
        inv_ref[p0_ref[t]] = t * TOP_K
        inv_ref[p1_ref[t]] = t * TOP_K + 1
        return carry

    lax.fori_loop(0, rows, issue, 0, unroll=8)
    for _ in range(TOP_K):
        pltpu.make_async_copy(src_ref, o_ref.at[pl.ds(0, rows * SUBLANES)], sem).wait()


def _scatter_rows(plan, src):
    dest, packed = plan["dest"], plan["packed"]
    T = dest.shape[0]
    P = T * TOP_K + N_EXPERTS * MOE_BLK
    rows = GATHER_ROWS
    return pl.pallas_call(
        _scatter_kernel,
        grid_spec=pltpu.PrefetchScalarGridSpec(
            num_scalar_prefetch=5,
            grid=(T // rows,),
            in_specs=[pl.BlockSpec((rows * SUBLANES, LANES), lambda i, *refs: (i, 0))],
            out_specs=[pl.BlockSpec(memory_space=pl.ANY), pl.BlockSpec(memory_space=pltpu.SMEM)],
            scratch_shapes=[
                pltpu.VMEM((MOE_BLK * SUBLANES, LANES), src.dtype),
                pltpu.SemaphoreType.DMA(()),
                pltpu.SemaphoreType.DMA(()),
            ],
        ),
        out_shape=[jax.ShapeDtypeStruct((P * SUBLANES, LANES), src.dtype),
                   jax.ShapeDtypeStruct((T * TOP_K,), jnp.int32)],
        compiler_params=_params(("arbitrary",)),
    )(dest[:, 0], dest[:, 1], packed[:, 0], packed[:, 1], plan["zero_block"], src)


def _expert_kernel(layer, be_ref, first_ref, next_ref, slot_ref, nu_ref, nv_ref, pb_ref, inv_ref,
                   xs_ref, w1_hbm, w3_hbm, w2_hbm, yt_ref,
                   w1f, w3f, w2f, w1b, w3b, w2b, ybuf, sems, ysems):
    i = pl.program_id(0)
    D = w1b.shape[0]
    blk = xs_ref.shape[0] // SUBLANES

    def drain_rows(j):
        n = nv_ref[j]
        b = j % 2
        for bit in range(blk.bit_length()):
            size = (1 << bit) * SUBLANES

            @pl.when((n >> bit) & 1 == 1)
            def _():
                pltpu.make_async_copy(ybuf.at[b, pl.ds(0, size)], yt_ref.at[pl.ds(0, size)], ysems.at[b]).wait()

    def weight_copies(e, slot):
        return (pltpu.make_async_copy(w1_hbm.at[layer, e], w1f.at[slot], sems.at[slot, 0]),
                pltpu.make_async_copy(w3_hbm.at[layer, e], w3f.at[slot], sems.at[slot, 1]),
                pltpu.make_async_copy(w2_hbm.at[layer, e], w2f.at[slot], sems.at[slot, 2]))

    @pl.when(i == 0)
    def _():
        for c in weight_copies(be_ref[0], 0):
            c.start()

    @pl.when(first_ref[i] == 1)
    def _():
        slot = slot_ref[i]
        for c in weight_copies(be_ref[i], slot):
            c.wait()

        @pl.when(next_ref[i] >= 0)
        def _():
            for c in weight_copies(next_ref[i], 1 - slot):
                c.start()

        w1b[...] = w1f[slot].astype(BF16)
        w3b[...] = w3f[slot].astype(BF16)
        w2b[...] = w2f[slot].astype(BF16)

    @pl.when(i >= 2)
    def _():
        drain_rows(i - 2)

    @pl.when(i < nu_ref[0])
    def _():
        b = i % 2
        n = nv_ref[i]

        def swiglu(n_rows):
            tiles = pl.ds(0, n_rows * SUBLANES)
            lo, hi = _unpack_bf16_pair(_load_row_tiles(xs_ref.at[tiles]))
            lo = lo.astype(BF16)
            hi = hi.astype(BF16)
            a = _dot(lo, w1b[0:D // 2, :]) + _dot(hi, w1b[D // 2:, :])
            g = _dot(lo, w3b[0:D // 2, :]) + _dot(hi, w3b[D // 2:, :])
            hid = ((a * _sigmoid(a)) * g).astype(BF16)
            _store_row_tiles(ybuf.at[b, tiles],
                             _pack_bf16_pair(_dot(hid, w2b[:, 0:D // 2]), _dot(hid, w2b[:, D // 2:])))

        @pl.when(n > blk // 2)
        def _():
            swiglu(blk)

        @pl.when(n <= blk // 2)
        def _():
            swiglu(blk // 2)

        first_pos = pb_ref[i]
        unroll = 8

        def issue_row(r):
            a_idx = inv_ref[first_pos + r]
            pltpu.make_async_copy(ybuf.at[b, pl.ds(pl.multiple_of(r * SUBLANES, SUBLANES), SUBLANES)],
                                  yt_ref.at[pl.ds(pl.multiple_of(a_idx * SUBLANES, SUBLANES), SUBLANES)],
                                  ysems.at[b]).start()

        def issue_group(g, carry):
            for u in range(unroll):
                issue_row(g * unroll + u)
            return carry

        def issue_one(r, carry):
            issue_row(r)
            return carry

        lax.fori_loop(0, n // unroll, issue_group, 0)
        lax.fori_loop((n // unroll) * unroll, n, issue_one, 0)

    @pl.when(i == pl.num_programs(0) - 1)
    def _():
        drain_rows(i - 1)
        drain_rows(i)


def _segment_plan(block_expert, n_used):
    n = block_expert.shape[0]
    idx = jnp.arange(n, dtype=jnp.int32)
    prev = jnp.concatenate([jnp.full((1,), -1, jnp.int32), block_expert[:-1]])
    first = ((block_expert != prev) & (idx < n_used[0])).astype(jnp.int32)
    slot = (jnp.cumsum(first) - 1) % 2
    later_first = jnp.where(first == 1, idx, n)
    next_idx = lax.cummin(jnp.concatenate([later_first[1:], jnp.full((1,), n, jnp.int32)]), reverse=True)
    next_expert = jnp.where(next_idx < n, block_expert[jnp.minimum(next_idx, n - 1)], -1)
    return first, next_expert.astype(jnp.int32), slot.astype(jnp.int32)


def _experts(plan, inv, xs, w1, w3, w2, layer):
    _, _, D, DE = w1.shape
    blk = MOE_BLK
    block_expert, n_used = plan["block_expert"], plan["n_used"]
    n_blocks = block_expert.shape[0]
    n_assign = plan["dest"].shape[0] * TOP_K
    first, next_expert, slot = _segment_plan(block_expert, n_used)
    rowmap = lambda i, *refs: (jnp.minimum(i, jnp.maximum(refs[4][0] - 1, 0)), 0)
    hbm = pl.BlockSpec(memory_space=pl.ANY)
    return pl.pallas_call(
        functools.partial(_expert_kernel, layer),
        grid_spec=pltpu.PrefetchScalarGridSpec(
            num_scalar_prefetch=8,
            grid=(n_blocks,),
            in_specs=[pl.BlockSpec((blk * SUBLANES, LANES), rowmap), hbm, hbm, hbm],
            out_specs=hbm,
            scratch_shapes=[
                pltpu.VMEM((2, D, DE), F32),
                pltpu.VMEM((2, D, DE), F32),
                pltpu.VMEM((2, DE, D), F32),
                pltpu.VMEM((D, DE), BF16),
                pltpu.VMEM((D, DE), BF16),
                pltpu.VMEM((DE, D), BF16),
                pltpu.VMEM((2, blk * SUBLANES, LANES), U32),
                pltpu.SemaphoreType.DMA((2, 3)),
                pltpu.SemaphoreType.DMA((2,)),
            ],
        ),
        out_shape=jax.ShapeDtypeStruct((n_assign * SUBLANES, LANES), U32),
        compiler_params=_params(("arbitrary",)),
    )(block_expert, first, next_expert, slot, n_used, plan["n_valid"], plan["packed_base"], inv, xs, w1, w3, w2)


def _final_combine_kernel(yt_ref, h_ref, g2_ref, ew_ref, fg_ref, o_ref):
    hn = h_ref[...] + g2_ref[0] * _moe_mix(yt_ref, ew_ref)
    o_ref[...] = hn * lax.rsqrt(jnp.mean(hn * hn, axis=-1, keepdims=True) + EPS) * fg_ref[...]


def _final_combine(yt, h, mod3, ew, final_g, S):
    T, D = h.shape
    tc = 256
    per_b = S // tc
    return pl.pallas_call(
        _final_combine_kernel,
        grid=(T // tc,),
        in_specs=[
            pl.BlockSpec((tc * TOP_K * SUBLANES, LANES), lambda i: (i, 0)),
            pl.BlockSpec((tc, D), lambda i: (i, 0)),
            pl.BlockSpec((1, 1, D), lambda i: (i // per_b, 0, 5)),
            pl.BlockSpec((tc, LANES), lambda i: (i, 0)),
            pl.BlockSpec((1, D), lambda i: (0, 0)),
        ],
        out_specs=pl.BlockSpec((tc, D), lambda i: (i, 0)),
        out_shape=jax.ShapeDtypeStruct((T, D), F32),
        compiler_params=_params(("parallel",)),
    )(yt, h, mod3, ew, final_g.reshape(1, D))


def kernel(x, c, ada_w, ada_b, norm1_g, w_in, gla_w_a2, gla_b_a, gla_norm_g, lru_conv_w, lru_conv_b,
           lru_wa, lru_ba, lru_wx, lru_bx, lru_lambda, diff_lq1, diff_lk1, diff_lq2, diff_lk2,
           diff_subln_g, rel_bias, w_out, norm2_g, router_g_w, router_g_b, router_e_w, router_e_b,
           moe_w1, moe_w3, moe_w2, final_g):
    B, S, D = x.shape
    T = B * S
    L = ada_w.shape[0]
    h = x.reshape(T, D)
    mod = _ada_mod(c, ada_w, ada_b)
    bias = _bias_tiles(rel_bias)
    w_in_perm = _permute_w_in(w_in)
    w_out_bf16 = w_out.astype(BF16)
    pending_moe = None
    for l in range(L):
        mod3 = mod[l][:, None, :]
        if pending_moe is None:
            proj = _inproj(h, mod3, norm1_g[l], w_in_perm, l, S)
        else:
            proj, h = _inproj(h, mod3, norm1_g[l], w_in_perm, l, S, pending_moe)
        wa2_pad = jnp.concatenate(
            [gla_w_a2[l], jnp.zeros((LANES - GLA_LOWRANK, GLA_KEY_WIDTH), F32)], axis=0).astype(BF16)
        o_gla = _gla(proj, wa2_pad, gla_b_a[l], gla_norm_g[l], B, S)
        w_gates = jnp.concatenate([_block_diag(lru_wa[l]), _block_diag(lru_wx[l])], axis=1).astype(BF16)
        b_gates = jnp.concatenate([lru_ba[l], lru_bx[l]]).reshape(1, 2 * LRU_WIDTH)
        o_lru = _lru(proj.reshape(B, S, PROJ_WIDTH), lru_conv_w[l], lru_conv_b[l], w_gates, b_gates,
                     lru_lambda[l]).reshape(T, LRU_WIDTH)
        lqk = jnp.stack([diff_lq1[l], diff_lk1[l], diff_lq2[l], diff_lk2[l]], axis=0)
        o_diff = _diff_attention(proj, bias, lqk, diff_subln_g[l], l, B, S)
        rw = jnp.concatenate(
            [router_g_w[l], router_e_w[l], jnp.zeros((D, LANES - N_GROUPS - N_EXPERTS), F32)], axis=1).astype(BF16)
        rb = jnp.concatenate(
            [router_g_b[l], router_e_b[l], jnp.zeros((LANES - N_GROUPS - N_EXPERTS,), F32)]).reshape(1, LANES)
        h, u2, info, ew, cnt = _outproj(h, o_gla, o_lru, o_diff, w_out_bf16, l, mod3, norm2_g[l], rw, rb, S)
        counts = cnt[0, N_GROUPS:N_GROUPS + N_EXPERTS].astype(jnp.int32)
        plan = _dispatch(info, counts)
        xs, inv = _scatter_rows(plan, u2)
        yt = _experts(plan, inv, xs, moe_w1, moe_w3, moe_w2, l)
        pending_moe = (yt, ew, mod3)
    out = _final_combine(yt, h, mod3, ew, final_g, S)
    return out.reshape(B, S, D)
```

```python
import functools
import math

import jax
import jax.numpy as jnp
from jax import lax
from jax.experimental import pallas as pl
from jax.experimental.pallas import tpu as pltpu

F32 = jnp.float32
BF16 = jnp.bfloat16
U32 = jnp.uint32

EPS = 1e-6
LOG2E = math.log2(math.e)
CHUNK = 64

GLA_DV = 128
GLA_DK = 64
GLA_HEADS = 6
GLA_WIDTH = GLA_HEADS * GLA_DV
GLA_KEY_WIDTH = GLA_HEADS * GLA_DK
GLA_LOWRANK = 16
GLA_TAU = 16.0

LRU_WIDTH = 512
LRU_BLOCKS = 8
LRU_BLOCK_DIM = LRU_WIDTH // LRU_BLOCKS
CONV_WIDTH = 4
LRU_C = 8.0

DIFF_DH = 64
DIFF_DV = 128
DIFF_HEADS = 6
DIFF_WIDTH = DIFF_HEADS * DIFF_DV

REL_BUCKETS = 32
REL_MAX_DIST = 128

N_GROUPS = 8
EXPERTS_PER_GROUP = 8
N_EXPERTS = 64
TOP_K = 2

LANES = 128
SUBLANES = 8
VMEM_LIMIT = 56 * 1024 * 1024

COL_GV = 0
COL_GOG = 768
COL_DQ = 1536
COL_DK = 2304
COL_DV = 3072
COL_GQ = 3840
COL_GK = 4224
COL_LY = 4608
COL_LX = 5120
COL_GA = 5632
PROJ_WIDTH = 5760

ATT_TILE = 512
GLA_TILE = 256
LRU_TILE = 256
MOE_BLK = 256
GATHER_ROWS = 512


def _params(sem, vmem=VMEM_LIMIT):
    return pltpu.CompilerParams(dimension_semantics=sem, vmem_limit_bytes=vmem)


def _sigmoid(x):
    return 0.5 * jnp.tanh(0.5 * x) + 0.5


def _softplus(x):
    return jnp.maximum(x, 0.0) + jnp.log1p(jnp.exp(-jnp.abs(x)))


def _dot(a, b):
    return jnp.dot(a, b, preferred_element_type=F32)


def _dot_nt(a, b):
    return lax.dot_general(a, b, (((1,), (1,)), ((), ())), preferred_element_type=F32)


def _dot_tn(a, b):
    return lax.dot_general(a, b, (((0,), (0,)), ((), ())), preferred_element_type=F32)


def _pack_bf16_pair(lo, hi):
    lo_bits = lax.bitcast_convert_type(lo.astype(BF16).astype(F32), U32)
    hi_bits = lax.bitcast_convert_type(hi.astype(BF16).astype(F32), U32)
    return (hi_bits & jnp.uint32(0xFFFF0000)) | (lo_bits >> 16)


def _unpack_bf16_pair(w):
    lo = lax.bitcast_convert_type(w << 16, F32)
    hi = lax.bitcast_convert_type(w & jnp.uint32(0xFFFF0000), F32)
    return lo, hi


def _store_row_tiles(ref, words):
    rows = words.shape[0]
    for s in range(SUBLANES):
        ref[pl.ds(s, rows, stride=SUBLANES), :] = words[:, s * LANES:(s + 1) * LANES]


def _load_row_tiles(ref):
    rows = ref.shape[0] // SUBLANES
    return jnp.concatenate([ref[pl.ds(s, rows, stride=SUBLANES), :] for s in range(SUBLANES)], axis=1)


def _ada_kernel(c_ref, w_ref, b_ref, o_ref):
    c = c_ref[...]
    s = c * _sigmoid(c)
    o_ref[0] = _dot(s.astype(BF16), w_ref[0].astype(BF16)) + b_ref[0]


def _ada_mod(c, ada_w, ada_b):
    L, D, N = ada_w.shape
    B = c.shape[0]
    tn = 1024
    return pl.pallas_call(
        _ada_kernel,
        grid=(L, N // tn),
        in_specs=[
            pl.BlockSpec((B, D), lambda l, j: (0, 0)),
            pl.BlockSpec((1, D, tn), lambda l, j: (l, 0, j)),
            pl.BlockSpec((1, 1, tn), lambda l, j: (l, 0, j)),
        ],
        out_specs=pl.BlockSpec((1, B, tn), lambda l, j: (l, 0, j)),
        out_shape=jax.ShapeDtypeStruct((L, B, N), F32),
        compiler_params=_params(("parallel", "parallel")),
    )(c, ada_w, ada_b.reshape(L, 1, N))


def _modulated_norm(x, g, sc, sh):
    ms = jnp.mean(x * x, axis=-1, keepdims=True)
    return (x * lax.rsqrt(ms + EPS) * g) * (1.0 + sc) + sh


def _moe_mix(yt_ref, ew_ref):
    rows = ew_ref.shape[0]

    def expert_rows(k):
        return jnp.concatenate([yt_ref[pl.ds(k * SUBLANES + s, rows, stride=TOP_K * SUBLANES), :]
                                for s in range(SUBLANES)], axis=1)

    ew = ew_ref[...]
    w0 = ew[:, 0:1]
    w1 = ew[:, 1:2]
    lo0, hi0 = _unpack_bf16_pair(expert_rows(0))
    lo1, hi1 = _unpack_bf16_pair(expert_rows(1))
    return jnp.concatenate([w0 * lo0 + w1 * lo1, w0 * hi0 + w1 * hi1], axis=1)


def _inproj_kernel(h_ref, sh_ref, sc_ref, g_ref, w_ref, o_ref, u_scr):
    @pl.when(pl.program_id(1) == 0)
    def _():
        u = _modulated_norm(h_ref[...], g_ref[...], sc_ref[0], sh_ref[0])
        u_scr[...] = u.astype(BF16)

    o_ref[...] = _dot(u_scr[...], w_ref[...]).astype(o_ref.dtype)


def _inproj_after_moe_kernel(h_ref, yt_ref, ew_ref, g2_ref, sh_ref, sc_ref, g_ref, w_ref, o_ref, hn_ref, u_scr):
    @pl.when(pl.program_id(1) == 0)
    def _():
        hn = h_ref[...] + g2_ref[0] * _moe_mix(yt_ref, ew_ref)
        hn_ref[...] = hn
        u = _modulated_norm(hn, g_ref[...], sc_ref[0], sh_ref[0])
        u_scr[...] = u.astype(BF16)

    o_ref[...] = _dot(u_scr[...], w_ref[...]).astype(o_ref.dtype)


def _inproj(h, mod3, norm_g, w_perm, layer, S, pending_moe=None):
    T, D = h.shape
    N = w_perm.shape[2]
    tm, tn = 512, 1152
    per_b = S // tm
    rows = pl.BlockSpec((tm, D), lambda i, j: (i, 0))
    mod_specs = [
        pl.BlockSpec((1, 1, D), lambda i, j: (i // per_b, 0, 0)),
        pl.BlockSpec((1, 1, D), lambda i, j: (i // per_b, 0, 1)),
        pl.BlockSpec((1, D), lambda i, j: (0, 0)),
        pl.BlockSpec((None, D, tn), lambda i, j: (layer, 0, j)),
    ]
    proj_spec = pl.BlockSpec((tm, tn), lambda i, j: (i, j))
    proj_shape = jax.ShapeDtypeStruct((T, N), BF16)
    common = dict(grid=(T // tm, N // tn), scratch_shapes=[pltpu.VMEM((tm, D), BF16)],
                  compiler_params=_params(("parallel", "arbitrary")))
    if pending_moe is None:
        return pl.pallas_call(_inproj_kernel, in_specs=[rows] + mod_specs, out_specs=proj_spec,
                              out_shape=proj_shape, **common)(h, mod3, mod3, norm_g.reshape(1, D), w_perm)
    yt, ew, prev_mod3 = pending_moe
    moe_specs = [
        pl.BlockSpec((tm * TOP_K * SUBLANES, LANES), lambda i, j: (i, 0)),
        pl.BlockSpec((tm, LANES), lambda i, j: (i, 0)),
        pl.BlockSpec((1, 1, D), lambda i, j: (i // per_b, 0, 5)),
    ]
    return pl.pallas_call(
        _inproj_after_moe_kernel, in_specs=[rows] + moe_specs + mod_specs, out_specs=[proj_spec, rows],
        out_shape=[proj_shape, jax.ShapeDtypeStruct((T, D), F32)], **common,
    )(h, yt, ew, prev_mod3, mod3, mod3, norm_g.reshape(1, D), w_perm)


W_IN_SEGMENTS = ((COL_GQ, 0, 384), (COL_GK, 384, 384), (COL_GV, 768, 768), (COL_GOG, 1536, 768),
                 (COL_GA, 2304, GLA_LOWRANK), (COL_LY, 2320, 512), (COL_LX, 2832, 512),
                 (COL_DQ, 3344, 768), (COL_DK, 4112, 768), (COL_DV, 4880, 768))


def _relayout_kernel(w_ref, o_ref):
    x = w_ref[0]
    for dst, src, width in W_IN_SEGMENTS:
        o_ref[0, :, dst:dst + width] = x[:, src:src + width]
    pad = slice(COL_GA + GLA_LOWRANK, COL_GA + LANES)
    o_ref[0, :, pad] = jnp.zeros((x.shape[0], LANES - GLA_LOWRANK), BF16)


def _permute_w_in(w):
    L, D, N = w.shape
    rt = 256
    return pl.pallas_call(
        _relayout_kernel,
        grid=(L, D // rt),
        in_specs=[pl.BlockSpec((1, rt, N), lambda l, i: (l, i, 0))],
        out_specs=pl.BlockSpec((1, rt, PROJ_WIDTH), lambda l, i: (l, i, 0)),
        out_shape=jax.ShapeDtypeStruct((L, D, PROJ_WIDTH), BF16),
        compiler_params=_params(("parallel", "parallel")),
    )(w.astype(BF16))


def _gla_kernel(q_ref, k_ref, v_ref, og_ref, alr_ref, wa2_ref, ba_ref, ng_ref, o_ref, st_ref):
    tb = q_ref.shape[0]
    n_chunks = tb // CHUNK

    @pl.when(pl.program_id(1) == 0)
    def _():
        st_ref[...] = jnp.zeros_like(st_ref)

    row = lax.broadcasted_iota(jnp.int32, (tb, tb), 0)
    col = lax.broadcasted_iota(jnp.int32, (tb, tb), 1)
    same_chunk = (row // CHUNK) == (col // CHUNK)
    causal = col <= row
    tril = jnp.where(same_chunk & causal, 1.0, 0.0).astype(BF16)
    lane = lax.broadcasted_iota(jnp.int32, (1, LANES), 1)
    half_masks = (lane < GLA_DK, lane >= GLA_DK)

    alr = alr_ref[...]
    cols = [slice(p * LANES, (p + 1) * LANES) for p in range(GLA_HEADS // 2)]
    z = [_dot(alr, wa2_ref[:, cs]) + ba_ref[:, cs] for cs in cols]
    la = [(jnp.minimum(zp, 0.0) - jnp.log1p(jnp.exp(-jnp.abs(zp)))) * (1.0 / GLA_TAU) for zp in z]
    la_hi = [x.astype(BF16) for x in la]
    la_lo = [(x - h.astype(F32)).astype(BF16) for x, h in zip(la, la_hi)]
    G = [_dot(tril, h) + _dot(tril, lo) for h, lo in zip(la_hi, la_lo)]
    Gl = [jnp.concatenate([jnp.broadcast_to(g[(c + 1) * CHUNK - 1:(c + 1) * CHUNK, :], (CHUNK, LANES))
                           for c in range(n_chunks)], axis=0) for g in G]
    eG = [jnp.exp(g) for g in G]
    enG = [jnp.exp(-g) for g in G]
    q = [q_ref[:, cs].astype(F32) * (GLA_DK ** -0.5) for cs in cols]
    k = [k_ref[:, cs].astype(F32) for cs in cols]
    kf = [(kp * e).astype(BF16) for kp, e in zip(k, eG)]
    kb = [(kp * e).astype(BF16) for kp, e in zip(k, enG)]
    kd = [kp * jnp.exp(gl - g) for kp, gl, g in zip(k, Gl, G)]

    heads = range(GLA_HEADS)
    pair = [h // 2 for h in heads]
    mask = [half_masks[h % 2] for h in heads]
    vcols = [slice(h * GLA_DV, (h + 1) * GLA_DV) for h in heads]
    qf_h = [jnp.where(mask[h], q[pair[h]] * eG[pair[h]], 0.0).astype(BF16) for h in heads]
    qb_h = [jnp.where(mask[h], q[pair[h]] * enG[pair[h]], 0.0).astype(BF16) for h in heads]
    kd_h = [jnp.where(mask[h], kd[pair[h]], 0.0).astype(BF16) for h in heads]
    v_h = [v_ref[:, vcols[h]] for h in heads]
    a_f = [_dot_nt(qf_h[h], kb[pair[h]]) for h in heads]
    a_b = [_dot_nt(qb_h[h], kf[pair[h]]) for h in heads]
    attn = [jnp.where(same_chunk, jnp.where(causal, a_f[h], a_b[h]), 0.0).astype(BF16) for h in heads]
    o_intra = [_dot(attn[h], v_h[h]) for h in heads]
    chunk_rows = [slice(c * CHUNK, (c + 1) * CHUNK) for c in range(n_chunks)]
    kv = [[_dot_tn(v_h[h][rs], kd_h[h][rs]) for rs in chunk_rows] for h in heads]
    for h in heads:
        st = st_ref[h]
        inter = []
        for c, rs in enumerate(chunk_rows):
            inter.append(_dot_nt(qf_h[h][rs], st.astype(BF16)))
            decay = jnp.exp(Gl[pair[h]][c * CHUNK:c * CHUNK + 1, :])
            st = st * decay + kv[h][c]
        st_ref[h] = st
        o = o_intra[h] + jnp.concatenate(inter, axis=0)
        o = o * lax.rsqrt(jnp.mean(o * o, axis=-1, keepdims=True) + EPS)
        og = og_ref[:, vcols[h]].astype(F32)
        o_ref[:, vcols[h]] = (o * ng_ref[:, vcols[h]] * (og * _sigmoid(og))).astype(o_ref.dtype)


def _gla(proj, wa2_pad, b_a, norm_g, B, S):
    T = proj.shape[0]
    tb = GLA_TILE
    nt = S // tb
    row = lambda b, i: b * nt + i
    return pl.pallas_call(
        _gla_kernel,
        grid=(B, nt),
        in_specs=[
            pl.BlockSpec((tb, GLA_KEY_WIDTH), lambda b, i: (row(b, i), COL_GQ // GLA_KEY_WIDTH)),
            pl.BlockSpec((tb, GLA_KEY_WIDTH), lambda b, i: (row(b, i), COL_GK // GLA_KEY_WIDTH)),
            pl.BlockSpec((tb, GLA_WIDTH), lambda b, i: (row(b, i), COL_GV // GLA_WIDTH)),
            pl.BlockSpec((tb, GLA_WIDTH), lambda b, i: (row(b, i), COL_GOG // GLA_WIDTH)),
            pl.BlockSpec((tb, LANES), lambda b, i: (row(b, i), COL_GA // LANES)),
            pl.BlockSpec((LANES, GLA_KEY_WIDTH), lambda b, i: (0, 0)),
            pl.BlockSpec((1, GLA_KEY_WIDTH), lambda b, i: (0, 0)),
            pl.BlockSpec((1, GLA_WIDTH), lambda b, i: (0, 0)),
        ],
        out_specs=pl.BlockSpec((tb, GLA_WIDTH), lambda b, i: (row(b, i), 0)),
        out_shape=jax.ShapeDtypeStruct((T, GLA_WIDTH), BF16),
        scratch_shapes=[pltpu.VMEM((GLA_HEADS, GLA_DV, LANES), F32)],
        compiler_params=_params(("parallel", "arbitrary")),
    )(proj, proj, proj, proj, proj, wa2_pad, b_a.reshape(1, -1), norm_g.reshape(1, -1))


def _lru_kernel(y_ref, x_ref, cw_ref, cb_ref, wg_ref, bg_ref, lam_ref, o_ref, *scratch):
    B, ts, W = x_ref.shape
    n_planes = W // LANES
    a_scr = scratch[0:n_planes]
    b_scr = scratch[n_planes:2 * n_planes]
    h_scr = scratch[2 * n_planes:3 * n_planes]
    xc_scr, tail_scr, carry_scr = scratch[3 * n_planes:]

    @pl.when(pl.program_id(0) == 0)
    def _():
        tail_scr[...] = jnp.zeros_like(tail_scr)
        carry_scr[...] = jnp.zeros_like(carry_scr)

    cw = cw_ref[...]
    cb = cb_ref[...]
    sp = _softplus(-lam_ref[...])
    row8 = lax.broadcasted_iota(jnp.int32, (8, W), 0)
    for b in range(B):
        x = x_ref[b].astype(F32)
        tail = tail_scr[b]
        xc = cb + cw[CONV_WIDTH - 1:CONV_WIDTH, :] * x
        head = cb + cw[CONV_WIDTH - 1:CONV_WIDTH, :] * x[0:8]
        for d in range(1, CONV_WIDTH):
            wd = cw[CONV_WIDTH - 1 - d:CONV_WIDTH - d, :]
            xr = pltpu.roll(x, d, 0)
            xc = xc + wd * xr
            head = head + wd * jnp.where(row8 < d, pltpu.roll(tail, d, 0), xr[0:8])
        tail_scr[b] = x[ts - 8:ts]
        xc_scr[...] = xc
        xc_scr[0:8] = head
        xc = xc_scr[...]
        gates = _sigmoid(_dot(xc.astype(BF16), wg_ref[...]) + bg_ref[...])
        r = gates[:, :W]
        ig = gates[:, W:]
        log_a = (-LRU_C) * r * sp
        a = jnp.exp(log_a)
        b_in = jnp.sqrt(-jnp.tanh(log_a) * (a * a + 1.0)) * (ig * xc)
        rows = slice(b * ts, (b + 1) * ts)
        for k in range(n_planes):
            a_scr[k][rows] = a[:, k * LANES:(k + 1) * LANES]
            b_scr[k][rows] = b_in[:, k * LANES:(k + 1) * LANES]

    def step(t, hs):
        idx = pl.ds(t, B, stride=ts)
        out = []
        for k in range(n_planes):
            hk = a_scr[k][idx, :] * hs[k] + b_scr[k][idx, :]
            h_scr[k][idx, :] = hk
            out.append(hk)
        return tuple(out)

    hs = lax.fori_loop(0, ts, step, tuple(carry_scr[k] for k in range(n_planes)), unroll=8)
    for k in range(n_planes):
        carry_scr[k] = hs[k]

    for b in range(B):
        rows = slice(b * ts, (b + 1) * ts)
        y = y_ref[b].astype(F32)
        gelu = 0.5 * y * (1.0 + jnp.tanh(math.sqrt(2.0 / math.pi) * (y + 0.044715 * (y * y * y))))
        h = jnp.concatenate([h_scr[k][rows] for k in range(n_planes)], axis=1)
        o_ref[b] = (h * gelu).astype(o_ref.dtype)


def _lru(proj3, conv_w, conv_b, w_gates, b_gates, lam):
    B, S, _ = proj3.shape
    W = LRU_WIDTH
    ts = LRU_TILE
    n_planes = W // LANES
    full = lambda shape: pl.BlockSpec(shape, lambda i: (0,) * len(shape))
    return pl.pallas_call(
        _lru_kernel,
        grid=(S // ts,),
        in_specs=[
            pl.BlockSpec((B, ts, W), lambda i: (0, i, COL_LY // W)),
            pl.BlockSpec((B, ts, W), lambda i: (0, i, COL_LX // W)),
            full((CONV_WIDTH, W)),
            full((1, W)),
            full((W, 2 * W)),
            full((1, 2 * W)),
            full((1, W)),
        ],
        out_specs=pl.BlockSpec((B, ts, W), lambda i: (0, i, 0)),
        out_shape=jax.ShapeDtypeStruct((B, S, W), BF16),
        scratch_shapes=(
            [pltpu.VMEM((B * ts, LANES), F32) for _ in range(3 * n_planes)]
            + [pltpu.VMEM((ts, W), F32), pltpu.VMEM((B, 8, W), F32), pltpu.VMEM((n_planes, B, LANES), F32)]),
        compiler_params=_params(("arbitrary",)),
    )(proj3, proj3, conv_w, conv_b.reshape(1, W), w_gates, b_gates, lam.reshape(1, W))


def _block_diag(w):
    n, d, _ = w.shape
    eye = jnp.eye(n, dtype=w.dtype)
    return (eye[:, None, :, None] * w[:, :, None, :]).reshape(n * d, n * d)


def _t5_bucket(rel):
    nb = REL_BUCKETS // 2
    ret = (rel > 0).astype(jnp.int32) * nb
    n = jnp.abs(rel)
    max_exact = nb // 2
    nf = jnp.maximum(n, 1).astype(jnp.float32)
    large = max_exact + (jnp.log(nf / max_exact) / math.log(REL_MAX_DIST / max_exact)
                         * (nb - max_exact)).astype(jnp.int32)
    large = jnp.minimum(large, nb - 1)
    return ret + jnp.where(n < max_exact, n, large)


def _bias_kernel(bucket_ref, table_ref, o_ref):
    h = pl.program_id(0)
    bucket = bucket_ref[0]
    acc = jnp.full(bucket.shape, -1e30, F32)
    for b in range(REL_BUCKETS):
        acc = jnp.where(bucket == b, table_ref[b, h] * LOG2E, acc)
    o_ref[0, 0] = acc


def _bias_tiles(rel_bias):
    t = ATT_TILE
    H = rel_bias.shape[1]
    qp = jnp.arange(t, dtype=jnp.int32)[:, None]
    kp = jnp.arange(t, dtype=jnp.int32)[None, :]
    mask = (kp // CHUNK) <= (qp // CHUNK)
    half = REL_BUCKETS // 2
    per_distance = _t5_bucket(-jnp.arange(2 * t, dtype=jnp.int32))
    edges = jnp.sum((per_distance[None, :] < jnp.arange(1, half, dtype=jnp.int32)[:, None]).astype(jnp.int32), axis=1)

    def bucket_2d(rel):
        passed = jnp.sum((jnp.abs(rel)[None] >= edges[:, None, None]).astype(jnp.int32), axis=0)
        return (rel > 0).astype(jnp.int32) * half + passed

    buckets = jnp.stack([jnp.where(mask, bucket_2d(kp - qp), REL_BUCKETS), bucket_2d(kp - t - qp)], axis=0)
    table = rel_bias.astype(F32)
    tiles = pl.pallas_call(
        _bias_kernel,
        grid=(H, 2),
        in_specs=[
            pl.BlockSpec((1, t, t), lambda h, k: (k, 0, 0)),
            pl.BlockSpec(memory_space=pltpu.SMEM),
        ],
        out_specs=pl.BlockSpec((1, 1, t, t), lambda h, k: (h, k, 0, 0)),
        out_shape=jax.ShapeDtypeStruct((H, 2, t, t), F32),
        compiler_params=_params(("parallel", "parallel")),
    )(buckets, table)
    far_bucket = _t5_bucket(jnp.full((1,), -t - 1, jnp.int32))
    far = jnp.sum(jnp.where(jnp.arange(REL_BUCKETS)[:, None] == far_bucket, table, 0.0), axis=0)
    return tiles, jnp.broadcast_to((far * LOG2E)[:, None, None], (H, 1, t))


def _diff_kernel(lam_init, q_ref, k_ref, v_ref, bias_ref, far_ref, lqk_ref, g_ref, o_ref,
                 qs_scr, m_scr, l_scr, acc_scr):
    i = pl.program_id(2)
    t = q_ref.shape[0]
    hq = t // 2
    lane = lax.broadcasted_iota(jnp.int32, (1, LANES), 1)
    q = q_ref[...].astype(F32) * (LOG2E * DIFF_DH ** -0.5)
    for half in range(2):
        qh = q[half * hq:(half + 1) * hq]
        qs_scr[(2 * half) * hq:(2 * half + 1) * hq] = jnp.where(lane < DIFF_DH, qh, 0.0).astype(BF16)
        qs_scr[(2 * half + 1) * hq:(2 * half + 2) * hq] = jnp.where(lane >= DIFF_DH, qh, 0.0).astype(BF16)
    m_scr[...] = jnp.full_like(m_scr, -1e30)
    l_scr[...] = jnp.zeros_like(l_scr)
    acc_scr[...] = jnp.zeros_like(acc_scr)

    def tile(rows, ks, bias):
        s = _dot_nt(qs_scr[rows, :], k_ref[ks, :]) + bias
        groups = [s[:, c * LANES:(c + 1) * LANES] for c in range(s.shape[1] // LANES)]
        mx = functools.reduce(jnp.maximum, groups)
        m_prev = m_scr[rows, :]
        m_new = jnp.maximum(m_prev, jnp.max(mx, axis=-1, keepdims=True))
        alpha = jnp.exp2(m_prev - m_new)
        ps = [jnp.exp2(g - m_new) for g in groups]
        l_scr[rows, :] = alpha * l_scr[rows, :] + functools.reduce(jnp.add, ps)
        p = jnp.concatenate(ps, axis=1).astype(BF16)
        acc_scr[rows, :] = alpha * acc_scr[rows, :] + _dot(p, v_ref[ks, :])
        m_scr[rows, :] = m_new

    def stacked(b, half):
        bh = b[half * hq:(half + 1) * hq]
        return [bh, bh]

    all_rows = slice(0, 2 * t)

    def far_body(j, carry):
        tile(all_rows, pl.ds(pl.multiple_of(j * t, t), t), far_ref[0])
        return carry

    lax.fori_loop(0, jnp.maximum(i - 1, 0), far_body, 0)

    @pl.when(i >= 1)
    def _():
        b = bias_ref[0, 1]
        tile(all_rows, pl.ds(pl.multiple_of((i - 1) * t, t), t),
             jnp.concatenate(stacked(b, 0) + stacked(b, 1), axis=0))

    b = bias_ref[0, 0]
    diag0 = pl.multiple_of(i * t, t)
    tile(slice(0, t), pl.ds(diag0, hq), jnp.concatenate(stacked(b[:, 0:hq], 0), axis=0))
    tile(slice(t, 2 * t), pl.ds(diag0, t), jnp.concatenate(stacked(b, 1), axis=0))

    lqk = lqk_ref[...]
    lam = (jnp.exp(jnp.sum(lqk[0:1] * lqk[1:2], axis=-1, keepdims=True))
           - jnp.exp(jnp.sum(lqk[2:3] * lqk[3:4], axis=-1, keepdims=True)) + lam_init)
    o = acc_scr[...] / jnp.sum(l_scr[...], axis=-1, keepdims=True)
    o = jnp.concatenate([o[0:hq] - lam * o[hq:t], o[t:t + hq] - lam * o[t + hq:2 * t]], axis=0)
    o = o * lax.rsqrt(jnp.mean(o * o, axis=-1, keepdims=True) + EPS)
    o_ref[...] = (o * g_ref[...] * (1.0 - lam_init)).astype(o_ref.dtype)


def _diff_attention(proj, bias, lqk, subln_g, layer_idx, B, S):
    T = proj.shape[0]
    t = ATT_TILE
    nq = S // t
    tiles, far = bias
    lam_init = 0.8 - 0.6 * math.exp(-0.3 * layer_idx)
    return pl.pallas_call(
        functools.partial(_diff_kernel, lam_init),
        grid=(B, DIFF_HEADS, nq),
        in_specs=[
            pl.BlockSpec((t, LANES), lambda b, h, i: (b * nq + i, COL_DQ // LANES + h)),
            pl.BlockSpec((S, LANES), lambda b, h, i: (b, COL_DK // LANES + h)),
            pl.BlockSpec((S, LANES), lambda b, h, i: (b, COL_DV // LANES + h)),
            pl.BlockSpec((1, 2, t, t), lambda b, h, i: (h, 0, 0, 0)),
            pl.BlockSpec((1, 1, t), lambda b, h, i: (h, 0, 0)),
            pl.BlockSpec((4, DIFF_DH), lambda b, h, i: (0, 0)),
            pl.BlockSpec((1, DIFF_DV), lambda b, h, i: (0, 0)),
        ],
        out_specs=pl.BlockSpec((t, LANES), lambda b, h, i: (b * nq + i, h)),
        out_shape=jax.ShapeDtypeStruct((T, DIFF_WIDTH), BF16),
        scratch_shapes=[
            pltpu.VMEM((2 * t, LANES), BF16),
            pltpu.VMEM((2 * t, LANES), F32),
            pltpu.VMEM((2 * t, LANES), F32),
            pltpu.VMEM((2 * t, DIFF_DV), F32),
        ],
        compiler_params=_params(("parallel", "parallel", "arbitrary")),
    )(proj, proj, proj, tiles, far, lqk, subln_g.reshape(1, DIFF_DV))


def _outproj_kernel(h_ref, og_ref, ol_ref, od_ref, w_ref, g1_ref, sh2_ref, sc2_ref, n2_ref, rw_ref, rb_ref,
                    hn_ref, u2_ref, eid_ref, ew_ref, cnt_ref):
    tm = h_ref.shape[0]

    @pl.when(pl.program_id(0) == 0)
    def _():
        cnt_ref[...] = jnp.zeros_like(cnt_ref)

    nr = tm // 2
    halves = [slice(half * nr, (half + 1) * nr) for half in range(2)]
    accs = []
    for rows in halves:
        acc = _dot(og_ref[rows, :], w_ref[0:GLA_WIDTH, :])
        acc += _dot(ol_ref[rows, :], w_ref[GLA_WIDTH:GLA_WIDTH + LRU_WIDTH, :])
        acc += _dot(od_ref[rows, :], w_ref[GLA_WIDTH + LRU_WIDTH:, :])
        accs.append(acc)
    for rows, acc in zip(halves, accs):
        _outproj_rows(rows, acc, h_ref, g1_ref, sh2_ref, sc2_ref, n2_ref, rw_ref, rb_ref,
                      hn_ref, u2_ref, eid_ref, ew_ref, cnt_ref)


def _outproj_rows(rows, acc, h_ref, g1_ref, sh2_ref, sc2_ref, n2_ref, rw_ref, rb_ref,
                  hn_ref, u2_ref, eid_ref, ew_ref, cnt_ref):
    D = h_ref.shape[1]
    nr = rows.stop - rows.start
    hn = h_ref[rows, :] + g1_ref[0] * acc
    hn_ref[rows, :] = hn
    u2 = _modulated_norm(hn, n2_ref[...], sc2_ref[0], sh2_ref[0])
    _store_row_tiles(u2_ref.at[pl.ds(rows.start * SUBLANES, nr * SUBLANES)],
                     _pack_bf16_pair(u2[:, :D // 2], u2[:, D // 2:]))

    logits = _dot(u2.astype(BF16), rw_ref[...]) + rb_ref[...]
    lane = lax.broadcasted_iota(jnp.int32, logits.shape, 1)
    lane_f = lane.astype(F32)
    neg = jnp.float32(-jnp.inf)
    gmask = lane < N_GROUPS
    gl = jnp.where(gmask, logits, neg)
    gmax = jnp.max(gl, axis=-1, keepdims=True)
    gidx = jnp.min(jnp.where(gl == gmax, lane_f, float(LANES)), axis=-1, keepdims=True)
    g_w = 1.0 / jnp.sum(jnp.where(gmask, jnp.exp(gl - gmax), 0.0), axis=-1, keepdims=True)
    egroup = ((lane - N_GROUPS) >> 3).astype(F32)
    emask = (lane >= N_GROUPS) & (lane < N_GROUPS + N_EXPERTS) & (egroup == gidx)
    el = jnp.where(emask, logits, neg)
    v1 = jnp.max(el, axis=-1, keepdims=True)
    i1 = jnp.min(jnp.where(el == v1, lane_f, float(LANES)), axis=-1, keepdims=True)
    el2 = jnp.where(lane_f == i1, neg, el)
    v2 = jnp.max(el2, axis=-1, keepdims=True)
    i2 = jnp.min(jnp.where(el2 == v2, lane_f, float(LANES)), axis=-1, keepdims=True)
    e21 = jnp.exp(v2 - v1)
    w1 = g_w / (1.0 + e21)
    w2 = g_w * e21 / (1.0 + e21)
    ew_ref[rows, :] = jnp.where(lane == 0, w1, jnp.where(lane == 1, w2, 0.0))

    oh1 = lane_f == i1
    oh2 = lane_f == i2
    both = jnp.where(oh1 | oh2, 1.0, 0.0).astype(BF16)
    row = lax.broadcasted_iota(jnp.int32, (nr, nr), 0)
    col = lax.broadcasted_iota(jnp.int32, (nr, nr), 1)
    earlier = _dot(jnp.where(col < row, 1.0, 0.0).astype(BF16), both) + cnt_ref[0:1, :]
    rank1 = jnp.sum(jnp.where(oh1, earlier, 0.0), axis=-1, keepdims=True)
    rank2 = jnp.sum(jnp.where(oh2, earlier, 0.0), axis=-1, keepdims=True)
    cnt_ref[0:1, :] = cnt_ref[0:1, :] + jnp.sum(both.astype(F32), axis=0, keepdims=True)
    info = jnp.where(lane == 0, i1 - float(N_GROUPS),
                     jnp.where(lane == 1, i2 - float(N_GROUPS),
                               jnp.where(lane == 2, rank1, jnp.where(lane == 3, rank2, 0.0))))
    eid_ref[rows, :] = info.astype(jnp.int32)


def _outproj(h, o_gla, o_lru, o_diff, w_out, layer, mod3, norm2_g, rw, rb, S):
    T, D = h.shape
    tm = 512
    per_b = S // tm
    rowblk = lambda width: pl.BlockSpec((tm, width), lambda i: (i, 0))
    modblk = lambda k: pl.BlockSpec((1, 1, D), lambda i: (i // per_b, 0, k))
    full = lambda shape: pl.BlockSpec(shape, lambda i: (0,) * len(shape))
    return pl.pallas_call(
        _outproj_kernel,
        grid=(T // tm,),
        in_specs=[
            rowblk(D), rowblk(GLA_WIDTH), rowblk(LRU_WIDTH), rowblk(DIFF_WIDTH),
            pl.BlockSpec((None, D, D), lambda i: (layer, 0, 0)),
            modblk(2), modblk(3), modblk(4),
            full((1, D)),
            full((D, LANES)),
            full((1, LANES)),
        ],
        out_specs=[rowblk(D), pl.BlockSpec((tm * SUBLANES, LANES), lambda i: (i, 0)), rowblk(LANES), rowblk(LANES),
                   full((SUBLANES, LANES))],
        out_shape=[
            jax.ShapeDtypeStruct((T, D), F32),
            jax.ShapeDtypeStruct((T * SUBLANES, LANES), U32),
            jax.ShapeDtypeStruct((T, LANES), jnp.int32),
            jax.ShapeDtypeStruct((T, LANES), F32),
            jax.ShapeDtypeStruct((SUBLANES, LANES), F32),
        ],
        compiler_params=_params(("arbitrary",)),
    )(h, o_gla, o_lru, o_diff, w_out, mod3, mod3, mod3, norm2_g.reshape(1, D), rw, rb)


def _dispatch(info, counts):
    T = info.shape[0]
    blk = MOE_BLK
    n_blocks = (T * TOP_K) // blk + N_EXPERTS
    expert = info[:, 0:TOP_K]
    rank = info[:, TOP_K:2 * TOP_K]
    padded = (counts + blk - 1) // blk * blk
    pends = jnp.cumsum(padded)
    pstarts = pends - padded
    ustarts = jnp.cumsum(counts) - counts
    onehot = expert[:, :, None] == jnp.arange(N_EXPERTS, dtype=jnp.int32)[None, None, :]
    dest = rank + jnp.sum(jnp.where(onehot, pstarts[None, None, :], 0), axis=-1)
    packed = rank + jnp.sum(jnp.where(onehot, ustarts[None, None, :], 0), axis=-1)
    n_used = (pends[-1] // blk).astype(jnp.int32)
    block_idx = jnp.arange(n_blocks, dtype=jnp.int32)
    block_expert = jnp.minimum(jnp.sum((pends[None, :] <= (block_idx * blk)[:, None]).astype(jnp.int32), axis=1),
                               N_EXPERTS - 1)
    last_used = jnp.sum(jnp.where(block_idx == jnp.maximum(n_used - 1, 0), block_expert, 0))
    block_expert = jnp.where(block_idx < n_used, block_expert, last_used).astype(jnp.int32)
    following = jnp.concatenate([block_expert[1:], jnp.full((1,), -1, jnp.int32)])
    zero_block = ((block_idx >= n_used - 1) | (following != block_expert)).astype(jnp.int32)
    owner = block_expert[:, None] == jnp.arange(N_EXPERTS, dtype=jnp.int32)[None, :]
    seg_end = jnp.sum(jnp.where(owner, (pstarts + counts)[None, :], 0), axis=1)
    n_valid = jnp.where(block_idx < n_used, jnp.clip(seg_end - block_idx * blk, 0, blk), 0).astype(jnp.int32)
    packed_base = (jnp.sum(jnp.where(owner, (ustarts - pstarts)[None, :], 0), axis=1) + block_idx * blk)
    packed_base = jnp.where(n_valid > 0, packed_base, 0).astype(jnp.int32)
    return dict(dest=dest.astype(jnp.int32), packed=packed.astype(jnp.int32), zero_block=zero_block,
                block_expert=block_expert, n_used=n_used.reshape(1), n_valid=n_valid, packed_base=packed_base)


def _scatter_kernel(d0_ref, d1_ref, p0_ref, p1_ref, zb_ref, src_ref, o_ref, inv_ref, zero_buf, sem, zero_sem):
    rows = src_ref.shape[0] // SUBLANES
    base = pl.program_id(0) * rows
    fill_rows = zero_buf.shape[0]
    n_blocks = o_ref.shape[0] // fill_rows

    @pl.when(pl.program_id(0) == 0)
    def _():
        zero_buf[...] = jnp.zeros_like(zero_buf)

        def for_each_fill(fn):
            def body(j, carry):
                @pl.when(zb_ref[j] == 1)
                def _():
                    fn(pltpu.make_async_copy(
                        zero_buf, o_ref.at[pl.ds(pl.multiple_of(j * fill_rows, fill_rows), fill_rows)], zero_sem))
                return carry
            lax.fori_loop(0, n_blocks, body, 0)

        for_each_fill(lambda copy: copy.start())
        for_each_fill(lambda copy: copy.wait())

    def row_copy(r, slot):
        return pltpu.make_async_copy(src_ref.at[pl.ds(pl.multiple_of(r * SUBLANES, SUBLANES), SUBLANES)],
                                     o_ref.at[pl.ds(pl.multiple_of(slot * SUBLANES, SUBLANES), SUBLANES)], sem)

    def issue(r, carry):
        t = base + r
        s0 = d0_ref[t]
        s1 = d1_ref[t]
        row_copy(r, s0).start()
        row_copy(r, s1).start()
        inv_ref[p0_ref[t]] = t * TOP_K
        inv_ref[p1_ref[t]] = t * TOP_K + 1
        return carry

    lax.fori_loop(0, rows, issue, 0, unroll=8)
    for _ in range(TOP_K):
        pltpu.make_async_copy(src_ref, o_ref.at[pl.ds(0, rows * SUBLANES)], sem).wait()


def _scatter_rows(plan, src):
    dest, packed = plan["dest"], plan["packed"]
    T = dest.shape[0]
    P = T * TOP_K + N_EXPERTS * MOE_BLK
    rows = GATHER_ROWS
    return pl.pallas_call(
        _scatter_kernel,
        grid_spec=pltpu.PrefetchScalarGridSpec(
            num_scalar_prefetch=5,
            grid=(T // rows,),
            in_specs=[pl.BlockSpec((rows * SUBLANES, LANES), lambda i, *refs: (i, 0))],
            out_specs=[pl.BlockSpec(memory_space=pl.ANY), pl.BlockSpec(memory_space=pltpu.SMEM)],
            scratch_shapes=[
                pltpu.VMEM((MOE_BLK * SUBLANES, LANES), src.dtype),
                pltpu.SemaphoreType.DMA(()),
                pltpu.SemaphoreType.DMA(()),
            ],
        ),
        out_shape=[jax.ShapeDtypeStruct((P * SUBLANES, LANES), src.dtype),
                   jax.ShapeDtypeStruct((T * TOP_K,), jnp.int32)],
        compiler_params=_params(("arbitrary",)),
    )(dest[:, 0], dest[:, 1], packed[:, 0], packed[:, 1], plan["zero_block"], src)


def _expert_kernel(layer, be_ref, first_ref, next_ref, slot_ref, nu_ref, nv_ref, pb_ref, inv_ref,
                   xs_ref, w1_hbm, w3_hbm, w2_hbm, yt_ref,
                   w1f, w3f, w2f, w1b, w3b, w2b, ybuf, sems, ysems):
    i = pl.program_id(0)
    D = w1b.shape[0]
    blk = xs_ref.shape[0] // SUBLANES

    def drain_rows(j):
        n = nv_ref[j]
        b = j % 2
        for bit in range(blk.bit_length()):
            size = (1 << bit) * SUBLANES

            @pl.when((n >> bit) & 1 == 1)
            def _():
                pltpu.make_async_copy(ybuf.at[b, pl.ds(0, size)], yt_ref.at[pl.ds(0, size)], ysems.at[b]).wait()

    def weight_copies(e, slot):
        return (pltpu.make_async_copy(w1_hbm.at[layer, e], w1f.at[slot], sems.at[slot, 0]),
                pltpu.make_async_copy(w3_hbm.at[layer, e], w3f.at[slot], sems.at[slot, 1]),
                pltpu.make_async_copy(w2_hbm.at[layer, e], w2f.at[slot], sems.at[slot, 2]))

    @pl.when(i == 0)
    def _():
        for c in weight_copies(be_ref[0], 0):
            c.start()

    @pl.when(first_ref[i] == 1)
    def _():
        slot = slot_ref[i]
        for c in weight_copies(be_ref[i], slot):
            c.wait()

        @pl.when(next_ref[i] >= 0)
        def _():
            for c in weight_copies(next_ref[i], 1 - slot):
                c.start()

        w1b[...] = w1f[slot].astype(BF16)
        w3b[...] = w3f[slot].astype(BF16)
        w2b[...] = w2f[slot].astype(BF16)

    @pl.when(i >= 2)
    def _():
        drain_rows(i - 2)

    @pl.when(i < nu_ref[0])
    def _():
        b = i % 2
        n = nv_ref[i]

        def swiglu(n_rows):
            tiles = pl.ds(0, n_rows * SUBLANES)
            lo, hi = _unpack_bf16_pair(_load_row_tiles(xs_ref.at[tiles]))
            lo = lo.astype(BF16)
            hi = hi.astype(BF16)
            a = _dot(lo, w1b[0:D // 2, :]) + _dot(hi, w1b[D // 2:, :])
            g = _dot(lo, w3b[0:D // 2, :]) + _dot(hi, w3b[D // 2:, :])
            hid = ((a * _sigmoid(a)) * g).astype(BF16)
            _store_row_tiles(ybuf.at[b, tiles],
                             _pack_bf16_pair(_dot(hid, w2b[:, 0:D // 2]), _dot(hid, w2b[:, D // 2:])))

        @pl.when(n > blk // 2)
        def _():
            swiglu(blk)

        @pl.when(n <= blk // 2)
        def _():
            swiglu(blk // 2)

        first_pos = pb_ref[i]
        unroll = 8

        def issue_row(r):
            a_idx = inv_ref[first_pos + r]
            pltpu.make_async_copy(ybuf.at[b, pl.ds(pl.multiple_of(r * SUBLANES, SUBLANES), SUBLANES)],
                                  yt_ref.at[pl.ds(pl.multiple_of(a_idx * SUBLANES, SUBLANES), SUBLANES)],
                                  ysems.at[b]).start()

        def issue_group(g, carry):
            for u in range(unroll):
                issue_row(g * unroll + u)
            return carry

        def issue_one(r, carry):
            issue_row(r)
            return carry

        lax.fori_loop(0, n // unroll, issue_group, 0)
        lax.fori_loop((n // unroll) * unroll, n, issue_one, 0)

    @pl.when(i == pl.num_programs(0) - 1)
    def _():
        drain_rows(i - 1)
        drain_rows(i)


def _segment_plan(block_expert, n_used):
    n = block_expert.shape[0]
    idx = jnp.arange(n, dtype=jnp.int32)
    prev = jnp.concatenate([jnp.full((1,), -1, jnp.int32), block_expert[:-1]])
    first = ((block_expert != prev) & (idx < n_used[0])).astype(jnp.int32)
    slot = (jnp.cumsum(first) - 1) % 2
    later_first = jnp.where(first == 1, idx, n)
    next_idx = lax.cummin(jnp.concatenate([later_first[1:], jnp.full((1,), n, jnp.int32)]), reverse=True)
    next_expert = jnp.where(next_idx < n, block_expert[jnp.minimum(next_idx, n - 1)], -1)
    return first, next_expert.astype(jnp.int32), slot.astype(jnp.int32)


def _experts(plan, inv, xs, w1, w3, w2, layer):
    _, _, D, DE = w1.shape
    blk = MOE_BLK
    block_expert, n_used = plan["block_expert"], plan["n_used"]
    n_blocks = block_expert.shape[0]
    n_assign = plan["dest"].shape[0] * TOP_K
    first, next_expert, slot = _segment_plan(block_expert, n_used)
    rowmap = lambda i, *refs: (jnp.minimum(i, jnp.maximum(refs[4][0] - 1, 0)), 0)
    hbm = pl.BlockSpec(memory_space=pl.ANY)
    return pl.pallas_call(
        functools.partial(_expert_kernel, layer),
        grid_spec=pltpu.PrefetchScalarGridSpec(
            num_scalar_prefetch=8,
            grid=(n_blocks,),
            in_specs=[pl.BlockSpec((blk * SUBLANES, LANES), rowmap), hbm, hbm, hbm],
            out_specs=hbm,
            scratch_shapes=[
                pltpu.VMEM((2, D, DE), F32),
                pltpu.VMEM((2, D, DE), F32),
                pltpu.VMEM((2, DE, D), F32),
                pltpu.VMEM((D, DE), BF16),
                pltpu.VMEM((D, DE), BF16),
                pltpu.VMEM((DE, D), BF16),
                pltpu.VMEM((2, blk * SUBLANES, LANES), U32),
                pltpu.SemaphoreType.DMA((2, 3)),
                pltpu.SemaphoreType.DMA((2,)),
            ],
        ),
        out_shape=jax.ShapeDtypeStruct((n_assign * SUBLANES, LANES), U32),
        compiler_params=_params(("arbitrary",)),
    )(block_expert, first, next_expert, slot, n_used, plan["n_valid"], plan["packed_base"], inv, xs, w1, w3, w2)


def _final_combine_kernel(yt_ref, h_ref, g2_ref, ew_ref, fg_ref, o_ref):
    hn = h_ref[...] + g2_ref[0] * _moe_mix(yt_ref, ew_ref)
    o_ref[...] = hn * lax.rsqrt(jnp.mean(hn * hn, axis=-1, keepdims=True) + EPS) * fg_ref[...]


def _final_combine(yt, h, mod3, ew, final_g, S):
    T, D = h.shape
    tc = 256
    per_b = S // tc
    return pl.pallas_call(
        _final_combine_kernel,
        grid=(T // tc,),
        in_specs=[
            pl.BlockSpec((tc * TOP_K * SUBLANES, LANES), lambda i: (i, 0)),
            pl.BlockSpec((tc, D), lambda i: (i, 0)),
            pl.BlockSpec((1, 1, D), lambda i: (i // per_b, 0, 5)),
            pl.BlockSpec((tc, LANES), lambda i: (i, 0)),
            pl.BlockSpec((1, D), lambda i: (0, 0)),
        ],
        out_specs=pl.BlockSpec((tc, D), lambda i: (i, 0)),
        out_shape=jax.ShapeDtypeStruct((T, D), F32),
        compiler_params=_params(("parallel",)),
    )(yt, h, mod3, ew, final_g.reshape(1, D))


def kernel(x, c, ada_w, ada_b, norm1_g, w_in, gla_w_a2, gla_b_a, gla_norm_g, lru_conv_w, lru_conv_b,
           lru_wa, lru_ba, lru_wx, lru_bx, lru_lambda, diff_lq1, diff_lk1, diff_lq2, diff_lk2,
           diff_subln_g, rel_bias, w_out, norm2_g, router_g_w, router_g_b, router_e_w, router_e_b,
           moe_w1, moe_w3, moe_w2, final_g):
    B, S, D = x.shape
    T = B * S
    L = ada_w.shape[0]
    h = x.reshape(T, D)
    mod = _ada_mod(c, ada_w, ada_b)
    bias = _bias_tiles(rel_bias)
    w_in_perm = _permute_w_in(w_in)
    w_out_bf16 = w_out.astype(BF16)
    pending_moe = None
    for l in range(L):
        mod3 = mod[l][:, None, :]
        if pending_moe is None:
            proj = _inproj(h, mod3, norm1_g[l], w_in_perm, l, S)
        else:
            proj, h = _inproj(h, mod3, norm1_g[l], w_in_perm, l, S, pending_moe)
        wa2_pad = jnp.concatenate(
            [gla_w_a2[l], jnp.zeros((LANES - GLA_LOWRANK, GLA_KEY_WIDTH), F32)], axis=0).astype(BF16)
        o_gla = _gla(proj, wa2_pad, gla_b_a[l], gla_norm_g[l], B, S)
        w_gates = jnp.concatenate([_block_diag(lru_wa[l]), _block_diag(lru_wx[l])], axis=1).astype(BF16)
        b_gates = jnp.concatenate([lru_ba[l], lru_bx[l]]).reshape(1, 2 * LRU_WIDTH)
        o_lru = _lru(proj.reshape(B, S, PROJ_WIDTH), lru_conv_w[l], lru_conv_b[l], w_gates, b_gates,
                     lru_lambda[l]).reshape(T, LRU_WIDTH)
        lqk = jnp.stack([diff_lq1[l], diff_lk1[l], diff_lq2[l], diff_lk2[l]], axis=0)
        o_diff = _diff_attention(proj, bias, lqk, diff_subln_g[l], l, B, S)
        rw = jnp.concatenate(
            [router_g_w[l], router_e_w[l], jnp.zeros((D, LANES - N_GROUPS - N_EXPERTS), F32)], axis=1).astype(BF16)
        rb = jnp.concatenate(
            [router_g_b[l], router_e_b[l], jnp.zeros((LANES - N_GROUPS - N_EXPERTS,), F32)]).reshape(1, LANES)
        h, u2, info, ew, cnt = _outproj(h, o_gla, o_lru, o_diff, w_out_bf16, l, mod3, norm2_g[l], rw, rb, S)
        counts = cnt[0, N_GROUPS:N_GROUPS + N_EXPERTS].astype(jnp.int32)
        plan = _dispatch(info, counts)
        xs, inv = _scatter_rows(plan, u2)
        yt = _experts(plan, inv, xs, moe_w1, moe_w3, moe_w2, l)
        pending_moe = (yt, ew, mod3)
    out = _final_combine(yt, h, mod3, ew, final_g, S)
    return out.reshape(B, S, D)
```

```python
import functools
import math

import jax
import jax.numpy as jnp
from jax import lax
from jax.experimental import pallas as pl
from jax.experimental.pallas import tpu as pltpu

F32 = jnp.float32
BF16 = jnp.bfloat16
U32 = jnp.uint32

EPS = 1e-6
LOG2E = math.log2(math.e)
CHUNK = 64

GLA_DV = 128
GLA_DK = 64
GLA_HEADS = 6
GLA_WIDTH = GLA_HEADS * GLA_DV
GLA_KEY_WIDTH = GLA_HEADS * GLA_DK
GLA_LOWRANK = 16
GLA_TAU = 16.0

LRU_WIDTH = 512
LRU_BLOCKS = 8
LRU_BLOCK_DIM = LRU_WIDTH // LRU_BLOCKS
CONV_WIDTH = 4
LRU_C = 8.0

DIFF_DH = 64
DIFF_DV = 128
DIFF_HEADS = 6
DIFF_WIDTH = DIFF_HEADS * DIFF_DV

REL_BUCKETS = 32
REL_MAX_DIST = 128

N_GROUPS = 8
EXPERTS_PER_GROUP = 8
N_EXPERTS = 64
TOP_K = 2

LANES = 128
SUBLANES = 8
VMEM_LIMIT = 56 * 1024 * 1024

COL_GV = 0
COL_GOG = 768
COL_DQ = 1536
COL_DK = 2304
COL_DV = 3072
COL_GQ = 3840
COL_GK = 4224
COL_LY = 4608
COL_LX = 5120
COL_GA = 5632
PROJ_WIDTH = 5760

ATT_TILE = 512
GLA_TILE = 256
LRU_TILE = 256
MOE_BLK = 256
GATHER_ROWS = 512


def _params(sem, vmem=VMEM_LIMIT):
    return pltpu.CompilerParams(dimension_semantics=sem, vmem_limit_bytes=vmem)


def _sigmoid(x):
    return 0.5 * jnp.tanh(0.5 * x) + 0.5


def _softplus(x):
    return jnp.maximum(x, 0.0) + jnp.log1p(jnp.exp(-jnp.abs(x)))


def _dot(a, b):
    return jnp.dot(a, b, preferred_element_type=F32)


def _dot_nt(a, b):
    return lax.dot_general(a, b, (((1,), (1,)), ((), ())), preferred_element_type=F32)


def _dot_tn(a, b):
    return lax.dot_general(a, b, (((0,), (0,)), ((), ())), preferred_element_type=F32)


def _pack_bf16_pair(lo, hi):
    lo_bits = lax.bitcast_convert_type(lo.astype(BF16).astype(F32), U32)
    hi_bits = lax.bitcast_convert_type(hi.astype(BF16).astype(F32), U32)
    return (hi_bits & jnp.uint32(0xFFFF0000)) | (lo_bits >> 16)


def _unpack_bf16_pair(w):
    lo = lax.bitcast_convert_type(w << 16, F32)
    hi = lax.bitcast_convert_type(w & jnp.uint32(0xFFFF0000), F32)
    return lo, hi


def _store_row_tiles(ref, words):
    rows = words.shape[0]
    for s in range(SUBLANES):
        ref[pl.ds(s, rows, stride=SUBLANES), :] = words[:, s * LANES:(s + 1) * LANES]


def _load_row_tiles(ref):
    rows = ref.shape[0] // SUBLANES
    return jnp.concatenate([ref[pl.ds(s, rows, stride=SUBLANES), :] for s in range(SUBLANES)], axis=1)


def _ada_kernel(c_ref, w_ref, b_ref, o_ref):
    c = c_ref[...]
    s = c * _sigmoid(c)
    o_ref[0] = _dot(s.astype(BF16), w_ref[0].astype(BF16)) + b_ref[0]


def _ada_mod(c, ada_w, ada_b):
    L, D, N = ada_w.shape
    B = c.shape[0]
    tn = 1024
    return pl.pallas_call(
        _ada_kernel,
        grid=(L, N // tn),
        in_specs=[
            pl.BlockSpec((B, D), lambda l, j: (0, 0)),
            pl.BlockSpec((1, D, tn), lambda l, j: (l, 0, j)),
            pl.BlockSpec((1, 1, tn), lambda l, j: (l, 0, j)),
        ],
        out_specs=pl.BlockSpec((1, B, tn), lambda l, j: (l, 0, j)),
        out_shape=jax.ShapeDtypeStruct((L, B, N), F32),
        compiler_params=_params(("parallel", "parallel")),
    )(c, ada_w, ada_b.reshape(L, 1, N))


def _modulated_norm(x, g, sc, sh):
    ms = jnp.mean(x * x, axis=-1, keepdims=True)
    return (x * lax.rsqrt(ms + EPS) * g) * (1.0 + sc) + sh


def _moe_mix(yt_ref, ew_ref):
    rows = ew_ref.shape[0]

    def expert_rows(k):
        return jnp.concatenate([yt_ref[pl.ds(k * SUBLANES + s, rows, stride=TOP_K * SUBLANES), :]
                                for s in range(SUBLANES)], axis=1)

    ew = ew_ref[...]
    w0 = ew[:, 0:1]
    w1 = ew[:, 1:2]
    lo0, hi0 = _unpack_bf16_pair(expert_rows(0))
    lo1, hi1 = _unpack_bf16_pair(expert_rows(1))
    return jnp.concatenate([w0 * lo0 + w1 * lo1, w0 * hi0 + w1 * hi1], axis=1)


def _inproj_kernel(h_ref, sh_ref, sc_ref, g_ref, w_ref, o_ref, u_scr):
    tm = h_ref.shape[0]
    nr = tm // 2

    @pl.when(pl.program_id(1) == 0)
    def _():
        for half in range(2):
            rows = slice(half * nr, (half + 1) * nr)
            u = _modulated_norm(h_ref[rows, :], g_ref[...], sc_ref[0], sh_ref[0]).astype(BF16)
            u_scr[rows, :] = u
            o_ref[rows, :] = _dot(u, w_ref[...]).astype(o_ref.dtype)

    @pl.when(pl.program_id(1) > 0)
    def _():
        o_ref[...] = _dot(u_scr[...], w_ref[...]).astype(o_ref.dtype)


def _inproj_after_moe_kernel(h_ref, yt_ref, ew_ref, g2_ref, sh_ref, sc_ref, g_ref, w_ref, o_ref, hn_ref, u_scr):
    tm = h_ref.shape[0]
    nr = tm // 2

    @pl.when(pl.program_id(1) == 0)
    def _():
        for half in range(2):
            rows = slice(half * nr, (half + 1) * nr)
            tiles = pl.ds(half * nr * TOP_K * SUBLANES, nr * TOP_K * SUBLANES)
            hn = h_ref[rows, :] + g2_ref[0] * _moe_mix(yt_ref.at[tiles], ew_ref.at[rows])
            hn_ref[rows, :] = hn
            u = _modulated_norm(hn, g_ref[...], sc_ref[0], sh_ref[0]).astype(BF16)
            u_scr[rows, :] = u
            o_ref[rows, :] = _dot(u, w_ref[...]).astype(o_ref.dtype)

    @pl.when(pl.program_id(1) > 0)
    def _():
        o_ref[...] = _dot(u_scr[...], w_ref[...]).astype(o_ref.dtype)


def _inproj(h, mod3, norm_g, w_perm, layer, S, pending_moe=None):
    T, D = h.shape
    N = w_perm.shape[2]
    tm, tn = 512, 1152
    per_b = S // tm
    rows = pl.BlockSpec((tm, D), lambda i, j: (i, 0))
    mod_specs = [
        pl.BlockSpec((1, 1, D), lambda i, j: (i // per_b, 0, 0)),
        pl.BlockSpec((1, 1, D), lambda i, j: (i // per_b, 0, 1)),
        pl.BlockSpec((1, D), lambda i, j: (0, 0)),
        pl.BlockSpec((None, D, tn), lambda i, j: (layer, 0, j)),
    ]
    proj_spec = pl.BlockSpec((tm, tn), lambda i, j: (i, j))
    proj_shape = jax.ShapeDtypeStruct((T, N), BF16)
    common = dict(grid=(T // tm, N // tn), scratch_shapes=[pltpu.VMEM((tm, D), BF16)],
                  compiler_params=_params(("parallel", "arbitrary")))
    if pending_moe is None:
        return pl.pallas_call(_inproj_kernel, in_specs=[rows] + mod_specs, out_specs=proj_spec,
                              out_shape=proj_shape, **common)(h, mod3, mod3, norm_g.reshape(1, D), w_perm)
    yt, ew, prev_mod3 = pending_moe
    moe_specs = [
        pl.BlockSpec((tm * TOP_K * SUBLANES, LANES), lambda i, j: (i, 0)),
        pl.BlockSpec((tm, LANES), lambda i, j: (i, 0)),
        pl.BlockSpec((1, 1, D), lambda i, j: (i // per_b, 0, 5)),
    ]
    return pl.pallas_call(
        _inproj_after_moe_kernel, in_specs=[rows] + moe_specs + mod_specs, out_specs=[proj_spec, rows],
        out_shape=[proj_shape, jax.ShapeDtypeStruct((T, D), F32)], **common,
    )(h, yt, ew, prev_mod3, mod3, mod3, norm_g.reshape(1, D), w_perm)


W_IN_SEGMENTS = ((COL_GQ, 0, 384), (COL_GK, 384, 384), (COL_GV, 768, 768), (COL_GOG, 1536, 768),
                 (COL_GA, 2304, GLA_LOWRANK), (COL_LY, 2320, 512), (COL_LX, 2832, 512),
                 (COL_DQ, 3344, 768), (COL_DK, 4112, 768), (COL_DV, 4880, 768))


def _relayout_kernel(w_ref, o_ref):
    x = w_ref[0]
    for dst, src, width in W_IN_SEGMENTS:
        o_ref[0, :, dst:dst + width] = x[:, src:src + width]
    pad = slice(COL_GA + GLA_LOWRANK, COL_GA + LANES)
    o_ref[0, :, pad] = jnp.zeros((x.shape[0], LANES - GLA_LOWRANK), BF16)


def _permute_w_in(w):
    L, D, N = w.shape
    rt = 256
    return pl.pallas_call(
        _relayout_kernel,
        grid=(L, D // rt),
        in_specs=[pl.BlockSpec((1, rt, N), lambda l, i: (l, i, 0))],
        out_specs=pl.BlockSpec((1, rt, PROJ_WIDTH), lambda l, i: (l, i, 0)),
        out_shape=jax.ShapeDtypeStruct((L, D, PROJ_WIDTH), BF16),
        compiler_params=_params(("parallel", "parallel")),
    )(w.astype(BF16))


def _gla_kernel(q_ref, k_ref, v_ref, og_ref, alr_ref, wa2_ref, ba_ref, ng_ref, o_ref, st_ref):
    tb = q_ref.shape[0]
    n_chunks = tb // CHUNK

    @pl.when(pl.program_id(1) == 0)
    def _():
        st_ref[...] = jnp.zeros_like(st_ref)

    row = lax.broadcasted_iota(jnp.int32, (tb, tb), 0)
    col = lax.broadcasted_iota(jnp.int32, (tb, tb), 1)
    same_chunk = (row // CHUNK) == (col // CHUNK)
    causal = col <= row
    tril = jnp.where(same_chunk & causal, 1.0, 0.0).astype(BF16)
    lane = lax.broadcasted_iota(jnp.int32, (1, LANES), 1)
    half_masks = (lane < GLA_DK, lane >= GLA_DK)

    alr = alr_ref[...]
    cols = [slice(p * LANES, (p + 1) * LANES) for p in range(GLA_HEADS // 2)]
    z = [_dot(alr, wa2_ref[:, cs]) + ba_ref[:, cs] for cs in cols]
    la = [(jnp.minimum(zp, 0.0) - jnp.log1p(jnp.exp(-jnp.abs(zp)))) * (1.0 / GLA_TAU) for zp in z]
    la_hi = [x.astype(BF16) for x in la]
    la_lo = [(x - h.astype(F32)).astype(BF16) for x, h in zip(la, la_hi)]
    G = [_dot(tril, h) + _dot(tril, lo) for h, lo in zip(la_hi, la_lo)]
    Gl = [jnp.concatenate([jnp.broadcast_to(g[(c + 1) * CHUNK - 1:(c + 1) * CHUNK, :], (CHUNK, LANES))
                           for c in range(n_chunks)], axis=0) for g in G]
    eG = [jnp.exp(g) for g in G]
    enG = [jnp.exp(-g) for g in G]
    q = [q_ref[:, cs].astype(F32) * (GLA_DK ** -0.5) for cs in cols]
    k = [k_ref[:, cs].astype(F32) for cs in cols]
    kf = [(kp * e).astype(BF16) for kp, e in zip(k, eG)]
    kb = [(kp * e).astype(BF16) for kp, e in zip(k, enG)]
    kd = [kp * jnp.exp(gl - g) for kp, gl, g in zip(k, Gl, G)]

    heads = range(GLA_HEADS)
    pair = [h // 2 for h in heads]
    mask = [half_masks[h % 2] for h in heads]
    vcols = [slice(h * GLA_DV, (h + 1) * GLA_DV) for h in heads]
    qf_h = [jnp.where(mask[h], q[pair[h]] * eG[pair[h]], 0.0).astype(BF16) for h in heads]
    qb_h = [jnp.where(mask[h], q[pair[h]] * enG[pair[h]], 0.0).astype(BF16) for h in heads]
    kd_h = [jnp.where(mask[h], kd[pair[h]], 0.0).astype(BF16) for h in heads]
    v_h = [v_ref[:, vcols[h]] for h in heads]
    a_f = [_dot_nt(qf_h[h], kb[pair[h]]) for h in heads]
    a_b = [_dot_nt(qb_h[h], kf[pair[h]]) for h in heads]
    attn = [jnp.where(same_chunk, jnp.where(causal, a_f[h], a_b[h]), 0.0).astype(BF16) for h in heads]
    o_intra = [_dot(attn[h], v_h[h]) for h in heads]
    chunk_rows = [slice(c * CHUNK, (c + 1) * CHUNK) for c in range(n_chunks)]
    kv = [[_dot_tn(v_h[h][rs], kd_h[h][rs]) for rs in chunk_rows] for h in heads]
    for h in heads:
        st = st_ref[h]
        inter = []
        for c, rs in enumerate(chunk_rows):
            inter.append(_dot_nt(qf_h[h][rs], st.astype(BF16)))
            decay = jnp.exp(Gl[pair[h]][c * CHUNK:c * CHUNK + 1, :])
            st = st * decay + kv[h][c]
        st_ref[h] = st
        o = o_intra[h] + jnp.concatenate(inter, axis=0)
        o = o * lax.rsqrt(jnp.mean(o * o, axis=-1, keepdims=True) + EPS)
        og = og_ref[:, vcols[h]].astype(F32)
        o_ref[:, vcols[h]] = (o * ng_ref[:, vcols[h]] * (og * _sigmoid(og))).astype(o_ref.dtype)


def _gla(proj, wa2_pad, b_a, norm_g, B, S):
    T = proj.shape[0]
    tb = GLA_TILE
    nt = S // tb
    row = lambda b, i: b * nt + i
    return pl.pallas_call(
        _gla_kernel,
        grid=(B, nt),
        in_specs=[
            pl.BlockSpec((tb, GLA_KEY_WIDTH), lambda b, i: (row(b, i), COL_GQ // GLA_KEY_WIDTH)),
            pl.BlockSpec((tb, GLA_KEY_WIDTH), lambda b, i: (row(b, i), COL_GK // GLA_KEY_WIDTH)),
            pl.BlockSpec((tb, GLA_WIDTH), lambda b, i: (row(b, i), COL_GV // GLA_WIDTH)),
            pl.BlockSpec((tb, GLA_WIDTH), lambda b, i: (row(b, i), COL_GOG // GLA_WIDTH)),
            pl.BlockSpec((tb, LANES), lambda b, i: (row(b, i), COL_GA // LANES)),
            pl.BlockSpec((LANES, GLA_KEY_WIDTH), lambda b, i: (0, 0)),
            pl.BlockSpec((1, GLA_KEY_WIDTH), lambda b, i: (0, 0)),
            pl.BlockSpec((1, GLA_WIDTH), lambda b, i: (0, 0)),
        ],
        out_specs=pl.BlockSpec((tb, GLA_WIDTH), lambda b, i: (row(b, i), 0)),
        out_shape=jax.ShapeDtypeStruct((T, GLA_WIDTH), BF16),
        scratch_shapes=[pltpu.VMEM((GLA_HEADS, GLA_DV, LANES), F32)],
        compiler_params=_params(("parallel", "arbitrary")),
    )(proj, proj, proj, proj, proj, wa2_pad, b_a.reshape(1, -1), norm_g.reshape(1, -1))


def _lru_kernel(y_ref, x_ref, cw_ref, cb_ref, wg_ref, bg_ref, lam_ref, o_ref, *scratch):
    B, ts, W = x_ref.shape
    n_planes = W // LANES
    a_scr = scratch[0:n_planes]
    b_scr = scratch[n_planes:2 * n_planes]
    h_scr = scratch[2 * n_planes:3 * n_planes]
    xc_scr, tail_scr, carry_scr = scratch[3 * n_planes:]

    @pl.when(pl.program_id(0) == 0)
    def _():
        tail_scr[...] = jnp.zeros_like(tail_scr)
        carry_scr[...] = jnp.zeros_like(carry_scr)

    cw = cw_ref[...]
    cb = cb_ref[...]
    sp = _softplus(-lam_ref[...])
    row8 = lax.broadcasted_iota(jnp.int32, (8, W), 0)
    for b in range(B):
        x = x_ref[b].astype(F32)
        tail = tail_scr[b]
        xc = cb + cw[CONV_WIDTH - 1:CONV_WIDTH, :] * x
        head = cb + cw[CONV_WIDTH - 1:CONV_WIDTH, :] * x[0:8]
        for d in range(1, CONV_WIDTH):
            wd = cw[CONV_WIDTH - 1 - d:CONV_WIDTH - d, :]
            xr = pltpu.roll(x, d, 0)
            xc = xc + wd * xr
            head = head + wd * jnp.where(row8 < d, pltpu.roll(tail, d, 0), xr[0:8])
        tail_scr[b] = x[ts - 8:ts]
        xc_scr[...] = xc
        xc_scr[0:8] = head
        xc = xc_scr[...]
        gates = _sigmoid(_dot(xc.astype(BF16), wg_ref[...]) + bg_ref[...])
        r = gates[:, :W]
        ig = gates[:, W:]
        log_a = (-LRU_C) * r * sp
        a = jnp.exp(log_a)
        b_in = jnp.sqrt(-jnp.tanh(log_a) * (a * a + 1.0)) * (ig * xc)
        rows = slice(b * ts, (b + 1) * ts)
        for k in range(n_planes):
            a_scr[k][rows] = a[:, k * LANES:(k + 1) * LANES]
            b_scr[k][rows] = b_in[:, k * LANES:(k + 1) * LANES]

    def step(t, hs):
        idx = pl.ds(t, B, stride=ts)
        out = []
        for k in range(n_planes):
            hk = a_scr[k][idx, :] * hs[k] + b_scr[k][idx, :]
            h_scr[k][idx, :] = hk
            out.append(hk)
        return tuple(out)

    hs = lax.fori_loop(0, ts, step, tuple(carry_scr[k] for k in range(n_planes)), unroll=8)
    for k in range(n_planes):
        carry_scr[k] = hs[k]

    for b in range(B):
        rows = slice(b * ts, (b + 1) * ts)
        y = y_ref[b].astype(F32)
        gelu = 0.5 * y * (1.0 + jnp.tanh(math.sqrt(2.0 / math.pi) * (y + 0.044715 * (y * y * y))))
        h = jnp.concatenate([h_scr[k][rows] for k in range(n_planes)], axis=1)
        o_ref[b] = (h * gelu).astype(o_ref.dtype)


def _lru(proj3, conv_w, conv_b, w_gates, b_gates, lam):
    B, S, _ = proj3.shape
    W = LRU_WIDTH
    ts = LRU_TILE
    n_planes = W // LANES
    full = lambda shape: pl.BlockSpec(shape, lambda i: (0,) * len(shape))
    return pl.pallas_call(
        _lru_kernel,
        grid=(S // ts,),
        in_specs=[
            pl.BlockSpec((B, ts, W), lambda i: (0, i, COL_LY // W)),
            pl.BlockSpec((B, ts, W), lambda i: (0, i, COL_LX // W)),
            full((CONV_WIDTH, W)),
            full((1, W)),
            full((W, 2 * W)),
            full((1, 2 * W)),
            full((1, W)),
        ],
        out_specs=pl.BlockSpec((B, ts, W), lambda i: (0, i, 0)),
        out_shape=jax.ShapeDtypeStruct((B, S, W), BF16),
        scratch_shapes=(
            [pltpu.VMEM((B * ts, LANES), F32) for _ in range(3 * n_planes)]
            + [pltpu.VMEM((ts, W), F32), pltpu.VMEM((B, 8, W), F32), pltpu.VMEM((n_planes, B, LANES), F32)]),
        compiler_params=_params(("arbitrary",)),
    )(proj3, proj3, conv_w, conv_b.reshape(1, W), w_gates, b_gates, lam.reshape(1, W))


def _block_diag(w):
    n, d, _ = w.shape
    eye = jnp.eye(n, dtype=w.dtype)
    return (eye[:, None, :, None] * w[:, :, None, :]).reshape(n * d, n * d)


def _t5_bucket(rel):
    nb = REL_BUCKETS // 2
    ret = (rel > 0).astype(jnp.int32) * nb
    n = jnp.abs(rel)
    max_exact = nb // 2
    nf = jnp.maximum(n, 1).astype(jnp.float32)
    large = max_exact + (jnp.log(nf / max_exact) / math.log(REL_MAX_DIST / max_exact)
                         * (nb - max_exact)).astype(jnp.int32)
    large = jnp.minimum(large, nb - 1)
    return ret + jnp.where(n < max_exact, n, large)


def _bias_kernel(bucket_ref, table_ref, o_ref):
    h = pl.program_id(0)
    bucket = bucket_ref[0]
    acc = jnp.full(bucket.shape, -1e30, F32)
    for b in range(REL_BUCKETS):
        acc = jnp.where(bucket == b, table_ref[b, h] * LOG2E, acc)
    o_ref[0, 0] = acc


def _bias_tiles(rel_bias):
    t = ATT_TILE
    H = rel_bias.shape[1]
    qp = jnp.arange(t, dtype=jnp.int32)[:, None]
    kp = jnp.arange(t, dtype=jnp.int32)[None, :]
    mask = (kp // CHUNK) <= (qp // CHUNK)
    half = REL_BUCKETS // 2
    per_distance = _t5_bucket(-jnp.arange(2 * t, dtype=jnp.int32))
    edges = jnp.sum((per_distance[None, :] < jnp.arange(1, half, dtype=jnp.int32)[:, None]).astype(jnp.int32), axis=1)

    def bucket_2d(rel):
        passed = jnp.sum((jnp.abs(rel)[None] >= edges[:, None, None]).astype(jnp.int32), axis=0)
        return (rel > 0).astype(jnp.int32) * half + passed

    buckets = jnp.stack([jnp.where(mask, bucket_2d(kp - qp), REL_BUCKETS), bucket_2d(kp - t - qp)], axis=0)
    table = rel_bias.astype(F32)
    tiles = pl.pallas_call(
        _bias_kernel,
        grid=(H, 2),
        in_specs=[
            pl.BlockSpec((1, t, t), lambda h, k: (k, 0, 0)),
            pl.BlockSpec(memory_space=pltpu.SMEM),
        ],
        out_specs=pl.BlockSpec((1, 1, t, t), lambda h, k: (h, k, 0, 0)),
        out_shape=jax.ShapeDtypeStruct((H, 2, t, t), F32),
        compiler_params=_params(("parallel", "parallel")),
    )(buckets, table)
    far_bucket = _t5_bucket(jnp.full((1,), -t - 1, jnp.int32))
    far = jnp.sum(jnp.where(jnp.arange(REL_BUCKETS)[:, None] == far_bucket, table, 0.0), axis=0)
    return tiles, jnp.broadcast_to((far * LOG2E)[:, None, None], (H, 1, t))


def _diff_kernel(lam_init, q_ref, k_ref, v_ref, bias_ref, far_ref, lqk_ref, g_ref, o_ref,
                 qs_scr, m_scr, l_scr, acc_scr):
    i = pl.program_id(2)
    t = q_ref.shape[0]
    hq = t // 2
    lane = lax.broadcasted_iota(jnp.int32, (1, LANES), 1)
    q = q_ref[...].astype(F32) * (LOG2E * DIFF_DH ** -0.5)
    for half in range(2):
        qh = q[half * hq:(half + 1) * hq]
        qs_scr[(2 * half) * hq:(2 * half + 1) * hq] = jnp.where(lane < DIFF_DH, qh, 0.0).astype(BF16)
        qs_scr[(2 * half + 1) * hq:(2 * half + 2) * hq] = jnp.where(lane >= DIFF_DH, qh, 0.0).astype(BF16)
    m_scr[...] = jnp.full_like(m_scr, -1e30)
    l_scr[...] = jnp.zeros_like(l_scr)
    acc_scr[...] = jnp.zeros_like(acc_scr)

    def tile(rows, ks, bias):
        s = _dot_nt(qs_scr[rows, :], k_ref[ks, :]) + bias
        groups = [s[:, c * LANES:(c + 1) * LANES] for c in range(s.shape[1] // LANES)]
        mx = functools.reduce(jnp.maximum, groups)
        m_prev = m_scr[rows, :]
        m_new = jnp.maximum(m_prev, jnp.max(mx, axis=-1, keepdims=True))
        alpha = jnp.exp2(m_prev - m_new)
        ps = [jnp.exp2(g - m_new) for g in groups]
        l_scr[rows, :] = alpha * l_scr[rows, :] + functools.reduce(jnp.add, ps)
        p = jnp.concatenate(ps, axis=1).astype(BF16)
        acc_scr[rows, :] = alpha * acc_scr[rows, :] + _dot(p, v_ref[ks, :])
        m_scr[rows, :] = m_new

    def stacked(b, half):
        bh = b[half * hq:(half + 1) * hq]
        return [bh, bh]

    all_rows = slice(0, 2 * t)

    def far_body(j, carry):
        tile(all_rows, pl.ds(pl.multiple_of(j * t, t), t), far_ref[0])
        return carry

    lax.fori_loop(0, jnp.maximum(i - 1, 0), far_body, 0)

    @pl.when(i >= 1)
    def _():
        b = bias_ref[0, 1]
        tile(all_rows, pl.ds(pl.multiple_of((i - 1) * t, t), t),
             jnp.concatenate(stacked(b, 0) + stacked(b, 1), axis=0))

    b = bias_ref[0, 0]
    diag0 = pl.multiple_of(i * t, t)
    tile(slice(0, t), pl.ds(diag0, hq), jnp.concatenate(stacked(b[:, 0:hq], 0), axis=0))
    tile(slice(t, 2 * t), pl.ds(diag0, t), jnp.concatenate(stacked(b, 1), axis=0))

    lqk = lqk_ref[...]
    lam = (jnp.exp(jnp.sum(lqk[0:1] * lqk[1:2], axis=-1, keepdims=True))
           - jnp.exp(jnp.sum(lqk[2:3] * lqk[3:4], axis=-1, keepdims=True)) + lam_init)
    o = acc_scr[...] / jnp.sum(l_scr[...], axis=-1, keepdims=True)
    o = jnp.concatenate([o[0:hq] - lam * o[hq:t], o[t:t + hq] - lam * o[t + hq:2 * t]], axis=0)
    o = o * lax.rsqrt(jnp.mean(o * o, axis=-1, keepdims=True) + EPS)
    o_ref[...] = (o * g_ref[...] * (1.0 - lam_init)).astype(o_ref.dtype)


def _diff_attention(proj, bias, lqk, subln_g, layer_idx, B, S):
    T = proj.shape[0]
    t = ATT_TILE
    nq = S // t
    tiles, far = bias
    lam_init = 0.8 - 0.6 * math.exp(-0.3 * layer_idx)
    return pl.pallas_call(
        functools.partial(_diff_kernel, lam_init),
        grid=(B, DIFF_HEADS, nq),
        in_specs=[
            pl.BlockSpec((t, LANES), lambda b, h, i: (b * nq + i, COL_DQ // LANES + h)),
            pl.BlockSpec((S, LANES), lambda b, h, i: (b, COL_DK // LANES + h)),
            pl.BlockSpec((S, LANES), lambda b, h, i: (b, COL_DV // LANES + h)),
            pl.BlockSpec((1, 2, t, t), lambda b, h, i: (h, 0, 0, 0)),
            pl.BlockSpec((1, 1, t), lambda b, h, i: (h, 0, 0)),
            pl.BlockSpec((4, DIFF_DH), lambda b, h, i: (0, 0)),
            pl.BlockSpec((1, DIFF_DV), lambda b, h, i: (0, 0)),
        ],
        out_specs=pl.BlockSpec((t, LANES), lambda b, h, i: (b * nq + i, h)),
        out_shape=jax.ShapeDtypeStruct((T, DIFF_WIDTH), BF16),
        scratch_shapes=[
            pltpu.VMEM((2 * t, LANES), BF16),
            pltpu.VMEM((2 * t, LANES), F32),
            pltpu.VMEM((2 * t, LANES), F32),
            pltpu.VMEM((2 * t, DIFF_DV), F32),
        ],
        compiler_params=_params(("parallel", "parallel", "arbitrary")),
    )(proj, proj, proj, tiles, far, lqk, subln_g.reshape(1, DIFF_DV))


def _outproj_kernel(h_ref, og_ref, ol_ref, od_ref, w_ref, g1_ref, sh2_ref, sc2_ref, n2_ref, rw_ref, rb_ref,
                    hn_ref, u2_ref, eid_ref, ew_ref, cnt_ref):
    tm = h_ref.shape[0]

    @pl.when(pl.program_id(0) == 0)
    def _():
        cnt_ref[...] = jnp.zeros_like(cnt_ref)

    nr = tm // 2
    halves = [slice(half * nr, (half + 1) * nr) for half in range(2)]
    accs = []
    for rows in halves:
        acc = _dot(og_ref[rows, :], w_ref[0:GLA_WIDTH, :])
        acc += _dot(ol_ref[rows, :], w_ref[GLA_WIDTH:GLA_WIDTH + LRU_WIDTH, :])
        acc += _dot(od_ref[rows, :], w_ref[GLA_WIDTH + LRU_WIDTH:, :])
        accs.append(acc)
    for rows, acc in zip(halves, accs):
        _outproj_rows(rows, acc, h_ref, g1_ref, sh2_ref, sc2_ref, n2_ref, rw_ref, rb_ref,
                      hn_ref, u2_ref, eid_ref, ew_ref, cnt_ref)


def _outproj_rows(rows, acc, h_ref, g1_ref, sh2_ref, sc2_ref, n2_ref, rw_ref, rb_ref,
                  hn_ref, u2_ref, eid_ref, ew_ref, cnt_ref):
    D = h_ref.shape[1]
    nr = rows.stop - rows.start
    hn = h_ref[rows, :] + g1_ref[0] * acc
    hn_ref[rows, :] = hn
    u2 = _modulated_norm(hn, n2_ref[...], sc2_ref[0], sh2_ref[0])
    _store_row_tiles(u2_ref.at[pl.ds(rows.start * SUBLANES, nr * SUBLANES)],
                     _pack_bf16_pair(u2[:, :D // 2], u2[:, D // 2:]))

    logits = _dot(u2.astype(BF16), rw_ref[...]) + rb_ref[...]
    lane = lax.broadcasted_iota(jnp.int32, logits.shape, 1)
    lane_f = lane.astype(F32)
    neg = jnp.float32(-jnp.inf)
    gmask = lane < N_GROUPS
    gl = jnp.where(gmask, logits, neg)
    gmax = jnp.max(gl, axis=-1, keepdims=True)
    gidx = jnp.min(jnp.where(gl == gmax, lane_f, float(LANES)), axis=-1, keepdims=True)
    g_w = 1.0 / jnp.sum(jnp.where(gmask, jnp.exp(gl - gmax), 0.0), axis=-1, keepdims=True)
    egroup = ((lane - N_GROUPS) >> 3).astype(F32)
    emask = (lane >= N_GROUPS) & (lane < N_GROUPS + N_EXPERTS) & (egroup == gidx)
    el = jnp.where(emask, logits, neg)
    v1 = jnp.max(el, axis=-1, keepdims=True)
    i1 = jnp.min(jnp.where(el == v1, lane_f, float(LANES)), axis=-1, keepdims=True)
    el2 = jnp.where(lane_f == i1, neg, el)
    v2 = jnp.max(el2, axis=-1, keepdims=True)
    i2 = jnp.min(jnp.where(el2 == v2, lane_f, float(LANES)), axis=-1, keepdims=True)
    e21 = jnp.exp(v2 - v1)
    w1 = g_w / (1.0 + e21)
    w2 = g_w * e21 / (1.0 + e21)
    ew_ref[rows, :] = jnp.where(lane == 0, w1, jnp.where(lane == 1, w2, 0.0))

    oh1 = lane_f == i1
    oh2 = lane_f == i2
    both = jnp.where(oh1 | oh2, 1.0, 0.0).astype(BF16)
    row = lax.broadcasted_iota(jnp.int32, (nr, nr), 0)
    col = lax.broadcasted_iota(jnp.int32, (nr, nr), 1)
    earlier = _dot(jnp.where(col < row, 1.0, 0.0).astype(BF16), both) + cnt_ref[0:1, :]
    rank1 = jnp.sum(jnp.where(oh1, earlier, 0.0), axis=-1, keepdims=True)
    rank2 = jnp.sum(jnp.where(oh2, earlier, 0.0), axis=-1, keepdims=True)
    cnt_ref[0:1, :] = cnt_ref[0:1, :] + jnp.sum(both.astype(F32), axis=0, keepdims=True)
    info = jnp.where(lane == 0, i1 - float(N_GROUPS),
                     jnp.where(lane == 1, i2 - float(N_GROUPS),
                               jnp.where(lane == 2, rank1, jnp.where(lane == 3, rank2, 0.0))))
    eid_ref[rows, :] = info.astype(jnp.int32)


def _outproj(h, o_gla, o_lru, o_diff, w_out, layer, mod3, norm2_g, rw, rb, S):
    T, D = h.shape
    tm = 512
    per_b = S // tm
    rowblk = lambda width: pl.BlockSpec((tm, width), lambda i: (i, 0))
    modblk = lambda k: pl.BlockSpec((1, 1, D), lambda i: (i // per_b, 0, k))
    full = lambda shape: pl.BlockSpec(shape, lambda i: (0,) * len(shape))
    return pl.pallas_call(
        _outproj_kernel,
        grid=(T // tm,),
        in_specs=[
            rowblk(D), rowblk(GLA_WIDTH), rowblk(LRU_WIDTH), rowblk(DIFF_WIDTH),
            pl.BlockSpec((None, D, D), lambda i: (layer, 0, 0)),
            modblk(2), modblk(3), modblk(4),
            full((1, D)),
            full((D, LANES)),
            full((1, LANES)),
        ],
        out_specs=[rowblk(D), pl.BlockSpec((tm * SUBLANES, LANES), lambda i: (i, 0)), rowblk(LANES), rowblk(LANES),
                   full((SUBLANES, LANES))],
        out_shape=[
            jax.ShapeDtypeStruct((T, D), F32),
            jax.ShapeDtypeStruct((T * SUBLANES, LANES), U32),
            jax.ShapeDtypeStruct((T, LANES), jnp.int32),
            jax.ShapeDtypeStruct((T, LANES), F32),
            jax.ShapeDtypeStruct((SUBLANES, LANES), F32),
        ],
        compiler_params=_params(("arbitrary",)),
    )(h, o_gla, o_lru, o_diff, w_out, mod3, mod3, mod3, norm2_g.reshape(1, D), rw, rb)


def _dispatch(info, counts):
    T = info.shape[0]
    blk = MOE_BLK
    n_blocks = (T * TOP_K) // blk + N_EXPERTS
    expert = info[:, 0:TOP_K]
    rank = info[:, TOP_K:2 * TOP_K]
    padded = (counts + blk - 1) // blk * blk
    pends = jnp.cumsum(padded)
    pstarts = pends - padded
    ustarts = jnp.cumsum(counts) - counts
    onehot = expert[:, :, None] == jnp.arange(N_EXPERTS, dtype=jnp.int32)[None, None, :]
    dest = rank + jnp.sum(jnp.where(onehot, pstarts[None, None, :], 0), axis=-1)
    packed = rank + jnp.sum(jnp.where(onehot, ustarts[None, None, :], 0), axis=-1)
    n_used = (pends[-1] // blk).astype(jnp.int32)
    block_idx = jnp.arange(n_blocks, dtype=jnp.int32)
    block_expert = jnp.minimum(jnp.sum((pends[None, :] <= (block_idx * blk)[:, None]).astype(jnp.int32), axis=1),
                               N_EXPERTS - 1)
    last_used = jnp.sum(jnp.where(block_idx == jnp.maximum(n_used - 1, 0), block_expert, 0))
    block_expert = jnp.where(block_idx < n_used, block_expert, last_used).astype(jnp.int32)
    following = jnp.concatenate([block_expert[1:], jnp.full((1,), -1, jnp.int32)])
    zero_block = ((block_idx >= n_used - 1) | (following != block_expert)).astype(jnp.int32)
    owner = block_expert[:, None] == jnp.arange(N_EXPERTS, dtype=jnp.int32)[None, :]
    seg_end = jnp.sum(jnp.where(owner, (pstarts + counts)[None, :], 0), axis=1)
    n_valid = jnp.where(block_idx < n_used, jnp.clip(seg_end - block_idx * blk, 0, blk), 0).astype(jnp.int32)
    packed_base = (jnp.sum(jnp.where(owner, (ustarts - pstarts)[None, :], 0), axis=1) + block_idx * blk)
    packed_base = jnp.where(n_valid > 0, packed_base, 0).astype(jnp.int32)
    return dict(dest=dest.astype(jnp.int32), packed=packed.astype(jnp.int32), zero_block=zero_block,
                block_expert=block_expert, n_used=n_used.reshape(1), n_valid=n_valid, packed_base=packed_base)


def _scatter_kernel(d0_ref, d1_ref, p0_ref, p1_ref, zb_ref, src_ref, o_ref, inv_ref, zero_buf, sem, zero_sem):
    rows = src_ref.shape[0] // SUBLANES
    base = pl.program_id(0) * rows
    fill_rows = zero_buf.shape[0]
    n_blocks = o_ref.shape[0] // fill_rows

    @pl.when(pl.program_id(0) == 0)
    def _():
        zero_buf[...] = jnp.zeros_like(zero_buf)

        def for_each_fill(fn):
            def body(j, carry):
                @pl.when(zb_ref[j] == 1)
                def _():
                    fn(pltpu.make_async_copy(
                        zero_buf, o_ref.at[pl.ds(pl.multiple_of(j * fill_rows, fill_rows), fill_rows)], zero_sem))
                return carry
            lax.fori_loop(0, n_blocks, body, 0)

        for_each_fill(lambda copy: copy.start())
        for_each_fill(lambda copy: copy.wait())

    def row_copy(r, slot):
        return pltpu.make_async_copy(src_ref.at[pl.ds(pl.multiple_of(r * SUBLANES, SUBLANES), SUBLANES)],
                                     o_ref.at[pl.ds(pl.multiple_of(slot * SUBLANES, SUBLANES), SUBLANES)], sem)

    def issue(r, carry):
        t = base + r
        s0 = d0_ref[t]
        s1 = d1_ref[t]
        row_copy(r, s0).start()
        row_copy(r, s1).start()
        inv_ref[p0_ref[t]] = t * TOP_K
        inv_ref[p1_ref[t]] = t * TOP_K + 1
        return carry

    lax.fori_loop(0, rows, issue, 0, unroll=8)
    for _ in range(TOP_K):
        pltpu.make_async_copy(src_ref, o_ref.at[pl.ds(0, rows * SUBLANES)], sem).wait()


def _scatter_rows(plan, src):
    dest, packed = plan["dest"], plan["packed"]
    T = dest.shape[0]
    P = T * TOP_K + N_EXPERTS * MOE_BLK
    rows = GATHER_ROWS
    return pl.pallas_call(
        _scatter_kernel,
        grid_spec=pltpu.PrefetchScalarGridSpec(
            num_scalar_prefetch=5,
            grid=(T // rows,),
            in_specs=[pl.BlockSpec((rows * SUBLANES, LANES), lambda i, *refs: (i, 0))],
            out_specs=[pl.BlockSpec(memory_space=pl.ANY), pl.BlockSpec(memory_space=pltpu.SMEM)],
            scratch_shapes=[
                pltpu.VMEM((MOE_BLK * SUBLANES, LANES), src.dtype),
                pltpu.SemaphoreType.DMA(()),
                pltpu.SemaphoreType.DMA(()),
            ],
        ),
        out_shape=[jax.ShapeDtypeStruct((P * SUBLANES, LANES), src.dtype),
                   jax.ShapeDtypeStruct((T * TOP_K,), jnp.int32)],
        compiler_params=_params(("arbitrary",)),
    )(dest[:, 0], dest[:, 1], packed[:, 0], packed[:, 1], plan["zero_block"], src)


def _expert_kernel(layer, be_ref, first_ref, next_ref, slot_ref, nu_ref, nv_ref, pb_ref, inv_ref,
                   xs_ref, w1_hbm, w3_hbm, w2_hbm, yt_ref,
                   w1f, w3f, w2f, w1b, w3b, w2b, ybuf, sems, ysems):
    i = pl.program_id(0)
    D = w1b.shape[0]
    blk = xs_ref.shape[0] // SUBLANES

    def drain_rows(j):
        n = nv_ref[j]
        b = j % 2
        for bit in range(blk.bit_length()):
            size = (1 << bit) * SUBLANES

            @pl.when((n >> bit) & 1 == 1)
            def _():
                pltpu.make_async_copy(ybuf.at[b, pl.ds(0, size)], yt_ref.at[pl.ds(0, size)], ysems.at[b]).wait()

    def weight_copies(e, slot):
        return (pltpu.make_async_copy(w1_hbm.at[layer, e], w1f.at[slot], sems.at[slot, 0]),
                pltpu.make_async_copy(w3_hbm.at[layer, e], w3f.at[slot], sems.at[slot, 1]),
                pltpu.make_async_copy(w2_hbm.at[layer, e], w2f.at[slot], sems.at[slot, 2]))

    @pl.when(i == 0)
    def _():
        for c in weight_copies(be_ref[0], 0):
            c.start()

    @pl.when(first_ref[i] == 1)
    def _():
        slot = slot_ref[i]
        for c in weight_copies(be_ref[i], slot):
            c.wait()

        @pl.when(next_ref[i] >= 0)
        def _():
            for c in weight_copies(next_ref[i], 1 - slot):
                c.start()

        w1b[...] = w1f[slot].astype(BF16)
        w3b[...] = w3f[slot].astype(BF16)
        w2b[...] = w2f[slot].astype(BF16)

    @pl.when(i >= 2)
    def _():
        drain_rows(i - 2)

    @pl.when(i < nu_ref[0])
    def _():
        b = i % 2
        n = nv_ref[i]

        def swiglu(n_rows):
            tiles = pl.ds(0, n_rows * SUBLANES)
            lo, hi = _unpack_bf16_pair(_load_row_tiles(xs_ref.at[tiles]))
            lo = lo.astype(BF16)
            hi = hi.astype(BF16)
            a = _dot(lo, w1b[0:D // 2, :]) + _dot(hi, w1b[D // 2:, :])
            g = _dot(lo, w3b[0:D // 2, :]) + _dot(hi, w3b[D // 2:, :])
            hid = ((a * _sigmoid(a)) * g).astype(BF16)
            _store_row_tiles(ybuf.at[b, tiles],
                             _pack_bf16_pair(_dot(hid, w2b[:, 0:D // 2]), _dot(hid, w2b[:, D // 2:])))

        @pl.when(n > blk // 2)
        def _():
            swiglu(blk)

        @pl.when(n <= blk // 2)
        def _():
            swiglu(blk // 2)

        first_pos = pb_ref[i]
        unroll = 8

        def issue_row(r):
            a_idx = inv_ref[first_pos + r]
            pltpu.make_async_copy(ybuf.at[b, pl.ds(pl.multiple_of(r * SUBLANES, SUBLANES), SUBLANES)],
                                  yt_ref.at[pl.ds(pl.multiple_of(a_idx * SUBLANES, SUBLANES), SUBLANES)],
                                  ysems.at[b]).start()

        def issue_group(g, carry):
            for u in range(unroll):
                issue_row(g * unroll + u)
            return carry

        def issue_one(r, carry):
            issue_row(r)
            return carry

        lax.fori_loop(0, n // unroll, issue_group, 0)
        lax.fori_loop((n // unroll) * unroll, n, issue_one, 0)

    @pl.when(i == pl.num_programs(0) - 1)
    def _():
        drain_rows(i - 1)
        drain_rows(i)


def _segment_plan(block_expert, n_used):
    n = block_expert.shape[0]
    idx = jnp.arange(n, dtype=jnp.int32)
    prev = jnp.concatenate([jnp.full((1,), -1, jnp.int32), block_expert[:-1]])
    first = ((block_expert != prev) & (idx < n_used[0])).astype(jnp.int32)
    slot = (jnp.cumsum(first) - 1) % 2
    later_first = jnp.where(first == 1, idx, n)
    next_idx = lax.cummin(jnp.concatenate([later_first[1:], jnp.full((1,), n, jnp.int32)]), reverse=True)
    next_expert = jnp.where(next_idx < n, block_expert[jnp.minimum(next_idx, n - 1)], -1)
    return first, next_expert.astype(jnp.int32), slot.astype(jnp.int32)


def _experts(plan, inv, xs, w1, w3, w2, layer):
    _, _, D, DE = w1.shape
    blk = MOE_BLK
    block_expert, n_used = plan["block_expert"], plan["n_used"]
    n_blocks = block_expert.shape[0]
    n_assign = plan["dest"].shape[0] * TOP_K
    first, next_expert, slot = _segment_plan(block_expert, n_used)
    rowmap = lambda i, *refs: (jnp.minimum(i, jnp.maximum(refs[4][0] - 1, 0)), 0)
    hbm = pl.BlockSpec(memory_space=pl.ANY)
    return pl.pallas_call(
        functools.partial(_expert_kernel, layer),
        grid_spec=pltpu.PrefetchScalarGridSpec(
            num_scalar_prefetch=8,
            grid=(n_blocks,),
            in_specs=[pl.BlockSpec((blk * SUBLANES, LANES), rowmap), hbm, hbm, hbm],
            out_specs=hbm,
            scratch_shapes=[
                pltpu.VMEM((2, D, DE), F32),
                pltpu.VMEM((2, D, DE), F32),
                pltpu.VMEM((2, DE, D), F32),
                pltpu.VMEM((D, DE), BF16),
                pltpu.VMEM((D, DE), BF16),
                pltpu.VMEM((DE, D), BF16),
                pltpu.VMEM((2, blk * SUBLANES, LANES), U32),
                pltpu.SemaphoreType.DMA((2, 3)),
                pltpu.SemaphoreType.DMA((2,)),
            ],
        ),
        out_shape=jax.ShapeDtypeStruct((n_assign * SUBLANES, LANES), U32),
        compiler_params=_params(("arbitrary",)),
    )(block_expert, first, next_expert, slot, n_used, plan["n_valid"], plan["packed_base"], inv, xs, w1, w3, w2)


def _final_combine_kernel(yt_ref, h_ref, g2_ref, ew_ref, fg_ref, o_ref):
    hn = h_ref[...] + g2_ref[0] * _moe_mix(yt_ref, ew_ref)
    o_ref[...] = hn * lax.rsqrt(jnp.mean(hn * hn, axis=-1, keepdims=True) + EPS) * fg_ref[...]


def _final_combine(yt, h, mod3, ew, final_g, S):
    T, D = h.shape
    tc = 256
    per_b = S // tc
    return pl.pallas_call(
        _final_combine_kernel,
        grid=(T // tc,),
        in_specs=[
            pl.BlockSpec((tc * TOP_K * SUBLANES, LANES), lambda i: (i, 0)),
            pl.BlockSpec((tc, D), lambda i: (i, 0)),
            pl.BlockSpec((1, 1, D), lambda i: (i // per_b, 0, 5)),
            pl.BlockSpec((tc, LANES), lambda i: (i, 0)),
            pl.BlockSpec((1, D), lambda i: (0, 0)),
        ],
        out_specs=pl.BlockSpec((tc, D), lambda i: (i, 0)),
        out_shape=jax.ShapeDtypeStruct((T, D), F32),
        compiler_params=_params(("parallel",)),
    )(yt, h, mod3, ew, final_g.reshape(1, D))


def kernel(x, c, ada_w, ada_b, norm1_g, w_in, gla_w_a2, gla_b_a, gla_norm_g, lru_conv_w, lru_conv_b,
           lru_wa, lru_ba, lru_wx, lru_bx, lru_lambda, diff_lq1, diff_lk1, diff_lq2, diff_lk2,
           diff_subln_g, rel_bias, w_out, norm2_g, router_g_w, router_g_b, router_e_w, router_e_b,
           moe_w1, moe_w3, moe_w2, final_g):
    B, S, D = x.shape
    T = B * S
    L = ada_w.shape[0]
    h = x.reshape(T, D)
    mod = _ada_mod(c, ada_w, ada_b)
    bias = _bias_tiles(rel_bias)
    w_in_perm = _permute_w_in(w_in)
    w_out_bf16 = w_out.astype(BF16)
    pending_moe = None
    for l in range(L):
        mod3 = mod[l][:, None, :]
        if pending_moe is None:
            proj = _inproj(h, mod3, norm1_g[l], w_in_perm, l, S)
        else:
            proj, h = _inproj(h, mod3, norm1_g[l], w_in_perm, l, S, pending_moe)
        wa2_pad = jnp.concatenate(
            [gla_w_a2[l], jnp.zeros((LANES - GLA_LOWRANK, GLA_KEY_WIDTH), F32)], axis=0).astype(BF16)
        o_gla = _gla(proj, wa2_pad, gla_b_a[l], gla_norm_g[l], B, S)
        w_gates = jnp.concatenate([_block_diag(lru_wa[l]), _block_diag(lru_wx[l])], axis=1).astype(BF16)
        b_gates = jnp.concatenate([lru_ba[l], lru_bx[l]]).reshape(1, 2 * LRU_WIDTH)
        o_lru = _lru(proj.reshape(B, S, PROJ_WIDTH), lru_conv_w[l], lru_conv_b[l], w_gates, b_gates,
                     lru_lambda[l]).reshape(T, LRU_WIDTH)
        lqk = jnp.stack([diff_lq1[l], diff_lk1[l], diff_lq2[l], diff_lk2[l]], axis=0)
        o_diff = _diff_attention(proj, bias, lqk, diff_subln_g[l], l, B, S)
        rw = jnp.concatenate(
            [router_g_w[l], router_e_w[l], jnp.zeros((D, LANES - N_GROUPS - N_EXPERTS), F32)], axis=1).astype(BF16)
        rb = jnp.concatenate(
            [router_g_b[l], router_e_b[l], jnp.zeros((LANES - N_GROUPS - N_EXPERTS,), F32)]).reshape(1, LANES)
        h, u2, info, ew, cnt = _outproj(h, o_gla, o_lru, o_diff, w_out_bf16, l, mod3, norm2_g[l], rw, rb, S)
        counts = cnt[0, N_GROUPS:N_GROUPS + N_EXPERTS].astype(jnp.int32)
        plan = _dispatch(info, counts)
        xs, inv = _scatter_rows(plan, u2)
        yt = _experts(plan, inv, xs, moe_w1, moe_w3, moe_w2, l)
        pending_moe = (yt, ew, mod3)
    out = _final_combine(yt, h, mod3, ew, final_g, S)
    return out.reshape(B, S, D)
```

```python
import functools
import math

import jax
import jax.numpy as jnp
from jax import lax
from jax.experimental import pallas as pl
from jax.experimental.pallas import tpu as pltpu

F32 = jnp.float32
BF16 = jnp.bfloat16
U32 = jnp.uint32

EPS = 1e-6
LOG2E = math.log2(math.e)
CHUNK = 64

GLA_DV = 128
GLA_DK = 64
GLA_HEADS = 6
GLA_WIDTH = GLA_HEADS * GLA_DV
GLA_KEY_WIDTH = GLA_HEADS * GLA_DK
GLA_LOWRANK = 16
GLA_TAU = 16.0

LRU_WIDTH = 512
LRU_BLOCKS = 8
LRU_BLOCK_DIM = LRU_WIDTH // LRU_BLOCKS
CONV_WIDTH = 4
LRU_C = 8.0

DIFF_DH = 64
DIFF_DV = 128
DIFF_HEADS = 6
DIFF_WIDTH = DIFF_HEADS * DIFF_DV

REL_BUCKETS = 32
REL_MAX_DIST = 128

N_GROUPS = 8
EXPERTS_PER_GROUP = 8
N_EXPERTS = 64
TOP_K = 2

LANES = 128
SUBLANES = 8
VMEM_LIMIT = 56 * 1024 * 1024

COL_GV = 0
COL_GOG = 768
COL_DQ = 1536
COL_DK = 2304
COL_DV = 3072
COL_GQ = 3840
COL_GK = 4224
COL_LY = 4608
COL_LX = 5120
COL_GA = 5632
PROJ_WIDTH = 5760

ATT_TILE = 512
GLA_TILE = 256
LRU_TILE = 256
MOE_BLK = 256
GATHER_ROWS = 512


def _params(sem, vmem=VMEM_LIMIT):
    return pltpu.CompilerParams(dimension_semantics=sem, vmem_limit_bytes=vmem)


def _sigmoid(x):
    return 0.5 * jnp.tanh(0.5 * x) + 0.5


def _softplus(x):
    return jnp.maximum(x, 0.0) + jnp.log1p(jnp.exp(-jnp.abs(x)))


def _dot(a, b):
    return jnp.dot(a, b, preferred_element_type=F32)


def _dot_nt(a, b):
    return lax.dot_general(a, b, (((1,), (1,)), ((), ())), preferred_element_type=F32)


def _dot_tn(a, b):
    return lax.dot_general(a, b, (((0,), (0,)), ((), ())), preferred_element_type=F32)


def _pack_bf16_pair(lo, hi):
    lo_bits = lax.bitcast_convert_type(lo.astype(BF16).astype(F32), U32)
    hi_bits = lax.bitcast_convert_type(hi.astype(BF16).astype(F32), U32)
    return (hi_bits & jnp.uint32(0xFFFF0000)) | (lo_bits >> 16)


def _unpack_bf16_pair(w):
    lo = lax.bitcast_convert_type(w << 16, F32)
    hi = lax.bitcast_convert_type(w & jnp.uint32(0xFFFF0000), F32)
    return lo, hi


def _store_row_tiles(ref, words):
    rows = words.shape[0]
    for s in range(SUBLANES):
        ref[pl.ds(s, rows, stride=SUBLANES), :] = words[:, s * LANES:(s + 1) * LANES]


def _load_row_tiles(ref):
    rows = ref.shape[0] // SUBLANES
    return jnp.concatenate([ref[pl.ds(s, rows, stride=SUBLANES), :] for s in range(SUBLANES)], axis=1)


def _ada_kernel(c_ref, w_ref, b_ref, o_ref):
    c = c_ref[...]
    s = c * _sigmoid(c)
    o_ref[0] = _dot(s.astype(BF16), w_ref[0].astype(BF16)) + b_ref[0]


def _ada_mod(c, ada_w, ada_b):
    L, D, N = ada_w.shape
    B = c.shape[0]
    tn = 1024
    return pl.pallas_call(
        _ada_kernel,
        grid=(L, N // tn),
        in_specs=[
            pl.BlockSpec((B, D), lambda l, j: (0, 0)),
            pl.BlockSpec((1, D, tn), lambda l, j: (l, 0, j)),
            pl.BlockSpec((1, 1, tn), lambda l, j: (l, 0, j)),
        ],
        out_specs=pl.BlockSpec((1, B, tn), lambda l, j: (l, 0, j)),
        out_shape=jax.ShapeDtypeStruct((L, B, N), F32),
        compiler_params=_params(("parallel", "parallel")),
    )(c, ada_w, ada_b.reshape(L, 1, N))


def _modulated_norm(x, g, sc, sh):
    ms = jnp.mean(x * x, axis=-1, keepdims=True)
    return (x * lax.rsqrt(ms + EPS) * g) * (1.0 + sc) + sh


def _moe_mix(yt_ref, ew_ref):
    rows = ew_ref.shape[0]

    def expert_rows(k):
        return jnp.concatenate([yt_ref[pl.ds(k * SUBLANES + s, rows, stride=TOP_K * SUBLANES), :]
                                for s in range(SUBLANES)], axis=1)

    ew = ew_ref[...]
    w0 = ew[:, 0:1]
    w1 = ew[:, 1:2]
    lo0, hi0 = _unpack_bf16_pair(expert_rows(0))
    lo1, hi1 = _unpack_bf16_pair(expert_rows(1))
    return jnp.concatenate([w0 * lo0 + w1 * lo1, w0 * hi0 + w1 * hi1], axis=1)


def _inproj_kernel(h_ref, sh_ref, sc_ref, g_ref, w_ref, o_ref, u_scr):
    tm = h_ref.shape[0]
    nr = tm // 2

    @pl.when(pl.program_id(1) == 0)
    def _():
        for half in range(2):
            rows = slice(half * nr, (half + 1) * nr)
            u = _modulated_norm(h_ref[rows, :], g_ref[...], sc_ref[0], sh_ref[0]).astype(BF16)
            u_scr[rows, :] = u
            o_ref[rows, :] = _dot(u, w_ref[...]).astype(o_ref.dtype)

    @pl.when(pl.program_id(1) > 0)
    def _():
        o_ref[...] = _dot(u_scr[...], w_ref[...]).astype(o_ref.dtype)


def _inproj_after_moe_kernel(h_ref, yt_ref, ew_ref, g2_ref, sh_ref, sc_ref, g_ref, w_ref, o_ref, hn_ref, u_scr):
    tm = h_ref.shape[0]
    nr = tm // 2

    @pl.when(pl.program_id(1) == 0)
    def _():
        for half in range(2):
            rows = slice(half * nr, (half + 1) * nr)
            tiles = pl.ds(half * nr * TOP_K * SUBLANES, nr * TOP_K * SUBLANES)
            hn = h_ref[rows, :] + g2_ref[0] * _moe_mix(yt_ref.at[tiles], ew_ref.at[rows])
            hn_ref[rows, :] = hn
            u = _modulated_norm(hn, g_ref[...], sc_ref[0], sh_ref[0]).astype(BF16)
            u_scr[rows, :] = u
            o_ref[rows, :] = _dot(u, w_ref[...]).astype(o_ref.dtype)

    @pl.when(pl.program_id(1) > 0)
    def _():
        o_ref[...] = _dot(u_scr[...], w_ref[...]).astype(o_ref.dtype)


def _inproj(h, mod3, norm_g, w_perm, layer, S, pending_moe=None):
    T, D = h.shape
    N = w_perm.shape[2]
    tm, tn = 512, 1152
    per_b = S // tm
    rows = pl.BlockSpec((tm, D), lambda i, j: (i, 0))
    mod_specs = [
        pl.BlockSpec((1, 1, D), lambda i, j: (i // per_b, 0, 0)),
        pl.BlockSpec((1, 1, D), lambda i, j: (i // per_b, 0, 1)),
        pl.BlockSpec((1, D), lambda i, j: (0, 0)),
        pl.BlockSpec((None, D, tn), lambda i, j: (layer, 0, j)),
    ]
    proj_spec = pl.BlockSpec((tm, tn), lambda i, j: (i, j))
    proj_shape = jax.ShapeDtypeStruct((T, N), BF16)
    common = dict(grid=(T // tm, N // tn), scratch_shapes=[pltpu.VMEM((tm, D), BF16)],
                  compiler_params=_params(("parallel", "arbitrary")))
    if pending_moe is None:
        return pl.pallas_call(_inproj_kernel, in_specs=[rows] + mod_specs, out_specs=proj_spec,
                              out_shape=proj_shape, **common)(h, mod3, mod3, norm_g.reshape(1, D), w_perm)
    yt, ew, prev_mod3 = pending_moe
    moe_specs = [
        pl.BlockSpec((tm * TOP_K * SUBLANES, LANES), lambda i, j: (i, 0)),
        pl.BlockSpec((tm, LANES), lambda i, j: (i, 0)),
        pl.BlockSpec((1, 1, D), lambda i, j: (i // per_b, 0, 5)),
    ]
    return pl.pallas_call(
        _inproj_after_moe_kernel, in_specs=[rows] + moe_specs + mod_specs, out_specs=[proj_spec, rows],
        out_shape=[proj_shape, jax.ShapeDtypeStruct((T, D), F32)], **common,
    )(h, yt, ew, prev_mod3, mod3, mod3, norm_g.reshape(1, D), w_perm)


W_IN_SEGMENTS = ((COL_GQ, 0, 384), (COL_GK, 384, 384), (COL_GV, 768, 768), (COL_GOG, 1536, 768),
                 (COL_GA, 2304, GLA_LOWRANK), (COL_LY, 2320, 512), (COL_LX, 2832, 512),
                 (COL_DQ, 3344, 768), (COL_DK, 4112, 768), (COL_DV, 4880, 768))


def _relayout_kernel(w_ref, o_ref):
    x = w_ref[0]
    for dst, src, width in W_IN_SEGMENTS:
        o_ref[0, :, dst:dst + width] = x[:, src:src + width]
    pad = slice(COL_GA + GLA_LOWRANK, COL_GA + LANES)
    o_ref[0, :, pad] = jnp.zeros((x.shape[0], LANES - GLA_LOWRANK), BF16)


def _permute_w_in(w):
    L, D, N = w.shape
    rt = 256
    return pl.pallas_call(
        _relayout_kernel,
        grid=(L, D // rt),
        in_specs=[pl.BlockSpec((1, rt, N), lambda l, i: (l, i, 0))],
        out_specs=pl.BlockSpec((1, rt, PROJ_WIDTH), lambda l, i: (l, i, 0)),
        out_shape=jax.ShapeDtypeStruct((L, D, PROJ_WIDTH), BF16),
        compiler_params=_params(("parallel", "parallel")),
    )(w.astype(BF16))


def _gla_kernel(q_ref, k_ref, v_ref, og_ref, alr_ref, wa2_ref, ba_ref, ng_ref, o_ref, st_ref):
    tb = q_ref.shape[0]
    n_chunks = tb // CHUNK

    @pl.when(pl.program_id(1) == 0)
    def _():
        st_ref[...] = jnp.zeros_like(st_ref)

    row = lax.broadcasted_iota(jnp.int32, (tb, tb), 0)
    col = lax.broadcasted_iota(jnp.int32, (tb, tb), 1)
    same_chunk = (row // CHUNK) == (col // CHUNK)
    causal = col <= row
    tril = jnp.where(same_chunk & causal, 1.0, 0.0).astype(BF16)
    lane = lax.broadcasted_iota(jnp.int32, (1, LANES), 1)
    half_masks = (lane < GLA_DK, lane >= GLA_DK)

    alr = alr_ref[...]
    cols = [slice(p * LANES, (p + 1) * LANES) for p in range(GLA_HEADS // 2)]
    z = [_dot(alr, wa2_ref[:, cs]) + ba_ref[:, cs] for cs in cols]
    la = [(jnp.minimum(zp, 0.0) - jnp.log1p(jnp.exp(-jnp.abs(zp)))) * (1.0 / GLA_TAU) for zp in z]
    la_hi = [x.astype(BF16) for x in la]
    la_lo = [(x - h.astype(F32)).astype(BF16) for x, h in zip(la, la_hi)]
    G = [_dot(tril, h) + _dot(tril, lo) for h, lo in zip(la_hi, la_lo)]
    Gl = [jnp.concatenate([jnp.broadcast_to(g[(c + 1) * CHUNK - 1:(c + 1) * CHUNK, :], (CHUNK, LANES))
                           for c in range(n_chunks)], axis=0) for g in G]
    eG = [jnp.exp(g) for g in G]
    enG = [jnp.exp(-g) for g in G]
    q = [q_ref[:, cs].astype(F32) * (GLA_DK ** -0.5) for cs in cols]
    k = [k_ref[:, cs].astype(F32) for cs in cols]
    kf = [(kp * e).astype(BF16) for kp, e in zip(k, eG)]
    kb = [(kp * e).astype(BF16) for kp, e in zip(k, enG)]
    kd = [kp * jnp.exp(gl - g) for kp, gl, g in zip(k, Gl, G)]

    heads = range(GLA_HEADS)
    pair = [h // 2 for h in heads]
    mask = [half_masks[h % 2] for h in heads]
    vcols = [slice(h * GLA_DV, (h + 1) * GLA_DV) for h in heads]
    qf_h = [jnp.where(mask[h], q[pair[h]] * eG[pair[h]], 0.0).astype(BF16) for h in heads]
    qb_h = [jnp.where(mask[h], q[pair[h]] * enG[pair[h]], 0.0).astype(BF16) for h in heads]
    kd_h = [jnp.where(mask[h], kd[pair[h]], 0.0).astype(BF16) for h in heads]
    v_h = [v_ref[:, vcols[h]] for h in heads]
    a_f = [_dot_nt(qf_h[h], kb[pair[h]]) for h in heads]
    a_b = [_dot_nt(qb_h[h], kf[pair[h]]) for h in heads]
    attn = [jnp.where(same_chunk, jnp.where(causal, a_f[h], a_b[h]), 0.0).astype(BF16) for h in heads]
    o_intra = [_dot(attn[h], v_h[h]) for h in heads]
    chunk_rows = [slice(c * CHUNK, (c + 1) * CHUNK) for c in range(n_chunks)]
    kv = [[_dot_tn(v_h[h][rs], kd_h[h][rs]) for rs in chunk_rows] for h in heads]
    for h in heads:
        st = st_ref[h]
        inter = []
        for c, rs in enumerate(chunk_rows):
            inter.append(_dot_nt(qf_h[h][rs], st.astype(BF16)))
            decay = jnp.exp(Gl[pair[h]][c * CHUNK:c * CHUNK + 1, :])
            st = st * decay + kv[h][c]
        st_ref[h] = st
        o = o_intra[h] + jnp.concatenate(inter, axis=0)
        o = o * lax.rsqrt(jnp.mean(o * o, axis=-1, keepdims=True) + EPS)
        og = og_ref[:, vcols[h]].astype(F32)
        o_ref[:, vcols[h]] = (o * ng_ref[:, vcols[h]] * (og * _sigmoid(og))).astype(o_ref.dtype)


def _gla(proj, wa2_pad, b_a, norm_g, B, S):
    T = proj.shape[0]
    tb = GLA_TILE
    nt = S // tb
    row = lambda b, i: b * nt + i
    return pl.pallas_call(
        _gla_kernel,
        grid=(B, nt),
        in_specs=[
            pl.BlockSpec((tb, GLA_KEY_WIDTH), lambda b, i: (row(b, i), COL_GQ // GLA_KEY_WIDTH)),
            pl.BlockSpec((tb, GLA_KEY_WIDTH), lambda b, i: (row(b, i), COL_GK // GLA_KEY_WIDTH)),
            pl.BlockSpec((tb, GLA_WIDTH), lambda b, i: (row(b, i), COL_GV // GLA_WIDTH)),
            pl.BlockSpec((tb, GLA_WIDTH), lambda b, i: (row(b, i), COL_GOG // GLA_WIDTH)),
            pl.BlockSpec((tb, LANES), lambda b, i: (row(b, i), COL_GA // LANES)),
            pl.BlockSpec((LANES, GLA_KEY_WIDTH), lambda b, i: (0, 0)),
            pl.BlockSpec((1, GLA_KEY_WIDTH), lambda b, i: (0, 0)),
            pl.BlockSpec((1, GLA_WIDTH), lambda b, i: (0, 0)),
        ],
        out_specs=pl.BlockSpec((tb, GLA_WIDTH), lambda b, i: (row(b, i), 0)),
        out_shape=jax.ShapeDtypeStruct((T, GLA_WIDTH), BF16),
        scratch_shapes=[pltpu.VMEM((GLA_HEADS, GLA_DV, LANES), F32)],
        compiler_params=_params(("parallel", "arbitrary")),
    )(proj, proj, proj, proj, proj, wa2_pad, b_a.reshape(1, -1), norm_g.reshape(1, -1))


def _lru_kernel(y_ref, x_ref, cw_ref, cb_ref, wg_ref, bg_ref, lam_ref, o_ref, *scratch):
    B, ts, W = x_ref.shape
    n_planes = W // LANES
    a_scr = scratch[0:n_planes]
    b_scr = scratch[n_planes:2 * n_planes]
    h_scr = scratch[2 * n_planes:3 * n_planes]
    xc_scr, tail_scr, carry_scr = scratch[3 * n_planes:]

    @pl.when(pl.program_id(0) == 0)
    def _():
        tail_scr[...] = jnp.zeros_like(tail_scr)
        carry_scr[...] = jnp.zeros_like(carry_scr)

    cw = cw_ref[...]
    cb = cb_ref[...]
    sp = _softplus(-lam_ref[...])
    row8 = lax.broadcasted_iota(jnp.int32, (8, W), 0)
    for b in range(B):
        x = x_ref[b].astype(F32)
        tail = tail_scr[b]
        xc = cb + cw[CONV_WIDTH - 1:CONV_WIDTH, :] * x
        head = cb + cw[CONV_WIDTH - 1:CONV_WIDTH, :] * x[0:8]
        for d in range(1, CONV_WIDTH):
            wd = cw[CONV_WIDTH - 1 - d:CONV_WIDTH - d, :]
            xr = pltpu.roll(x, d, 0)
            xc = xc + wd * xr
            head = head + wd * jnp.where(row8 < d, pltpu.roll(tail, d, 0), xr[0:8])
        tail_scr[b] = x[ts - 8:ts]
        xc_scr[...] = xc
        xc_scr[0:8] = head
        xc = xc_scr[...]
        gates = _sigmoid(_dot(xc.astype(BF16), wg_ref[...]) + bg_ref[...])
        r = gates[:, :W]
        ig = gates[:, W:]
        log_a = (-LRU_C) * r * sp
        a = jnp.exp(log_a)
        b_in = jnp.sqrt(-jnp.tanh(log_a) * (a * a + 1.0)) * (ig * xc)
        rows = slice(b * ts, (b + 1) * ts)
        for k in range(n_planes):
            a_scr[k][rows] = a[:, k * LANES:(k + 1) * LANES]
            b_scr[k][rows] = b_in[:, k * LANES:(k + 1) * LANES]

    def step(t, hs):
        idx = pl.ds(t, B, stride=ts)
        out = []
        for k in range(n_planes):
            hk = a_scr[k][idx, :] * hs[k] + b_scr[k][idx, :]
            h_scr[k][idx, :] = hk
            out.append(hk)
        return tuple(out)

    hs = lax.fori_loop(0, ts, step, tuple(carry_scr[k] for k in range(n_planes)), unroll=8)
    for k in range(n_planes):
        carry_scr[k] = hs[k]

    for b in range(B):
        rows = slice(b * ts, (b + 1) * ts)
        y = y_ref[b].astype(F32)
        gelu = 0.5 * y * (1.0 + jnp.tanh(math.sqrt(2.0 / math.pi) * (y + 0.044715 * (y * y * y))))
        h = jnp.concatenate([h_scr[k][rows] for k in range(n_planes)], axis=1)
        o_ref[b] = (h * gelu).astype(o_ref.dtype)


def _lru(proj3, conv_w, conv_b, w_gates, b_gates, lam):
    B, S, _ = proj3.shape
    W = LRU_WIDTH
    ts = LRU_TILE
    n_planes = W // LANES
    full = lambda shape: pl.BlockSpec(shape, lambda i: (0,) * len(shape))
    return pl.pallas_call(
        _lru_kernel,
        grid=(S // ts,),
        in_specs=[
            pl.BlockSpec((B, ts, W), lambda i: (0, i, COL_LY // W)),
            pl.BlockSpec((B, ts, W), lambda i: (0, i, COL_LX // W)),
            full((CONV_WIDTH, W)),
            full((1, W)),
            full((W, 2 * W)),
            full((1, 2 * W)),
            full((1, W)),
        ],
        out_specs=pl.BlockSpec((B, ts, W), lambda i: (0, i, 0)),
        out_shape=jax.ShapeDtypeStruct((B, S, W), BF16),
        scratch_shapes=(
            [pltpu.VMEM((B * ts, LANES), F32) for _ in range(3 * n_planes)]
            + [pltpu.VMEM((ts, W), F32), pltpu.VMEM((B, 8, W), F32), pltpu.VMEM((n_planes, B, LANES), F32)]),
        compiler_params=_params(("arbitrary",)),
    )(proj3, proj3, conv_w, conv_b.reshape(1, W), w_gates, b_gates, lam.reshape(1, W))


def _block_diag(w):
    n, d, _ = w.shape
    eye = jnp.eye(n, dtype=w.dtype)
    return (eye[:, None, :, None] * w[:, :, None, :]).reshape(n * d, n * d)


def _t5_bucket(rel):
    nb = REL_BUCKETS // 2
    ret = (rel > 0).astype(jnp.int32) * nb
    n = jnp.abs(rel)
    max_exact = nb // 2
    nf = jnp.maximum(n, 1).astype(jnp.float32)
    large = max_exact + (jnp.log(nf / max_exact) / math.log(REL_MAX_DIST / max_exact)
                         * (nb - max_exact)).astype(jnp.int32)
    large = jnp.minimum(large, nb - 1)
    return ret + jnp.where(n < max_exact, n, large)


def _bias_kernel(bucket_ref, table_ref, o_ref):
    h = pl.program_id(0)
    bucket = bucket_ref[0]
    acc = jnp.full(bucket.shape, -1e30, F32)
    for b in range(REL_BUCKETS):
        acc = jnp.where(bucket == b, table_ref[b, h] * LOG2E, acc)
    o_ref[0, 0] = acc


def _bias_tiles(rel_bias):
    t = ATT_TILE
    H = rel_bias.shape[1]
    qp = jnp.arange(t, dtype=jnp.int32)[:, None]
    kp = jnp.arange(t, dtype=jnp.int32)[None, :]
    mask = (kp // CHUNK) <= (qp // CHUNK)
    half = REL_BUCKETS // 2
    per_distance = _t5_bucket(-jnp.arange(2 * t, dtype=jnp.int32))
    edges = jnp.sum((per_distance[None, :] < jnp.arange(1, half, dtype=jnp.int32)[:, None]).astype(jnp.int32), axis=1)

    def bucket_2d(rel):
        passed = jnp.sum((jnp.abs(rel)[None] >= edges[:, None, None]).astype(jnp.int32), axis=0)
        return (rel > 0).astype(jnp.int32) * half + passed

    buckets = jnp.stack([jnp.where(mask, bucket_2d(kp - qp), REL_BUCKETS), bucket_2d(kp - t - qp)], axis=0)
    table = rel_bias.astype(F32)
    tiles = pl.pallas_call(
        _bias_kernel,
        grid=(H, 2),
        in_specs=[
            pl.BlockSpec((1, t, t), lambda h, k: (k, 0, 0)),
            pl.BlockSpec(memory_space=pltpu.SMEM),
        ],
        out_specs=pl.BlockSpec((1, 1, t, t), lambda h, k: (h, k, 0, 0)),
        out_shape=jax.ShapeDtypeStruct((H, 2, t, t), F32),
        compiler_params=_params(("parallel", "parallel")),
    )(buckets, table)
    far_bucket = _t5_bucket(jnp.full((1,), -t - 1, jnp.int32))
    far = jnp.sum(jnp.where(jnp.arange(REL_BUCKETS)[:, None] == far_bucket, table, 0.0), axis=0)
    return tiles, jnp.broadcast_to((far * LOG2E)[:, None, None], (H, 1, t))


def _diff_kernel(lam_init, q_ref, k_ref, v_ref, bias_ref, far_ref, lqk_ref, g_ref, o_ref,
                 qs_scr, m_scr, l_scr, acc_scr):
    i = pl.program_id(2)
    t = q_ref.shape[0]
    hq = t // 2
    lane = lax.broadcasted_iota(jnp.int32, (1, LANES), 1)
    q = q_ref[...].astype(F32) * (LOG2E * DIFF_DH ** -0.5)
    for half in range(2):
        qh = q[half * hq:(half + 1) * hq]
        qs_scr[(2 * half) * hq:(2 * half + 1) * hq] = jnp.where(lane < DIFF_DH, qh, 0.0).astype(BF16)
        qs_scr[(2 * half + 1) * hq:(2 * half + 2) * hq] = jnp.where(lane >= DIFF_DH, qh, 0.0).astype(BF16)
    m_scr[...] = jnp.full_like(m_scr, -1e30)
    l_scr[...] = jnp.zeros_like(l_scr)
    acc_scr[...] = jnp.zeros_like(acc_scr)

    def tile(rows, ks, bias):
        s = _dot_nt(qs_scr[rows, :], k_ref[ks, :]) + bias
        groups = [s[:, c * LANES:(c + 1) * LANES] for c in range(s.shape[1] // LANES)]
        mx = functools.reduce(jnp.maximum, groups)
        m_prev = m_scr[rows, :]
        m_new = jnp.maximum(m_prev, jnp.max(mx, axis=-1, keepdims=True))
        alpha = jnp.exp2(m_prev - m_new)
        ps = [jnp.exp2(g - m_new) for g in groups]
        l_scr[rows, :] = alpha * l_scr[rows, :] + functools.reduce(jnp.add, ps)
        p = jnp.concatenate(ps, axis=1).astype(BF16)
        acc_scr[rows, :] = alpha * acc_scr[rows, :] + _dot(p, v_ref[ks, :])
        m_scr[rows, :] = m_new

    def stacked(b, half):
        bh = b[half * hq:(half + 1) * hq]
        return [bh, bh]

    all_rows = slice(0, 2 * t)

    def far_body(j, carry):
        tile(all_rows, pl.ds(pl.multiple_of(j * t, t), t), far_ref[0])
        return carry

    lax.fori_loop(0, jnp.maximum(i - 1, 0), far_body, 0)

    @pl.when(i >= 1)
    def _():
        b = bias_ref[0, 1]
        tile(all_rows, pl.ds(pl.multiple_of((i - 1) * t, t), t),
             jnp.concatenate(stacked(b, 0) + stacked(b, 1), axis=0))

    b = bias_ref[0, 0]
    diag0 = pl.multiple_of(i * t, t)
    tile(slice(0, t), pl.ds(diag0, hq), jnp.concatenate(stacked(b[:, 0:hq], 0), axis=0))
    tile(slice(t, 2 * t), pl.ds(diag0, t), jnp.concatenate(stacked(b, 1), axis=0))

    lqk = lqk_ref[...]
    lam = (jnp.exp(jnp.sum(lqk[0:1] * lqk[1:2], axis=-1, keepdims=True))
           - jnp.exp(jnp.sum(lqk[2:3] * lqk[3:4], axis=-1, keepdims=True)) + lam_init)
    o = acc_scr[...] / jnp.sum(l_scr[...], axis=-1, keepdims=True)
    o = jnp.concatenate([o[0:hq] - lam * o[hq:t], o[t:t + hq] - lam * o[t + hq:2 * t]], axis=0)
    o = o * lax.rsqrt(jnp.mean(o * o, axis=-1, keepdims=True) + EPS)
    o_ref[...] = (o * g_ref[...] * (1.0 - lam_init)).astype(o_ref.dtype)


def _diff_attention(proj, bias, lqk, subln_g, layer_idx, B, S):
    T = proj.shape[0]
    t = ATT_TILE
    nq = S // t
    tiles, far = bias
    lam_init = 0.8 - 0.6 * math.exp(-0.3 * layer_idx)
    return pl.pallas_call(
        functools.partial(_diff_kernel, lam_init),
        grid=(B, DIFF_HEADS, nq),
        in_specs=[
            pl.BlockSpec((t, LANES), lambda b, h, i: (b * nq + i, COL_DQ // LANES + h)),
            pl.BlockSpec((S, LANES), lambda b, h, i: (b, COL_DK // LANES + h)),
            pl.BlockSpec((S, LANES), lambda b, h, i: (b, COL_DV // LANES + h)),
            pl.BlockSpec((1, 2, t, t), lambda b, h, i: (h, 0, 0, 0)),
            pl.BlockSpec((1, 1, t), lambda b, h, i: (h, 0, 0)),
            pl.BlockSpec((4, DIFF_DH), lambda b, h, i: (0, 0)),
            pl.BlockSpec((1, DIFF_DV), lambda b, h, i: (0, 0)),
        ],
        out_specs=pl.BlockSpec((t, LANES), lambda b, h, i: (b * nq + i, h)),
        out_shape=jax.ShapeDtypeStruct((T, DIFF_WIDTH), BF16),
        scratch_shapes=[
            pltpu.VMEM((2 * t, LANES), BF16),
            pltpu.VMEM((2 * t, LANES), F32),
            pltpu.VMEM((2 * t, LANES), F32),
            pltpu.VMEM((2 * t, DIFF_DV), F32),
        ],
        compiler_params=_params(("parallel", "parallel", "arbitrary")),
    )(proj, proj, proj, tiles, far, lqk, subln_g.reshape(1, DIFF_DV))


def _outproj_kernel(h_ref, og_ref, ol_ref, od_ref, w_ref, g1_ref, sh2_ref, sc2_ref, n2_ref, rw_ref, rb_ref,
                    hn_ref, u2_ref, eid_ref, ew_ref, cnt_ref):
    tm = h_ref.shape[0]

    @pl.when(pl.program_id(0) == 0)
    def _():
        cnt_ref[...] = jnp.zeros_like(cnt_ref)

    nr = tm // 2
    halves = [slice(half * nr, (half + 1) * nr) for half in range(2)]
    accs = []
    for rows in halves:
        acc = _dot(og_ref[rows, :], w_ref[0:GLA_WIDTH, :])
        acc += _dot(ol_ref[rows, :], w_ref[GLA_WIDTH:GLA_WIDTH + LRU_WIDTH, :])
        acc += _dot(od_ref[rows, :], w_ref[GLA_WIDTH + LRU_WIDTH:, :])
        accs.append(acc)
    for rows, acc in zip(halves, accs):
        _outproj_rows(rows, acc, h_ref, g1_ref, sh2_ref, sc2_ref, n2_ref, rw_ref, rb_ref,
                      hn_ref, u2_ref, eid_ref, ew_ref, cnt_ref)


def _outproj_rows(rows, acc, h_ref, g1_ref, sh2_ref, sc2_ref, n2_ref, rw_ref, rb_ref,
                  hn_ref, u2_ref, eid_ref, ew_ref, cnt_ref):
    D = h_ref.shape[1]
    nr = rows.stop - rows.start
    hn = h_ref[rows, :] + g1_ref[0] * acc
    hn_ref[rows, :] = hn
    u2 = _modulated_norm(hn, n2_ref[...], sc2_ref[0], sh2_ref[0])
    _store_row_tiles(u2_ref.at[pl.ds(rows.start * SUBLANES, nr * SUBLANES)],
                     _pack_bf16_pair(u2[:, :D // 2], u2[:, D // 2:]))

    logits = _dot(u2.astype(BF16), rw_ref[...]) + rb_ref[...]
    lane = lax.broadcasted_iota(jnp.int32, logits.shape, 1)
    lane_f = lane.astype(F32)
    neg = jnp.float32(-jnp.inf)
    gmask = lane < N_GROUPS
    gl = jnp.where(gmask, logits, neg)
    gmax = jnp.max(gl, axis=-1, keepdims=True)
    gidx = jnp.min(jnp.where(gl == gmax, lane_f, float(LANES)), axis=-1, keepdims=True)
    g_w = 1.0 / jnp.sum(jnp.where(gmask, jnp.exp(gl - gmax), 0.0), axis=-1, keepdims=True)
    egroup = ((lane - N_GROUPS) >> 3).astype(F32)
    emask = (lane >= N_GROUPS) & (lane < N_GROUPS + N_EXPERTS) & (egroup == gidx)
    el = jnp.where(emask, logits, neg)
    v1 = jnp.max(el, axis=-1, keepdims=True)
    i1 = jnp.min(jnp.where(el == v1, lane_f, float(LANES)), axis=-1, keepdims=True)
    el2 = jnp.where(lane_f == i1, neg, el)
    v2 = jnp.max(el2, axis=-1, keepdims=True)
    i2 = jnp.min(jnp.where(el2 == v2, lane_f, float(LANES)), axis=-1, keepdims=True)
    e21 = jnp.exp(v2 - v1)
    w1 = g_w / (1.0 + e21)
    w2 = g_w * e21 / (1.0 + e21)
    ew_ref[rows, :] = jnp.where(lane == 0, w1, jnp.where(lane == 1, w2, 0.0))

    oh1 = lane_f == i1
    oh2 = lane_f == i2
    both = jnp.where(oh1 | oh2, 1.0, 0.0).astype(BF16)
    row = lax.broadcasted_iota(jnp.int32, (nr, nr), 0)
    col = lax.broadcasted_iota(jnp.int32, (nr, nr), 1)
    earlier = _dot(jnp.where(col < row, 1.0, 0.0).astype(BF16), both) + cnt_ref[0:1, :]
    rank1 = jnp.sum(jnp.where(oh1, earlier, 0.0), axis=-1, keepdims=True)
    rank2 = jnp.sum(jnp.where(oh2, earlier, 0.0), axis=-1, keepdims=True)
    cnt_ref[0:1, :] = cnt_ref[0:1, :] + jnp.sum(both.astype(F32), axis=0, keepdims=True)
    info = jnp.where(lane == 0, i1 - float(N_GROUPS),
                     jnp.where(lane == 1, i2 - float(N_GROUPS),
                               jnp.where(lane == 2, rank1, jnp.where(lane == 3, rank2, 0.0))))
    eid_ref[rows, :] = info.astype(jnp.int32)


def _outproj(h, o_gla, o_lru, o_diff, w_out, layer, mod3, norm2_g, rw, rb, S):
    T, D = h.shape
    tm = 512
    per_b = S // tm
    rowblk = lambda width: pl.BlockSpec((tm, width), lambda i: (i, 0))
    modblk = lambda k: pl.BlockSpec((1, 1, D), lambda i: (i // per_b, 0, k))
    full = lambda shape: pl.BlockSpec(shape, lambda i: (0,) * len(shape))
    return pl.pallas_call(
        _outproj_kernel,
        grid=(T // tm,),
        in_specs=[
            rowblk(D), rowblk(GLA_WIDTH), rowblk(LRU_WIDTH), rowblk(DIFF_WIDTH),
            pl.BlockSpec((None, D, D), lambda i: (layer, 0, 0)),
            modblk(2), modblk(3), modblk(4),
            full((1, D)),
            full((D, LANES)),
            full((1, LANES)),
        ],
        out_specs=[rowblk(D), pl.BlockSpec((tm * SUBLANES, LANES), lambda i: (i, 0)), rowblk(LANES), rowblk(LANES),
                   full((SUBLANES, LANES))],
        out_shape=[
            jax.ShapeDtypeStruct((T, D), F32),
            jax.ShapeDtypeStruct((T * SUBLANES, LANES), U32),
            jax.ShapeDtypeStruct((T, LANES), jnp.int32),
            jax.ShapeDtypeStruct((T, LANES), F32),
            jax.ShapeDtypeStruct((SUBLANES, LANES), F32),
        ],
        compiler_params=_params(("arbitrary",)),
    )(h, o_gla, o_lru, o_diff, w_out, mod3, mod3, mod3, norm2_g.reshape(1, D), rw, rb)


def _dispatch(info, counts):
    T = info.shape[0]
    blk = MOE_BLK
    n_blocks = (T * TOP_K) // blk + N_EXPERTS
    padded = (counts + blk - 1) // blk * blk
    pends = jnp.cumsum(padded)
    pstarts = pends - padded
    ustarts = jnp.cumsum(counts) - counts
    n_used = (pends[-1] // blk).astype(jnp.int32)
    block_idx = jnp.arange(n_blocks, dtype=jnp.int32)
    block_expert = jnp.minimum(jnp.sum((pends[None, :] <= (block_idx * blk)[:, None]).astype(jnp.int32), axis=1),
                               N_EXPERTS - 1)
    last_used = jnp.sum(jnp.where(block_idx == jnp.maximum(n_used - 1, 0), block_expert, 0))
    block_expert = jnp.where(block_idx < n_used, block_expert, last_used).astype(jnp.int32)
    following = jnp.concatenate([block_expert[1:], jnp.full((1,), -1, jnp.int32)])
    zero_block = ((block_idx >= n_used - 1) | (following != block_expert)).astype(jnp.int32)
    owner = block_expert[:, None] == jnp.arange(N_EXPERTS, dtype=jnp.int32)[None, :]
    seg_end = jnp.sum(jnp.where(owner, (pstarts + counts)[None, :], 0), axis=1)
    n_valid = jnp.where(block_idx < n_used, jnp.clip(seg_end - block_idx * blk, 0, blk), 0).astype(jnp.int32)
    packed_base = (jnp.sum(jnp.where(owner, (ustarts - pstarts)[None, :], 0), axis=1) + block_idx * blk)
    packed_base = jnp.where(n_valid > 0, packed_base, 0).astype(jnp.int32)
    return dict(experts=(info[:, 0], info[:, 1]), ranks=(info[:, 2], info[:, 3]),
                slot_start=pstarts.astype(jnp.int32), packed_start=ustarts.astype(jnp.int32), zero_block=zero_block,
                block_expert=block_expert, n_used=n_used.reshape(1), n_valid=n_valid, packed_base=packed_base)


def _scatter_kernel(e0_ref, e1_ref, r0_ref, r1_ref, ss_ref, ps_ref, zb_ref, src_ref, o_ref, inv_ref,
                    zero_buf, sem, zero_sem):
    rows = src_ref.shape[0] // SUBLANES
    base = pl.program_id(0) * rows
    fill_rows = zero_buf.shape[0]
    n_blocks = o_ref.shape[0] // fill_rows

    @pl.when(pl.program_id(0) == 0)
    def _():
        zero_buf[...] = jnp.zeros_like(zero_buf)

        def for_each_fill(fn):
            def body(j, carry):
                @pl.when(zb_ref[j] == 1)
                def _():
                    fn(pltpu.make_async_copy(
                        zero_buf, o_ref.at[pl.ds(pl.multiple_of(j * fill_rows, fill_rows), fill_rows)], zero_sem))
                return carry
            lax.fori_loop(0, n_blocks, body, 0)

        for_each_fill(lambda copy: copy.start())
        for_each_fill(lambda copy: copy.wait())

    def row_copy(r, slot):
        return pltpu.make_async_copy(src_ref.at[pl.ds(pl.multiple_of(r * SUBLANES, SUBLANES), SUBLANES)],
                                     o_ref.at[pl.ds(pl.multiple_of(slot * SUBLANES, SUBLANES), SUBLANES)], sem)

    def issue(r, carry):
        t = base + r
        for k, (e_ref, r_ref) in enumerate(((e0_ref, r0_ref), (e1_ref, r1_ref))):
            expert = e_ref[t]
            rank = r_ref[t]
            row_copy(r, ss_ref[expert] + rank).start()
            inv_ref[ps_ref[expert] + rank] = t * TOP_K + k
        return carry

    lax.fori_loop(0, rows, issue, 0, unroll=8)
    for _ in range(TOP_K):
        pltpu.make_async_copy(src_ref, o_ref.at[pl.ds(0, rows * SUBLANES)], sem).wait()


def _scatter_rows(plan, src):
    T = src.shape[0] // SUBLANES
    P = T * TOP_K + N_EXPERTS * MOE_BLK
    rows = GATHER_ROWS
    return pl.pallas_call(
        _scatter_kernel,
        grid_spec=pltpu.PrefetchScalarGridSpec(
            num_scalar_prefetch=7,
            grid=(T // rows,),
            in_specs=[pl.BlockSpec((rows * SUBLANES, LANES), lambda i, *refs: (i, 0))],
            out_specs=[pl.BlockSpec(memory_space=pl.ANY), pl.BlockSpec(memory_space=pltpu.SMEM)],
            scratch_shapes=[
                pltpu.VMEM((MOE_BLK * SUBLANES, LANES), src.dtype),
                pltpu.SemaphoreType.DMA(()),
                pltpu.SemaphoreType.DMA(()),
            ],
        ),
        out_shape=[jax.ShapeDtypeStruct((P * SUBLANES, LANES), src.dtype),
                   jax.ShapeDtypeStruct((T * TOP_K,), jnp.int32)],
        compiler_params=_params(("arbitrary",)),
    )(*plan["experts"], *plan["ranks"], plan["slot_start"], plan["packed_start"], plan["zero_block"], src)


def _expert_kernel(layer, be_ref, first_ref, next_ref, slot_ref, nu_ref, nv_ref, pb_ref, inv_ref,
                   xs_ref, w1_hbm, w3_hbm, w2_hbm, yt_ref,
                   w1f, w3f, w2f, w1b, w3b, w2b, ybuf, sems, ysems):
    i = pl.program_id(0)
    D = w1b.shape[0]
    blk = xs_ref.shape[0] // SUBLANES

    def drain_rows(j):
        n = nv_ref[j]
        b = j % 2
        for bit in range(blk.bit_length()):
            size = (1 << bit) * SUBLANES

            @pl.when((n >> bit) & 1 == 1)
            def _():
                pltpu.make_async_copy(ybuf.at[b, pl.ds(0, size)], yt_ref.at[pl.ds(0, size)], ysems.at[b]).wait()

    def weight_copies(e, slot):
        return (pltpu.make_async_copy(w1_hbm.at[layer, e], w1f.at[slot], sems.at[slot, 0]),
                pltpu.make_async_copy(w3_hbm.at[layer, e], w3f.at[slot], sems.at[slot, 1]),
                pltpu.make_async_copy(w2_hbm.at[layer, e], w2f.at[slot], sems.at[slot, 2]))

    @pl.when(i == 0)
    def _():
        for c in weight_copies(be_ref[0], 0):
            c.start()

    @pl.when(first_ref[i] == 1)
    def _():
        slot = slot_ref[i]
        for c in weight_copies(be_ref[i], slot):
            c.wait()

        @pl.when(next_ref[i] >= 0)
        def _():
            for c in weight_copies(next_ref[i], 1 - slot):
                c.start()

        w1b[...] = w1f[slot].astype(BF16)
        w3b[...] = w3f[slot].astype(BF16)
        w2b[...] = w2f[slot].astype(BF16)

    @pl.when(i >= 2)
    def _():
        drain_rows(i - 2)

    @pl.when(i < nu_ref[0])
    def _():
        b = i % 2
        n = nv_ref[i]

        def swiglu(n_rows):
            tiles = pl.ds(0, n_rows * SUBLANES)
            lo, hi = _unpack_bf16_pair(_load_row_tiles(xs_ref.at[tiles]))
            lo = lo.astype(BF16)
            hi = hi.astype(BF16)
            a = _dot(lo, w1b[0:D // 2, :]) + _dot(hi, w1b[D // 2:, :])
            g = _dot(lo, w3b[0:D // 2, :]) + _dot(hi, w3b[D // 2:, :])
            hid = ((a * _sigmoid(a)) * g).astype(BF16)
            _store_row_tiles(ybuf.at[b, tiles],
                             _pack_bf16_pair(_dot(hid, w2b[:, 0:D // 2]), _dot(hid, w2b[:, D // 2:])))

        @pl.when(n > blk // 2)
        def _():
            swiglu(blk)

        @pl.when(n <= blk // 2)
        def _():
            swiglu(blk // 2)

        first_pos = pb_ref[i]
        unroll = 8

        def issue_row(r):
            a_idx = inv_ref[first_pos + r]
            pltpu.make_async_copy(ybuf.at[b, pl.ds(pl.multiple_of(r * SUBLANES, SUBLANES), SUBLANES)],
                                  yt_ref.at[pl.ds(pl.multiple_of(a_idx * SUBLANES, SUBLANES), SUBLANES)],
                                  ysems.at[b]).start()

        def issue_group(g, carry):
            for u in range(unroll):
                issue_row(g * unroll + u)
            return carry

        def issue_one(r, carry):
            issue_row(r)
            return carry

        lax.fori_loop(0, n // unroll, issue_group, 0)
        lax.fori_loop((n // unroll) * unroll, n, issue_one, 0)

    @pl.when(i == pl.num_programs(0) - 1)
    def _():
        drain_rows(i - 1)
        drain_rows(i)


def _segment_plan(block_expert, n_used):
    n = block_expert.shape[0]
    idx = jnp.arange(n, dtype=jnp.int32)
    prev = jnp.concatenate([jnp.full((1,), -1, jnp.int32), block_expert[:-1]])
    first = ((block_expert != prev) & (idx < n_used[0])).astype(jnp.int32)
    slot = (jnp.cumsum(first) - 1) % 2
    later_first = jnp.where(first == 1, idx, n)
    next_idx = lax.cummin(jnp.concatenate([later_first[1:], jnp.full((1,), n, jnp.int32)]), reverse=True)
    next_expert = jnp.where(next_idx < n, block_expert[jnp.minimum(next_idx, n - 1)], -1)
    return first, next_expert.astype(jnp.int32), slot.astype(jnp.int32)


def _experts(plan, inv, xs, w1, w3, w2, layer):
    _, _, D, DE = w1.shape
    blk = MOE_BLK
    block_expert, n_used = plan["block_expert"], plan["n_used"]
    n_blocks = block_expert.shape[0]
    n_assign = inv.shape[0]
    first, next_expert, slot = _segment_plan(block_expert, n_used)
    rowmap = lambda i, *refs: (jnp.minimum(i, jnp.maximum(refs[4][0] - 1, 0)), 0)
    hbm = pl.BlockSpec(memory_space=pl.ANY)
    return pl.pallas_call(
        functools.partial(_expert_kernel, layer),
        grid_spec=pltpu.PrefetchScalarGridSpec(
            num_scalar_prefetch=8,
            grid=(n_blocks,),
            in_specs=[pl.BlockSpec((blk * SUBLANES, LANES), rowmap), hbm, hbm, hbm],
            out_specs=hbm,
            scratch_shapes=[
                pltpu.VMEM((2, D, DE), F32),
                pltpu.VMEM((2, D, DE), F32),
                pltpu.VMEM((2, DE, D), F32),
                pltpu.VMEM((D, DE), BF16),
                pltpu.VMEM((D, DE), BF16),
                pltpu.VMEM((DE, D), BF16),
                pltpu.VMEM((2, blk * SUBLANES, LANES), U32),
                pltpu.SemaphoreType.DMA((2, 3)),
                pltpu.SemaphoreType.DMA((2,)),
            ],
        ),
        out_shape=jax.ShapeDtypeStruct((n_assign * SUBLANES, LANES), U32),
        compiler_params=_params(("arbitrary",)),
    )(block_expert, first, next_expert, slot, n_used, plan["n_valid"], plan["packed_base"], inv, xs, w1, w3, w2)


def _final_combine_kernel(yt_ref, h_ref, g2_ref, ew_ref, fg_ref, o_ref):
    hn = h_ref[...] + g2_ref[0] * _moe_mix(yt_ref, ew_ref)
    o_ref[...] = hn * lax.rsqrt(jnp.mean(hn * hn, axis=-1, keepdims=True) + EPS) * fg_ref[...]


def _final_combine(yt, h, mod3, ew, final_g, S):
    T, D = h.shape
    tc = 256
    per_b = S // tc
    return pl.pallas_call(
        _final_combine_kernel,
        grid=(T // tc,),
        in_specs=[
            pl.BlockSpec((tc * TOP_K * SUBLANES, LANES), lambda i: (i, 0)),
            pl.BlockSpec((tc, D), lambda i: (i, 0)),
            pl.BlockSpec((1, 1, D), lambda i: (i // per_b, 0, 5)),
            pl.BlockSpec((tc, LANES), lambda i: (i, 0)),
            pl.BlockSpec((1, D), lambda i: (0, 0)),
        ],
        out_specs=pl.BlockSpec((tc, D), lambda i: (i, 0)),
        out_shape=jax.ShapeDtypeStruct((T, D), F32),
        compiler_params=_params(("parallel",)),
    )(yt, h, mod3, ew, final_g.reshape(1, D))


def kernel(x, c, ada_w, ada_b, norm1_g, w_in, gla_w_a2, gla_b_a, gla_norm_g, lru_conv_w, lru_conv_b,
           lru_wa, lru_ba, lru_wx, lru_bx, lru_lambda, diff_lq1, diff_lk1, diff_lq2, diff_lk2,
           diff_subln_g, rel_bias, w_out, norm2_g, router_g_w, router_g_b, router_e_w, router_e_b,
           moe_w1, moe_w3, moe_w2, final_g):
    B, S, D = x.shape
    T = B * S
    L = ada_w.shape[0]
    h = x.reshape(T, D)
    mod = _ada_mod(c, ada_w, ada_b)
    bias = _bias_tiles(rel_bias)
    w_in_perm = _permute_w_in(w_in)
    w_out_bf16 = w_out.astype(BF16)
    pending_moe = None
    for l in range(L):
        mod3 = mod[l][:, None, :]
        if pending_moe is None:
            proj = _inproj(h, mod3, norm1_g[l], w_in_perm, l, S)
        else:
            proj, h = _inproj(h, mod3, norm1_g[l], w_in_perm, l, S, pending_moe)
        wa2_pad = jnp.concatenate(
            [gla_w_a2[l], jnp.zeros((LANES - GLA_LOWRANK, GLA_KEY_WIDTH), F32)], axis=0).astype(BF16)
        o_gla = _gla(proj, wa2_pad, gla_b_a[l], gla_norm_g[l], B, S)
        w_gates = jnp.concatenate([_block_diag(lru_wa[l]), _block_diag(lru_wx[l])], axis=1).astype(BF16)
        b_gates = jnp.concatenate([lru_ba[l], lru_bx[l]]).reshape(1, 2 * LRU_WIDTH)
        o_lru = _lru(proj.reshape(B, S, PROJ_WIDTH), lru_conv_w[l], lru_conv_b[l], w_gates, b_gates,
                     lru_lambda[l]).reshape(T, LRU_WIDTH)
        lqk = jnp.stack([diff_lq1[l], diff_lk1[l], diff_lq2[l], diff_lk2[l]], axis=0)
        o_diff = _diff_attention(proj, bias, lqk, diff_subln_g[l], l, B, S)
        rw = jnp.concatenate(
            [router_g_w[l], router_e_w[l], jnp.zeros((D, LANES - N_GROUPS - N_EXPERTS), F32)], axis=1).astype(BF16)
        rb = jnp.concatenate(
            [router_g_b[l], router_e_b[l], jnp.zeros((LANES - N_GROUPS - N_EXPERTS,), F32)]).reshape(1, LANES)
        h, u2, info, ew, cnt = _outproj(h, o_gla, o_lru, o_diff, w_out_bf16, l, mod3, norm2_g[l], rw, rb, S)
        counts = cnt[0, N_GROUPS:N_GROUPS + N_EXPERTS].astype(jnp.int32)
        plan = _dispatch(info, counts)
        xs, inv = _scatter_rows(plan, u2)
        yt = _experts(plan, inv, xs, moe_w1, moe_w3, moe_w2, l)
        pending_moe = (yt, ew, mod3)
    out = _final_combine(yt, h, mod3, ew, final_g, S)
    return out.reshape(B, S, D)
```

```python
import functools
import math

import jax
import jax.numpy as jnp
from jax import lax
from jax.experimental import pallas as pl
from jax.experimental.pallas import tpu as pltpu

F32 = jnp.float32
BF16 = jnp.bfloat16
U32 = jnp.uint32

EPS = 1e-6
LOG2E = math.log2(math.e)
CHUNK = 64

GLA_DV = 128
GLA_DK = 64
GLA_HEADS = 6
GLA_WIDTH = GLA_HEADS * GLA_DV
GLA_KEY_WIDTH = GLA_HEADS * GLA_DK
GLA_LOWRANK = 16
GLA_TAU = 16.0

LRU_WIDTH = 512
LRU_BLOCKS = 8
LRU_BLOCK_DIM = LRU_WIDTH // LRU_BLOCKS
CONV_WIDTH = 4
LRU_C = 8.0

DIFF_DH = 64
DIFF_DV = 128
DIFF_HEADS = 6
DIFF_WIDTH = DIFF_HEADS * DIFF_DV

REL_BUCKETS = 32
REL_MAX_DIST = 128

N_GROUPS = 8
EXPERTS_PER_GROUP = 8
N_EXPERTS = 64
TOP_K = 2

LANES = 128
SUBLANES = 8
VMEM_LIMIT = 56 * 1024 * 1024

COL_GV = 0
COL_GOG = 768
COL_DQ = 1536
COL_DK = 2304
COL_DV = 3072
COL_GQ = 3840
COL_GK = 4224
COL_LY = 4608
COL_LX = 5120
COL_GA = 5632
PROJ_WIDTH = 5760

ATT_TILE = 512
GLA_TILE = 256
LRU_TILE = 256
MOE_BLK = 256
GATHER_ROWS = 512


def _params(sem, vmem=VMEM_LIMIT):
    return pltpu.CompilerParams(dimension_semantics=sem, vmem_limit_bytes=vmem)


def _sigmoid(x):
    return 0.5 * jnp.tanh(0.5 * x) + 0.5


def _softplus(x):
    return jnp.maximum(x, 0.0) + jnp.log1p(jnp.exp(-jnp.abs(x)))


def _dot(a, b):
    return jnp.dot(a, b, preferred_element_type=F32)


def _dot_nt(a, b):
    return lax.dot_general(a, b, (((1,), (1,)), ((), ())), preferred_element_type=F32)


def _dot_tn(a, b):
    return lax.dot_general(a, b, (((0,), (0,)), ((), ())), preferred_element_type=F32)


def _pack_bf16_pair(lo, hi):
    lo_bits = lax.bitcast_convert_type(lo.astype(BF16).astype(F32), U32)
    hi_bits = lax.bitcast_convert_type(hi.astype(BF16).astype(F32), U32)
    return (hi_bits & jnp.uint32(0xFFFF0000)) | (lo_bits >> 16)


def _unpack_bf16_pair(w):
    lo = lax.bitcast_convert_type(w << 16, F32)
    hi = lax.bitcast_convert_type(w & jnp.uint32(0xFFFF0000), F32)
    return lo, hi


def _store_row_tiles(ref, words):
    rows = words.shape[0]
    for s in range(SUBLANES):
        ref[pl.ds(s, rows, stride=SUBLANES), :] = words[:, s * LANES:(s + 1) * LANES]


def _load_row_tiles(ref):
    rows = ref.shape[0] // SUBLANES
    return jnp.concatenate([ref[pl.ds(s, rows, stride=SUBLANES), :] for s in range(SUBLANES)], axis=1)


def _ada_kernel(c_ref, w_ref, b_ref, o_ref):
    c = c_ref[...]
    s = c * _sigmoid(c)
    o_ref[0] = _dot(s.astype(BF16), w_ref[0].astype(BF16)) + b_ref[0]


def _ada_mod(c, ada_w, ada_b):
    L, D, N = ada_w.shape
    B = c.shape[0]
    tn = 1024
    return pl.pallas_call(
        _ada_kernel,
        grid=(L, N // tn),
        in_specs=[
            pl.BlockSpec((B, D), lambda l, j: (0, 0)),
            pl.BlockSpec((1, D, tn), lambda l, j: (l, 0, j)),
            pl.BlockSpec((1, 1, tn), lambda l, j: (l, 0, j)),
        ],
        out_specs=pl.BlockSpec((1, B, tn), lambda l, j: (l, 0, j)),
        out_shape=jax.ShapeDtypeStruct((L, B, N), F32),
        compiler_params=_params(("parallel", "parallel")),
    )(c, ada_w, ada_b.reshape(L, 1, N))


def _modulated_norm(x, g, sc, sh):
    ms = jnp.mean(x * x, axis=-1, keepdims=True)
    return (x * lax.rsqrt(ms + EPS) * g) * (1.0 + sc) + sh


def _moe_mix(yt_ref, ew_ref):
    rows = ew_ref.shape[0]

    def expert_rows(k):
        return jnp.concatenate([yt_ref[pl.ds(k * SUBLANES + s, rows, stride=TOP_K * SUBLANES), :]
                                for s in range(SUBLANES)], axis=1)

    ew = ew_ref[...]
    w0 = ew[:, 0:1]
    w1 = ew[:, 1:2]
    lo0, hi0 = _unpack_bf16_pair(expert_rows(0))
    lo1, hi1 = _unpack_bf16_pair(expert_rows(1))
    return jnp.concatenate([w0 * lo0 + w1 * lo1, w0 * hi0 + w1 * hi1], axis=1)


def _inproj_kernel(h_ref, sh_ref, sc_ref, g_ref, w_ref, o_ref, u_scr):
    tm = h_ref.shape[0]
    nr = tm // 2

    @pl.when(pl.program_id(1) == 0)
    def _():
        for half in range(2):
            rows = slice(half * nr, (half + 1) * nr)
            u = _modulated_norm(h_ref[rows, :], g_ref[...], sc_ref[0], sh_ref[0]).astype(BF16)
            u_scr[rows, :] = u
            o_ref[rows, :] = _dot(u, w_ref[...]).astype(o_ref.dtype)

    @pl.when(pl.program_id(1) > 0)
    def _():
        o_ref[...] = _dot(u_scr[...], w_ref[...]).astype(o_ref.dtype)


def _inproj_after_moe_kernel(h_ref, yt_ref, ew_ref, g2_ref, sh_ref, sc_ref, g_ref, w_ref, o_ref, hn_ref, u_scr):
    tm = h_ref.shape[0]
    nr = tm // 2

    @pl.when(pl.program_id(1) == 0)
    def _():
        for half in range(2):
            rows = slice(half * nr, (half + 1) * nr)
            tiles = pl.ds(half * nr * TOP_K * SUBLANES, nr * TOP_K * SUBLANES)
            hn = h_ref[rows, :] + g2_ref[0] * _moe_mix(yt_ref.at[tiles], ew_ref.at[rows])
            hn_ref[rows, :] = hn
            u = _modulated_norm(hn, g_ref[...], sc_ref[0], sh_ref[0]).astype(BF16)
            u_scr[rows, :] = u
            o_ref[rows, :] = _dot(u, w_ref[...]).astype(o_ref.dtype)

    @pl.when(pl.program_id(1) > 0)
    def _():
        o_ref[...] = _dot(u_scr[...], w_ref[...]).astype(o_ref.dtype)


def _inproj(h, mod3, norm_g, w_perm, layer, S, pending_moe=None):
    T, D = h.shape
    N = w_perm.shape[2]
    tm, tn = 512, 1152
    per_b = S // tm
    rows = pl.BlockSpec((tm, D), lambda i, j: (i, 0))
    mod_specs = [
        pl.BlockSpec((1, 1, D), lambda i, j: (i // per_b, 0, 0)),
        pl.BlockSpec((1, 1, D), lambda i, j: (i // per_b, 0, 1)),
        pl.BlockSpec((1, D), lambda i, j: (0, 0)),
        pl.BlockSpec((None, D, tn), lambda i, j: (layer, 0, j)),
    ]
    proj_spec = pl.BlockSpec((tm, tn), lambda i, j: (i, j))
    proj_shape = jax.ShapeDtypeStruct((T, N), BF16)
    common = dict(grid=(T // tm, N // tn), scratch_shapes=[pltpu.VMEM((tm, D), BF16)],
                  compiler_params=_params(("parallel", "arbitrary")))
    if pending_moe is None:
        return pl.pallas_call(_inproj_kernel, in_specs=[rows] + mod_specs, out_specs=proj_spec,
                              out_shape=proj_shape, **common)(h, mod3, mod3, norm_g.reshape(1, D), w_perm)
    yt, ew, prev_mod3 = pending_moe
    moe_specs = [
        pl.BlockSpec((tm * TOP_K * SUBLANES, LANES), lambda i, j: (i, 0)),
        pl.BlockSpec((tm, LANES), lambda i, j: (i, 0)),
        pl.BlockSpec((1, 1, D), lambda i, j: (i // per_b, 0, 5)),
    ]
    return pl.pallas_call(
        _inproj_after_moe_kernel, in_specs=[rows] + moe_specs + mod_specs, out_specs=[proj_spec, rows],
        out_shape=[proj_shape, jax.ShapeDtypeStruct((T, D), F32)], **common,
    )(h, yt, ew, prev_mod3, mod3, mod3, norm_g.reshape(1, D), w_perm)


W_IN_SEGMENTS = ((COL_GQ, 0, 384), (COL_GK, 384, 384), (COL_GV, 768, 768), (COL_GOG, 1536, 768),
                 (COL_GA, 2304, GLA_LOWRANK), (COL_LY, 2320, 512), (COL_LX, 2832, 512),
                 (COL_DQ, 3344, 768), (COL_DK, 4112, 768), (COL_DV, 4880, 768))


def _relayout_kernel(w_ref, o_ref):
    x = w_ref[0]
    for dst, src, width in W_IN_SEGMENTS:
        o_ref[0, :, dst:dst + width] = x[:, src:src + width]
    pad = slice(COL_GA + GLA_LOWRANK, COL_GA + LANES)
    o_ref[0, :, pad] = jnp.zeros((x.shape[0], LANES - GLA_LOWRANK), BF16)


def _permute_w_in(w):
    L, D, N = w.shape
    rt = 256
    return pl.pallas_call(
        _relayout_kernel,
        grid=(L, D // rt),
        in_specs=[pl.BlockSpec((1, rt, N), lambda l, i: (l, i, 0))],
        out_specs=pl.BlockSpec((1, rt, PROJ_WIDTH), lambda l, i: (l, i, 0)),
        out_shape=jax.ShapeDtypeStruct((L, D, PROJ_WIDTH), BF16),
        compiler_params=_params(("parallel", "parallel")),
    )(w.astype(BF16))


def _gla_kernel(q_ref, k_ref, v_ref, og_ref, alr_ref, wa2_ref, ba_ref, ng_ref, o_ref, st_ref):
    tb = q_ref.shape[0]
    n_chunks = tb // CHUNK

    @pl.when(pl.program_id(1) == 0)
    def _():
        st_ref[...] = jnp.zeros_like(st_ref)

    row = lax.broadcasted_iota(jnp.int32, (tb, tb), 0)
    col = lax.broadcasted_iota(jnp.int32, (tb, tb), 1)
    same_chunk = (row // CHUNK) == (col // CHUNK)
    causal = col <= row
    tril = jnp.where(same_chunk & causal, 1.0, 0.0).astype(BF16)
    lane = lax.broadcasted_iota(jnp.int32, (1, LANES), 1)
    half_masks = (lane < GLA_DK, lane >= GLA_DK)

    alr = alr_ref[...]
    cols = [slice(p * LANES, (p + 1) * LANES) for p in range(GLA_HEADS // 2)]
    z = [_dot(alr, wa2_ref[:, cs]) + ba_ref[:, cs] for cs in cols]
    la = [(jnp.minimum(zp, 0.0) - jnp.log1p(jnp.exp(-jnp.abs(zp)))) * (1.0 / GLA_TAU) for zp in z]
    la_hi = [x.astype(BF16) for x in la]
    la_lo = [(x - h.astype(F32)).astype(BF16) for x, h in zip(la, la_hi)]
    G = [_dot(tril, h) + _dot(tril, lo) for h, lo in zip(la_hi, la_lo)]
    Gl = [jnp.concatenate([jnp.broadcast_to(g[(c + 1) * CHUNK - 1:(c + 1) * CHUNK, :], (CHUNK, LANES))
                           for c in range(n_chunks)], axis=0) for g in G]
    eG = [jnp.exp(g) for g in G]
    enG = [jnp.exp(-g) for g in G]
    q = [q_ref[:, cs].astype(F32) * (GLA_DK ** -0.5) for cs in cols]
    k = [k_ref[:, cs].astype(F32) for cs in cols]
    kf = [(kp * e).astype(BF16) for kp, e in zip(k, eG)]
    kb = [(kp * e).astype(BF16) for kp, e in zip(k, enG)]
    kd = [kp * jnp.exp(gl - g) for kp, gl, g in zip(k, Gl, G)]

    heads = range(GLA_HEADS)
    pair = [h // 2 for h in heads]
    mask = [half_masks[h % 2] for h in heads]
    vcols = [slice(h * GLA_DV, (h + 1) * GLA_DV) for h in heads]
    qf_h = [jnp.where(mask[h], q[pair[h]] * eG[pair[h]], 0.0).astype(BF16) for h in heads]
    qb_h = [jnp.where(mask[h], q[pair[h]] * enG[pair[h]], 0.0).astype(BF16) for h in heads]
    kd_h = [jnp.where(mask[h], kd[pair[h]], 0.0).astype(BF16) for h in heads]
    v_h = [v_ref[:, vcols[h]] for h in heads]
    a_f = [_dot_nt(qf_h[h], kb[pair[h]]) for h in heads]
    a_b = [_dot_nt(qb_h[h], kf[pair[h]]) for h in heads]
    attn = [jnp.where(same_chunk, jnp.where(causal, a_f[h], a_b[h]), 0.0).astype(BF16) for h in heads]
    o_intra = [_dot(attn[h], v_h[h]) for h in heads]
    chunk_rows = [slice(c * CHUNK, (c + 1) * CHUNK) for c in range(n_chunks)]
    kv = [[_dot_tn(v_h[h][rs], kd_h[h][rs]) for rs in chunk_rows] for h in heads]
    for h in heads:
        st = st_ref[h]
        inter = []
        for c, rs in enumerate(chunk_rows):
            inter.append(_dot_nt(qf_h[h][rs], st.astype(BF16)))
            decay = jnp.exp(Gl[pair[h]][c * CHUNK:c * CHUNK + 1, :])
            st = st * decay + kv[h][c]
        st_ref[h] = st
        o = o_intra[h] + jnp.concatenate(inter, axis=0)
        o = o * lax.rsqrt(jnp.mean(o * o, axis=-1, keepdims=True) + EPS)
        og = og_ref[:, vcols[h]].astype(F32)
        o_ref[:, vcols[h]] = (o * ng_ref[:, vcols[h]] * (og * _sigmoid(og))).astype(o_ref.dtype)


def _gla(proj, wa2_pad, b_a, norm_g, B, S):
    T = proj.shape[0]
    tb = GLA_TILE
    nt = S // tb
    row = lambda b, i: b * nt + i
    return pl.pallas_call(
        _gla_kernel,
        grid=(B, nt),
        in_specs=[
            pl.BlockSpec((tb, GLA_KEY_WIDTH), lambda b, i: (row(b, i), COL_GQ // GLA_KEY_WIDTH)),
            pl.BlockSpec((tb, GLA_KEY_WIDTH), lambda b, i: (row(b, i), COL_GK // GLA_KEY_WIDTH)),
            pl.BlockSpec((tb, GLA_WIDTH), lambda b, i: (row(b, i), COL_GV // GLA_WIDTH)),
            pl.BlockSpec((tb, GLA_WIDTH), lambda b, i: (row(b, i), COL_GOG // GLA_WIDTH)),
            pl.BlockSpec((tb, LANES), lambda b, i: (row(b, i), COL_GA // LANES)),
            pl.BlockSpec((LANES, GLA_KEY_WIDTH), lambda b, i: (0, 0)),
            pl.BlockSpec((1, GLA_KEY_WIDTH), lambda b, i: (0, 0)),
            pl.BlockSpec((1, GLA_WIDTH), lambda b, i: (0, 0)),
        ],
        out_specs=pl.BlockSpec((tb, GLA_WIDTH), lambda b, i: (row(b, i), 0)),
        out_shape=jax.ShapeDtypeStruct((T, GLA_WIDTH), BF16),
        scratch_shapes=[pltpu.VMEM((GLA_HEADS, GLA_DV, LANES), F32)],
        compiler_params=_params(("parallel", "arbitrary")),
    )(proj, proj, proj, proj, proj, wa2_pad, b_a.reshape(1, -1), norm_g.reshape(1, -1))


def _lru_kernel(y_ref, x_ref, cw_ref, cb_ref, wg_ref, bg_ref, lam_ref, o_ref, *scratch):
    B, ts, W = x_ref.shape
    n_planes = W // LANES
    a_scr = scratch[0:n_planes]
    b_scr = scratch[n_planes:2 * n_planes]
    h_scr = scratch[2 * n_planes:3 * n_planes]
    xc_scr, tail_scr, carry_scr = scratch[3 * n_planes:]

    @pl.when(pl.program_id(0) == 0)
    def _():
        tail_scr[...] = jnp.zeros_like(tail_scr)
        carry_scr[...] = jnp.zeros_like(carry_scr)

    cw = cw_ref[...]
    cb = cb_ref[...]
    sp = _softplus(-lam_ref[...])
    row8 = lax.broadcasted_iota(jnp.int32, (8, W), 0)
    for b in range(B):
        x = x_ref[b].astype(F32)
        tail = tail_scr[b]
        xc = cb + cw[CONV_WIDTH - 1:CONV_WIDTH, :] * x
        head = cb + cw[CONV_WIDTH - 1:CONV_WIDTH, :] * x[0:8]
        for d in range(1, CONV_WIDTH):
            wd = cw[CONV_WIDTH - 1 - d:CONV_WIDTH - d, :]
            xr = pltpu.roll(x, d, 0)
            xc = xc + wd * xr
            head = head + wd * jnp.where(row8 < d, pltpu.roll(tail, d, 0), xr[0:8])
        tail_scr[b] = x[ts - 8:ts]
        xc_scr[...] = xc
        xc_scr[0:8] = head
        xc = xc_scr[...]
        gates = _sigmoid(_dot(xc.astype(BF16), wg_ref[...]) + bg_ref[...])
        r = gates[:, :W]
        ig = gates[:, W:]
        log_a = (-LRU_C) * r * sp
        a = jnp.exp(log_a)
        b_in = jnp.sqrt(-jnp.tanh(log_a) * (a * a + 1.0)) * (ig * xc)
        rows = slice(b * ts, (b + 1) * ts)
        for k in range(n_planes):
            a_scr[k][rows] = a[:, k * LANES:(k + 1) * LANES]
            b_scr[k][rows] = b_in[:, k * LANES:(k + 1) * LANES]

    def step(t, hs):
        idx = pl.ds(t, B, stride=ts)
        out = []
        for k in range(n_planes):
            hk = a_scr[k][idx, :] * hs[k] + b_scr[k][idx, :]
            h_scr[k][idx, :] = hk
            out.append(hk)
        return tuple(out)

    hs = lax.fori_loop(0, ts, step, tuple(carry_scr[k] for k in range(n_planes)), unroll=8)
    for k in range(n_planes):
        carry_scr[k] = hs[k]

    for b in range(B):
        rows = slice(b * ts, (b + 1) * ts)
        y = y_ref[b].astype(F32)
        gelu = 0.5 * y * (1.0 + jnp.tanh(math.sqrt(2.0 / math.pi) * (y + 0.044715 * (y * y * y))))
        h = jnp.concatenate([h_scr[k][rows] for k in range(n_planes)], axis=1)
        o_ref[b] = (h * gelu).astype(o_ref.dtype)


def _lru(proj3, conv_w, conv_b, w_gates, b_gates, lam):
    B, S, _ = proj3.shape
    W = LRU_WIDTH
    ts = LRU_TILE
    n_planes = W // LANES
    full = lambda shape: pl.BlockSpec(shape, lambda i: (0,) * len(shape))
    return pl.pallas_call(
        _lru_kernel,
        grid=(S // ts,),
        in_specs=[
            pl.BlockSpec((B, ts, W), lambda i: (0, i, COL_LY // W)),
            pl.BlockSpec((B, ts, W), lambda i: (0, i, COL_LX // W)),
            full((CONV_WIDTH, W)),
            full((1, W)),
            full((W, 2 * W)),
            full((1, 2 * W)),
            full((1, W)),
        ],
        out_specs=pl.BlockSpec((B, ts, W), lambda i: (0, i, 0)),
        out_shape=jax.ShapeDtypeStruct((B, S, W), BF16),
        scratch_shapes=(
            [pltpu.VMEM((B * ts, LANES), F32) for _ in range(3 * n_planes)]
            + [pltpu.VMEM((ts, W), F32), pltpu.VMEM((B, 8, W), F32), pltpu.VMEM((n_planes, B, LANES), F32)]),
        compiler_params=_params(("arbitrary",)),
    )(proj3, proj3, conv_w, conv_b.reshape(1, W), w_gates, b_gates, lam.reshape(1, W))


def _block_diag(w):
    n, d, _ = w.shape
    eye = jnp.eye(n, dtype=w.dtype)
    return (eye[:, None, :, None] * w[:, :, None, :]).reshape(n * d, n * d)


def _t5_bucket(rel):
    nb = REL_BUCKETS // 2
    ret = (rel > 0).astype(jnp.int32) * nb
    n = jnp.abs(rel)
    max_exact = nb // 2
    nf = jnp.maximum(n, 1).astype(jnp.float32)
    large = max_exact + (jnp.log(nf / max_exact) / math.log(REL_MAX_DIST / max_exact)
                         * (nb - max_exact)).astype(jnp.int32)
    large = jnp.minimum(large, nb - 1)
    return ret + jnp.where(n < max_exact, n, large)


def _bias_kernel(bucket_ref, table_ref, o_ref):
    h = pl.program_id(0)
    bucket = bucket_ref[0]
    acc = jnp.full(bucket.shape, -1e30, F32)
    for b in range(REL_BUCKETS):
        acc = jnp.where(bucket == b, table_ref[b, h] * LOG2E, acc)
    o_ref[0, 0] = acc


def _bias_tiles(rel_bias):
    t = ATT_TILE
    H = rel_bias.shape[1]
    qp = jnp.arange(t, dtype=jnp.int32)[:, None]
    kp = jnp.arange(t, dtype=jnp.int32)[None, :]
    mask = (kp // CHUNK) <= (qp // CHUNK)
    half = REL_BUCKETS // 2
    per_distance = _t5_bucket(-jnp.arange(2 * t, dtype=jnp.int32))
    edges = jnp.sum((per_distance[None, :] < jnp.arange(1, half, dtype=jnp.int32)[:, None]).astype(jnp.int32), axis=1)

    def bucket_2d(rel):
        passed = jnp.sum((jnp.abs(rel)[None] >= edges[:, None, None]).astype(jnp.int32), axis=0)
        return (rel > 0).astype(jnp.int32) * half + passed

    buckets = jnp.stack([jnp.where(mask, bucket_2d(kp - qp), REL_BUCKETS), bucket_2d(kp - t - qp)], axis=0)
    table = rel_bias.astype(F32)
    tiles = pl.pallas_call(
        _bias_kernel,
        grid=(H, 2),
        in_specs=[
            pl.BlockSpec((1, t, t), lambda h, k: (k, 0, 0)),
            pl.BlockSpec(memory_space=pltpu.SMEM),
        ],
        out_specs=pl.BlockSpec((1, 1, t, t), lambda h, k: (h, k, 0, 0)),
        out_shape=jax.ShapeDtypeStruct((H, 2, t, t), F32),
        compiler_params=_params(("parallel", "parallel")),
    )(buckets, table)
    far_bucket = _t5_bucket(jnp.full((1,), -t - 1, jnp.int32))
    far = jnp.sum(jnp.where(jnp.arange(REL_BUCKETS)[:, None] == far_bucket, table, 0.0), axis=0)
    return tiles, jnp.broadcast_to((far * LOG2E)[:, None, None], (H, 1, t))


def _diff_kernel(lam_init, q_ref, k_ref, v_ref, bias_ref, far_ref, lqk_ref, g_ref, o_ref,
                 qs_scr, m_scr, l_scr, acc_scr):
    i = pl.program_id(2)
    t = q_ref.shape[0]
    hq = t // 2
    lane = lax.broadcasted_iota(jnp.int32, (1, LANES), 1)
    q = q_ref[...].astype(F32) * (LOG2E * DIFF_DH ** -0.5)
    for half in range(2):
        qh = q[half * hq:(half + 1) * hq]
        qs_scr[(2 * half) * hq:(2 * half + 1) * hq] = jnp.where(lane < DIFF_DH, qh, 0.0).astype(BF16)
        qs_scr[(2 * half + 1) * hq:(2 * half + 2) * hq] = jnp.where(lane >= DIFF_DH, qh, 0.0).astype(BF16)
    m_scr[...] = jnp.full_like(m_scr, -1e30)
    l_scr[...] = jnp.zeros_like(l_scr)
    acc_scr[...] = jnp.zeros_like(acc_scr)

    def tile(rows, ks, bias):
        s = _dot_nt(qs_scr[rows, :], k_ref[ks, :]) + bias
        groups = [s[:, c * LANES:(c + 1) * LANES] for c in range(s.shape[1] // LANES)]
        mx = functools.reduce(jnp.maximum, groups)
        m_prev = m_scr[rows, :]
        m_new = jnp.maximum(m_prev, jnp.max(mx, axis=-1, keepdims=True))
        alpha = jnp.exp2(m_prev - m_new)
        ps = [jnp.exp2(g - m_new) for g in groups]
        l_scr[rows, :] = alpha * l_scr[rows, :] + functools.reduce(jnp.add, ps)
        p = jnp.concatenate(ps, axis=1).astype(BF16)
        acc_scr[rows, :] = alpha * acc_scr[rows, :] + _dot(p, v_ref[ks, :])
        m_scr[rows, :] = m_new

    def stacked(b, half):
        bh = b[half * hq:(half + 1) * hq]
        return [bh, bh]

    all_rows = slice(0, 2 * t)

    def far_body(j, carry):
        tile(all_rows, pl.ds(pl.multiple_of(j * t, t), t), far_ref[0])
        return carry

    lax.fori_loop(0, jnp.maximum(i - 1, 0), far_body, 0)

    @pl.when(i >= 1)
    def _():
        b = bias_ref[0, 1]
        tile(all_rows, pl.ds(pl.multiple_of((i - 1) * t, t), t),
             jnp.concatenate(stacked(b, 0) + stacked(b, 1), axis=0))

    b = bias_ref[0, 0]
    diag0 = pl.multiple_of(i * t, t)
    tile(slice(0, t), pl.ds(diag0, hq), jnp.concatenate(stacked(b[:, 0:hq], 0), axis=0))
    tile(slice(t, 2 * t), pl.ds(diag0, t), jnp.concatenate(stacked(b, 1), axis=0))

    lqk = lqk_ref[...]
    lam = (jnp.exp(jnp.sum(lqk[0:1] * lqk[1:2], axis=-1, keepdims=True))
           - jnp.exp(jnp.sum(lqk[2:3] * lqk[3:4], axis=-1, keepdims=True)) + lam_init)
    o = acc_scr[...] / jnp.sum(l_scr[...], axis=-1, keepdims=True)
    o = jnp.concatenate([o[0:hq] - lam * o[hq:t], o[t:t + hq] - lam * o[t + hq:2 * t]], axis=0)
    o = o * lax.rsqrt(jnp.mean(o * o, axis=-1, keepdims=True) + EPS)
    o_ref[...] = (o * g_ref[...] * (1.0 - lam_init)).astype(o_ref.dtype)


def _diff_attention(proj, bias, lqk, subln_g, layer_idx, B, S):
    T = proj.shape[0]
    t = ATT_TILE
    nq = S // t
    tiles, far = bias
    lam_init = 0.8 - 0.6 * math.exp(-0.3 * layer_idx)
    return pl.pallas_call(
        functools.partial(_diff_kernel, lam_init),
        grid=(B, DIFF_HEADS, nq),
        in_specs=[
            pl.BlockSpec((t, LANES), lambda b, h, i: (b * nq + i, COL_DQ // LANES + h)),
            pl.BlockSpec((S, LANES), lambda b, h, i: (b, COL_DK // LANES + h)),
            pl.BlockSpec((S, LANES), lambda b, h, i: (b, COL_DV // LANES + h)),
            pl.BlockSpec((1, 2, t, t), lambda b, h, i: (h, 0, 0, 0)),
            pl.BlockSpec((1, 1, t), lambda b, h, i: (h, 0, 0)),
            pl.BlockSpec((4, DIFF_DH), lambda b, h, i: (0, 0)),
            pl.BlockSpec((1, DIFF_DV), lambda b, h, i: (0, 0)),
        ],
        out_specs=pl.BlockSpec((t, LANES), lambda b, h, i: (b * nq + i, h)),
        out_shape=jax.ShapeDtypeStruct((T, DIFF_WIDTH), BF16),
        scratch_shapes=[
            pltpu.VMEM((2 * t, LANES), BF16),
            pltpu.VMEM((2 * t, LANES), F32),
            pltpu.VMEM((2 * t, LANES), F32),
            pltpu.VMEM((2 * t, DIFF_DV), F32),
        ],
        compiler_params=_params(("parallel", "parallel", "arbitrary")),
    )(proj, proj, proj, tiles, far, lqk, subln_g.reshape(1, DIFF_DV))


def _outproj_kernel(h_ref, og_ref, ol_ref, od_ref, w_ref, g1_ref, sh2_ref, sc2_ref, n2_ref, rw_ref, rb_ref,
                    hn_ref, u2_ref, eid_ref, ew_ref, cnt_ref):
    tm = h_ref.shape[0]

    @pl.when(pl.program_id(0) == 0)
    def _():
        cnt_ref[...] = jnp.zeros_like(cnt_ref)

    nr = tm // 2
    halves = [slice(half * nr, (half + 1) * nr) for half in range(2)]
    accs = []
    for rows in halves:
        acc = _dot(og_ref[rows, :], w_ref[0:GLA_WIDTH, :])
        acc += _dot(ol_ref[rows, :], w_ref[GLA_WIDTH:GLA_WIDTH + LRU_WIDTH, :])
        acc += _dot(od_ref[rows, :], w_ref[GLA_WIDTH + LRU_WIDTH:, :])
        accs.append(acc)
    for rows, acc in zip(halves, accs):
        _outproj_rows(rows, acc, h_ref, g1_ref, sh2_ref, sc2_ref, n2_ref, rw_ref, rb_ref,
                      hn_ref, u2_ref, eid_ref, ew_ref, cnt_ref)


def _outproj_rows(rows, acc, h_ref, g1_ref, sh2_ref, sc2_ref, n2_ref, rw_ref, rb_ref,
                  hn_ref, u2_ref, eid_ref, ew_ref, cnt_ref):
    D = h_ref.shape[1]
    nr = rows.stop - rows.start
    hn = h_ref[rows, :] + g1_ref[0] * acc
    hn_ref[rows, :] = hn
    u2 = _modulated_norm(hn, n2_ref[...], sc2_ref[0], sh2_ref[0])
    _store_row_tiles(u2_ref.at[pl.ds(rows.start * SUBLANES, nr * SUBLANES)],
                     _pack_bf16_pair(u2[:, :D // 2], u2[:, D // 2:]))

    logits = _dot(u2.astype(BF16), rw_ref[...]) + rb_ref[...]
    lane = lax.broadcasted_iota(jnp.int32, logits.shape, 1)
    lane_f = lane.astype(F32)
    neg = jnp.float32(-jnp.inf)
    gmask = lane < N_GROUPS
    gl = jnp.where(gmask, logits, neg)
    gmax = jnp.max(gl, axis=-1, keepdims=True)
    gidx = jnp.min(jnp.where(gl == gmax, lane_f, float(LANES)), axis=-1, keepdims=True)
    g_w = 1.0 / jnp.sum(jnp.where(gmask, jnp.exp(gl - gmax), 0.0), axis=-1, keepdims=True)
    egroup = ((lane - N_GROUPS) >> 3).astype(F32)
    emask = (lane >= N_GROUPS) & (lane < N_GROUPS + N_EXPERTS) & (egroup == gidx)
    el = jnp.where(emask, logits, neg)
    v1 = jnp.max(el, axis=-1, keepdims=True)
    i1 = jnp.min(jnp.where(el == v1, lane_f, float(LANES)), axis=-1, keepdims=True)
    el2 = jnp.where(lane_f == i1, neg, el)
    v2 = jnp.max(el2, axis=-1, keepdims=True)
    i2 = jnp.min(jnp.where(el2 == v2, lane_f, float(LANES)), axis=-1, keepdims=True)
    e21 = jnp.exp(v2 - v1)
    w1 = g_w / (1.0 + e21)
    w2 = g_w * e21 / (1.0 + e21)
    ew_ref[rows, :] = jnp.where(lane == 0, w1, jnp.where(lane == 1, w2, 0.0))

    oh1 = lane_f == i1
    oh2 = lane_f == i2
    both = jnp.where(oh1 | oh2, 1.0, 0.0).astype(BF16)
    row = lax.broadcasted_iota(jnp.int32, (nr, nr), 0)
    col = lax.broadcasted_iota(jnp.int32, (nr, nr), 1)
    earlier = _dot(jnp.where(col < row, 1.0, 0.0).astype(BF16), both) + cnt_ref[0:1, :]
    rank1 = jnp.sum(jnp.where(oh1, earlier, 0.0), axis=-1, keepdims=True)
    rank2 = jnp.sum(jnp.where(oh2, earlier, 0.0), axis=-1, keepdims=True)
    cnt_ref[0:1, :] = cnt_ref[0:1, :] + jnp.sum(both.astype(F32), axis=0, keepdims=True)
    info = jnp.where(lane == 0, i1 - float(N_GROUPS),
                     jnp.where(lane == 1, i2 - float(N_GROUPS),
                               jnp.where(lane == 2, rank1, jnp.where(lane == 3, rank2, 0.0))))
    eid_ref[:, rows] = jnp.transpose(info)[0:SUBLANES, :].astype(jnp.int32)


def _outproj(h, o_gla, o_lru, o_diff, w_out, layer, mod3, norm2_g, rw, rb, S):
    T, D = h.shape
    tm = 512
    per_b = S // tm
    rowblk = lambda width: pl.BlockSpec((tm, width), lambda i: (i, 0))
    modblk = lambda k: pl.BlockSpec((1, 1, D), lambda i: (i // per_b, 0, k))
    full = lambda shape: pl.BlockSpec(shape, lambda i: (0,) * len(shape))
    return pl.pallas_call(
        _outproj_kernel,
        grid=(T // tm,),
        in_specs=[
            rowblk(D), rowblk(GLA_WIDTH), rowblk(LRU_WIDTH), rowblk(DIFF_WIDTH),
            pl.BlockSpec((None, D, D), lambda i: (layer, 0, 0)),
            modblk(2), modblk(3), modblk(4),
            full((1, D)),
            full((D, LANES)),
            full((1, LANES)),
        ],
        out_specs=[rowblk(D), pl.BlockSpec((tm * SUBLANES, LANES), lambda i: (i, 0)),
                   pl.BlockSpec((SUBLANES, tm), lambda i: (0, i)), rowblk(LANES),
                   full((SUBLANES, LANES))],
        out_shape=[
            jax.ShapeDtypeStruct((T, D), F32),
            jax.ShapeDtypeStruct((T * SUBLANES, LANES), U32),
            jax.ShapeDtypeStruct((SUBLANES, T), jnp.int32),
            jax.ShapeDtypeStruct((T, LANES), F32),
            jax.ShapeDtypeStruct((SUBLANES, LANES), F32),
        ],
        compiler_params=_params(("arbitrary",)),
    )(h, o_gla, o_lru, o_diff, w_out, mod3, mod3, mod3, norm2_g.reshape(1, D), rw, rb)


def _dispatch(info, counts):
    T = info.shape[1]
    blk = MOE_BLK
    n_blocks = (T * TOP_K) // blk + N_EXPERTS
    padded = (counts + blk - 1) // blk * blk
    pends = jnp.cumsum(padded)
    pstarts = pends - padded
    ustarts = jnp.cumsum(counts) - counts
    n_used = (pends[-1] // blk).astype(jnp.int32)
    block_idx = jnp.arange(n_blocks, dtype=jnp.int32)
    block_expert = jnp.minimum(jnp.sum((pends[None, :] <= (block_idx * blk)[:, None]).astype(jnp.int32), axis=1),
                               N_EXPERTS - 1)
    last_used = jnp.sum(jnp.where(block_idx == jnp.maximum(n_used - 1, 0), block_expert, 0))
    block_expert = jnp.where(block_idx < n_used, block_expert, last_used).astype(jnp.int32)
    following = jnp.concatenate([block_expert[1:], jnp.full((1,), -1, jnp.int32)])
    zero_block = ((block_idx >= n_used - 1) | (following != block_expert)).astype(jnp.int32)
    owner = block_expert[:, None] == jnp.arange(N_EXPERTS, dtype=jnp.int32)[None, :]
    seg_end = jnp.sum(jnp.where(owner, (pstarts + counts)[None, :], 0), axis=1)
    n_valid = jnp.where(block_idx < n_used, jnp.clip(seg_end - block_idx * blk, 0, blk), 0).astype(jnp.int32)
    packed_base = (jnp.sum(jnp.where(owner, (ustarts - pstarts)[None, :], 0), axis=1) + block_idx * blk)
    packed_base = jnp.where(n_valid > 0, packed_base, 0).astype(jnp.int32)
    return dict(experts=(info[0], info[1]), ranks=(info[2], info[3]),
                slot_start=pstarts.astype(jnp.int32), packed_start=ustarts.astype(jnp.int32), zero_block=zero_block,
                block_expert=block_expert, n_used=n_used.reshape(1), n_valid=n_valid, packed_base=packed_base)


def _scatter_kernel(e0_ref, e1_ref, r0_ref, r1_ref, ss_ref, ps_ref, zb_ref, src_ref, o_ref, inv_ref,
                    zero_buf, sem, zero_sem):
    rows = src_ref.shape[0] // SUBLANES
    base = pl.program_id(0) * rows
    fill_rows = zero_buf.shape[0]
    n_blocks = o_ref.shape[0] // fill_rows

    @pl.when(pl.program_id(0) == 0)
    def _():
        zero_buf[...] = jnp.zeros_like(zero_buf)

        def for_each_fill(fn):
            def body(j, carry):
                @pl.when(zb_ref[j] == 1)
                def _():
                    fn(pltpu.make_async_copy(
                        zero_buf, o_ref.at[pl.ds(pl.multiple_of(j * fill_rows, fill_rows), fill_rows)], zero_sem))
                return carry
            lax.fori_loop(0, n_blocks, body, 0)

        for_each_fill(lambda copy: copy.start())
        for_each_fill(lambda copy: copy.wait())

    def row_copy(r, slot):
        return pltpu.make_async_copy(src_ref.at[pl.ds(pl.multiple_of(r * SUBLANES, SUBLANES), SUBLANES)],
                                     o_ref.at[pl.ds(pl.multiple_of(slot * SUBLANES, SUBLANES), SUBLANES)], sem)

    def issue(r, carry):
        t = base + r
        for k, (e_ref, r_ref) in enumerate(((e0_ref, r0_ref), (e1_ref, r1_ref))):
            expert = e_ref[t]
            rank = r_ref[t]
            row_copy(r, ss_ref[expert] + rank).start()
            inv_ref[ps_ref[expert] + rank] = t * TOP_K + k
        return carry

    lax.fori_loop(0, rows, issue, 0, unroll=8)
    for _ in range(TOP_K):
        pltpu.make_async_copy(src_ref, o_ref.at[pl.ds(0, rows * SUBLANES)], sem).wait()


def _scatter_rows(plan, src):
    T = src.shape[0] // SUBLANES
    P = T * TOP_K + N_EXPERTS * MOE_BLK
    rows = GATHER_ROWS
    return pl.pallas_call(
        _scatter_kernel,
        grid_spec=pltpu.PrefetchScalarGridSpec(
            num_scalar_prefetch=7,
            grid=(T // rows,),
            in_specs=[pl.BlockSpec((rows * SUBLANES, LANES), lambda i, *refs: (i, 0))],
            out_specs=[pl.BlockSpec(memory_space=pl.ANY), pl.BlockSpec(memory_space=pltpu.SMEM)],
            scratch_shapes=[
                pltpu.VMEM((MOE_BLK * SUBLANES, LANES), src.dtype),
                pltpu.SemaphoreType.DMA(()),
                pltpu.SemaphoreType.DMA(()),
            ],
        ),
        out_shape=[jax.ShapeDtypeStruct((P * SUBLANES, LANES), src.dtype),
                   jax.ShapeDtypeStruct((T * TOP_K,), jnp.int32)],
        compiler_params=_params(("arbitrary",)),
    )(*plan["experts"], *plan["ranks"], plan["slot_start"], plan["packed_start"], plan["zero_block"], src)


def _expert_kernel(layer, be_ref, first_ref, next_ref, slot_ref, nu_ref, nv_ref, pb_ref, inv_ref,
                   xs_ref, w1_hbm, w3_hbm, w2_hbm, yt_ref,
                   w1f, w3f, w2f, w1b, w3b, w2b, ybuf, sems, ysems):
    i = pl.program_id(0)
    D = w1b.shape[0]
    blk = xs_ref.shape[0] // SUBLANES

    def drain_rows(j):
        n = nv_ref[j]
        b = j % 2
        for bit in range(blk.bit_length()):
            size = (1 << bit) * SUBLANES

            @pl.when((n >> bit) & 1 == 1)
            def _():
                pltpu.make_async_copy(ybuf.at[b, pl.ds(0, size)], yt_ref.at[pl.ds(0, size)], ysems.at[b]).wait()

    def weight_copies(e, slot):
        return (pltpu.make_async_copy(w1_hbm.at[layer, e], w1f.at[slot], sems.at[slot, 0]),
                pltpu.make_async_copy(w3_hbm.at[layer, e], w3f.at[slot], sems.at[slot, 1]),
                pltpu.make_async_copy(w2_hbm.at[layer, e], w2f.at[slot], sems.at[slot, 2]))

    @pl.when(i == 0)
    def _():
        for c in weight_copies(be_ref[0], 0):
            c.start()

    @pl.when(first_ref[i] == 1)
    def _():
        slot = slot_ref[i]
        for c in weight_copies(be_ref[i], slot):
            c.wait()

        @pl.when(next_ref[i] >= 0)
        def _():
            for c in weight_copies(next_ref[i], 1 - slot):
                c.start()

        w1b[...] = w1f[slot].astype(BF16)
        w3b[...] = w3f[slot].astype(BF16)
        w2b[...] = w2f[slot].astype(BF16)

    @pl.when(i >= 2)
    def _():
        drain_rows(i - 2)

    @pl.when(i < nu_ref[0])
    def _():
        b = i % 2
        n = nv_ref[i]

        def swiglu(n_rows):
            tiles = pl.ds(0, n_rows * SUBLANES)
            lo, hi = _unpack_bf16_pair(_load_row_tiles(xs_ref.at[tiles]))
            lo = lo.astype(BF16)
            hi = hi.astype(BF16)
            a = _dot(lo, w1b[0:D // 2, :]) + _dot(hi, w1b[D // 2:, :])
            g = _dot(lo, w3b[0:D // 2, :]) + _dot(hi, w3b[D // 2:, :])
            hid = ((a * _sigmoid(a)) * g).astype(BF16)
            _store_row_tiles(ybuf.at[b, tiles],
                             _pack_bf16_pair(_dot(hid, w2b[:, 0:D // 2]), _dot(hid, w2b[:, D // 2:])))

        @pl.when(n > blk // 2)
        def _():
            swiglu(blk)

        @pl.when(n <= blk // 2)
        def _():
            swiglu(blk // 2)

        first_pos = pb_ref[i]
        unroll = 8

        def issue_row(r):
            a_idx = inv_ref[first_pos + r]
            pltpu.make_async_copy(ybuf.at[b, pl.ds(pl.multiple_of(r * SUBLANES, SUBLANES), SUBLANES)],
                                  yt_ref.at[pl.ds(pl.multiple_of(a_idx * SUBLANES, SUBLANES), SUBLANES)],
                                  ysems.at[b]).start()

        def issue_group(g, carry):
            for u in range(unroll):
                issue_row(g * unroll + u)
            return carry

        def issue_one(r, carry):
            issue_row(r)
            return carry

        lax.fori_loop(0, n // unroll, issue_group, 0)
        lax.fori_loop((n // unroll) * unroll, n, issue_one, 0)

    @pl.when(i == pl.num_programs(0) - 1)
    def _():
        drain_rows(i - 1)
        drain_rows(i)


def _segment_plan(block_expert, n_used):
    n = block_expert.shape[0]
    idx = jnp.arange(n, dtype=jnp.int32)
    prev = jnp.concatenate([jnp.full((1,), -1, jnp.int32), block_expert[:-1]])
    first = ((block_expert != prev) & (idx < n_used[0])).astype(jnp.int32)
    slot = (jnp.cumsum(first) - 1) % 2
    later_first = jnp.where(first == 1, idx, n)
    next_idx = lax.cummin(jnp.concatenate([later_first[1:], jnp.full((1,), n, jnp.int32)]), reverse=True)
    next_expert = jnp.where(next_idx < n, block_expert[jnp.minimum(next_idx, n - 1)], -1)
    return first, next_expert.astype(jnp.int32), slot.astype(jnp.int32)


def _experts(plan, inv, xs, w1, w3, w2, layer):
    _, _, D, DE = w1.shape
    blk = MOE_BLK
    block_expert, n_used = plan["block_expert"], plan["n_used"]
    n_blocks = block_expert.shape[0]
    n_assign = inv.shape[0]
    first, next_expert, slot = _segment_plan(block_expert, n_used)
    rowmap = lambda i, *refs: (jnp.minimum(i, jnp.maximum(refs[4][0] - 1, 0)), 0)
    hbm = pl.BlockSpec(memory_space=pl.ANY)
    return pl.pallas_call(
        functools.partial(_expert_kernel, layer),
        grid_spec=pltpu.PrefetchScalarGridSpec(
            num_scalar_prefetch=8,
            grid=(n_blocks,),
            in_specs=[pl.BlockSpec((blk * SUBLANES, LANES), rowmap), hbm, hbm, hbm],
            out_specs=hbm,
            scratch_shapes=[
                pltpu.VMEM((2, D, DE), F32),
                pltpu.VMEM((2, D, DE), F32),
                pltpu.VMEM((2, DE, D), F32),
                pltpu.VMEM((D, DE), BF16),
                pltpu.VMEM((D, DE), BF16),
                pltpu.VMEM((DE, D), BF16),
                pltpu.VMEM((2, blk * SUBLANES, LANES), U32),
                pltpu.SemaphoreType.DMA((2, 3)),
                pltpu.SemaphoreType.DMA((2,)),
            ],
        ),
        out_shape=jax.ShapeDtypeStruct((n_assign * SUBLANES, LANES), U32),
        compiler_params=_params(("arbitrary",)),
    )(block_expert, first, next_expert, slot, n_used, plan["n_valid"], plan["packed_base"], inv, xs, w1, w3, w2)


def _final_combine_kernel(yt_ref, h_ref, g2_ref, ew_ref, fg_ref, o_ref):
    hn = h_ref[...] + g2_ref[0] * _moe_mix(yt_ref, ew_ref)
    o_ref[...] = hn * lax.rsqrt(jnp.mean(hn * hn, axis=-1, keepdims=True) + EPS) * fg_ref[...]


def _final_combine(yt, h, mod3, ew, final_g, S):
    T, D = h.shape
    tc = 256
    per_b = S // tc
    return pl.pallas_call(
        _final_combine_kernel,
        grid=(T // tc,),
        in_specs=[
            pl.BlockSpec((tc * TOP_K * SUBLANES, LANES), lambda i: (i, 0)),
            pl.BlockSpec((tc, D), lambda i: (i, 0)),
            pl.BlockSpec((1, 1, D), lambda i: (i // per_b, 0, 5)),
            pl.BlockSpec((tc, LANES), lambda i: (i, 0)),
            pl.BlockSpec((1, D), lambda i: (0, 0)),
        ],
        out_specs=pl.BlockSpec((tc, D), lambda i: (i, 0)),
        out_shape=jax.ShapeDtypeStruct((T, D), F32),
        compiler_params=_params(("parallel",)),
    )(yt, h, mod3, ew, final_g.reshape(1, D))


def kernel(x, c, ada_w, ada_b, norm1_g, w_in, gla_w_a2, gla_b_a, gla_norm_g, lru_conv_w, lru_conv_b,
           lru_wa, lru_ba, lru_wx, lru_bx, lru_lambda, diff_lq1, diff_lk1, diff_lq2, diff_lk2,
           diff_subln_g, rel_bias, w_out, norm2_g, router_g_w, router_g_b, router_e_w, router_e_b,
           moe_w1, moe_w3, moe_w2, final_g):
    B, S, D = x.shape
    T = B * S
    L = ada_w.shape[0]
    h = x.reshape(T, D)
    mod = _ada_mod(c, ada_w, ada_b)
    bias = _bias_tiles(rel_bias)
    w_in_perm = _permute_w_in(w_in)
    w_out_bf16 = w_out.astype(BF16)
    pending_moe = None
    for l in range(L):
        mod3 = mod[l][:, None, :]
        if pending_moe is None:
            proj = _inproj(h, mod3, norm1_g[l], w_in_perm, l, S)
        else:
            proj, h = _inproj(h, mod3, norm1_g[l], w_in_perm, l, S, pending_moe)
        wa2_pad = jnp.concatenate(
            [gla_w_a2[l], jnp.zeros((LANES - GLA_LOWRANK, GLA_KEY_WIDTH), F32)], axis=0).astype(BF16)
        o_gla = _gla(proj, wa2_pad, gla_b_a[l], gla_norm_g[l], B, S)
        w_gates = jnp.concatenate([_block_diag(lru_wa[l]), _block_diag(lru_wx[l])], axis=1).astype(BF16)
        b_gates = jnp.concatenate([lru_ba[l], lru_bx[l]]).reshape(1, 2 * LRU_WIDTH)
        o_lru = _lru(proj.reshape(B, S, PROJ_WIDTH), lru_conv_w[l], lru_conv_b[l], w_gates, b_gates,
                     lru_lambda[l]).reshape(T, LRU_WIDTH)
        lqk = jnp.stack([diff_lq1[l], diff_lk1[l], diff_lq2[l], diff_lk2[l]], axis=0)
        o_diff = _diff_attention(proj, bias, lqk, diff_subln_g[l], l, B, S)
        rw = jnp.concatenate(
            [router_g_w[l], router_e_w[l], jnp.zeros((D, LANES - N_GROUPS - N_EXPERTS), F32)], axis=1).astype(BF16)
        rb = jnp.concatenate(
            [router_g_b[l], router_e_b[l], jnp.zeros((LANES - N_GROUPS - N_EXPERTS,), F32)]).reshape(1, LANES)
        h, u2, info, ew, cnt = _outproj(h, o_gla, o_lru, o_diff, w_out_bf16, l, mod3, norm2_g[l], rw, rb, S)
        counts = cnt[0, N_GROUPS:N_GROUPS + N_EXPERTS].astype(jnp.int32)
        plan = _dispatch(info, counts)
        xs, inv = _scatter_rows(plan, u2)
        yt = _experts(plan, inv, xs, moe_w1, moe_w3, moe_w2, l)
        pending_moe = (yt, ew, mod3)
    out = _final_combine(yt, h, mod3, ew, final_g, S)
    return out.reshape(B, S, D)
```

```python
import functools
import math

import jax
import jax.numpy as jnp
from jax import lax
from jax.experimental import pallas as pl
from jax.experimental.pallas import tpu as pltpu

F32 = jnp.float32
BF16 = jnp.bfloat16
U32 = jnp.uint32

EPS = 1e-6
LOG2E = math.log2(math.e)
CHUNK = 64

GLA_DV = 128
GLA_DK = 64
GLA_HEADS = 6
GLA_WIDTH = GLA_HEADS * GLA_DV
GLA_KEY_WIDTH = GLA_HEADS * GLA_DK
GLA_LOWRANK = 16
GLA_TAU = 16.0

LRU_WIDTH = 512
LRU_BLOCKS = 8
LRU_BLOCK_DIM = LRU_WIDTH // LRU_BLOCKS
CONV_WIDTH = 4
LRU_C = 8.0

DIFF_DH = 64
DIFF_DV = 128
DIFF_HEADS = 6
DIFF_WIDTH = DIFF_HEADS * DIFF_DV

REL_BUCKETS = 32
REL_MAX_DIST = 128

N_GROUPS = 8
EXPERTS_PER_GROUP = 8
N_EXPERTS = 64
TOP_K = 2

LANES = 128
SUBLANES = 8
VMEM_LIMIT = 56 * 1024 * 1024

COL_GV = 0
COL_GOG = 768
COL_DQ = 1536
COL_DK = 2304
COL_DV = 3072
COL_GQ = 3840
COL_GK = 4224
COL_LY = 4608
COL_LX = 5120
COL_GA = 5632
PROJ_WIDTH = 5760

ATT_TILE = 512
GLA_TILE = 256
LRU_TILE = 256
MOE_BLK = 256
GATHER_ROWS = 2048


def _params(sem, vmem=VMEM_LIMIT):
    return pltpu.CompilerParams(dimension_semantics=sem, vmem_limit_bytes=vmem)


def _sigmoid(x):
    return 0.5 * jnp.tanh(0.5 * x) + 0.5


def _softplus(x):
    return jnp.maximum(x, 0.0) + jnp.log1p(jnp.exp(-jnp.abs(x)))


def _dot(a, b):
    return jnp.dot(a, b, preferred_element_type=F32)


def _dot_nt(a, b):
    return lax.dot_general(a, b, (((1,), (1,)), ((), ())), preferred_element_type=F32)


def _dot_tn(a, b):
    return lax.dot_general(a, b, (((0,), (0,)), ((), ())), preferred_element_type=F32)


def _pack_bf16_pair(lo, hi):
    lo_bits = lax.bitcast_convert_type(lo.astype(BF16).astype(F32), U32)
    hi_bits = lax.bitcast_convert_type(hi.astype(BF16).astype(F32), U32)
    return (hi_bits & jnp.uint32(0xFFFF0000)) | (lo_bits >> 16)


def _unpack_bf16_pair(w):
    lo = lax.bitcast_convert_type(w << 16, F32)
    hi = lax.bitcast_convert_type(w & jnp.uint32(0xFFFF0000), F32)
    return lo, hi


def _store_row_tiles(ref, words):
    rows = words.shape[0]
    for s in range(SUBLANES):
        ref[pl.ds(s, rows, stride=SUBLANES), :] = words[:, s * LANES:(s + 1) * LANES]


def _load_row_tiles(ref):
    rows = ref.shape[0] // SUBLANES
    return jnp.concatenate([ref[pl.ds(s, rows, stride=SUBLANES), :] for s in range(SUBLANES)], axis=1)


def _ada_kernel(c_ref, w_ref, b_ref, o_ref):
    c = c_ref[...]
    s = c * _sigmoid(c)
    o_ref[0] = _dot(s.astype(BF16), w_ref[0].astype(BF16)) + b_ref[0]


def _ada_mod(c, ada_w, ada_b):
    L, D, N = ada_w.shape
    B = c.shape[0]
    tn = 1024
    return pl.pallas_call(
        _ada_kernel,
        grid=(L, N // tn),
        in_specs=[
            pl.BlockSpec((B, D), lambda l, j: (0, 0)),
            pl.BlockSpec((1, D, tn), lambda l, j: (l, 0, j)),
            pl.BlockSpec((1, 1, tn), lambda l, j: (l, 0, j)),
        ],
        out_specs=pl.BlockSpec((1, B, tn), lambda l, j: (l, 0, j)),
        out_shape=jax.ShapeDtypeStruct((L, B, N), F32),
        compiler_params=_params(("parallel", "parallel")),
    )(c, ada_w, ada_b.reshape(L, 1, N))


def _modulated_norm(x, g, sc, sh):
    ms = jnp.mean(x * x, axis=-1, keepdims=True)
    return (x * lax.rsqrt(ms + EPS) * g) * (1.0 + sc) + sh


def _moe_mix(yt_ref, ew_ref):
    rows = ew_ref.shape[0]

    def expert_rows(k):
        return jnp.concatenate([yt_ref[pl.ds(k * SUBLANES + s, rows, stride=TOP_K * SUBLANES), :]
                                for s in range(SUBLANES)], axis=1)

    ew = ew_ref[...]
    w0 = ew[:, 0:1]
    w1 = ew[:, 1:2]
    lo0, hi0 = _unpack_bf16_pair(expert_rows(0))
    lo1, hi1 = _unpack_bf16_pair(expert_rows(1))
    return jnp.concatenate([w0 * lo0 + w1 * lo1, w0 * hi0 + w1 * hi1], axis=1)


def _inproj_kernel(h_ref, sh_ref, sc_ref, g_ref, w_ref, o_ref, u_scr):
    tm = h_ref.shape[0]
    nr = tm // 2

    @pl.when(pl.program_id(1) == 0)
    def _():
        for half in range(2):
            rows = slice(half * nr, (half + 1) * nr)
            u = _modulated_norm(h_ref[rows, :], g_ref[...], sc_ref[0], sh_ref[0]).astype(BF16)
            u_scr[rows, :] = u
            o_ref[rows, :] = _dot(u, w_ref[...]).astype(o_ref.dtype)

    @pl.when(pl.program_id(1) > 0)
    def _():
        o_ref[...] = _dot(u_scr[...], w_ref[...]).astype(o_ref.dtype)


def _inproj_after_moe_kernel(h_ref, yt_ref, ew_ref, g2_ref, sh_ref, sc_ref, g_ref, w_ref, o_ref, hn_ref, u_scr):
    tm = h_ref.shape[0]
    nr = tm // 2

    @pl.when(pl.program_id(1) == 0)
    def _():
        for half in range(2):
            rows = slice(half * nr, (half + 1) * nr)
            tiles = pl.ds(half * nr * TOP_K * SUBLANES, nr * TOP_K * SUBLANES)
            hn = h_ref[rows, :] + g2_ref[0] * _moe_mix(yt_ref.at[tiles], ew_ref.at[rows])
            hn_ref[rows, :] = hn
            u = _modulated_norm(hn, g_ref[...], sc_ref[0], sh_ref[0]).astype(BF16)
            u_scr[rows, :] = u
            o_ref[rows, :] = _dot(u, w_ref[...]).astype(o_ref.dtype)

    @pl.when(pl.program_id(1) > 0)
    def _():
        o_ref[...] = _dot(u_scr[...], w_ref[...]).astype(o_ref.dtype)


def _inproj(h, mod3, norm_g, w_perm, layer, S, pending_moe=None):
    T, D = h.shape
    N = w_perm.shape[2]
    tm, tn = 512, 1152
    per_b = S // tm
    rows = pl.BlockSpec((tm, D), lambda i, j: (i, 0))
    mod_specs = [
        pl.BlockSpec((1, 1, D), lambda i, j: (i // per_b, 0, 0)),
        pl.BlockSpec((1, 1, D), lambda i, j: (i // per_b, 0, 1)),
        pl.BlockSpec((1, D), lambda i, j: (0, 0)),
        pl.BlockSpec((None, D, tn), lambda i, j: (layer, 0, j)),
    ]
    proj_spec = pl.BlockSpec((tm, tn), lambda i, j: (i, j))
    proj_shape = jax.ShapeDtypeStruct((T, N), BF16)
    common = dict(grid=(T // tm, N // tn), scratch_shapes=[pltpu.VMEM((tm, D), BF16)],
                  compiler_params=_params(("parallel", "arbitrary")))
    if pending_moe is None:
        return pl.pallas_call(_inproj_kernel, in_specs=[rows] + mod_specs, out_specs=proj_spec,
                              out_shape=proj_shape, **common)(h, mod3, mod3, norm_g.reshape(1, D), w_perm)
    yt, ew, prev_mod3 = pending_moe
    moe_specs = [
        pl.BlockSpec((tm * TOP_K * SUBLANES, LANES), lambda i, j: (i, 0)),
        pl.BlockSpec((tm, LANES), lambda i, j: (i, 0)),
        pl.BlockSpec((1, 1, D), lambda i, j: (i // per_b, 0, 5)),
    ]
    return pl.pallas_call(
        _inproj_after_moe_kernel, in_specs=[rows] + moe_specs + mod_specs, out_specs=[proj_spec, rows],
        out_shape=[proj_shape, jax.ShapeDtypeStruct((T, D), F32)], **common,
    )(h, yt, ew, prev_mod3, mod3, mod3, norm_g.reshape(1, D), w_perm)


W_IN_SEGMENTS = ((COL_GQ, 0, 384), (COL_GK, 384, 384), (COL_GV, 768, 768), (COL_GOG, 1536, 768),
                 (COL_GA, 2304, GLA_LOWRANK), (COL_LY, 2320, 512), (COL_LX, 2832, 512),
                 (COL_DQ, 3344, 768), (COL_DK, 4112, 768), (COL_DV, 4880, 768))


def _relayout_kernel(w_ref, o_ref):
    x = w_ref[0]
    for dst, src, width in W_IN_SEGMENTS:
        o_ref[0, :, dst:dst + width] = x[:, src:src + width]
    pad = slice(COL_GA + GLA_LOWRANK, COL_GA + LANES)
    o_ref[0, :, pad] = jnp.zeros((x.shape[0], LANES - GLA_LOWRANK), BF16)


def _permute_w_in(w):
    L, D, N = w.shape
    rt = 256
    return pl.pallas_call(
        _relayout_kernel,
        grid=(L, D // rt),
        in_specs=[pl.BlockSpec((1, rt, N), lambda l, i: (l, i, 0))],
        out_specs=pl.BlockSpec((1, rt, PROJ_WIDTH), lambda l, i: (l, i, 0)),
        out_shape=jax.ShapeDtypeStruct((L, D, PROJ_WIDTH), BF16),
        compiler_params=_params(("parallel", "parallel")),
    )(w.astype(BF16))


def _gla_kernel(q_ref, k_ref, v_ref, og_ref, alr_ref, wa2_ref, ba_ref, ng_ref, o_ref, st_ref):
    tb = q_ref.shape[0]
    n_chunks = tb // CHUNK

    @pl.when(pl.program_id(1) == 0)
    def _():
        st_ref[...] = jnp.zeros_like(st_ref)

    row = lax.broadcasted_iota(jnp.int32, (tb, tb), 0)
    col = lax.broadcasted_iota(jnp.int32, (tb, tb), 1)
    same_chunk = (row // CHUNK) == (col // CHUNK)
    causal = col <= row
    tril = jnp.where(same_chunk & causal, 1.0, 0.0).astype(BF16)
    lane = lax.broadcasted_iota(jnp.int32, (1, LANES), 1)
    half_masks = (lane < GLA_DK, lane >= GLA_DK)

    alr = alr_ref[...]
    cols = [slice(p * LANES, (p + 1) * LANES) for p in range(GLA_HEADS // 2)]
    z = [_dot(alr, wa2_ref[:, cs]) + ba_ref[:, cs] for cs in cols]
    la = [(jnp.minimum(zp, 0.0) - jnp.log1p(jnp.exp(-jnp.abs(zp)))) * (1.0 / GLA_TAU) for zp in z]
    la_hi = [x.astype(BF16) for x in la]
    la_lo = [(x - h.astype(F32)).astype(BF16) for x, h in zip(la, la_hi)]
    G = [_dot(tril, h) + _dot(tril, lo) for h, lo in zip(la_hi, la_lo)]
    Gl = [jnp.concatenate([jnp.broadcast_to(g[(c + 1) * CHUNK - 1:(c + 1) * CHUNK, :], (CHUNK, LANES))
                           for c in range(n_chunks)], axis=0) for g in G]
    eG = [jnp.exp(g) for g in G]
    enG = [jnp.exp(-g) for g in G]
    q = [q_ref[:, cs].astype(F32) * (GLA_DK ** -0.5) for cs in cols]
    k = [k_ref[:, cs].astype(F32) for cs in cols]
    kf = [(kp * e).astype(BF16) for kp, e in zip(k, eG)]
    kb = [(kp * e).astype(BF16) for kp, e in zip(k, enG)]
    kd = [kp * jnp.exp(gl - g) for kp, gl, g in zip(k, Gl, G)]

    heads = range(GLA_HEADS)
    pair = [h // 2 for h in heads]
    mask = [half_masks[h % 2] for h in heads]
    vcols = [slice(h * GLA_DV, (h + 1) * GLA_DV) for h in heads]
    qf_h = [jnp.where(mask[h], q[pair[h]] * eG[pair[h]], 0.0).astype(BF16) for h in heads]
    qb_h = [jnp.where(mask[h], q[pair[h]] * enG[pair[h]], 0.0).astype(BF16) for h in heads]
    kd_h = [jnp.where(mask[h], kd[pair[h]], 0.0).astype(BF16) for h in heads]
    v_h = [v_ref[:, vcols[h]] for h in heads]
    a_f = [_dot_nt(qf_h[h], kb[pair[h]]) for h in heads]
    a_b = [_dot_nt(qb_h[h], kf[pair[h]]) for h in heads]
    attn = [jnp.where(same_chunk, jnp.where(causal, a_f[h], a_b[h]), 0.0).astype(BF16) for h in heads]
    o_intra = [_dot(attn[h], v_h[h]) for h in heads]
    chunk_rows = [slice(c * CHUNK, (c + 1) * CHUNK) for c in range(n_chunks)]
    kv = [[_dot_tn(v_h[h][rs], kd_h[h][rs]) for rs in chunk_rows] for h in heads]
    for h in heads:
        st = st_ref[h]
        inter = []
        for c, rs in enumerate(chunk_rows):
            inter.append(_dot_nt(qf_h[h][rs], st.astype(BF16)))
            decay = jnp.exp(Gl[pair[h]][c * CHUNK:c * CHUNK + 1, :])
            st = st * decay + kv[h][c]
        st_ref[h] = st
        o = o_intra[h] + jnp.concatenate(inter, axis=0)
        o = o * lax.rsqrt(jnp.mean(o * o, axis=-1, keepdims=True) + EPS)
        og = og_ref[:, vcols[h]].astype(F32)
        o_ref[:, vcols[h]] = (o * ng_ref[:, vcols[h]] * (og * _sigmoid(og))).astype(o_ref.dtype)


def _gla(proj, wa2_pad, b_a, norm_g, B, S):
    T = proj.shape[0]
    tb = GLA_TILE
    nt = S // tb
    row = lambda b, i: b * nt + i
    return pl.pallas_call(
        _gla_kernel,
        grid=(B, nt),
        in_specs=[
            pl.BlockSpec((tb, GLA_KEY_WIDTH), lambda b, i: (row(b, i), COL_GQ // GLA_KEY_WIDTH)),
            pl.BlockSpec((tb, GLA_KEY_WIDTH), lambda b, i: (row(b, i), COL_GK // GLA_KEY_WIDTH)),
            pl.BlockSpec((tb, GLA_WIDTH), lambda b, i: (row(b, i), COL_GV // GLA_WIDTH)),
            pl.BlockSpec((tb, GLA_WIDTH), lambda b, i: (row(b, i), COL_GOG // GLA_WIDTH)),
            pl.BlockSpec((tb, LANES), lambda b, i: (row(b, i), COL_GA // LANES)),
            pl.BlockSpec((LANES, GLA_KEY_WIDTH), lambda b, i: (0, 0)),
            pl.BlockSpec((1, GLA_KEY_WIDTH), lambda b, i: (0, 0)),
            pl.BlockSpec((1, GLA_WIDTH), lambda b, i: (0, 0)),
        ],
        out_specs=pl.BlockSpec((tb, GLA_WIDTH), lambda b, i: (row(b, i), 0)),
        out_shape=jax.ShapeDtypeStruct((T, GLA_WIDTH), BF16),
        scratch_shapes=[pltpu.VMEM((GLA_HEADS, GLA_DV, LANES), F32)],
        compiler_params=_params(("parallel", "arbitrary")),
    )(proj, proj, proj, proj, proj, wa2_pad, b_a.reshape(1, -1), norm_g.reshape(1, -1))


def _lru_kernel(y_ref, x_ref, cw_ref, cb_ref, wg_ref, bg_ref, lam_ref, o_ref, *scratch):
    B, ts, W = x_ref.shape
    n_planes = W // LANES
    a_scr = scratch[0:n_planes]
    b_scr = scratch[n_planes:2 * n_planes]
    h_scr = scratch[2 * n_planes:3 * n_planes]
    xc_scr, tail_scr, carry_scr = scratch[3 * n_planes:]

    @pl.when(pl.program_id(0) == 0)
    def _():
        tail_scr[...] = jnp.zeros_like(tail_scr)
        carry_scr[...] = jnp.zeros_like(carry_scr)

    cw = cw_ref[...]
    cb = cb_ref[...]
    sp = _softplus(-lam_ref[...])
    row8 = lax.broadcasted_iota(jnp.int32, (8, W), 0)
    for b in range(B):
        x = x_ref[b].astype(F32)
        tail = tail_scr[b]
        xc = cb + cw[CONV_WIDTH - 1:CONV_WIDTH, :] * x
        head = cb + cw[CONV_WIDTH - 1:CONV_WIDTH, :] * x[0:8]
        for d in range(1, CONV_WIDTH):
            wd = cw[CONV_WIDTH - 1 - d:CONV_WIDTH - d, :]
            xr = pltpu.roll(x, d, 0)
            xc = xc + wd * xr
            head = head + wd * jnp.where(row8 < d, pltpu.roll(tail, d, 0), xr[0:8])
        tail_scr[b] = x[ts - 8:ts]
        xc_scr[...] = xc
        xc_scr[0:8] = head
        xc = xc_scr[...]
        gates = _sigmoid(_dot(xc.astype(BF16), wg_ref[...]) + bg_ref[...])
        r = gates[:, :W]
        ig = gates[:, W:]
        log_a = (-LRU_C) * r * sp
        a = jnp.exp(log_a)
        b_in = jnp.sqrt(-jnp.tanh(log_a) * (a * a + 1.0)) * (ig * xc)
        rows = slice(b * ts, (b + 1) * ts)
        for k in range(n_planes):
            a_scr[k][rows] = a[:, k * LANES:(k + 1) * LANES]
            b_scr[k][rows] = b_in[:, k * LANES:(k + 1) * LANES]

    def step(t, hs):
        idx = pl.ds(t, B, stride=ts)
        out = []
        for k in range(n_planes):
            hk = a_scr[k][idx, :] * hs[k] + b_scr[k][idx, :]
            h_scr[k][idx, :] = hk
            out.append(hk)
        return tuple(out)

    hs = lax.fori_loop(0, ts, step, tuple(carry_scr[k] for k in range(n_planes)), unroll=8)
    for k in range(n_planes):
        carry_scr[k] = hs[k]

    for b in range(B):
        rows = slice(b * ts, (b + 1) * ts)
        y = y_ref[b].astype(F32)
        gelu = 0.5 * y * (1.0 + jnp.tanh(math.sqrt(2.0 / math.pi) * (y + 0.044715 * (y * y * y))))
        h = jnp.concatenate([h_scr[k][rows] for k in range(n_planes)], axis=1)
        o_ref[b] = (h * gelu).astype(o_ref.dtype)


def _lru(proj3, conv_w, conv_b, w_gates, b_gates, lam):
    B, S, _ = proj3.shape
    W = LRU_WIDTH
    ts = LRU_TILE
    n_planes = W // LANES
    full = lambda shape: pl.BlockSpec(shape, lambda i: (0,) * len(shape))
    return pl.pallas_call(
        _lru_kernel,
        grid=(S // ts,),
        in_specs=[
            pl.BlockSpec((B, ts, W), lambda i: (0, i, COL_LY // W)),
            pl.BlockSpec((B, ts, W), lambda i: (0, i, COL_LX // W)),
            full((CONV_WIDTH, W)),
            full((1, W)),
            full((W, 2 * W)),
            full((1, 2 * W)),
            full((1, W)),
        ],
        out_specs=pl.BlockSpec((B, ts, W), lambda i: (0, i, 0)),
        out_shape=jax.ShapeDtypeStruct((B, S, W), BF16),
        scratch_shapes=(
            [pltpu.VMEM((B * ts, LANES), F32) for _ in range(3 * n_planes)]
            + [pltpu.VMEM((ts, W), F32), pltpu.VMEM((B, 8, W), F32), pltpu.VMEM((n_planes, B, LANES), F32)]),
        compiler_params=_params(("arbitrary",)),
    )(proj3, proj3, conv_w, conv_b.reshape(1, W), w_gates, b_gates, lam.reshape(1, W))


def _block_diag(w):
    n, d, _ = w.shape
    eye = jnp.eye(n, dtype=w.dtype)
    return (eye[:, None, :, None] * w[:, :, None, :]).reshape(n * d, n * d)


def _t5_bucket(rel):
    nb = REL_BUCKETS // 2
    ret = (rel > 0).astype(jnp.int32) * nb
    n = jnp.abs(rel)
    max_exact = nb // 2
    nf = jnp.maximum(n, 1).astype(jnp.float32)
    large = max_exact + (jnp.log(nf / max_exact) / math.log(REL_MAX_DIST / max_exact)
                         * (nb - max_exact)).astype(jnp.int32)
    large = jnp.minimum(large, nb - 1)
    return ret + jnp.where(n < max_exact, n, large)


def _bias_kernel(bucket_ref, table_ref, o_ref):
    h = pl.program_id(0)
    bucket = bucket_ref[0]
    acc = jnp.full(bucket.shape, -1e30, F32)
    for b in range(REL_BUCKETS):
        acc = jnp.where(bucket == b, table_ref[b, h] * LOG2E, acc)
    o_ref[0, 0] = acc


def _bias_tiles(rel_bias):
    t = ATT_TILE
    H = rel_bias.shape[1]
    qp = jnp.arange(t, dtype=jnp.int32)[:, None]
    kp = jnp.arange(t, dtype=jnp.int32)[None, :]
    mask = (kp // CHUNK) <= (qp // CHUNK)
    half = REL_BUCKETS // 2
    per_distance = _t5_bucket(-jnp.arange(2 * t, dtype=jnp.int32))
    edges = jnp.sum((per_distance[None, :] < jnp.arange(1, half, dtype=jnp.int32)[:, None]).astype(jnp.int32), axis=1)

    def bucket_2d(rel):
        passed = jnp.sum((jnp.abs(rel)[None] >= edges[:, None, None]).astype(jnp.int32), axis=0)
        return (rel > 0).astype(jnp.int32) * half + passed

    buckets = jnp.stack([jnp.where(mask, bucket_2d(kp - qp), REL_BUCKETS), bucket_2d(kp - t - qp)], axis=0)
    table = rel_bias.astype(F32)
    tiles = pl.pallas_call(
        _bias_kernel,
        grid=(H, 2),
        in_specs=[
            pl.BlockSpec((1, t, t), lambda h, k: (k, 0, 0)),
            pl.BlockSpec(memory_space=pltpu.SMEM),
        ],
        out_specs=pl.BlockSpec((1, 1, t, t), lambda h, k: (h, k, 0, 0)),
        out_shape=jax.ShapeDtypeStruct((H, 2, t, t), F32),
        compiler_params=_params(("parallel", "parallel")),
    )(buckets, table)
    far_bucket = _t5_bucket(jnp.full((1,), -t - 1, jnp.int32))
    far = jnp.sum(jnp.where(jnp.arange(REL_BUCKETS)[:, None] == far_bucket, table, 0.0), axis=0)
    return tiles, jnp.broadcast_to((far * LOG2E)[:, None, None], (H, 1, t))


def _diff_kernel(lam_init, q_ref, k_ref, v_ref, bias_ref, far_ref, lqk_ref, g_ref, o_ref,
                 qs_scr, m_scr, l_scr, acc_scr):
    i = pl.program_id(2)
    t = q_ref.shape[0]
    hq = t // 2
    lane = lax.broadcasted_iota(jnp.int32, (1, LANES), 1)
    q = q_ref[...].astype(F32) * (LOG2E * DIFF_DH ** -0.5)
    for half in range(2):
        qh = q[half * hq:(half + 1) * hq]
        qs_scr[(2 * half) * hq:(2 * half + 1) * hq] = jnp.where(lane < DIFF_DH, qh, 0.0).astype(BF16)
        qs_scr[(2 * half + 1) * hq:(2 * half + 2) * hq] = jnp.where(lane >= DIFF_DH, qh, 0.0).astype(BF16)
    m_scr[...] = jnp.full_like(m_scr, -1e30)
    l_scr[...] = jnp.zeros_like(l_scr)
    acc_scr[...] = jnp.zeros_like(acc_scr)

    def tile(rows, ks, bias):
        s = _dot_nt(qs_scr[rows, :], k_ref[ks, :]) + bias
        groups = [s[:, c * LANES:(c + 1) * LANES] for c in range(s.shape[1] // LANES)]
        mx = functools.reduce(jnp.maximum, groups)
        m_prev = m_scr[rows, :]
        m_new = jnp.maximum(m_prev, jnp.max(mx, axis=-1, keepdims=True))
        alpha = jnp.exp2(m_prev - m_new)
        ps = [jnp.exp2(g - m_new) for g in groups]
        l_scr[rows, :] = alpha * l_scr[rows, :] + functools.reduce(jnp.add, ps)
        p = jnp.concatenate(ps, axis=1).astype(BF16)
        acc_scr[rows, :] = alpha * acc_scr[rows, :] + _dot(p, v_ref[ks, :])
        m_scr[rows, :] = m_new

    def stacked(b, half):
        bh = b[half * hq:(half + 1) * hq]
        return [bh, bh]

    all_rows = slice(0, 2 * t)

    def far_body(j, carry):
        tile(all_rows, pl.ds(pl.multiple_of(j * t, t), t), far_ref[0])
        return carry

    lax.fori_loop(0, jnp.maximum(i - 1, 0), far_body, 0)

    @pl.when(i >= 1)
    def _():
        b = bias_ref[0, 1]
        tile(all_rows, pl.ds(pl.multiple_of((i - 1) * t, t), t),
             jnp.concatenate(stacked(b, 0) + stacked(b, 1), axis=0))

    b = bias_ref[0, 0]
    diag0 = pl.multiple_of(i * t, t)
    tile(slice(0, t), pl.ds(diag0, hq), jnp.concatenate(stacked(b[:, 0:hq], 0), axis=0))
    tile(slice(t, 2 * t), pl.ds(diag0, t), jnp.concatenate(stacked(b, 1), axis=0))

    lqk = lqk_ref[...]
    lam = (jnp.exp(jnp.sum(lqk[0:1] * lqk[1:2], axis=-1, keepdims=True))
           - jnp.exp(jnp.sum(lqk[2:3] * lqk[3:4], axis=-1, keepdims=True)) + lam_init)
    o = acc_scr[...] / jnp.sum(l_scr[...], axis=-1, keepdims=True)
    o = jnp.concatenate([o[0:hq] - lam * o[hq:t], o[t:t + hq] - lam * o[t + hq:2 * t]], axis=0)
    o = o * lax.rsqrt(jnp.mean(o * o, axis=-1, keepdims=True) + EPS)
    o_ref[...] = (o * g_ref[...] * (1.0 - lam_init)).astype(o_ref.dtype)


def _diff_attention(proj, bias, lqk, subln_g, layer_idx, B, S):
    T = proj.shape[0]
    t = ATT_TILE
    nq = S // t
    tiles, far = bias
    lam_init = 0.8 - 0.6 * math.exp(-0.3 * layer_idx)
    return pl.pallas_call(
        functools.partial(_diff_kernel, lam_init),
        grid=(B, DIFF_HEADS, nq),
        in_specs=[
            pl.BlockSpec((t, LANES), lambda b, h, i: (b * nq + i, COL_DQ // LANES + h)),
            pl.BlockSpec((S, LANES), lambda b, h, i: (b, COL_DK // LANES + h)),
            pl.BlockSpec((S, LANES), lambda b, h, i: (b, COL_DV // LANES + h)),
            pl.BlockSpec((1, 2, t, t), lambda b, h, i: (h, 0, 0, 0)),
            pl.BlockSpec((1, 1, t), lambda b, h, i: (h, 0, 0)),
            pl.BlockSpec((4, DIFF_DH), lambda b, h, i: (0, 0)),
            pl.BlockSpec((1, DIFF_DV), lambda b, h, i: (0, 0)),
        ],
        out_specs=pl.BlockSpec((t, LANES), lambda b, h, i: (b * nq + i, h)),
        out_shape=jax.ShapeDtypeStruct((T, DIFF_WIDTH), BF16),
        scratch_shapes=[
            pltpu.VMEM((2 * t, LANES), BF16),
            pltpu.VMEM((2 * t, LANES), F32),
            pltpu.VMEM((2 * t, LANES), F32),
            pltpu.VMEM((2 * t, DIFF_DV), F32),
        ],
        compiler_params=_params(("parallel", "parallel", "arbitrary")),
    )(proj, proj, proj, tiles, far, lqk, subln_g.reshape(1, DIFF_DV))


def _outproj_kernel(h_ref, og_ref, ol_ref, od_ref, w_ref, g1_ref, sh2_ref, sc2_ref, n2_ref, rw_ref, rb_ref,
                    hn_ref, u2_ref, eid_ref, ew_ref, cnt_ref):
    tm = h_ref.shape[0]

    @pl.when(pl.program_id(0) == 0)
    def _():
        cnt_ref[...] = jnp.zeros_like(cnt_ref)

    nr = tm // 2
    halves = [slice(half * nr, (half + 1) * nr) for half in range(2)]
    accs = []
    for rows in halves:
        acc = _dot(og_ref[rows, :], w_ref[0:GLA_WIDTH, :])
        acc += _dot(ol_ref[rows, :], w_ref[GLA_WIDTH:GLA_WIDTH + LRU_WIDTH, :])
        acc += _dot(od_ref[rows, :], w_ref[GLA_WIDTH + LRU_WIDTH:, :])
        accs.append(acc)
    for rows, acc in zip(halves, accs):
        _outproj_rows(rows, acc, h_ref, g1_ref, sh2_ref, sc2_ref, n2_ref, rw_ref, rb_ref,
                      hn_ref, u2_ref, eid_ref, ew_ref, cnt_ref)


def _outproj_rows(rows, acc, h_ref, g1_ref, sh2_ref, sc2_ref, n2_ref, rw_ref, rb_ref,
                  hn_ref, u2_ref, eid_ref, ew_ref, cnt_ref):
    D = h_ref.shape[1]
    nr = rows.stop - rows.start
    hn = h_ref[rows, :] + g1_ref[0] * acc
    hn_ref[rows, :] = hn
    u2 = _modulated_norm(hn, n2_ref[...], sc2_ref[0], sh2_ref[0])
    _store_row_tiles(u2_ref.at[pl.ds(rows.start * SUBLANES, nr * SUBLANES)],
                     _pack_bf16_pair(u2[:, :D // 2], u2[:, D // 2:]))

    logits = _dot(u2.astype(BF16), rw_ref[...]) + rb_ref[...]
    lane = lax.broadcasted_iota(jnp.int32, logits.shape, 1)
    lane_f = lane.astype(F32)
    neg = jnp.float32(-jnp.inf)
    gmask = lane < N_GROUPS
    gl = jnp.where(gmask, logits, neg)
    gmax = jnp.max(gl, axis=-1, keepdims=True)
    gidx = jnp.min(jnp.where(gl == gmax, lane_f, float(LANES)), axis=-1, keepdims=True)
    g_w = 1.0 / jnp.sum(jnp.where(gmask, jnp.exp(gl - gmax), 0.0), axis=-1, keepdims=True)
    egroup = ((lane - N_GROUPS) >> 3).astype(F32)
    emask = (lane >= N_GROUPS) & (lane < N_GROUPS + N_EXPERTS) & (egroup == gidx)
    el = jnp.where(emask, logits, neg)
    v1 = jnp.max(el, axis=-1, keepdims=True)
    i1 = jnp.min(jnp.where(el == v1, lane_f, float(LANES)), axis=-1, keepdims=True)
    el2 = jnp.where(lane_f == i1, neg, el)
    v2 = jnp.max(el2, axis=-1, keepdims=True)
    i2 = jnp.min(jnp.where(el2 == v2, lane_f, float(LANES)), axis=-1, keepdims=True)
    e21 = jnp.exp(v2 - v1)
    w1 = g_w / (1.0 + e21)
    w2 = g_w * e21 / (1.0 + e21)
    ew_ref[rows, :] = jnp.where(lane == 0, w1, jnp.where(lane == 1, w2, 0.0))

    oh1 = lane_f == i1
    oh2 = lane_f == i2
    both = jnp.where(oh1 | oh2, 1.0, 0.0).astype(BF16)
    row = lax.broadcasted_iota(jnp.int32, (nr, nr), 0)
    col = lax.broadcasted_iota(jnp.int32, (nr, nr), 1)
    earlier = _dot(jnp.where(col < row, 1.0, 0.0).astype(BF16), both) + cnt_ref[0:1, :]
    rank1 = jnp.sum(jnp.where(oh1, earlier, 0.0), axis=-1, keepdims=True)
    rank2 = jnp.sum(jnp.where(oh2, earlier, 0.0), axis=-1, keepdims=True)
    cnt_ref[0:1, :] = cnt_ref[0:1, :] + jnp.sum(both.astype(F32), axis=0, keepdims=True)
    info = jnp.where(lane == 0, i1 - float(N_GROUPS),
                     jnp.where(lane == 1, i2 - float(N_GROUPS),
                               jnp.where(lane == 2, rank1, jnp.where(lane == 3, rank2, 0.0))))
    eid_ref[:, rows] = jnp.transpose(info)[0:SUBLANES, :].astype(jnp.int32)


def _outproj(h, o_gla, o_lru, o_diff, w_out, layer, mod3, norm2_g, rw, rb, S):
    T, D = h.shape
    tm = 512
    per_b = S // tm
    rowblk = lambda width: pl.BlockSpec((tm, width), lambda i: (i, 0))
    modblk = lambda k: pl.BlockSpec((1, 1, D), lambda i: (i // per_b, 0, k))
    full = lambda shape: pl.BlockSpec(shape, lambda i: (0,) * len(shape))
    return pl.pallas_call(
        _outproj_kernel,
        grid=(T // tm,),
        in_specs=[
            rowblk(D), rowblk(GLA_WIDTH), rowblk(LRU_WIDTH), rowblk(DIFF_WIDTH),
            pl.BlockSpec((None, D, D), lambda i: (layer, 0, 0)),
            modblk(2), modblk(3), modblk(4),
            full((1, D)),
            full((D, LANES)),
            full((1, LANES)),
        ],
        out_specs=[rowblk(D), pl.BlockSpec((tm * SUBLANES, LANES), lambda i: (i, 0)),
                   pl.BlockSpec((SUBLANES, tm), lambda i: (0, i)), rowblk(LANES),
                   full((SUBLANES, LANES))],
        out_shape=[
            jax.ShapeDtypeStruct((T, D), F32),
            jax.ShapeDtypeStruct((T * SUBLANES, LANES), U32),
            jax.ShapeDtypeStruct((SUBLANES, T), jnp.int32),
            jax.ShapeDtypeStruct((T, LANES), F32),
            jax.ShapeDtypeStruct((SUBLANES, LANES), F32),
        ],
        compiler_params=_params(("arbitrary",)),
    )(h, o_gla, o_lru, o_diff, w_out, mod3, mod3, mod3, norm2_g.reshape(1, D), rw, rb)


def _dispatch(info, counts):
    T = info.shape[1]
    blk = MOE_BLK
    n_blocks = (T * TOP_K) // blk + N_EXPERTS
    padded = (counts + blk - 1) // blk * blk
    pends = jnp.cumsum(padded)
    pstarts = pends - padded
    ustarts = jnp.cumsum(counts) - counts
    n_used = (pends[-1] // blk).astype(jnp.int32)
    block_idx = jnp.arange(n_blocks, dtype=jnp.int32)
    block_expert = jnp.minimum(jnp.sum((pends[None, :] <= (block_idx * blk)[:, None]).astype(jnp.int32), axis=1),
                               N_EXPERTS - 1)
    last_used = jnp.sum(jnp.where(block_idx == jnp.maximum(n_used - 1, 0), block_expert, 0))
    block_expert = jnp.where(block_idx < n_used, block_expert, last_used).astype(jnp.int32)
    following = jnp.concatenate([block_expert[1:], jnp.full((1,), -1, jnp.int32)])
    zero_block = ((block_idx >= n_used - 1) | (following != block_expert)).astype(jnp.int32)
    owner = block_expert[:, None] == jnp.arange(N_EXPERTS, dtype=jnp.int32)[None, :]
    seg_end = jnp.sum(jnp.where(owner, (pstarts + counts)[None, :], 0), axis=1)
    n_valid = jnp.where(block_idx < n_used, jnp.clip(seg_end - block_idx * blk, 0, blk), 0).astype(jnp.int32)
    packed_base = (jnp.sum(jnp.where(owner, (ustarts - pstarts)[None, :], 0), axis=1) + block_idx * blk)
    packed_base = jnp.where(n_valid > 0, packed_base, 0).astype(jnp.int32)
    return dict(experts=(info[0], info[1]), ranks=(info[2], info[3]),
                slot_start=pstarts.astype(jnp.int32), packed_start=ustarts.astype(jnp.int32), zero_block=zero_block,
                block_expert=block_expert, n_used=n_used.reshape(1), n_valid=n_valid, packed_base=packed_base)


def _scatter_kernel(e0_ref, e1_ref, r0_ref, r1_ref, ss_ref, ps_ref, zb_ref, src_ref, o_ref, inv_ref,
                    zero_buf, sem, zero_sem):
    rows = src_ref.shape[0] // SUBLANES
    base = pl.program_id(0) * rows
    fill_rows = zero_buf.shape[0]
    n_blocks = o_ref.shape[0] // fill_rows

    @pl.when(pl.program_id(0) == 0)
    def _():
        zero_buf[...] = jnp.zeros_like(zero_buf)

        def for_each_fill(fn):
            def body(j, carry):
                @pl.when(zb_ref[j] == 1)
                def _():
                    fn(pltpu.make_async_copy(
                        zero_buf, o_ref.at[pl.ds(pl.multiple_of(j * fill_rows, fill_rows), fill_rows)], zero_sem))
                return carry
            lax.fori_loop(0, n_blocks, body, 0)

        for_each_fill(lambda copy: copy.start())
        for_each_fill(lambda copy: copy.wait())

    def row_copy(r, slot):
        return pltpu.make_async_copy(src_ref.at[pl.ds(pl.multiple_of(r * SUBLANES, SUBLANES), SUBLANES)],
                                     o_ref.at[pl.ds(pl.multiple_of(slot * SUBLANES, SUBLANES), SUBLANES)], sem)

    def issue(r, carry):
        t = base + r
        for k, (e_ref, r_ref) in enumerate(((e0_ref, r0_ref), (e1_ref, r1_ref))):
            expert = e_ref[t]
            rank = r_ref[t]
            row_copy(r, ss_ref[expert] + rank).start()
            inv_ref[ps_ref[expert] + rank] = t * TOP_K + k
        return carry

    lax.fori_loop(0, rows, issue, 0, unroll=8)
    for _ in range(TOP_K):
        pltpu.make_async_copy(src_ref, o_ref.at[pl.ds(0, rows * SUBLANES)], sem).wait()


def _scatter_rows(plan, src):
    T = src.shape[0] // SUBLANES
    P = T * TOP_K + N_EXPERTS * MOE_BLK
    rows = GATHER_ROWS
    return pl.pallas_call(
        _scatter_kernel,
        grid_spec=pltpu.PrefetchScalarGridSpec(
            num_scalar_prefetch=7,
            grid=(T // rows,),
            in_specs=[pl.BlockSpec((rows * SUBLANES, LANES), lambda i, *refs: (i, 0))],
            out_specs=[pl.BlockSpec(memory_space=pl.ANY), pl.BlockSpec(memory_space=pltpu.SMEM)],
            scratch_shapes=[
                pltpu.VMEM((MOE_BLK * SUBLANES, LANES), src.dtype),
                pltpu.SemaphoreType.DMA(()),
                pltpu.SemaphoreType.DMA(()),
            ],
        ),
        out_shape=[jax.ShapeDtypeStruct((P * SUBLANES, LANES), src.dtype),
                   jax.ShapeDtypeStruct((T * TOP_K,), jnp.int32)],
        compiler_params=_params(("arbitrary",)),
    )(*plan["experts"], *plan["ranks"], plan["slot_start"], plan["packed_start"], plan["zero_block"], src)


def _expert_kernel(layer, be_ref, first_ref, next_ref, slot_ref, nu_ref, nv_ref, pb_ref, inv_ref,
                   xs_ref, w1_hbm, w3_hbm, w2_hbm, yt_ref,
                   w1f, w3f, w2f, w1b, w3b, w2b, ybuf, sems, ysems):
    i = pl.program_id(0)
    D = w1b.shape[0]
    blk = xs_ref.shape[0] // SUBLANES

    def drain_rows(j):
        n = nv_ref[j]
        b = j % 2
        for bit in range(blk.bit_length()):
            size = (1 << bit) * SUBLANES

            @pl.when((n >> bit) & 1 == 1)
            def _():
                pltpu.make_async_copy(ybuf.at[b, pl.ds(0, size)], yt_ref.at[pl.ds(0, size)], ysems.at[b]).wait()

    def weight_copies(e, slot):
        return (pltpu.make_async_copy(w1_hbm.at[layer, e], w1f.at[slot], sems.at[slot, 0]),
                pltpu.make_async_copy(w3_hbm.at[layer, e], w3f.at[slot], sems.at[slot, 1]),
                pltpu.make_async_copy(w2_hbm.at[layer, e], w2f.at[slot], sems.at[slot, 2]))

    @pl.when(i == 0)
    def _():
        for c in weight_copies(be_ref[0], 0):
            c.start()

    @pl.when(first_ref[i] == 1)
    def _():
        slot = slot_ref[i]
        for c in weight_copies(be_ref[i], slot):
            c.wait()

        @pl.when(next_ref[i] >= 0)
        def _():
            for c in weight_copies(next_ref[i], 1 - slot):
                c.start()

        w1b[...] = w1f[slot].astype(BF16)
        w3b[...] = w3f[slot].astype(BF16)
        w2b[...] = w2f[slot].astype(BF16)

    @pl.when(i >= 2)
    def _():
        drain_rows(i - 2)

    @pl.when(i < nu_ref[0])
    def _():
        b = i % 2
        n = nv_ref[i]

        def swiglu(n_rows):
            tiles = pl.ds(0, n_rows * SUBLANES)
            lo, hi = _unpack_bf16_pair(_load_row_tiles(xs_ref.at[tiles]))
            lo = lo.astype(BF16)
            hi = hi.astype(BF16)
            a = _dot(lo, w1b[0:D // 2, :]) + _dot(hi, w1b[D // 2:, :])
            g = _dot(lo, w3b[0:D // 2, :]) + _dot(hi, w3b[D // 2:, :])
            hid = ((a * _sigmoid(a)) * g).astype(BF16)
            _store_row_tiles(ybuf.at[b, tiles],
                             _pack_bf16_pair(_dot(hid, w2b[:, 0:D // 2]), _dot(hid, w2b[:, D // 2:])))

        @pl.when(n > blk // 2)
        def _():
            swiglu(blk)

        @pl.when(n <= blk // 2)
        def _():
            swiglu(blk // 2)

        first_pos = pb_ref[i]
        unroll = 8

        def issue_row(r):
            a_idx = inv_ref[first_pos + r]
            pltpu.make_async_copy(ybuf.at[b, pl.ds(pl.multiple_of(r * SUBLANES, SUBLANES), SUBLANES)],
                                  yt_ref.at[pl.ds(pl.multiple_of(a_idx * SUBLANES, SUBLANES), SUBLANES)],
                                  ysems.at[b]).start()

        def issue_group(g, carry):
            for u in range(unroll):
                issue_row(g * unroll + u)
            return carry

        def issue_one(r, carry):
            issue_row(r)
            return carry

        lax.fori_loop(0, n // unroll, issue_group, 0)
        lax.fori_loop((n // unroll) * unroll, n, issue_one, 0)

    @pl.when(i == pl.num_programs(0) - 1)
    def _():
        drain_rows(i - 1)
        drain_rows(i)


def _segment_plan(block_expert, n_used):
    n = block_expert.shape[0]
    idx = jnp.arange(n, dtype=jnp.int32)
    prev = jnp.concatenate([jnp.full((1,), -1, jnp.int32), block_expert[:-1]])
    first = ((block_expert != prev) & (idx < n_used[0])).astype(jnp.int32)
    slot = (jnp.cumsum(first) - 1) % 2
    later_first = jnp.where(first == 1, idx, n)
    next_idx = lax.cummin(jnp.concatenate([later_first[1:], jnp.full((1,), n, jnp.int32)]), reverse=True)
    next_expert = jnp.where(next_idx < n, block_expert[jnp.minimum(next_idx, n - 1)], -1)
    return first, next_expert.astype(jnp.int32), slot.astype(jnp.int32)


def _experts(plan, inv, xs, w1, w3, w2, layer):
    _, _, D, DE = w1.shape
    blk = MOE_BLK
    block_expert, n_used = plan["block_expert"], plan["n_used"]
    n_blocks = block_expert.shape[0]
    n_assign = inv.shape[0]
    first, next_expert, slot = _segment_plan(block_expert, n_used)
    rowmap = lambda i, *refs: (jnp.minimum(i, jnp.maximum(refs[4][0] - 1, 0)), 0)
    hbm = pl.BlockSpec(memory_space=pl.ANY)
    return pl.pallas_call(
        functools.partial(_expert_kernel, layer),
        grid_spec=pltpu.PrefetchScalarGridSpec(
            num_scalar_prefetch=8,
            grid=(n_blocks,),
            in_specs=[pl.BlockSpec((blk * SUBLANES, LANES), rowmap), hbm, hbm, hbm],
            out_specs=hbm,
            scratch_shapes=[
                pltpu.VMEM((2, D, DE), F32),
                pltpu.VMEM((2, D, DE), F32),
                pltpu.VMEM((2, DE, D), F32),
                pltpu.VMEM((D, DE), BF16),
                pltpu.VMEM((D, DE), BF16),
                pltpu.VMEM((DE, D), BF16),
                pltpu.VMEM((2, blk * SUBLANES, LANES), U32),
                pltpu.SemaphoreType.DMA((2, 3)),
                pltpu.SemaphoreType.DMA((2,)),
            ],
        ),
        out_shape=jax.ShapeDtypeStruct((n_assign * SUBLANES, LANES), U32),
        compiler_params=_params(("arbitrary",)),
    )(block_expert, first, next_expert, slot, n_used, plan["n_valid"], plan["packed_base"], inv, xs, w1, w3, w2)


def _final_combine_kernel(yt_ref, h_ref, g2_ref, ew_ref, fg_ref, o_ref):
    hn = h_ref[...] + g2_ref[0] * _moe_mix(yt_ref, ew_ref)
    o_ref[...] = hn * lax.rsqrt(jnp.mean(hn * hn, axis=-1, keepdims=True) + EPS) * fg_ref[...]


def _final_combine(yt, h, mod3, ew, final_g, S):
    T, D = h.shape
    tc = 256
    per_b = S // tc
    return pl.pallas_call(
        _final_combine_kernel,
        grid=(T // tc,),
        in_specs=[
            pl.BlockSpec((tc * TOP_K * SUBLANES, LANES), lambda i: (i, 0)),
            pl.BlockSpec((tc, D), lambda i: (i, 0)),
            pl.BlockSpec((1, 1, D), lambda i: (i // per_b, 0, 5)),
            pl.BlockSpec((tc, LANES), lambda i: (i, 0)),
            pl.BlockSpec((1, D), lambda i: (0, 0)),
        ],
        out_specs=pl.BlockSpec((tc, D), lambda i: (i, 0)),
        out_shape=jax.ShapeDtypeStruct((T, D), F32),
        compiler_params=_params(("parallel",)),
    )(yt, h, mod3, ew, final_g.reshape(1, D))


def kernel(x, c, ada_w, ada_b, norm1_g, w_in, gla_w_a2, gla_b_a, gla_norm_g, lru_conv_w, lru_conv_b,
           lru_wa, lru_ba, lru_wx, lru_bx, lru_lambda, diff_lq1, diff_lk1, diff_lq2, diff_lk2,
           diff_subln_g, rel_bias, w_out, norm2_g, router_g_w, router_g_b, router_e_w, router_e_b,
           moe_w1, moe_w3, moe_w2, final_g):
    B, S, D = x.shape
    T = B * S
    L = ada_w.shape[0]
    h = x.reshape(T, D)
    mod = _ada_mod(c, ada_w, ada_b)
    bias = _bias_tiles(rel_bias)
    w_in_perm = _permute_w_in(w_in)
    w_out_bf16 = w_out.astype(BF16)
    pending_moe = None
    for l in range(L):
        mod3 = mod[l][:, None, :]
        if pending_moe is None:
            proj = _inproj(h, mod3, norm1_g[l], w_in_perm, l, S)
        else:
            proj, h = _inproj(h, mod3, norm1_g[l], w_in_perm, l, S, pending_moe)
        wa2_pad = jnp.concatenate(
            [gla_w_a2[l], jnp.zeros((LANES - GLA_LOWRANK, GLA_KEY_WIDTH), F32)], axis=0).astype(BF16)
        o_gla = _gla(proj, wa2_pad, gla_b_a[l], gla_norm_g[l], B, S)
        w_gates = jnp.concatenate([_block_diag(lru_wa[l]), _block_diag(lru_wx[l])], axis=1).astype(BF16)
        b_gates = jnp.concatenate([lru_ba[l], lru_bx[l]]).reshape(1, 2 * LRU_WIDTH)
        o_lru = _lru(proj.reshape(B, S, PROJ_WIDTH), lru_conv_w[l], lru_conv_b[l], w_gates, b_gates,
                     lru_lambda[l]).reshape(T, LRU_WIDTH)
        lqk = jnp.stack([diff_lq1[l], diff_lk1[l], diff_lq2[l], diff_lk2[l]], axis=0)
        o_diff = _diff_attention(proj, bias, lqk, diff_subln_g[l], l, B, S)
        rw = jnp.concatenate(
            [router_g_w[l], router_e_w[l], jnp.zeros((D, LANES - N_GROUPS - N_EXPERTS), F32)], axis=1).astype(BF16)
        rb = jnp.concatenate(
            [router_g_b[l], router_e_b[l], jnp.zeros((LANES - N_GROUPS - N_EXPERTS,), F32)]).reshape(1, LANES)
        h, u2, info, ew, cnt = _outproj(h, o_gla, o_lru, o_diff, w_out_bf16, l, mod3, norm2_g[l], rw, rb, S)
        counts = cnt[0, N_GROUPS:N_GROUPS + N_EXPERTS].astype(jnp.int32)
        plan = _dispatch(info, counts)
        xs, inv = _scatter_rows(plan, u2)
        yt = _experts(plan, inv, xs, moe_w1, moe_w3, moe_w2, l)
        pending_moe = (yt, ew, mod3)
    out = _final_combine(yt, h, mod3, ew, final_g, S)
    return out.reshape(B, S, D)
```

```python
import functools
import math

import jax
import jax.numpy as jnp
from jax import lax
from jax.experimental import pallas as pl
from jax.experimental.pallas import tpu as pltpu

F32 = jnp.float32
BF16 = jnp.bfloat16
U32 = jnp.uint32

EPS = 1e-6
LOG2E = math.log2(math.e)
CHUNK = 64

GLA_DV = 128
GLA_DK = 64
GLA_HEADS = 6
GLA_WIDTH = GLA_HEADS * GLA_DV
GLA_KEY_WIDTH = GLA_HEADS * GLA_DK
GLA_LOWRANK = 16
GLA_TAU = 16.0

LRU_WIDTH = 512
LRU_BLOCKS = 8
LRU_BLOCK_DIM = LRU_WIDTH // LRU_BLOCKS
CONV_WIDTH = 4
LRU_C = 8.0

DIFF_DH = 64
DIFF_DV = 128
DIFF_HEADS = 6
DIFF_WIDTH = DIFF_HEADS * DIFF_DV

REL_BUCKETS = 32
REL_MAX_DIST = 128

N_GROUPS = 8
EXPERTS_PER_GROUP = 8
N_EXPERTS = 64
TOP_K = 2

LANES = 128
SUBLANES = 8
VMEM_LIMIT = 56 * 1024 * 1024

COL_GV = 0
COL_GOG = 768
COL_DQ = 1536
COL_DK = 2304
COL_DV = 3072
COL_GQ = 3840
COL_GK = 4224
COL_LY = 4608
COL_LX = 5120
COL_GA = 5632
PROJ_WIDTH = 5760

ATT_TILE = 512
GLA_TILE = 256
LRU_TILE = 256
MOE_BLK = 256
GATHER_ROWS = 2048


def _params(sem, vmem=VMEM_LIMIT):
    return pltpu.CompilerParams(dimension_semantics=sem, vmem_limit_bytes=vmem)


def _sigmoid(x):
    return 0.5 * jnp.tanh(0.5 * x) + 0.5


def _softplus(x):
    return jnp.maximum(x, 0.0) + jnp.log1p(jnp.exp(-jnp.abs(x)))


def _dot(a, b):
    return jnp.dot(a, b, preferred_element_type=F32)


def _dot_nt(a, b):
    return lax.dot_general(a, b, (((1,), (1,)), ((), ())), preferred_element_type=F32)


def _dot_tn(a, b):
    return lax.dot_general(a, b, (((0,), (0,)), ((), ())), preferred_element_type=F32)


def _pack_bf16_pair(lo, hi):
    lo_bits = lax.bitcast_convert_type(lo.astype(BF16).astype(F32), U32)
    hi_bits = lax.bitcast_convert_type(hi.astype(BF16).astype(F32), U32)
    return (hi_bits & jnp.uint32(0xFFFF0000)) | (lo_bits >> 16)


def _unpack_bf16_pair(w):
    lo = lax.bitcast_convert_type(w << 16, F32)
    hi = lax.bitcast_convert_type(w & jnp.uint32(0xFFFF0000), F32)
    return lo, hi


def _store_row_tiles(ref, words):
    rows = words.shape[0]
    for s in range(SUBLANES):
        ref[pl.ds(s, rows, stride=SUBLANES), :] = words[:, s * LANES:(s + 1) * LANES]


def _load_row_tiles(ref):
    rows = ref.shape[0] // SUBLANES
    return jnp.concatenate([ref[pl.ds(s, rows, stride=SUBLANES), :] for s in range(SUBLANES)], axis=1)


def _ada_kernel(c_ref, w_ref, b_ref, o_ref):
    c = c_ref[...]
    s = c * _sigmoid(c)
    o_ref[0] = _dot(s.astype(BF16), w_ref[0].astype(BF16)) + b_ref[0]


def _ada_mod(c, ada_w, ada_b):
    L, D, N = ada_w.shape
    B = c.shape[0]
    tn = 1024
    return pl.pallas_call(
        _ada_kernel,
        grid=(L, N // tn),
        in_specs=[
            pl.BlockSpec((B, D), lambda l, j: (0, 0)),
            pl.BlockSpec((1, D, tn), lambda l, j: (l, 0, j)),
            pl.BlockSpec((1, 1, tn), lambda l, j: (l, 0, j)),
        ],
        out_specs=pl.BlockSpec((1, B, tn), lambda l, j: (l, 0, j)),
        out_shape=jax.ShapeDtypeStruct((L, B, N), F32),
        compiler_params=_params(("parallel", "parallel")),
    )(c, ada_w, ada_b.reshape(L, 1, N))


def _modulated_norm(x, g, sc, sh):
    ms = jnp.mean(x * x, axis=-1, keepdims=True)
    return (x * lax.rsqrt(ms + EPS) * g) * (1.0 + sc) + sh


def _moe_mix(yt_ref, ew_ref):
    rows = ew_ref.shape[0]

    def expert_rows(k):
        return jnp.concatenate([yt_ref[pl.ds(k * SUBLANES + s, rows, stride=TOP_K * SUBLANES), :]
                                for s in range(SUBLANES)], axis=1)

    ew = ew_ref[...]
    w0 = ew[:, 0:1]
    w1 = ew[:, 1:2]
    lo0, hi0 = _unpack_bf16_pair(expert_rows(0))
    lo1, hi1 = _unpack_bf16_pair(expert_rows(1))
    return jnp.concatenate([w0 * lo0 + w1 * lo1, w0 * hi0 + w1 * hi1], axis=1)


def _inproj_kernel(h_ref, sh_ref, sc_ref, g_ref, w_ref, o_ref, u_scr):
    tm = h_ref.shape[0]
    nr = tm // 2

    @pl.when(pl.program_id(1) == 0)
    def _():
        for half in range(2):
            rows = slice(half * nr, (half + 1) * nr)
            u = _modulated_norm(h_ref[rows, :], g_ref[...], sc_ref[0], sh_ref[0]).astype(BF16)
            u_scr[rows, :] = u
            o_ref[rows, :] = _dot(u, w_ref[...]).astype(o_ref.dtype)

    @pl.when(pl.program_id(1) > 0)
    def _():
        o_ref[...] = _dot(u_scr[...], w_ref[...]).astype(o_ref.dtype)


def _inproj_after_moe_kernel(h_ref, yt_ref, ew_ref, g2_ref, sh_ref, sc_ref, g_ref, w_ref, o_ref, hn_ref, u_scr):
    tm = h_ref.shape[0]
    nr = tm // 2

    @pl.when(pl.program_id(1) == 0)
    def _():
        for half in range(2):
            rows = slice(half * nr, (half + 1) * nr)
            tiles = pl.ds(half * nr * TOP_K * SUBLANES, nr * TOP_K * SUBLANES)
            hn = h_ref[rows, :] + g2_ref[0] * _moe_mix(yt_ref.at[tiles], ew_ref.at[rows])
            hn_ref[rows, :] = hn
            u = _modulated_norm(hn, g_ref[...], sc_ref[0], sh_ref[0]).astype(BF16)
            u_scr[rows, :] = u
            o_ref[rows, :] = _dot(u, w_ref[...]).astype(o_ref.dtype)

    @pl.when(pl.program_id(1) > 0)
    def _():
        o_ref[...] = _dot(u_scr[...], w_ref[...]).astype(o_ref.dtype)


def _inproj(h, mod3, norm_g, w_perm, layer, S, pending_moe=None):
    T, D = h.shape
    N = w_perm.shape[2]
    tm, tn = 512, 1152
    per_b = S // tm
    rows = pl.BlockSpec((tm, D), lambda i, j: (i, 0))
    mod_specs = [
        pl.BlockSpec((1, 1, D), lambda i, j: (i // per_b, 0, 0)),
        pl.BlockSpec((1, 1, D), lambda i, j: (i // per_b, 0, 1)),
        pl.BlockSpec((1, D), lambda i, j: (0, 0)),
        pl.BlockSpec((None, D, tn), lambda i, j: (layer, 0, j)),
    ]
    proj_spec = pl.BlockSpec((tm, tn), lambda i, j: (i, j))
    proj_shape = jax.ShapeDtypeStruct((T, N), BF16)
    common = dict(grid=(T // tm, N // tn), scratch_shapes=[pltpu.VMEM((tm, D), BF16)],
                  compiler_params=_params(("parallel", "arbitrary")))
    if pending_moe is None:
        return pl.pallas_call(_inproj_kernel, in_specs=[rows] + mod_specs, out_specs=proj_spec,
                              out_shape=proj_shape, **common)(h, mod3, mod3, norm_g.reshape(1, D), w_perm)
    yt, ew, prev_mod3 = pending_moe
    moe_specs = [
        pl.BlockSpec((tm * TOP_K * SUBLANES, LANES), lambda i, j: (i, 0)),
        pl.BlockSpec((tm, LANES), lambda i, j: (i, 0)),
        pl.BlockSpec((1, 1, D), lambda i, j: (i // per_b, 0, 5)),
    ]
    return pl.pallas_call(
        _inproj_after_moe_kernel, in_specs=[rows] + moe_specs + mod_specs, out_specs=[proj_spec, rows],
        out_shape=[proj_shape, jax.ShapeDtypeStruct((T, D), F32)], **common,
    )(h, yt, ew, prev_mod3, mod3, mod3, norm_g.reshape(1, D), w_perm)


W_IN_SEGMENTS = ((COL_GQ, 0, 384), (COL_GK, 384, 384), (COL_GV, 768, 768), (COL_GOG, 1536, 768),
                 (COL_GA, 2304, GLA_LOWRANK), (COL_LY, 2320, 512), (COL_LX, 2832, 512),
                 (COL_DQ, 3344, 768), (COL_DK, 4112, 768), (COL_DV, 4880, 768))


def _relayout_kernel(w_ref, o_ref):
    x = w_ref[0]
    for dst, src, width in W_IN_SEGMENTS:
        o_ref[0, :, dst:dst + width] = x[:, src:src + width]
    pad = slice(COL_GA + GLA_LOWRANK, COL_GA + LANES)
    o_ref[0, :, pad] = jnp.zeros((x.shape[0], LANES - GLA_LOWRANK), BF16)


def _permute_w_in(w):
    L, D, N = w.shape
    rt = 256
    return pl.pallas_call(
        _relayout_kernel,
        grid=(L, D // rt),
        in_specs=[pl.BlockSpec((1, rt, N), lambda l, i: (l, i, 0))],
        out_specs=pl.BlockSpec((1, rt, PROJ_WIDTH), lambda l, i: (l, i, 0)),
        out_shape=jax.ShapeDtypeStruct((L, D, PROJ_WIDTH), BF16),
        compiler_params=_params(("parallel", "parallel")),
    )(w.astype(BF16))


def _gla_kernel(q_ref, k_ref, v_ref, og_ref, alr_ref, wa2_ref, ba_ref, ng_ref, o_ref, st_ref):
    tb = q_ref.shape[0]
    n_chunks = tb // CHUNK

    @pl.when(pl.program_id(1) == 0)
    def _():
        st_ref[...] = jnp.zeros_like(st_ref)

    row = lax.broadcasted_iota(jnp.int32, (tb, tb), 0)
    col = lax.broadcasted_iota(jnp.int32, (tb, tb), 1)
    same_chunk = (row // CHUNK) == (col // CHUNK)
    causal = col <= row
    tril = jnp.where(same_chunk & causal, 1.0, 0.0).astype(BF16)
    lane = lax.broadcasted_iota(jnp.int32, (1, LANES), 1)
    half_masks = (lane < GLA_DK, lane >= GLA_DK)

    alr = alr_ref[...]
    cols = [slice(p * LANES, (p + 1) * LANES) for p in range(GLA_HEADS // 2)]
    z = [_dot(alr, wa2_ref[:, cs]) + ba_ref[:, cs] for cs in cols]
    la = [(jnp.minimum(zp, 0.0) - jnp.log1p(jnp.exp(-jnp.abs(zp)))) * (1.0 / GLA_TAU) for zp in z]
    la_hi = [x.astype(BF16) for x in la]
    la_lo = [(x - h.astype(F32)).astype(BF16) for x, h in zip(la, la_hi)]
    G = [_dot(tril, h) + _dot(tril, lo) for h, lo in zip(la_hi, la_lo)]
    Gl = [jnp.concatenate([jnp.broadcast_to(g[(c + 1) * CHUNK - 1:(c + 1) * CHUNK, :], (CHUNK, LANES))
                           for c in range(n_chunks)], axis=0) for g in G]
    eG = [jnp.exp(g) for g in G]
    enG = [jnp.exp(-g) for g in G]
    q = [q_ref[:, cs].astype(F32) * (GLA_DK ** -0.5) for cs in cols]
    k = [k_ref[:, cs].astype(F32) for cs in cols]
    kf = [(kp * e).astype(BF16) for kp, e in zip(k, eG)]
    kb = [(kp * e).astype(BF16) for kp, e in zip(k, enG)]
    kd = [kp * jnp.exp(gl - g) for kp, gl, g in zip(k, Gl, G)]

    heads = range(GLA_HEADS)
    pair = [h // 2 for h in heads]
    mask = [half_masks[h % 2] for h in heads]
    vcols = [slice(h * GLA_DV, (h + 1) * GLA_DV) for h in heads]
    qf_h = [jnp.where(mask[h], q[pair[h]] * eG[pair[h]], 0.0).astype(BF16) for h in heads]
    qb_h = [jnp.where(mask[h], q[pair[h]] * enG[pair[h]], 0.0).astype(BF16) for h in heads]
    kd_h = [jnp.where(mask[h], kd[pair[h]], 0.0).astype(BF16) for h in heads]
    v_h = [v_ref[:, vcols[h]] for h in heads]
    a_f = [_dot_nt(qf_h[h], kb[pair[h]]) for h in heads]
    a_b = [_dot_nt(qb_h[h], kf[pair[h]]) for h in heads]
    attn = [jnp.where(same_chunk, jnp.where(causal, a_f[h], a_b[h]), 0.0).astype(BF16) for h in heads]
    o_intra = [_dot(attn[h], v_h[h]) for h in heads]
    chunk_rows = [slice(c * CHUNK, (c + 1) * CHUNK) for c in range(n_chunks)]
    kv = [[_dot_tn(v_h[h][rs], kd_h[h][rs]) for rs in chunk_rows] for h in heads]
    for h in heads:
        st = st_ref[h]
        inter = []
        for c, rs in enumerate(chunk_rows):
            inter.append(_dot_nt(qf_h[h][rs], st.astype(BF16)))
            decay = jnp.exp(Gl[pair[h]][c * CHUNK:c * CHUNK + 1, :])
            st = st * decay + kv[h][c]
        st_ref[h] = st
        o = o_intra[h] + jnp.concatenate(inter, axis=0)
        o = o * lax.rsqrt(jnp.mean(o * o, axis=-1, keepdims=True) + EPS)
        og = og_ref[:, vcols[h]].astype(F32)
        o_ref[:, vcols[h]] = (o * ng_ref[:, vcols[h]] * (og * _sigmoid(og))).astype(o_ref.dtype)


def _gla(proj, wa2_pad, b_a, norm_g, B, S):
    T = proj.shape[0]
    tb = GLA_TILE
    nt = S // tb
    row = lambda b, i: b * nt + i
    return pl.pallas_call(
        _gla_kernel,
        grid=(B, nt),
        in_specs=[
            pl.BlockSpec((tb, GLA_KEY_WIDTH), lambda b, i: (row(b, i), COL_GQ // GLA_KEY_WIDTH)),
            pl.BlockSpec((tb, GLA_KEY_WIDTH), lambda b, i: (row(b, i), COL_GK // GLA_KEY_WIDTH)),
            pl.BlockSpec((tb, GLA_WIDTH), lambda b, i: (row(b, i), COL_GV // GLA_WIDTH)),
            pl.BlockSpec((tb, GLA_WIDTH), lambda b, i: (row(b, i), COL_GOG // GLA_WIDTH)),
            pl.BlockSpec((tb, LANES), lambda b, i: (row(b, i), COL_GA // LANES)),
            pl.BlockSpec((LANES, GLA_KEY_WIDTH), lambda b, i: (0, 0)),
            pl.BlockSpec((1, GLA_KEY_WIDTH), lambda b, i: (0, 0)),
            pl.BlockSpec((1, GLA_WIDTH), lambda b, i: (0, 0)),
        ],
        out_specs=pl.BlockSpec((tb, GLA_WIDTH), lambda b, i: (row(b, i), 0)),
        out_shape=jax.ShapeDtypeStruct((T, GLA_WIDTH), BF16),
        scratch_shapes=[pltpu.VMEM((GLA_HEADS, GLA_DV, LANES), F32)],
        compiler_params=_params(("parallel", "arbitrary")),
    )(proj, proj, proj, proj, proj, wa2_pad, b_a.reshape(1, -1), norm_g.reshape(1, -1))


def _lru_kernel(y_ref, x_ref, cw_ref, cb_ref, wg_ref, bg_ref, lam_ref, o_ref, *scratch):
    B, ts, W = x_ref.shape
    n_planes = W // LANES
    a_scr = scratch[0:n_planes]
    b_scr = scratch[n_planes:2 * n_planes]
    h_scr = scratch[2 * n_planes:3 * n_planes]
    xc_scr, tail_scr, carry_scr = scratch[3 * n_planes:]

    @pl.when(pl.program_id(0) == 0)
    def _():
        tail_scr[...] = jnp.zeros_like(tail_scr)
        carry_scr[...] = jnp.zeros_like(carry_scr)

    cw = cw_ref[...]
    cb = cb_ref[...]
    sp = _softplus(-lam_ref[...])
    row8 = lax.broadcasted_iota(jnp.int32, (8, W), 0)
    for b in range(B):
        x = x_ref[b].astype(F32)
        tail = tail_scr[b]
        xc = cb + cw[CONV_WIDTH - 1:CONV_WIDTH, :] * x
        head = cb + cw[CONV_WIDTH - 1:CONV_WIDTH, :] * x[0:8]
        for d in range(1, CONV_WIDTH):
            wd = cw[CONV_WIDTH - 1 - d:CONV_WIDTH - d, :]
            xr = pltpu.roll(x, d, 0)
            xc = xc + wd * xr
            head = head + wd * jnp.where(row8 < d, pltpu.roll(tail, d, 0), xr[0:8])
        tail_scr[b] = x[ts - 8:ts]
        xc_scr[...] = xc
        xc_scr[0:8] = head
        xc = xc_scr[...]
        gates = _sigmoid(_dot(xc.astype(BF16), wg_ref[...]) + bg_ref[...])
        r = gates[:, :W]
        ig = gates[:, W:]
        log_a = (-LRU_C) * r * sp
        a = jnp.exp(log_a)
        b_in = jnp.sqrt(-jnp.tanh(log_a) * (a * a + 1.0)) * (ig * xc)
        rows = slice(b * ts, (b + 1) * ts)
        for k in range(n_planes):
            a_scr[k][rows] = a[:, k * LANES:(k + 1) * LANES]
            b_scr[k][rows] = b_in[:, k * LANES:(k + 1) * LANES]

    def step(t, hs):
        idx = pl.ds(t, B, stride=ts)
        out = []
        for k in range(n_planes):
            hk = a_scr[k][idx, :] * hs[k] + b_scr[k][idx, :]
            h_scr[k][idx, :] = hk
            out.append(hk)
        return tuple(out)

    hs = lax.fori_loop(0, ts, step, tuple(carry_scr[k] for k in range(n_planes)), unroll=8)
    for k in range(n_planes):
        carry_scr[k] = hs[k]

    for b in range(B):
        rows = slice(b * ts, (b + 1) * ts)
        y = y_ref[b].astype(F32)
        gelu = 0.5 * y * (1.0 + jnp.tanh(math.sqrt(2.0 / math.pi) * (y + 0.044715 * (y * y * y))))
        h = jnp.concatenate([h_scr[k][rows] for k in range(n_planes)], axis=1)
        o_ref[b] = (h * gelu).astype(o_ref.dtype)


def _lru(proj3, conv_w, conv_b, w_gates, b_gates, lam):
    B, S, _ = proj3.shape
    W = LRU_WIDTH
    ts = LRU_TILE
    n_planes = W // LANES
    full = lambda shape: pl.BlockSpec(shape, lambda i: (0,) * len(shape))
    return pl.pallas_call(
        _lru_kernel,
        grid=(S // ts,),
        in_specs=[
            pl.BlockSpec((B, ts, W), lambda i: (0, i, COL_LY // W)),
            pl.BlockSpec((B, ts, W), lambda i: (0, i, COL_LX // W)),
            full((CONV_WIDTH, W)),
            full((1, W)),
            full((W, 2 * W)),
            full((1, 2 * W)),
            full((1, W)),
        ],
        out_specs=pl.BlockSpec((B, ts, W), lambda i: (0, i, 0)),
        out_shape=jax.ShapeDtypeStruct((B, S, W), BF16),
        scratch_shapes=(
            [pltpu.VMEM((B * ts, LANES), F32) for _ in range(3 * n_planes)]
            + [pltpu.VMEM((ts, W), F32), pltpu.VMEM((B, 8, W), F32), pltpu.VMEM((n_planes, B, LANES), F32)]),
        compiler_params=_params(("arbitrary",)),
    )(proj3, proj3, conv_w, conv_b.reshape(1, W), w_gates, b_gates, lam.reshape(1, W))


def _block_diag(w):
    n, d, _ = w.shape
    eye = jnp.eye(n, dtype=w.dtype)
    return (eye[:, None, :, None] * w[:, :, None, :]).reshape(n * d, n * d)


def _t5_bucket(rel):
    nb = REL_BUCKETS // 2
    ret = (rel > 0).astype(jnp.int32) * nb
    n = jnp.abs(rel)
    max_exact = nb // 2
    nf = jnp.maximum(n, 1).astype(jnp.float32)
    large = max_exact + (jnp.log(nf / max_exact) / math.log(REL_MAX_DIST / max_exact)
                         * (nb - max_exact)).astype(jnp.int32)
    large = jnp.minimum(large, nb - 1)
    return ret + jnp.where(n < max_exact, n, large)


def _bias_kernel(bucket_ref, table_ref, o_ref):
    h = pl.program_id(0)
    bucket = bucket_ref[0]
    acc = jnp.full(bucket.shape, -1e30, F32)
    for b in range(REL_BUCKETS):
        acc = jnp.where(bucket == b, table_ref[b, h] * LOG2E, acc)
    o_ref[0, 0] = acc


def _bias_tiles(rel_bias):
    t = ATT_TILE
    H = rel_bias.shape[1]
    qp = jnp.arange(t, dtype=jnp.int32)[:, None]
    kp = jnp.arange(t, dtype=jnp.int32)[None, :]
    mask = (kp // CHUNK) <= (qp // CHUNK)
    half = REL_BUCKETS // 2
    per_distance = _t5_bucket(-jnp.arange(2 * t, dtype=jnp.int32))
    edges = jnp.sum((per_distance[None, :] < jnp.arange(1, half, dtype=jnp.int32)[:, None]).astype(jnp.int32), axis=1)

    def bucket_2d(rel):
        passed = jnp.sum((jnp.abs(rel)[None] >= edges[:, None, None]).astype(jnp.int32), axis=0)
        return (rel > 0).astype(jnp.int32) * half + passed

    buckets = jnp.stack([jnp.where(mask, bucket_2d(kp - qp), REL_BUCKETS), bucket_2d(kp - t - qp)], axis=0)
    table = rel_bias.astype(F32)
    tiles = pl.pallas_call(
        _bias_kernel,
        grid=(H, 2),
        in_specs=[
            pl.BlockSpec((1, t, t), lambda h, k: (k, 0, 0)),
            pl.BlockSpec(memory_space=pltpu.SMEM),
        ],
        out_specs=pl.BlockSpec((1, 1, t, t), lambda h, k: (h, k, 0, 0)),
        out_shape=jax.ShapeDtypeStruct((H, 2, t, t), F32),
        compiler_params=_params(("parallel", "parallel")),
    )(buckets, table)
    far_bucket = _t5_bucket(jnp.full((1,), -t - 1, jnp.int32))
    far = jnp.sum(jnp.where(jnp.arange(REL_BUCKETS)[:, None] == far_bucket, table, 0.0), axis=0)
    return tiles, jnp.broadcast_to((far * LOG2E)[:, None, None], (H, 1, t))


def _diff_kernel(lam_init, q_ref, k_ref, v_ref, bias_ref, far_ref, lqk_ref, g_ref, o_ref,
                 qs_scr, m_scr, l_scr, acc_scr):
    i = pl.program_id(2)
    t = q_ref.shape[0]
    hq = t // 2
    lane = lax.broadcasted_iota(jnp.int32, (1, LANES), 1)
    q = q_ref[...].astype(F32) * (LOG2E * DIFF_DH ** -0.5)
    for half in range(2):
        qh = q[half * hq:(half + 1) * hq]
        qs_scr[(2 * half) * hq:(2 * half + 1) * hq] = jnp.where(lane < DIFF_DH, qh, 0.0).astype(BF16)
        qs_scr[(2 * half + 1) * hq:(2 * half + 2) * hq] = jnp.where(lane >= DIFF_DH, qh, 0.0).astype(BF16)
    m_scr[...] = jnp.full_like(m_scr, -1e30)
    l_scr[...] = jnp.zeros_like(l_scr)
    acc_scr[...] = jnp.zeros_like(acc_scr)

    def tile(rows, ks, bias):
        s = _dot_nt(qs_scr[rows, :], k_ref[ks, :]) + bias
        groups = [s[:, c * LANES:(c + 1) * LANES] for c in range(s.shape[1] // LANES)]
        mx = functools.reduce(jnp.maximum, groups)
        m_prev = m_scr[rows, :]
        m_new = jnp.maximum(m_prev, jnp.max(mx, axis=-1, keepdims=True))
        alpha = jnp.exp2(m_prev - m_new)
        ps = [jnp.exp2(g - m_new) for g in groups]
        l_scr[rows, :] = alpha * l_scr[rows, :] + functools.reduce(jnp.add, ps)
        p = jnp.concatenate(ps, axis=1).astype(BF16)
        acc_scr[rows, :] = alpha * acc_scr[rows, :] + _dot(p, v_ref[ks, :])
        m_scr[rows, :] = m_new

    def stacked(b, half):
        bh = b[half * hq:(half + 1) * hq]
        return [bh, bh]

    all_rows = slice(0, 2 * t)

    def far_body(j, carry):
        tile(all_rows, pl.ds(pl.multiple_of(j * t, t), t), far_ref[0])
        return carry

    lax.fori_loop(0, jnp.maximum(i - 1, 0), far_body, 0)

    @pl.when(i >= 1)
    def _():
        b = bias_ref[0, 1]
        tile(all_rows, pl.ds(pl.multiple_of((i - 1) * t, t), t),
             jnp.concatenate(stacked(b, 0) + stacked(b, 1), axis=0))

    b = bias_ref[0, 0]
    diag0 = pl.multiple_of(i * t, t)
    tile(slice(0, t), pl.ds(diag0, hq), jnp.concatenate(stacked(b[:, 0:hq], 0), axis=0))
    tile(slice(t, 2 * t), pl.ds(diag0, t), jnp.concatenate(stacked(b, 1), axis=0))

    lqk = lqk_ref[...]
    lam = (jnp.exp(jnp.sum(lqk[0:1] * lqk[1:2], axis=-1, keepdims=True))
           - jnp.exp(jnp.sum(lqk[2:3] * lqk[3:4], axis=-1, keepdims=True)) + lam_init)
    o = acc_scr[...] / jnp.sum(l_scr[...], axis=-1, keepdims=True)
    o = jnp.concatenate([o[0:hq] - lam * o[hq:t], o[t:t + hq] - lam * o[t + hq:2 * t]], axis=0)
    o = o * lax.rsqrt(jnp.mean(o * o, axis=-1, keepdims=True) + EPS)
    o_ref[...] = (o * g_ref[...] * (1.0 - lam_init)).astype(o_ref.dtype)


def _diff_attention(proj, bias, lqk, subln_g, layer_idx, B, S):
    T = proj.shape[0]
    t = ATT_TILE
    nq = S // t
    tiles, far = bias
    lam_init = 0.8 - 0.6 * math.exp(-0.3 * layer_idx)
    return pl.pallas_call(
        functools.partial(_diff_kernel, lam_init),
        grid=(B, DIFF_HEADS, nq),
        in_specs=[
            pl.BlockSpec((t, LANES), lambda b, h, i: (b * nq + i, COL_DQ // LANES + h)),
            pl.BlockSpec((S, LANES), lambda b, h, i: (b, COL_DK // LANES + h)),
            pl.BlockSpec((S, LANES), lambda b, h, i: (b, COL_DV // LANES + h)),
            pl.BlockSpec((1, 2, t, t), lambda b, h, i: (h, 0, 0, 0)),
            pl.BlockSpec((1, 1, t), lambda b, h, i: (h, 0, 0)),
            pl.BlockSpec((4, DIFF_DH), lambda b, h, i: (0, 0)),
            pl.BlockSpec((1, DIFF_DV), lambda b, h, i: (0, 0)),
        ],
        out_specs=pl.BlockSpec((t, LANES), lambda b, h, i: (b * nq + i, h)),
        out_shape=jax.ShapeDtypeStruct((T, DIFF_WIDTH), BF16),
        scratch_shapes=[
            pltpu.VMEM((2 * t, LANES), BF16),
            pltpu.VMEM((2 * t, LANES), F32),
            pltpu.VMEM((2 * t, LANES), F32),
            pltpu.VMEM((2 * t, DIFF_DV), F32),
        ],
        compiler_params=_params(("parallel", "parallel", "arbitrary")),
    )(proj, proj, proj, tiles, far, lqk, subln_g.reshape(1, DIFF_DV))


def _outproj_kernel(h_ref, og_ref, ol_ref, od_ref, w_ref, g1_ref, sh2_ref, sc2_ref, n2_ref, rw_ref, rb_ref,
                    hn_ref, u2_ref, eid_ref, ew_ref, cnt_ref):
    tm = h_ref.shape[0]

    @pl.when(pl.program_id(0) == 0)
    def _():
        cnt_ref[...] = jnp.zeros_like(cnt_ref)

    nr = tm // 2
    halves = [slice(half * nr, (half + 1) * nr) for half in range(2)]
    accs = []
    for rows in halves:
        acc = _dot(og_ref[rows, :], w_ref[0:GLA_WIDTH, :])
        acc += _dot(ol_ref[rows, :], w_ref[GLA_WIDTH:GLA_WIDTH + LRU_WIDTH, :])
        acc += _dot(od_ref[rows, :], w_ref[GLA_WIDTH + LRU_WIDTH:, :])
        accs.append(acc)
    for rows, acc in zip(halves, accs):
        _outproj_rows(rows, acc, h_ref, g1_ref, sh2_ref, sc2_ref, n2_ref, rw_ref, rb_ref,
                      hn_ref, u2_ref, eid_ref, ew_ref, cnt_ref)


def _outproj_rows(rows, acc, h_ref, g1_ref, sh2_ref, sc2_ref, n2_ref, rw_ref, rb_ref,
                  hn_ref, u2_ref, eid_ref, ew_ref, cnt_ref):
    D = h_ref.shape[1]
    nr = rows.stop - rows.start
    hn = h_ref[rows, :] + g1_ref[0] * acc
    hn_ref[rows, :] = hn
    u2 = _modulated_norm(hn, n2_ref[...], sc2_ref[0], sh2_ref[0])
    _store_row_tiles(u2_ref.at[pl.ds(rows.start * SUBLANES, nr * SUBLANES)],
                     _pack_bf16_pair(u2[:, :D // 2], u2[:, D // 2:]))

    logits = _dot(u2.astype(BF16), rw_ref[...]) + rb_ref[...]
    lane = lax.broadcasted_iota(jnp.int32, logits.shape, 1)
    lane_f = lane.astype(F32)
    neg = jnp.float32(-jnp.inf)
    gmask = lane < N_GROUPS
    gl = jnp.where(gmask, logits, neg)
    gmax = jnp.max(gl, axis=-1, keepdims=True)
    gidx = jnp.min(jnp.where(gl == gmax, lane_f, float(LANES)), axis=-1, keepdims=True)
    g_w = 1.0 / jnp.sum(jnp.where(gmask, jnp.exp(gl - gmax), 0.0), axis=-1, keepdims=True)
    egroup = ((lane - N_GROUPS) >> 3).astype(F32)
    emask = (lane >= N_GROUPS) & (lane < N_GROUPS + N_EXPERTS) & (egroup == gidx)
    el = jnp.where(emask, logits, neg)
    v1 = jnp.max(el, axis=-1, keepdims=True)
    i1 = jnp.min(jnp.where(el == v1, lane_f, float(LANES)), axis=-1, keepdims=True)
    el2 = jnp.where(lane_f == i1, neg, el)
    v2 = jnp.max(el2, axis=-1, keepdims=True)
    i2 = jnp.min(jnp.where(el2 == v2, lane_f, float(LANES)), axis=-1, keepdims=True)
    e21 = jnp.exp(v2 - v1)
    w1 = g_w / (1.0 + e21)
    w2 = g_w * e21 / (1.0 + e21)
    ew_ref[rows, :] = jnp.where(lane == 0, w1, jnp.where(lane == 1, w2, 0.0))

    oh1 = lane_f == i1
    oh2 = lane_f == i2
    both = jnp.where(oh1 | oh2, 1.0, 0.0).astype(BF16)
    row = lax.broadcasted_iota(jnp.int32, (nr, nr), 0)
    col = lax.broadcasted_iota(jnp.int32, (nr, nr), 1)
    earlier = _dot(jnp.where(col < row, 1.0, 0.0).astype(BF16), both) + cnt_ref[0:1, :]
    rank1 = jnp.sum(jnp.where(oh1, earlier, 0.0), axis=-1, keepdims=True)
    rank2 = jnp.sum(jnp.where(oh2, earlier, 0.0), axis=-1, keepdims=True)
    cnt_ref[0:1, :] = cnt_ref[0:1, :] + jnp.sum(both.astype(F32), axis=0, keepdims=True)
    info = jnp.where(lane == 0, i1 - float(N_GROUPS),
                     jnp.where(lane == 1, i2 - float(N_GROUPS),
                               jnp.where(lane == 2, rank1, jnp.where(lane == 3, rank2, 0.0))))
    eid_ref[:, rows] = jnp.transpose(info)[0:SUBLANES, :].astype(jnp.int32)


def _outproj(h, o_gla, o_lru, o_diff, w_out, layer, mod3, norm2_g, rw, rb, S):
    T, D = h.shape
    tm = 512
    per_b = S // tm
    rowblk = lambda width: pl.BlockSpec((tm, width), lambda i: (i, 0))
    modblk = lambda k: pl.BlockSpec((1, 1, D), lambda i: (i // per_b, 0, k))
    full = lambda shape: pl.BlockSpec(shape, lambda i: (0,) * len(shape))
    return pl.pallas_call(
        _outproj_kernel,
        grid=(T // tm,),
        in_specs=[
            rowblk(D), rowblk(GLA_WIDTH), rowblk(LRU_WIDTH), rowblk(DIFF_WIDTH),
            pl.BlockSpec((None, D, D), lambda i: (layer, 0, 0)),
            modblk(2), modblk(3), modblk(4),
            full((1, D)),
            full((D, LANES)),
            full((1, LANES)),
        ],
        out_specs=[rowblk(D), pl.BlockSpec((tm * SUBLANES, LANES), lambda i: (i, 0)),
                   pl.BlockSpec((SUBLANES, tm), lambda i: (0, i)), rowblk(LANES),
                   full((SUBLANES, LANES))],
        out_shape=[
            jax.ShapeDtypeStruct((T, D), F32),
            jax.ShapeDtypeStruct((T * SUBLANES, LANES), U32),
            jax.ShapeDtypeStruct((SUBLANES, T), jnp.int32),
            jax.ShapeDtypeStruct((T, LANES), F32),
            jax.ShapeDtypeStruct((SUBLANES, LANES), F32),
        ],
        compiler_params=_params(("arbitrary",)),
    )(h, o_gla, o_lru, o_diff, w_out, mod3, mod3, mod3, norm2_g.reshape(1, D), rw, rb)


def _dispatch(info, counts):
    T = info.shape[1]
    blk = MOE_BLK
    n_blocks = (T * TOP_K) // blk + N_EXPERTS
    padded = (counts + blk - 1) // blk * blk
    pends = jnp.cumsum(padded)
    pstarts = pends - padded
    ustarts = jnp.cumsum(counts) - counts
    n_used = (pends[-1] // blk).astype(jnp.int32)
    block_idx = jnp.arange(n_blocks, dtype=jnp.int32)
    block_expert = jnp.minimum(jnp.sum((pends[None, :] <= (block_idx * blk)[:, None]).astype(jnp.int32), axis=1),
                               N_EXPERTS - 1)
    last_used = jnp.sum(jnp.where(block_idx == jnp.maximum(n_used - 1, 0), block_expert, 0))
    block_expert = jnp.where(block_idx < n_used, block_expert, last_used).astype(jnp.int32)
    following = jnp.concatenate([block_expert[1:], jnp.full((1,), -1, jnp.int32)])
    zero_block = ((block_idx >= n_used - 1) | (following != block_expert)).astype(jnp.int32)
    owner = block_expert[:, None] == jnp.arange(N_EXPERTS, dtype=jnp.int32)[None, :]
    seg_end = jnp.sum(jnp.where(owner, (pstarts + counts)[None, :], 0), axis=1)
    n_valid = jnp.where(block_idx < n_used, jnp.clip(seg_end - block_idx * blk, 0, blk), 0).astype(jnp.int32)
    packed_base = (jnp.sum(jnp.where(owner, (ustarts - pstarts)[None, :], 0), axis=1) + block_idx * blk)
    packed_base = jnp.where(n_valid > 0, packed_base, 0).astype(jnp.int32)
    return dict(experts=(info[0], info[1]), ranks=(info[2], info[3]),
                slot_start=pstarts.astype(jnp.int32), packed_start=ustarts.astype(jnp.int32), zero_block=zero_block,
                block_expert=block_expert, n_used=n_used.reshape(1), n_valid=n_valid, packed_base=packed_base)


def _scatter_kernel(e0_ref, e1_ref, r0_ref, r1_ref, ss_ref, ps_ref, zb_ref, src_ref, o_ref, inv_ref,
                    zero_buf, sem, zero_sem):
    rows = src_ref.shape[0] // SUBLANES
    base = pl.program_id(0) * rows
    fill_rows = zero_buf.shape[0]
    n_blocks = o_ref.shape[0] // fill_rows

    @pl.when(pl.program_id(0) == 0)
    def _():
        zero_buf[...] = jnp.zeros_like(zero_buf)

        def for_each_fill(fn):
            def body(j, carry):
                @pl.when(zb_ref[j] == 1)
                def _():
                    fn(pltpu.make_async_copy(
                        zero_buf, o_ref.at[pl.ds(pl.multiple_of(j * fill_rows, fill_rows), fill_rows)], zero_sem))
                return carry
            lax.fori_loop(0, n_blocks, body, 0)

        for_each_fill(lambda copy: copy.start())
        for_each_fill(lambda copy: copy.wait())

    def row_copy(r, slot):
        return pltpu.make_async_copy(src_ref.at[pl.ds(pl.multiple_of(r * SUBLANES, SUBLANES), SUBLANES)],
                                     o_ref.at[pl.ds(pl.multiple_of(slot * SUBLANES, SUBLANES), SUBLANES)], sem)

    def issue(r, carry):
        t = base + r
        for k, (e_ref, r_ref) in enumerate(((e0_ref, r0_ref), (e1_ref, r1_ref))):
            expert = e_ref[t]
            rank = r_ref[t]
            row_copy(r, ss_ref[expert] + rank).start(priority=k)
            inv_ref[ps_ref[expert] + rank] = t * TOP_K + k
        return carry

    lax.fori_loop(0, rows, issue, 0, unroll=8)
    for _ in range(TOP_K):
        pltpu.make_async_copy(src_ref, o_ref.at[pl.ds(0, rows * SUBLANES)], sem).wait()


def _scatter_rows(plan, src):
    T = src.shape[0] // SUBLANES
    P = T * TOP_K + N_EXPERTS * MOE_BLK
    rows = GATHER_ROWS
    return pl.pallas_call(
        _scatter_kernel,
        grid_spec=pltpu.PrefetchScalarGridSpec(
            num_scalar_prefetch=7,
            grid=(T // rows,),
            in_specs=[pl.BlockSpec((rows * SUBLANES, LANES), lambda i, *refs: (i, 0))],
            out_specs=[pl.BlockSpec(memory_space=pl.ANY), pl.BlockSpec(memory_space=pltpu.SMEM)],
            scratch_shapes=[
                pltpu.VMEM((MOE_BLK * SUBLANES, LANES), src.dtype),
                pltpu.SemaphoreType.DMA(()),
                pltpu.SemaphoreType.DMA(()),
            ],
        ),
        out_shape=[jax.ShapeDtypeStruct((P * SUBLANES, LANES), src.dtype),
                   jax.ShapeDtypeStruct((T * TOP_K,), jnp.int32)],
        compiler_params=_params(("arbitrary",)),
    )(*plan["experts"], *plan["ranks"], plan["slot_start"], plan["packed_start"], plan["zero_block"], src)


def _expert_kernel(layer, be_ref, first_ref, next_ref, slot_ref, nu_ref, nv_ref, pb_ref, inv_ref,
                   xs_ref, w1_hbm, w3_hbm, w2_hbm, yt_ref,
                   w1f, w3f, w2f, w1b, w3b, w2b, ybuf, sems, ysems):
    i = pl.program_id(0)
    D = w1b.shape[0]
    blk = xs_ref.shape[0] // SUBLANES

    def drain_rows(j):
        n = nv_ref[j]
        b = j % 2
        for bit in range(blk.bit_length()):
            size = (1 << bit) * SUBLANES

            @pl.when((n >> bit) & 1 == 1)
            def _():
                pltpu.make_async_copy(ybuf.at[b, pl.ds(0, size)], yt_ref.at[pl.ds(0, size)], ysems.at[b]).wait()

    def weight_copies(e, slot):
        return (pltpu.make_async_copy(w1_hbm.at[layer, e], w1f.at[slot], sems.at[slot, 0]),
                pltpu.make_async_copy(w3_hbm.at[layer, e], w3f.at[slot], sems.at[slot, 1]),
                pltpu.make_async_copy(w2_hbm.at[layer, e], w2f.at[slot], sems.at[slot, 2]))

    @pl.when(i == 0)
    def _():
        for c in weight_copies(be_ref[0], 0):
            c.start()

    @pl.when(first_ref[i] == 1)
    def _():
        slot = slot_ref[i]
        for c in weight_copies(be_ref[i], slot):
            c.wait()

        @pl.when(next_ref[i] >= 0)
        def _():
            for c in weight_copies(next_ref[i], 1 - slot):
                c.start()

        w1b[...] = w1f[slot].astype(BF16)
        w3b[...] = w3f[slot].astype(BF16)
        w2b[...] = w2f[slot].astype(BF16)

    @pl.when(i >= 2)
    def _():
        drain_rows(i - 2)

    @pl.when(i < nu_ref[0])
    def _():
        b = i % 2
        n = nv_ref[i]

        def swiglu(n_rows):
            tiles = pl.ds(0, n_rows * SUBLANES)
            lo, hi = _unpack_bf16_pair(_load_row_tiles(xs_ref.at[tiles]))
            lo = lo.astype(BF16)
            hi = hi.astype(BF16)
            a = _dot(lo, w1b[0:D // 2, :]) + _dot(hi, w1b[D // 2:, :])
            g = _dot(lo, w3b[0:D // 2, :]) + _dot(hi, w3b[D // 2:, :])
            hid = ((a * _sigmoid(a)) * g).astype(BF16)
            _store_row_tiles(ybuf.at[b, tiles],
                             _pack_bf16_pair(_dot(hid, w2b[:, 0:D // 2]), _dot(hid, w2b[:, D // 2:])))

        @pl.when(n > blk // 2)
        def _():
            swiglu(blk)

        @pl.when(n <= blk // 2)
        def _():
            swiglu(blk // 2)

        first_pos = pb_ref[i]
        unroll = 8

        def issue_row(r, priority):
            a_idx = inv_ref[first_pos + r]
            pltpu.make_async_copy(ybuf.at[b, pl.ds(pl.multiple_of(r * SUBLANES, SUBLANES), SUBLANES)],
                                  yt_ref.at[pl.ds(pl.multiple_of(a_idx * SUBLANES, SUBLANES), SUBLANES)],
                                  ysems.at[b]).start(priority=priority)

        def issue_group(g, carry):
            for u in range(unroll):
                issue_row(g * unroll + u, u % 2)
            return carry

        def issue_one(r, carry):
            issue_row(r, 0)
            return carry

        lax.fori_loop(0, n // unroll, issue_group, 0)
        lax.fori_loop((n // unroll) * unroll, n, issue_one, 0)

    @pl.when(i == pl.num_programs(0) - 1)
    def _():
        drain_rows(i - 1)
        drain_rows(i)


def _segment_plan(block_expert, n_used):
    n = block_expert.shape[0]
    idx = jnp.arange(n, dtype=jnp.int32)
    prev = jnp.concatenate([jnp.full((1,), -1, jnp.int32), block_expert[:-1]])
    first = ((block_expert != prev) & (idx < n_used[0])).astype(jnp.int32)
    slot = (jnp.cumsum(first) - 1) % 2
    later_first = jnp.where(first == 1, idx, n)
    next_idx = lax.cummin(jnp.concatenate([later_first[1:], jnp.full((1,), n, jnp.int32)]), reverse=True)
    next_expert = jnp.where(next_idx < n, block_expert[jnp.minimum(next_idx, n - 1)], -1)
    return first, next_expert.astype(jnp.int32), slot.astype(jnp.int32)


def _experts(plan, inv, xs, w1, w3, w2, layer):
    _, _, D, DE = w1.shape
    blk = MOE_BLK
    block_expert, n_used = plan["block_expert"], plan["n_used"]
    n_blocks = block_expert.shape[0]
    n_assign = inv.shape[0]
    first, next_expert, slot = _segment_plan(block_expert, n_used)
    rowmap = lambda i, *refs: (jnp.minimum(i, jnp.maximum(refs[4][0] - 1, 0)), 0)
    hbm = pl.BlockSpec(memory_space=pl.ANY)
    return pl.pallas_call(
        functools.partial(_expert_kernel, layer),
        grid_spec=pltpu.PrefetchScalarGridSpec(
            num_scalar_prefetch=8,
            grid=(n_blocks,),
            in_specs=[pl.BlockSpec((blk * SUBLANES, LANES), rowmap), hbm, hbm, hbm],
            out_specs=hbm,
            scratch_shapes=[
                pltpu.VMEM((2, D, DE), F32),
                pltpu.VMEM((2, D, DE), F32),
                pltpu.VMEM((2, DE, D), F32),
                pltpu.VMEM((D, DE), BF16),
                pltpu.VMEM((D, DE), BF16),
                pltpu.VMEM((DE, D), BF16),
                pltpu.VMEM((2, blk * SUBLANES, LANES), U32),
                pltpu.SemaphoreType.DMA((2, 3)),
                pltpu.SemaphoreType.DMA((2,)),
            ],
        ),
        out_shape=jax.ShapeDtypeStruct((n_assign * SUBLANES, LANES), U32),
        compiler_params=_params(("arbitrary",)),
    )(block_expert, first, next_expert, slot, n_used, plan["n_valid"], plan["packed_base"], inv, xs, w1, w3, w2)


def _final_combine_kernel(yt_ref, h_ref, g2_ref, ew_ref, fg_ref, o_ref):
    hn = h_ref[...] + g2_ref[0] * _moe_mix(yt_ref, ew_ref)
    o_ref[...] = hn * lax.rsqrt(jnp.mean(hn * hn, axis=-1, keepdims=True) + EPS) * fg_ref[...]


def _final_combine(yt, h, mod3, ew, final_g, S):
    T, D = h.shape
    tc = 256
    per_b = S // tc
    return pl.pallas_call(
        _final_combine_kernel,
        grid=(T // tc,),
        in_specs=[
            pl.BlockSpec((tc * TOP_K * SUBLANES, LANES), lambda i: (i, 0)),
            pl.BlockSpec((tc, D), lambda i: (i, 0)),
            pl.BlockSpec((1, 1, D), lambda i: (i // per_b, 0, 5)),
            pl.BlockSpec((tc, LANES), lambda i: (i, 0)),
            pl.BlockSpec((1, D), lambda i: (0, 0)),
        ],
        out_specs=pl.BlockSpec((tc, D), lambda i: (i, 0)),
        out_shape=jax.ShapeDtypeStruct((T, D), F32),
        compiler_params=_params(("parallel",)),
    )(yt, h, mod3, ew, final_g.reshape(1, D))


def kernel(x, c, ada_w, ada_b, norm1_g, w_in, gla_w_a2, gla_b_a, gla_norm_g, lru_conv_w, lru_conv_b,
           lru_wa, lru_ba, lru_wx, lru_bx, lru_lambda, diff_lq1, diff_lk1, diff_lq2, diff_lk2,
           diff_subln_g, rel_bias, w_out, norm2_g, router_g_w, router_g_b, router_e_w, router_e_b,
           moe_w1, moe_w3, moe_w2, final_g):
    B, S, D = x.shape
    T = B * S
    L = ada_w.shape[0]
    h = x.reshape(T, D)
    mod = _ada_mod(c, ada_w, ada_b)
    bias = _bias_tiles(rel_bias)
    w_in_perm = _permute_w_in(w_in)
    w_out_bf16 = w_out.astype(BF16)
    pending_moe = None
    for l in range(L):
        mod3 = mod[l][:, None, :]
        if pending_moe is None:
            proj = _inproj(h, mod3, norm1_g[l], w_in_perm, l, S)
        else:
            proj, h = _inproj(h, mod3, norm1_g[l], w_in_perm, l, S, pending_moe)
        wa2_pad = jnp.concatenate(
            [gla_w_a2[l], jnp.zeros((LANES - GLA_LOWRANK, GLA_KEY_WIDTH), F32)], axis=0).astype(BF16)
        o_gla = _gla(proj, wa2_pad, gla_b_a[l], gla_norm_g[l], B, S)
        w_gates = jnp.concatenate([_block_diag(lru_wa[l]), _block_diag(lru_wx[l])], axis=1).astype(BF16)
        b_gates = jnp.concatenate([lru_ba[l], lru_bx[l]]).reshape(1, 2 * LRU_WIDTH)
        o_lru = _lru(proj.reshape(B, S, PROJ_WIDTH), lru_conv_w[l], lru_conv_b[l], w_gates, b_gates,
                     lru_lambda[l]).reshape(T, LRU_WIDTH)
        lqk = jnp.stack([diff_lq1[l], diff_lk1[l], diff_lq2[l], diff_lk2[l]], axis=0)
        o_diff = _diff_attention(proj, bias, lqk, diff_subln_g[l], l, B, S)
        rw = jnp.concatenate(
            [router_g_w[l], router_e_w[l], jnp.zeros((D, LANES - N_GROUPS - N_EXPERTS), F32)], axis=1).astype(BF16)
        rb = jnp.concatenate(
            [router_g_b[l], router_e_b[l], jnp.zeros((LANES - N_GROUPS - N_EXPERTS,), F32)]).reshape(1, LANES)
        h, u2, info, ew, cnt = _outproj(h, o_gla, o_lru, o_diff, w_out_bf16, l, mod3, norm2_g[l], rw, rb, S)
        counts = cnt[0, N_GROUPS:N_GROUPS + N_EXPERTS].astype(jnp.int32)
        plan = _dispatch(info, counts)
        xs, inv = _scatter_rows(plan, u2)
        yt = _experts(plan, inv, xs, moe_w1, moe_w3, moe_w2, l)
        pending_moe = (yt, ew, mod3)
    out = _final_combine(yt, h, mod3, ew, final_g, S)
    return out.reshape(B, S, D)
```
